```python
import math
import jax, jax.numpy as jnp
from jax import lax
import numpy as np

D_MODEL = 1024
BATCH = 8
SEQ = 2048
DEPTH = 1
DEC_BATCH = 128
DEC_SEQ = 8
PAST_LEN = 16384
PAGE_SIZE = 128

D_LRU = D_MODEL // 2
D_RET = D_MODEL - D_LRU
N_LRU_BLOCKS = 8
LRU_BLOCK = D_LRU // N_LRU_BLOCKS
CONV_W = 4
LRU_C = 8.0
N_RET_HEADS = 4
RET_DK = D_RET // N_RET_HEADS
RET_DV = D_RET // N_RET_HEADS
D_QK = N_RET_HEADS * RET_DK
RET_CHUNK = 128
ROPE_BASE = 10000.0
D_IN_PROJ = 2 * D_LRU + 2 * D_QK + 2 * D_RET
SPLITS = (D_LRU, 2 * D_LRU, 2 * D_LRU + D_QK, 2 * D_LRU + 2 * D_QK, 2 * D_LRU + 2 * D_QK + D_RET)
N_EXPERT_GROUPS = 4
EXPERTS_PER_GROUP = 4
N_EXPERTS = N_EXPERT_GROUPS * EXPERTS_PER_GROUP
TOP_K_IN_GROUP = 2
D_EXPERT = 256
NORM_EPS = 1e-6
GN_EPS = 1e-5

kernel_name = 'hymba_rglru_retention_hmoe_step'

F32 = jnp.float32


def _rmsnorm(x, w):
    xf = x.astype(F32)
    return xf * lax.rsqrt(jnp.mean(xf * xf, axis=-1, keepdims=True) + NORM_EPS) * w.astype(F32)


def _causal_conv(x, buf, w, b):
    t = x.shape[1]
    xx = jnp.concatenate([buf.astype(F32), x], axis=1)
    out = b.astype(F32)
    for j in range(CONV_W):
        out = out + xx[:, j:j + t] * w[j]
    return out, xx[:, t:]


def _rglru(x, h0, w_r, b_r, w_i, b_i, lam):
    bsz, t, _ = x.shape
    xb = x.reshape(bsz, t, N_LRU_BLOCKS, LRU_BLOCK)
    r = jax.nn.sigmoid(jnp.einsum('btnj,njk->btnk', xb, w_r) + b_r).reshape(bsz, t, D_LRU)
    i = jax.nn.sigmoid(jnp.einsum('btnj,njk->btnk', xb, w_i) + b_i).reshape(bsz, t, D_LRU)
    log_a = -LRU_C * r * jax.nn.softplus(-lam.astype(F32))
    a = jnp.exp(log_a)
    u = jnp.sqrt(-jnp.expm1(2.0 * log_a)) * (i * x)

    def step(h, inp):
        a_t, u_t = inp
        h = a_t * h + u_t
        return h, h

    h_last, hs = lax.scan(step, h0.astype(F32), (a.transpose(1, 0, 2), u.transpose(1, 0, 2)))
    return hs.transpose(1, 0, 2), h_last


def _rotary(x, pos):
    half = x.shape[-1] // 2
    inv = ROPE_BASE ** (-jnp.arange(half, dtype=F32) / half)
    ang = pos[:, None] * inv[None, :]
    cos = jnp.cos(ang)[None, :, None, :]
    sin = jnp.sin(ang)[None, :, None, :]
    x1, x2 = x[..., :half], x[..., half:]
    return jnp.concatenate([x1 * cos - x2 * sin, x1 * sin + x2 * cos], axis=-1)


def _retention(q, k, v, s0):
    bsz, t, h, dk = q.shape
    dv = v.shape[-1]
    c = math.gcd(t, RET_CHUNK)
    nc = t // c
    log_g = jnp.log1p(-jnp.exp2(-5.0 - jnp.arange(h, dtype=F32)))
    q = q.reshape(bsz, nc, c, h, dk)
    k = k.reshape(bsz, nc, c, h, dk)
    v = v.reshape(bsz, nc, c, h, dv)
    idx = jnp.arange(c, dtype=F32)
    diff = idx[:, None] - idx[None, :]
    mask = jnp.where(diff[None] >= 0, jnp.exp(jnp.maximum(diff, 0.0)[None] * log_g[:, None, None]), 0.0)
    scores = jnp.einsum('bnihd,bnjhd->bnhij', q, k) * mask[None, None]
    inner = jnp.einsum('bnhij,bnjhe->bnihe', scores, v)
    w_state = jnp.exp((c - 1.0 - idx)[None, :] * log_g[:, None])
    kv = jnp.einsum('bnjhd,hj,bnjhe->nbhde', k, w_state, v)
    chunk_decay = jnp.exp(c * log_g)[:, None, None]

    def step(s, kv_n):
        return chunk_decay * s + kv_n, s

    s_last, s_prev = lax.scan(step, s0.astype(F32), kv)
    cross_decay = jnp.exp((idx + 1.0)[:, None] * log_g[None, :])
    cross = jnp.einsum('bnihd,nbhde->bnihe', q, s_prev) * cross_decay[None, None, :, :, None]
    return (inner + cross).reshape(bsz, t, h, dv), s_last


def _mixer(u, conv_buf, h0, s0, pos, w_in, conv_w, conv_b, w_rgate, b_rgate, w_igate, b_igate,
           lru_lambda, ret_gn_w, w_out):
    bsz, t, _ = u.shape
    z = u @ w_in
    xr, yg, q, k, v, g = jnp.split(z, SPLITS, axis=-1)
    xc, new_buf = _causal_conv(xr, conv_buf, conv_w, conv_b)
    hseq, h_last = _rglru(xc, h0, w_rgate, b_rgate, w_igate, b_igate, lru_lambda)
    out_a = hseq * jax.nn.gelu(yg, approximate=True)
    q = _rotary(q.reshape(bsz, t, N_RET_HEADS, RET_DK), pos) * (RET_DK ** -0.5)
    k = _rotary(k.reshape(bsz, t, N_RET_HEADS, RET_DK), pos)
    v = v.reshape(bsz, t, N_RET_HEADS, RET_DV)
    o, s_last = _retention(q, k, v, s0)
    mu = jnp.mean(o, axis=-1, keepdims=True)
    var = jnp.mean(jnp.square(o - mu), axis=-1, keepdims=True)
    o = ((o - mu) * lax.rsqrt(var + GN_EPS)).reshape(bsz, t, D_RET)
    out_b = o * ret_gn_w * jax.nn.silu(g)
    y = jnp.concatenate([out_a, out_b], axis=-1) @ w_out
    return y, new_buf, h_last, s_last


def _moe(u, w_rg, b_rg, w_re, b_re, w_g, w_u, w_d):
    bsz, t, d = u.shape
    xf = u.reshape(bsz * t, d)
    p_group = jax.nn.softmax((xf @ w_rg + b_rg).astype(F32), axis=-1)
    g_sel = jnp.argmax(p_group, axis=-1)
    p_g = jnp.take_along_axis(p_group, g_sel[:, None], axis=-1)[:, 0]
    e_logits = (xf @ w_re + b_re).astype(F32).reshape(-1, N_EXPERT_GROUPS, EXPERTS_PER_GROUP)
    in_group = jnp.take_along_axis(e_logits, g_sel[:, None, None], axis=1)[:, 0]
    p_e = jax.nn.softmax(in_group, axis=-1)
    top_w, top_i = lax.top_k(p_e, TOP_K_IN_GROUP)
    top_w = top_w / jnp.sum(top_w, axis=-1, keepdims=True)
    expert_id = g_sel[:, None] * EXPERTS_PER_GROUP + top_i
    gates = jnp.sum(jax.nn.one_hot(expert_id, N_EXPERTS, dtype=F32)
                    * (p_g[:, None] * top_w)[..., None], axis=1)
    h = jax.nn.silu(jnp.einsum('nd,edf->nef', xf, w_g)) * jnp.einsum('nd,edf->nef', xf, w_u)
    out = jnp.einsum('nef,efd->nd', h * gates[..., None], w_d)
    return out.reshape(bsz, t, d)


def _layer(x, c, conv_buf, h0, s0, pos, w_mod, b_mod, n_pre1, n_post1, n_pre2, n_post2,
           w_in, conv_w, conv_b, w_rgate, b_rgate, w_igate, b_igate, lru_lambda, ret_gn_w, w_out,
           w_rg, b_rg, w_re, b_re, w_g, w_u, w_d):
    mod = jax.nn.silu(c.astype(F32)) @ w_mod + b_mod
    sh1, sc1, g1, sh2, sc2, g2 = jnp.split(mod, 6, axis=-1)
    u = _rmsnorm(x, n_pre1) * (1.0 + sc1[:, None]) + sh1[:, None]
    m, buf, h_last, s_last = _mixer(u, conv_buf, h0, s0, pos, w_in, conv_w, conv_b, w_rgate, b_rgate,
                                    w_igate, b_igate, lru_lambda, ret_gn_w, w_out)
    x = x + g1[:, None] * _rmsnorm(m, n_post1)
    u2 = _rmsnorm(x, n_pre2) * (1.0 + sc2[:, None]) + sh2[:, None]
    f = _moe(u2, w_rg, b_rg, w_re, b_re, w_g, w_u, w_d)
    x = x + g2[:, None] * _rmsnorm(f, n_post2)
    return x, buf, h_last, s_last


def setup_inputs(seed: int = 0) -> dict:
    key = jax.random.key(seed)
    ks = jax.random.split(key, 32)
    L = DEPTH

    def nrm(k, shape, scale):
        return jax.random.normal(k, shape, F32) * scale

    u_lam = jax.random.uniform(ks[14], (L, D_LRU), F32, 0.9, 0.999)
    p_lam = u_lam ** (1.0 / LRU_C)
    return {
        'x_prompt': nrm(ks[0], (BATCH, SEQ, D_MODEL), 1.0),
        'x_sample': nrm(ks[1], (DEC_BATCH, DEC_SEQ, D_MODEL), 1.0),
        'state_conv': nrm(ks[2], (L, DEC_BATCH, CONV_W - 1, D_LRU), 1.0),
        'state_lru': nrm(ks[3], (L, DEC_BATCH, D_LRU), 0.5),
        'state_ret': nrm(ks[4], (L, DEC_BATCH, N_RET_HEADS, RET_DK, RET_DV), 4.0),
        'c_prompt': nrm(ks[5], (BATCH, D_MODEL), 1.0),
        'c_sample': nrm(ks[6], (DEC_BATCH, D_MODEL), 1.0),
        'w_mod': nrm(ks[7], (L, D_MODEL, 6 * D_MODEL), 0.5 * D_MODEL ** -0.5),
        'b_mod': nrm(ks[8], (L, 6 * D_MODEL), 0.02),
        'norm_pre_mix': 1.0 + nrm(ks[9], (L, D_MODEL), 0.05),
        'norm_post_mix': 1.0 + nrm(ks[10], (L, D_MODEL), 0.05),
        'norm_pre_ffn': 1.0 + nrm(ks[11], (L, D_MODEL), 0.05),
        'norm_post_ffn': 1.0 + nrm(ks[12], (L, D_MODEL), 0.05),
        'w_in': nrm(ks[13], (L, D_MODEL, D_IN_PROJ), D_MODEL ** -0.5),
        'conv_w': nrm(ks[15], (L, CONV_W, D_LRU), CONV_W ** -0.5),
        'conv_b': nrm(ks[16], (L, D_LRU), 0.02),
        'w_rgate': nrm(ks[17], (L, N_LRU_BLOCKS, LRU_BLOCK, LRU_BLOCK), LRU_BLOCK ** -0.5),
        'b_rgate': nrm(ks[18], (L, N_LRU_BLOCKS, LRU_BLOCK), 0.02),
        'w_igate': nrm(ks[19], (L, N_LRU_BLOCKS, LRU_BLOCK, LRU_BLOCK), LRU_BLOCK ** -0.5),
        'b_igate': nrm(ks[20], (L, N_LRU_BLOCKS, LRU_BLOCK), 0.02),
        'lru_lambda': jnp.log(p_lam) - jnp.log1p(-p_lam),
        'ret_gn_w': 1.0 + nrm(ks[21], (L, D_RET), 0.05),
        'w_out': nrm(ks[22], (L, D_MODEL, D_MODEL), D_MODEL ** -0.5),
        'w_router_group': nrm(ks[23], (L, D_MODEL, N_EXPERT_GROUPS), D_MODEL ** -0.5),
        'b_router_group': nrm(ks[24], (L, N_EXPERT_GROUPS), 0.01),
        'w_router_expert': nrm(ks[25], (L, D_MODEL, N_EXPERTS), D_MODEL ** -0.5),
        'b_router_expert': nrm(ks[26], (L, N_EXPERTS), 0.01),
        'w_exp_gate': nrm(ks[27], (L, N_EXPERTS, D_MODEL, D_EXPERT), D_MODEL ** -0.5),
        'w_exp_up': nrm(ks[28], (L, N_EXPERTS, D_MODEL, D_EXPERT), D_MODEL ** -0.5),
        'w_exp_down': nrm(ks[29], (L, N_EXPERTS, D_EXPERT, D_MODEL), D_EXPERT ** -0.5),
    }


def reference(x_prompt, x_sample, state_conv, state_lru, state_ret, c_prompt, c_sample,
              w_mod, b_mod, norm_pre_mix, norm_post_mix, norm_pre_ffn, norm_post_ffn,
              w_in, conv_w, conv_b, w_rgate, b_rgate, w_igate, b_igate, lru_lambda, ret_gn_w, w_out,
              w_router_group, b_router_group, w_router_expert, b_router_expert,
              w_exp_gate, w_exp_up, w_exp_down):
    bp, tp, _ = x_prompt.shape
    bs, ts, _ = x_sample.shape
    pos_p = jnp.arange(tp, dtype=F32)
    pos_s = jnp.float32(PAST_LEN) + jnp.arange(ts, dtype=F32)
    yp = x_prompt.astype(F32)
    ys = x_sample.astype(F32)
    zero_conv = jnp.zeros((bp, CONV_W - 1, D_LRU), F32)
    zero_h = jnp.zeros((bp, D_LRU), F32)
    zero_s = jnp.zeros((bp, N_RET_HEADS, RET_DK, RET_DV), F32)
    conv_p, lru_p, ret_p, conv_s, lru_s, ret_s = [], [], [], [], [], []
    for l in range(DEPTH):
        lw = (w_mod[l], b_mod[l], norm_pre_mix[l], norm_post_mix[l], norm_pre_ffn[l], norm_post_ffn[l],
              w_in[l], conv_w[l], conv_b[l], w_rgate[l], b_rgate[l], w_igate[l], b_igate[l],
              lru_lambda[l], ret_gn_w[l], w_out[l], w_router_group[l], b_router_group[l],
              w_router_expert[l], b_router_expert[l], w_exp_gate[l], w_exp_up[l], w_exp_down[l])
        yp, cb, hb, sb = _layer(yp, c_prompt, zero_conv, zero_h, zero_s, pos_p, *lw)
        conv_p.append(cb); lru_p.append(hb); ret_p.append(sb)
        ys, cb, hb, sb = _layer(ys, c_sample, state_conv[l], state_lru[l], state_ret[l], pos_s, *lw)
        conv_s.append(cb); lru_s.append(hb); ret_s.append(sb)
    return (yp.astype(x_prompt.dtype), ys.astype(x_sample.dtype),
            jnp.stack(conv_p).astype(state_conv.dtype), jnp.stack(lru_p).astype(state_lru.dtype),
            jnp.stack(ret_p).astype(state_ret.dtype),
            jnp.stack(conv_s).astype(state_conv.dtype), jnp.stack(lru_s).astype(state_lru.dtype),
            jnp.stack(ret_s).astype(state_ret.dtype))
```

```python
import functools
import math

import jax
import jax.numpy as jnp
from jax import lax
from jax.experimental import pallas as pl
from jax.experimental.pallas import tpu as pltpu

F32 = jnp.float32
BF16 = jnp.bfloat16

D_MODEL = 1024
D_LRU = 512
D_RET = 512
N_LRU_BLOCKS = 8
LRU_BLOCK = D_LRU // N_LRU_BLOCKS
CONV_W = 4
LRU_C = 8.0
N_HEADS = 4
DK = 128
DV = 128
RET_CHUNK = 128
ROPE_BASE = 10000.0
D_IN_PROJ = 3072
N_GROUPS = 4
PER_GROUP = 4
N_EXPERTS = 16
D_EXPERT = 256
NORM_EPS = 1e-6
GN_EPS = 1e-5
PAST_LEN = 16384

SUBLANES = 8
LANES = 128
GATE_HALF = 256
VMEM_LIMIT = 56 * 1024 * 1024

PROMPT_TILE = 256
SAMPLE_SEQS = 16
MOE_TILE = 512


def _silu(x):
    return x * jax.nn.sigmoid(x)


def _rms_scale(x):
    return lax.rsqrt(jnp.mean(x * x, axis=-1, keepdims=True) + NORM_EPS)


def _mod_kernel(c_ref, w_ref, b_ref, o_ref):
    s = _silu(c_ref[...]).astype(BF16)
    o_ref[...] = jnp.dot(s, w_ref[...].astype(BF16), preferred_element_type=F32) + b_ref[...]


def _mod_call(c_all, w_mod, b_mod):
    rows = c_all.shape[0]
    ncol = w_mod.shape[1]
    blk = 1024
    return pl.pallas_call(
        _mod_kernel,
        grid=(ncol // blk,),
        in_specs=[
            pl.BlockSpec((rows, D_MODEL), lambda j: (0, 0)),
            pl.BlockSpec((D_MODEL, blk), lambda j: (0, j)),
            pl.BlockSpec((1, blk), lambda j: (0, j)),
        ],
        out_specs=pl.BlockSpec((rows, blk), lambda j: (0, j)),
        out_shape=jax.ShapeDtypeStruct((rows, ncol), F32),
        compiler_params=pltpu.CompilerParams(
            dimension_semantics=("arbitrary",), vmem_limit_bytes=VMEM_LIMIT),
        name="mod",
    )(c_all, w_mod, b_mod)


def _in_proj(x3, mod3, n_pre1_ref, w_in_ref):
    bb, tt, _ = x3.shape
    sh1 = mod3[:, :, 0:D_MODEL]
    sc1 = mod3[:, :, D_MODEL:2 * D_MODEL]
    n1 = n_pre1_ref[...].reshape(1, 1, D_MODEL)
    u = x3 * _rms_scale(x3) * n1 * (1.0 + sc1) + sh1
    u2d = u.reshape(bb * tt, D_MODEL).astype(BF16)
    return jnp.dot(u2d, w_in_ref[...], preferred_element_type=F32)


def _lru_coeffs(xc, wg_ref, b_r_ref, b_i_ref, lam_ref):
    xcb = xc.astype(BF16)
    g0 = jnp.dot(xcb[:, :GATE_HALF], wg_ref[0], preferred_element_type=F32)
    g1 = jnp.dot(xcb[:, GATE_HALF:], wg_ref[1], preferred_element_type=F32)
    r = jax.nn.sigmoid(jnp.concatenate([g0[:, :GATE_HALF], g1[:, :GATE_HALF]], axis=1) + b_r_ref[...])
    i = jax.nn.sigmoid(jnp.concatenate([g0[:, GATE_HALF:], g1[:, GATE_HALF:]], axis=1) + b_i_ref[...])
    lam = lam_ref[...]
    sp = jnp.maximum(-lam, 0.0) + jnp.log1p(jnp.exp(-jnp.abs(lam)))
    log_a = -LRU_C * r * sp
    a = jnp.exp(log_a)
    gain = jnp.sqrt(-jnp.tanh(log_a) * (1.0 + a * a))
    return a, gain * (i * xc)


def _rope(xh, cos2, sin2, lane_axis):
    return xh * cos2 + pltpu.roll(xh, DK // 2, axis=lane_axis) * sin2


def _group_norm(o):
    mu = jnp.mean(o, axis=-1, keepdims=True)
    d = o - mu
    var = jnp.mean(d * d, axis=-1, keepdims=True)
    return d * lax.rsqrt(var + GN_EPS)


def _post_mixer(x3, mod3, out_a, out_b, w_out_ref, n_post1_ref, n_pre2_ref, w_router_ref, b_router_ref,
                x1_ref, u2_ref, gates_ref):
    bb, tt, _ = x3.shape
    m = bb * tt
    y = (jnp.dot(out_a.astype(BF16), w_out_ref[0:D_LRU, :], preferred_element_type=F32)
         + jnp.dot(out_b.astype(BF16), w_out_ref[D_LRU:, :], preferred_element_type=F32))
    g1 = mod3[:, :, 2 * D_MODEL:3 * D_MODEL]
    sh2 = mod3[:, :, 3 * D_MODEL:4 * D_MODEL]
    sc2 = mod3[:, :, 4 * D_MODEL:5 * D_MODEL]
    y3 = y.reshape(bb, tt, D_MODEL)
    x1 = x3 + g1 * (y3 * _rms_scale(y3) * n_post1_ref[...].reshape(1, 1, D_MODEL))
    u2 = x1 * _rms_scale(x1) * n_pre2_ref[...].reshape(1, 1, D_MODEL) * (1.0 + sc2) + sh2
    x1_ref[...] = x1.reshape(m, D_MODEL)
    u2b = u2.reshape(m, D_MODEL).astype(BF16)
    u2_ref[...] = u2b

    logits = jnp.dot(u2b, w_router_ref[...], preferred_element_type=F32) + b_router_ref[...]
    lane = lax.broadcasted_iota(jnp.int32, (m, LANES), 1)
    neg = jnp.float32(-jnp.inf)
    is_g = (lane >= N_EXPERTS) & (lane < N_EXPERTS + N_GROUPS)
    lg = jnp.where(is_g, logits, neg)
    eg = jnp.where(is_g, jnp.exp(logits - jnp.max(lg, axis=-1, keepdims=True)), 0.0)
    p_group = eg / jnp.sum(eg, axis=-1, keepdims=True)
    p_g = jnp.max(p_group, axis=-1, keepdims=True)
    g_lane = jnp.min(jnp.where(is_g & (p_group == p_g), lane, LANES), axis=-1, keepdims=True)
    e_lo = (g_lane - N_EXPERTS) * PER_GROUP
    in_g = (lane >= e_lo) & (lane < e_lo + PER_GROUP)
    le = jnp.where(in_g, logits, neg)
    ee = jnp.where(in_g, jnp.exp(logits - jnp.max(le, axis=-1, keepdims=True)), 0.0)
    p_e = ee / jnp.sum(ee, axis=-1, keepdims=True)
    pm = jnp.where(in_g, p_e, -1.0)
    w1 = jnp.max(pm, axis=-1, keepdims=True)
    i1 = jnp.min(jnp.where(pm == w1, lane, LANES), axis=-1, keepdims=True)
    pm2 = jnp.where(lane == i1, -1.0, pm)
    w2 = jnp.max(pm2, axis=-1, keepdims=True)
    i2 = jnp.min(jnp.where(pm2 == w2, lane, LANES), axis=-1, keepdims=True)
    wsum = w1 + w2
    gates_ref[...] = jnp.where(lane == i1, p_g * (w1 / wsum),
                               jnp.where(lane == i2, p_g * (w2 / wsum), 0.0))


def _scan_rows(a, b):
    n = a.shape[0]
    row = lax.broadcasted_iota(jnp.int32, a.shape, 0)
    s = 1
    while s < n:
        keep = row >= s
        a_sh = jnp.where(keep, pltpu.roll(a, s, axis=0), 1.0)
        b_sh = jnp.where(keep, pltpu.roll(b, s, axis=0), 0.0)
        b = a * b_sh + b
        a = a * a_sh
        s *= 2
    return a, b


def _prompt_mixer_kernel(x_ref, mod_ref, cos_ref, sin_ref,
                         n_pre1_ref, n_post1_ref, n_pre2_ref,
                         w_in_ref, conv_w_ref, conv_b_ref, wg_ref, b_r_ref, b_i_ref, lam_ref,
                         gn_w_ref, w_out_ref, w_router_ref, b_router_ref,
                         mask_ref, wstate_ref, cross_ref, cdecay_ref,
                         x1_ref, u2_ref, gates_ref, conv_out_ref, lru_out_ref, ret_out_ref,
                         conv_scr, h_scr, s_scr):
    t = pl.program_id(1)
    tt = x_ref.shape[1]

    @pl.when(t == 0)
    def _():
        conv_scr[...] = jnp.zeros_like(conv_scr)
        h_scr[...] = jnp.zeros_like(h_scr)
        s_scr[...] = jnp.zeros_like(s_scr)

    x3 = x_ref[...]
    mod3 = mod_ref[...]
    z = _in_proj(x3, mod3, n_pre1_ref, w_in_ref)
    xr = z[:, 0:D_LRU]
    yg = z[:, D_LRU:2 * D_LRU]
    q = z[:, 1024:1536]
    k = z[:, 1536:2048]
    v = z[:, 2048:2560]
    g = z[:, 2560:3072]

    xx = jnp.concatenate([conv_scr[...], xr], axis=0)
    xc = jnp.broadcast_to(conv_b_ref[...], (tt, D_LRU))
    for j in range(CONV_W):
        off = SUBLANES - (CONV_W - 1) + j
        xc = xc + xx[off:off + tt, :] * conv_w_ref[j:j + 1, :]
    conv_scr[...] = xr[tt - SUBLANES:, :]

    a, b = _lru_coeffs(xc, wg_ref, b_r_ref, b_i_ref, lam_ref)
    a_cum, b_cum = _scan_rows(a, b)
    hseq = b_cum + a_cum * h_scr[0:1, :]
    h_scr[...] = jnp.broadcast_to(hseq[tt - 1:tt, :], h_scr.shape)
    out_a = hseq * jax.nn.gelu(yg, approximate=True)

    cos2 = cos_ref[...]
    sin2 = sin_ref[...]
    scale = DK ** -0.5
    o_heads = []
    for h in range(N_HEADS):
        hs = slice(h * DK, (h + 1) * DK)
        qh = (_rope(q[:, hs], cos2, sin2, 1) * scale).astype(BF16)
        kh = _rope(k[:, hs], cos2, sin2, 1)
        vh = v[:, hs].astype(BF16)
        o_chunks = []
        for c in range(tt // RET_CHUNK):
            cs = slice(c * RET_CHUNK, (c + 1) * RET_CHUNK)
            qc = qh[cs]
            kc = kh[cs]
            vc = vh[cs]
            s_prev = s_scr[h]
            scores = lax.dot_general(qc, kc.astype(BF16), (((1,), (1,)), ((), ())),
                                     preferred_element_type=F32) * mask_ref[h]
            inner = jnp.dot(scores.astype(BF16), vc, preferred_element_type=F32)
            cross = jnp.dot(qc, s_prev.astype(BF16), preferred_element_type=F32) * cross_ref[:, hs]
            kw = (kc * wstate_ref[:, hs]).astype(BF16)
            kv = lax.dot_general(kw, vc, (((0,), (0,)), ((), ())), preferred_element_type=F32)
            s_scr[h] = cdecay_ref[:, hs] * s_prev + kv
            o_chunks.append(inner + cross)
        o_heads.append(_group_norm(jnp.concatenate(o_chunks, axis=0)))
    o = jnp.concatenate(o_heads, axis=1)
    out_b = o * gn_w_ref[...] * _silu(g)

    _post_mixer(x3, mod3, out_a, out_b, w_out_ref, n_post1_ref, n_pre2_ref, w_router_ref, b_router_ref,
                x1_ref, u2_ref, gates_ref)

    @pl.when(t == pl.num_programs(1) - 1)
    def _():
        conv_out_ref[0] = xr[tt - SUBLANES:, :]
        lru_out_ref[0] = hseq[tt - SUBLANES:, :]
        ret_out_ref[0] = s_scr[...]


def _const_spec(shape):
    nd = len(shape)
    return pl.BlockSpec(shape, lambda *_: (0,) * nd)


def _prompt_mixer_call(x, mod3, cos2, sin2, wts, tables):
    bsz, seq, _ = x.shape
    tt = PROMPT_TILE
    nt = seq // tt
    n_tok = bsz * seq
    (n_pre1, n_post1, n_pre2, w_in, conv_w, conv_b, wg, b_r, b_i, lam, gn_w, w_out, w_router, b_router) = wts
    mask, wstate, cross, cdecay = tables
    tok_spec = pl.BlockSpec((tt, D_MODEL), lambda b, t: (b * nt + t, 0))
    in_specs = [
        pl.BlockSpec((1, tt, D_MODEL), lambda b, t: (b, t, 0)),
        pl.BlockSpec((1, 1, 6 * D_MODEL), lambda b, t: (b, 0, 0)),
        pl.BlockSpec((tt, LANES), lambda b, t: (t, 0)),
        pl.BlockSpec((tt, LANES), lambda b, t: (t, 0)),
    ] + [_const_spec(w.shape) for w in wts] + [_const_spec(tb.shape) for tb in tables]
    out_specs = [
        tok_spec,
        tok_spec,
        pl.BlockSpec((tt, LANES), lambda b, t: (b * nt + t, 0)),
        pl.BlockSpec((1, SUBLANES, D_LRU), lambda b, t: (b, 0, 0)),
        pl.BlockSpec((1, SUBLANES, D_LRU), lambda b, t: (b, 0, 0)),
        pl.BlockSpec((1, N_HEADS, DK, DV), lambda b, t: (b, 0, 0, 0)),
    ]
    out_shape = [
        jax.ShapeDtypeStruct((n_tok, D_MODEL), F32),
        jax.ShapeDtypeStruct((n_tok, D_MODEL), BF16),
        jax.ShapeDtypeStruct((n_tok, LANES), F32),
        jax.ShapeDtypeStruct((bsz, SUBLANES, D_LRU), F32),
        jax.ShapeDtypeStruct((bsz, SUBLANES, D_LRU), F32),
        jax.ShapeDtypeStruct((bsz, N_HEADS, DK, DV), F32),
    ]
    return pl.pallas_call(
        _prompt_mixer_kernel,
        grid=(bsz, nt),
        in_specs=in_specs,
        out_specs=out_specs,
        out_shape=out_shape,
        scratch_shapes=[
            pltpu.VMEM((SUBLANES, D_LRU), F32),
            pltpu.VMEM((SUBLANES, D_LRU), F32),
            pltpu.VMEM((N_HEADS, DK, DV), F32),
        ],
        compiler_params=pltpu.CompilerParams(
            dimension_semantics=("arbitrary", "arbitrary"), vmem_limit_bytes=VMEM_LIMIT),
        name="prompt_mixer",
    )(x, mod3, cos2, sin2, *wts, *tables)


def _sample_mixer_kernel(x_ref, mod_ref, cos_ref, sin_ref, buf_ref, h0_ref, s0_ref,
                         n_pre1_ref, n_post1_ref, n_pre2_ref,
                         w_in_ref, conv_w_ref, conv_b_ref, wg_ref, b_r_ref, b_i_ref, lam_ref,
                         gn_w_ref, w_out_ref, w_router_ref, b_router_ref,
                         smask_ref, wstate_ref, cross_ref, cdecay_ref,
                         x1_ref, u2_ref, gates_ref, xr_out_ref, h_out_ref, ret_out_ref):
    bb, ts, _ = x_ref.shape
    m = bb * ts
    x3 = x_ref[...]
    mod3 = mod_ref[...]
    z = _in_proj(x3, mod3, n_pre1_ref, w_in_ref)
    xr = z[:, 0:D_LRU]
    yg = z[:, D_LRU:2 * D_LRU]
    q = z[:, 1024:1536]
    k = z[:, 1536:2048]
    v = z[:, 2048:2560]
    g = z[:, 2560:3072]
    xr_out_ref[...] = xr

    xr3 = xr.reshape(bb, ts, D_LRU)
    buf3 = buf_ref[...]
    tpos = lax.broadcasted_iota(jnp.int32, (bb, ts, D_LRU), 1)
    xc3 = jnp.broadcast_to(conv_b_ref[...].reshape(1, 1, D_LRU), (bb, ts, D_LRU))
    for j in range(CONV_W):
        back = CONV_W - 1 - j
        w_j = conv_w_ref[j:j + 1, :].reshape(1, 1, D_LRU)
        if back == 0:
            term = xr3
        else:
            cur = pltpu.roll(xr3, back, axis=1)
            up = CONV_W - 1 - back
            old = buf3 if up == 0 else pltpu.roll(buf3, ts - up, axis=1)
            term = jnp.where(tpos >= back, cur, old)
        xc3 = xc3 + term * w_j
    xc = xc3.reshape(m, D_LRU)

    a, b = _lru_coeffs(xc, wg_ref, b_r_ref, b_i_ref, lam_ref)
    a3 = a.reshape(bb, ts, D_LRU)
    b3 = b.reshape(bb, ts, D_LRU) + a3 * h0_ref[...]
    s = 1
    while s < ts:
        keep = tpos >= s
        a_sh = jnp.where(keep, pltpu.roll(a3, s, axis=1), 1.0)
        b_sh = jnp.where(keep, pltpu.roll(b3, s, axis=1), 0.0)
        b3 = a3 * b_sh + b3
        a3 = a3 * a_sh
        s *= 2
    hseq = b3.reshape(m, D_LRU)
    h_out_ref[...] = hseq
    out_a = hseq * jax.nn.gelu(yg, approximate=True)

    cos2 = cos_ref[...].reshape(1, ts, LANES)
    sin2 = sin_ref[...].reshape(1, ts, LANES)
    scale = DK ** -0.5
    o_heads = []
    for h in range(N_HEADS):
        hs = slice(h * DK, (h + 1) * DK)
        q3 = (_rope(q[:, hs].reshape(bb, ts, DK), cos2, sin2, 2) * scale).astype(BF16)
        k3 = _rope(k[:, hs].reshape(bb, ts, DK), cos2, sin2, 2)
        v3 = v[:, hs].reshape(bb, ts, DV).astype(BF16)
        q2 = q3.reshape(m, DK)
        k2 = k3.reshape(m, DK).astype(BF16)
        v2 = v3.reshape(m, DV)
        scores = lax.dot_general(q2, k2, (((1,), (1,)), ((), ())),
                                 preferred_element_type=F32) * smask_ref[h]
        inner = jnp.dot(scores.astype(BF16), v2, preferred_element_type=F32)
        s0h = s0_ref[:, h]
        cross = jnp.einsum('bid,bde->bie', q3, s0h.astype(BF16), preferred_element_type=F32)
        cross = cross * cross_ref[:, hs].reshape(1, ts, DV)
        kw3 = (k3 * wstate_ref[:, hs].reshape(1, ts, DK)).astype(BF16)
        kv = jnp.einsum('bjd,bje->bde', kw3, v3, preferred_element_type=F32)
        ret_out_ref[:, h] = cdecay_ref[:, hs].reshape(1, 1, DV) * s0h + kv
        o_heads.append(_group_norm(inner + cross.reshape(m, DV)))
    o = jnp.concatenate(o_heads, axis=1)
    out_b = o * gn_w_ref[...] * _silu(g)

    _post_mixer(x3, mod3, out_a, out_b, w_out_ref, n_post1_ref, n_pre2_ref, w_router_ref, b_router_ref,
                x1_ref, u2_ref, gates_ref)


def _sample_mixer_call(x, mod3, cos2, sin2, buf8, h0p, s0, wts, tables):
    bsz, ts, _ = x.shape
    bb = SAMPLE_SEQS
    m = bb * ts
    n_tok = bsz * ts
    seq_spec = lambda w: pl.BlockSpec((bb, ts, w), lambda i: (i, 0, 0))
    tok_spec = lambda w: pl.BlockSpec((m, w), lambda i: (i, 0))
    in_specs = [
        seq_spec(D_MODEL),
        pl.BlockSpec((bb, 1, 6 * D_MODEL), lambda i: (i, 0, 0)),
        _const_spec(cos2.shape),
        _const_spec(sin2.shape),
        seq_spec(D_LRU),
        seq_spec(D_LRU),
        pl.BlockSpec((bb, N_HEADS, DK, DV), lambda i: (i, 0, 0, 0)),
    ] + [_const_spec(w.shape) for w in wts] + [_const_spec(tb.shape) for tb in tables]
    out_specs = [
        tok_spec(D_MODEL),
        tok_spec(D_MODEL),
        tok_spec(LANES),
        tok_spec(D_LRU),
        tok_spec(D_LRU),
        pl.BlockSpec((bb, N_HEADS, DK, DV), lambda i: (i, 0, 0, 0)),
    ]
    out_shape = [
        jax.ShapeDtypeStruct((n_tok, D_MODEL), F32),
        jax.ShapeDtypeStruct((n_tok, D_MODEL), BF16),
        jax.ShapeDtypeStruct((n_tok, LANES), F32),
        jax.ShapeDtypeStruct((n_tok, D_LRU), F32),
        jax.ShapeDtypeStruct((n_tok, D_LRU), F32),
        jax.ShapeDtypeStruct((bsz, N_HEADS, DK, DV), F32),
    ]
    return pl.pallas_call(
        _sample_mixer_kernel,
        grid=(bsz // bb,),
        in_specs=in_specs,
        out_specs=out_specs,
        out_shape=out_shape,
        compiler_params=pltpu.CompilerParams(
            dimension_semantics=("arbitrary",), vmem_limit_bytes=VMEM_LIMIT),
        name="sample_mixer",
    )(x, mod3, cos2, sin2, buf8, h0p, s0, *wts, *tables)


def _moe_kernel(x1_ref, u2_ref, gates_ref, mod_ref, n_post2_ref, wg_ref, wu_ref, wd_ref, o_ref, acc_ref):
    e = pl.program_id(1)

    @pl.when(e == 0)
    def _():
        acc_ref[...] = jnp.zeros_like(acc_ref)

    x = u2_ref[...]
    lane = lax.broadcasted_iota(jnp.int32, gates_ref.shape, 1)
    gate = jnp.sum(jnp.where(lane == e, gates_ref[...], 0.0), axis=-1, keepdims=True)
    hg = jnp.dot(x, wg_ref[0], preferred_element_type=F32)
    hu = jnp.dot(x, wu_ref[0], preferred_element_type=F32)
    hid = (_silu(hg) * hu * gate).astype(BF16)
    acc_ref[...] += jnp.dot(hid, wd_ref[0], preferred_element_type=F32)

    @pl.when(e == pl.num_programs(1) - 1)
    def _():
        mod3 = mod_ref[...]
        bb = mod3.shape[0]
        f3 = acc_ref[...].reshape(bb, -1, D_MODEL)
        g2 = mod3[:, :, 5 * D_MODEL:6 * D_MODEL]
        x13 = x1_ref[...].reshape(f3.shape)
        out = x13 + g2 * (f3 * _rms_scale(f3) * n_post2_ref[...].reshape(1, 1, D_MODEL))
        o_ref[...] = out.reshape(o_ref.shape)


def _moe_call(x1, u2, gates, mod3, n_post2, wg, wu, wd, seqs_per_tile):
    n_tok = x1.shape[0]
    tm = MOE_TILE
    tok = lambda w: pl.BlockSpec((tm, w), lambda i, e: (i, 0))
    return pl.pallas_call(
        _moe_kernel,
        grid=(n_tok // tm, N_EXPERTS),
        in_specs=[
            tok(D_MODEL), tok(D_MODEL), tok(LANES),
            pl.BlockSpec((seqs_per_tile, 1, 6 * D_MODEL),
                         (lambda i, e: (i, 0, 0)) if seqs_per_tile > 1 else
                         (lambda i, e: ((i * tm) // (n_tok // mod3.shape[0]), 0, 0))),
            pl.BlockSpec((1, D_MODEL), lambda i, e: (0, 0)),
            pl.BlockSpec((1, D_MODEL, D_EXPERT), lambda i, e: (e, 0, 0)),
            pl.BlockSpec((1, D_MODEL, D_EXPERT), lambda i, e: (e, 0, 0)),
            pl.BlockSpec((1, D_EXPERT, D_MODEL), lambda i, e: (e, 0, 0)),
        ],
        out_specs=tok(D_MODEL),
        out_shape=jax.ShapeDtypeStruct((n_tok, D_MODEL), F32),
        scratch_shapes=[pltpu.VMEM((tm, D_MODEL), F32)],
        compiler_params=pltpu.CompilerParams(
            dimension_semantics=("arbitrary", "arbitrary"), vmem_limit_bytes=VMEM_LIMIT),
        name="moe",
    )(x1, u2, gates, mod3, n_post2, wg, wu, wd)


def _block_diag_gate(w_r, w_i):
    per_half = GATE_HALF // LRU_BLOCK
    halves = []
    for hb in range(D_LRU // GATE_HALF):
        blocks = []
        for w in (w_r, w_i):
            mat = jnp.zeros((GATE_HALF, GATE_HALF), F32)
            for n in range(per_half):
                lo = n * LRU_BLOCK
                mat = mat.at[lo:lo + LRU_BLOCK, lo:lo + LRU_BLOCK].set(w[hb * per_half + n])
            blocks.append(mat)
        halves.append(jnp.concatenate(blocks, axis=1))
    return jnp.stack(halves).astype(BF16)


def _rope_tables(pos):
    half = DK // 2
    inv = ROPE_BASE ** (-jnp.arange(half, dtype=F32) / half)
    ang = pos[:, None] * inv[None, :]
    cos = jnp.cos(ang)
    sin = jnp.sin(ang)
    return jnp.concatenate([cos, cos], axis=-1), jnp.concatenate([-sin, sin], axis=-1)


def _decay_tables(c):
    log_g = jnp.log1p(-jnp.exp2(-5.0 - jnp.arange(N_HEADS, dtype=F32)))
    idx = jnp.arange(c, dtype=F32)
    diff = idx[:, None] - idx[None, :]
    mask = jnp.where(diff[None] >= 0, jnp.exp(jnp.maximum(diff, 0.0)[None] * log_g[:, None, None]), 0.0)
    w_state = jnp.exp((c - 1.0 - idx)[None, :] * log_g[:, None])
    cross_decay = jnp.exp((idx + 1.0)[:, None] * log_g[None, :])
    chunk_decay = jnp.exp(c * log_g)
    wstate_full = jnp.repeat(w_state.T, DK, axis=1)
    cross_full = jnp.repeat(cross_decay, DV, axis=1)
    cdecay_full = jnp.repeat(chunk_decay, DV)[None, :]
    return mask, wstate_full, cross_full, cdecay_full


def kernel(x_prompt, x_sample, state_conv, state_lru, state_ret, c_prompt, c_sample, w_mod, b_mod, norm_pre_mix, norm_post_mix, norm_pre_ffn, norm_post_ffn, w_in, conv_w, conv_b, w_rgate, b_rgate, w_igate, b_igate, lru_lambda, ret_gn_w, w_out, w_router_group, b_router_group, w_router_expert, b_router_expert, w_exp_gate, w_exp_up, w_exp_down):
    bp, tp, _ = x_prompt.shape
    bs, ts, _ = x_sample.shape
    l = 0

    mod = _mod_call(jnp.concatenate([c_prompt, c_sample], axis=0), w_mod[l], b_mod[l][None, :])
    mod_p = mod[:bp][:, None, :]
    mod_s = mod[bp:][:, None, :]

    w_router = jnp.zeros((D_MODEL, LANES), F32)
    w_router = w_router.at[:, :N_EXPERTS].set(w_router_expert[l])
    w_router = w_router.at[:, N_EXPERTS:N_EXPERTS + N_GROUPS].set(w_router_group[l]).astype(BF16)
    b_router = jnp.zeros((1, LANES), F32)
    b_router = b_router.at[0, :N_EXPERTS].set(b_router_expert[l])
    b_router = b_router.at[0, N_EXPERTS:N_EXPERTS + N_GROUPS].set(b_router_group[l])

    row = lambda vec: vec.reshape(1, -1)
    wts = (row(norm_pre_mix[l]), row(norm_post_mix[l]), row(norm_pre_ffn[l]),
           w_in[l].astype(BF16), conv_w[l], row(conv_b[l]),
           _block_diag_gate(w_rgate[l], w_igate[l]),
           row(b_rgate[l]), row(b_igate[l]), row(lru_lambda[l]), row(ret_gn_w[l]),
           w_out[l].astype(BF16), w_router, b_router)

    cos_p, sin_p = _rope_tables(jnp.arange(tp, dtype=F32))
    x1_p, u2_p, gates_p, conv_p8, lru_p8, ret_p = _prompt_mixer_call(
        x_prompt, mod_p, cos_p, sin_p, wts, _decay_tables(math.gcd(tp, RET_CHUNK)))

    cos_s, sin_s = _rope_tables(jnp.float32(PAST_LEN) + jnp.arange(ts, dtype=F32))
    mask8, wstate_s, cross_s, cdecay_s = _decay_tables(math.gcd(ts, RET_CHUNK))
    eye = jnp.eye(SAMPLE_SEQS, dtype=F32)
    smask = jnp.stack([jnp.kron(eye, mask8[h]) for h in range(N_HEADS)])
    buf8 = jnp.pad(state_conv[l], ((0, 0), (0, ts - (CONV_W - 1)), (0, 0)))
    h0p = jnp.pad(state_lru[l][:, None, :], ((0, 0), (0, ts - 1), (0, 0)))
    x1_s, u2_s, gates_s, xr_s, h_s, ret_s = _sample_mixer_call(
        x_sample, mod_s, cos_s, sin_s, buf8, h0p, state_ret[l], wts,
        (smask, wstate_s, cross_s, cdecay_s))

    n_post2 = row(norm_post_ffn[l])
    wg = w_exp_gate[l].astype(BF16)
    wu = w_exp_up[l].astype(BF16)
    wd = w_exp_down[l].astype(BF16)
    y_p = _moe_call(x1_p, u2_p, gates_p, mod_p, n_post2, wg, wu, wd, 1)
    y_s = _moe_call(x1_s, u2_s, gates_s, mod_s, n_post2, wg, wu, wd, MOE_TILE // ts)

    conv_p = conv_p8[:, SUBLANES - (CONV_W - 1):, :]
    lru_p = lru_p8[:, SUBLANES - 1, :]
    xr_s3 = xr_s.reshape(bs, ts, D_LRU)
    conv_s = xr_s3[:, ts - (CONV_W - 1):, :]
    lru_s = h_s.reshape(bs, ts, D_LRU)[:, ts - 1, :]
    return (y_p.reshape(bp, tp, D_MODEL), y_s.reshape(bs, ts, D_MODEL),
            conv_p[None], lru_p[None], ret_p[None],
            conv_s[None], lru_s[None], ret_s[None])
```

```python
import functools
import math

import jax
import jax.numpy as jnp
from jax import lax
from jax.experimental import pallas as pl
from jax.experimental.pallas import tpu as pltpu

F32 = jnp.float32
BF16 = jnp.bfloat16

D_MODEL = 1024
D_LRU = 512
D_RET = 512
N_LRU_BLOCKS = 8
LRU_BLOCK = D_LRU // N_LRU_BLOCKS
CONV_W = 4
LRU_C = 8.0
N_HEADS = 4
DK = 128
DV = 128
RET_CHUNK = 128
ROPE_BASE = 10000.0
D_IN_PROJ = 3072
N_GROUPS = 4
PER_GROUP = 4
N_EXPERTS = 16
D_EXPERT = 256
NORM_EPS = 1e-6
GN_EPS = 1e-5
PAST_LEN = 16384

SUBLANES = 8
LANES = 128
GATE_HALF = 256
VMEM_LIMIT = 56 * 1024 * 1024

PROMPT_TILE = 256
SAMPLE_SEQS = 16
MOE_TILE = 256
FINAL_TILE = 256
DMA_LAG = 64
DMA_GROUP = 8

N_PAIRS = 6
N_CLASSES = N_GROUPS * N_PAIRS
ROW_GROUPS = D_MODEL // LANES


def _silu(x):
    return x * jax.nn.sigmoid(x)


def _rms_scale(x):
    return lax.rsqrt(jnp.mean(x * x, axis=-1, keepdims=True) + NORM_EPS)


def _masked_softmax(logits, mask):
    top = jnp.max(jnp.where(mask, logits, -jnp.inf), axis=-1, keepdims=True)
    e = jnp.where(mask, jnp.exp(logits - top), 0.0)
    return e / jnp.sum(e, axis=-1, keepdims=True)


def _mod_kernel(c_ref, w_ref, b_ref, o_ref):
    s = _silu(c_ref[...]).astype(BF16)
    o_ref[...] = jnp.dot(s, w_ref[...].astype(BF16), preferred_element_type=F32) + b_ref[...]


def _mod_call(c_all, w_mod, b_mod):
    rows = c_all.shape[0]
    ncol = w_mod.shape[1]
    blk = 1024
    return pl.pallas_call(
        _mod_kernel,
        grid=(ncol // blk,),
        in_specs=[
            pl.BlockSpec((rows, D_MODEL), lambda j: (0, 0)),
            pl.BlockSpec((D_MODEL, blk), lambda j: (0, j)),
            pl.BlockSpec((1, blk), lambda j: (0, j)),
        ],
        out_specs=pl.BlockSpec((rows, blk), lambda j: (0, j)),
        out_shape=jax.ShapeDtypeStruct((rows, ncol), F32),
        compiler_params=pltpu.CompilerParams(
            dimension_semantics=("arbitrary",), vmem_limit_bytes=VMEM_LIMIT),
        name="mod",
    )(c_all, w_mod, b_mod)


def _in_proj(x3, mod3, n_pre1_ref, w_in_ref):
    bb, tt, _ = x3.shape
    sh1 = mod3[:, :, 0:D_MODEL]
    sc1 = mod3[:, :, D_MODEL:2 * D_MODEL]
    n1 = n_pre1_ref[...].reshape(1, 1, D_MODEL)
    u = x3 * _rms_scale(x3) * n1 * (1.0 + sc1) + sh1
    u2d = u.reshape(bb * tt, D_MODEL).astype(BF16)
    return jnp.dot(u2d, w_in_ref[...], preferred_element_type=F32)


def _lru_coeffs(xc, wg_ref, b_r_ref, b_i_ref, lam_ref):
    xcb = xc.astype(BF16)
    g0 = jnp.dot(xcb[:, :GATE_HALF], wg_ref[0], preferred_element_type=F32)
    g1 = jnp.dot(xcb[:, GATE_HALF:], wg_ref[1], preferred_element_type=F32)
    r = jax.nn.sigmoid(jnp.concatenate([g0[:, :GATE_HALF], g1[:, :GATE_HALF]], axis=1) + b_r_ref[...])
    i = jax.nn.sigmoid(jnp.concatenate([g0[:, GATE_HALF:], g1[:, GATE_HALF:]], axis=1) + b_i_ref[...])
    lam = lam_ref[...]
    sp = jnp.maximum(-lam, 0.0) + jnp.log1p(jnp.exp(-jnp.abs(lam)))
    log_a = -LRU_C * r * sp
    a = jnp.exp(log_a)
    gain = jnp.sqrt(-jnp.tanh(log_a) * (1.0 + a * a))
    return a, gain * (i * xc)


def _rope(xh, cos2, sin2, lane_axis):
    return xh * cos2 + pltpu.roll(xh, DK // 2, axis=lane_axis) * sin2


def _group_norm(o):
    mu = jnp.mean(o, axis=-1, keepdims=True)
    d = o - mu
    var = jnp.mean(d * d, axis=-1, keepdims=True)
    return d * lax.rsqrt(var + GN_EPS)


def _post_mixer(x3, mod3, out_a, out_b, w_out_ref, n_post1_ref, n_pre2_ref, w_router_ref, b_router_ref,
                x1_ref, pk_ref, route_ref, cnt_scr):
    bb, tt, _ = x3.shape
    m = bb * tt
    y = (jnp.dot(out_a.astype(BF16), w_out_ref[0:D_LRU, :], preferred_element_type=F32)
         + jnp.dot(out_b.astype(BF16), w_out_ref[D_LRU:, :], preferred_element_type=F32))
    g1 = mod3[:, :, 2 * D_MODEL:3 * D_MODEL]
    sh2 = mod3[:, :, 3 * D_MODEL:4 * D_MODEL]
    sc2 = mod3[:, :, 4 * D_MODEL:5 * D_MODEL]
    y3 = y.reshape(bb, tt, D_MODEL)
    x1 = x3 + g1 * (y3 * _rms_scale(y3) * n_post1_ref[...].reshape(1, 1, D_MODEL))
    u2 = x1 * _rms_scale(x1) * n_pre2_ref[...].reshape(1, 1, D_MODEL) * (1.0 + sc2) + sh2
    x1_ref[...] = x1.reshape(m, D_MODEL)
    u2f = u2.reshape(m, D_MODEL)
    for j in range(ROW_GROUPS):
        pk_ref[pl.ds(j, m, stride=SUBLANES), :] = u2f[:, j * LANES:(j + 1) * LANES]
    u2b = u2f.astype(BF16)

    logits = jnp.dot(u2b, w_router_ref[...], preferred_element_type=F32) + b_router_ref[...]
    lane = lax.broadcasted_iota(jnp.int32, (m, LANES), 1)
    is_g = (lane >= N_EXPERTS) & (lane < N_EXPERTS + N_GROUPS)
    p_group = _masked_softmax(logits, is_g)
    p_g = jnp.max(p_group, axis=-1, keepdims=True)
    g_lane = jnp.min(jnp.where(is_g & (p_group == p_g), lane, LANES), axis=-1, keepdims=True)
    e_lo = (g_lane - N_EXPERTS) * PER_GROUP
    in_g = (lane >= e_lo) & (lane < e_lo + PER_GROUP)
    p_e = _masked_softmax(logits, in_g)
    pm = jnp.where(in_g, p_e, -1.0)
    w1 = jnp.max(pm, axis=-1, keepdims=True)
    i1 = jnp.min(jnp.where(pm == w1, lane, LANES), axis=-1, keepdims=True)
    pm2 = jnp.where(lane == i1, -1.0, pm)
    w2 = jnp.max(pm2, axis=-1, keepdims=True)
    i2 = jnp.min(jnp.where(pm2 == w2, lane, LANES), axis=-1, keepdims=True)
    a = jnp.minimum(i1, i2) - e_lo
    b = jnp.maximum(i1, i2) - e_lo
    pair = jnp.where(a == 0, b - 1, jnp.where(a == 1, b + 1, 5))
    cls = (g_lane - N_EXPERTS) * N_PAIRS + pair
    onehot = lane == cls
    r_i = lax.broadcasted_iota(jnp.int32, (m, m), 0)
    c_i = lax.broadcasted_iota(jnp.int32, (m, m), 1)
    earlier = jnp.where(r_i > c_i, 1.0, 0.0).astype(BF16)
    prefix = jnp.dot(earlier, jnp.where(onehot, 1.0, 0.0).astype(BF16), preferred_element_type=F32)
    run = cnt_scr[0:1, :]
    rank = jnp.sum(jnp.where(onehot, prefix + run, 0.0), axis=-1, keepdims=True)
    cnt_scr[...] = jnp.broadcast_to(
        run + jnp.sum(jnp.where(onehot, 1.0, 0.0), axis=0, keepdims=True), cnt_scr.shape)
    route_ref[...] = jnp.where(lane == 0, cls.astype(F32), jnp.where(lane == 1, rank, 0.0))


def _scan_rows(a, b):
    n = a.shape[0]
    row = lax.broadcasted_iota(jnp.int32, a.shape, 0)
    s = 1
    while s < n:
        keep = row >= s
        a_sh = jnp.where(keep, pltpu.roll(a, s, axis=0), 1.0)
        b_sh = jnp.where(keep, pltpu.roll(b, s, axis=0), 0.0)
        b = a * b_sh + b
        a = a * a_sh
        s *= 2
    return a, b


def _prompt_mixer_kernel(x_ref, mod_ref, cos_ref, sin_ref,
                         n_pre1_ref, n_post1_ref, n_pre2_ref,
                         w_in_ref, conv_w_ref, conv_b_ref, wg_ref, b_r_ref, b_i_ref, lam_ref,
                         gn_w_ref, w_out_ref, w_router_ref, b_router_ref,
                         mask_ref, wstate_ref, cross_ref, cdecay_ref,
                         x1_ref, pk_ref, route_ref, conv_out_ref, lru_out_ref, ret_out_ref, cnt_out_ref,
                         conv_scr, h_scr, s_scr, cnt_scr):
    t = pl.program_id(1)
    tt = x_ref.shape[1]

    @pl.when((pl.program_id(0) == 0) & (t == 0))
    def _():
        cnt_scr[...] = jnp.zeros_like(cnt_scr)

    @pl.when(t == 0)
    def _():
        conv_scr[...] = jnp.zeros_like(conv_scr)
        h_scr[...] = jnp.zeros_like(h_scr)
        s_scr[...] = jnp.zeros_like(s_scr)

    x3 = x_ref[...]
    mod3 = mod_ref[...]
    z = _in_proj(x3, mod3, n_pre1_ref, w_in_ref)
    xr = z[:, 0:D_LRU]
    yg = z[:, D_LRU:2 * D_LRU]
    q = z[:, 1024:1536]
    k = z[:, 1536:2048]
    v = z[:, 2048:2560]
    g = z[:, 2560:3072]

    xx = jnp.concatenate([conv_scr[...], xr], axis=0)
    xc = jnp.broadcast_to(conv_b_ref[...], (tt, D_LRU))
    for j in range(CONV_W):
        off = SUBLANES - (CONV_W - 1) + j
        xc = xc + xx[off:off + tt, :] * conv_w_ref[j:j + 1, :]
    conv_scr[...] = xr[tt - SUBLANES:, :]

    a, b = _lru_coeffs(xc, wg_ref, b_r_ref, b_i_ref, lam_ref)
    a_cum, b_cum = _scan_rows(a, b)
    hseq = b_cum + a_cum * h_scr[0:1, :]
    h_scr[...] = jnp.broadcast_to(hseq[tt - 1:tt, :], h_scr.shape)
    out_a = hseq * jax.nn.gelu(yg, approximate=True)

    cos2 = cos_ref[...]
    sin2 = sin_ref[...]
    scale = DK ** -0.5
    o_heads = []
    for h in range(N_HEADS):
        hs = slice(h * DK, (h + 1) * DK)
        qh = (_rope(q[:, hs], cos2, sin2, 1) * scale).astype(BF16)
        kh = _rope(k[:, hs], cos2, sin2, 1)
        vh = v[:, hs].astype(BF16)
        o_chunks = []
        for c in range(tt // RET_CHUNK):
            cs = slice(c * RET_CHUNK, (c + 1) * RET_CHUNK)
            qc = qh[cs]
            kc = kh[cs]
            vc = vh[cs]
            s_prev = s_scr[h]
            scores = lax.dot_general(qc, kc.astype(BF16), (((1,), (1,)), ((), ())),
                                     preferred_element_type=F32) * mask_ref[h]
            inner = jnp.dot(scores.astype(BF16), vc, preferred_element_type=F32)
            cross = jnp.dot(qc, s_prev.astype(BF16), preferred_element_type=F32) * cross_ref[:, hs]
            kw = (kc * wstate_ref[:, hs]).astype(BF16)
            kv = lax.dot_general(kw, vc, (((0,), (0,)), ((), ())), preferred_element_type=F32)
            s_scr[h] = cdecay_ref[:, hs] * s_prev + kv
            o_chunks.append(inner + cross)
        o_heads.append(_group_norm(jnp.concatenate(o_chunks, axis=0)))
    o = jnp.concatenate(o_heads, axis=1)
    out_b = o * gn_w_ref[...] * _silu(g)

    _post_mixer(x3, mod3, out_a, out_b, w_out_ref, n_post1_ref, n_pre2_ref, w_router_ref, b_router_ref,
                x1_ref, pk_ref, route_ref, cnt_scr)
    cnt_out_ref[...] = cnt_scr[...]

    @pl.when(t == pl.num_programs(1) - 1)
    def _():
        conv_out_ref[0] = xr[tt - SUBLANES:, :]
        lru_out_ref[0] = hseq[tt - SUBLANES:, :]
        ret_out_ref[0] = s_scr[...]


def _const_spec(shape):
    nd = len(shape)
    return pl.BlockSpec(shape, lambda *_: (0,) * nd)


def _prompt_mixer_call(x, mod3, cos2, sin2, wts, tables):
    bsz, seq, _ = x.shape
    tt = PROMPT_TILE
    nt = seq // tt
    n_tok = bsz * seq
    (n_pre1, n_post1, n_pre2, w_in, conv_w, conv_b, wg, b_r, b_i, lam, gn_w, w_out, w_router, b_router) = wts
    mask, wstate, cross, cdecay = tables
    tok_spec = pl.BlockSpec((tt, D_MODEL), lambda b, t: (b * nt + t, 0))
    in_specs = [
        pl.BlockSpec((1, tt, D_MODEL), lambda b, t: (b, t, 0)),
        pl.BlockSpec((1, 1, 6 * D_MODEL), lambda b, t: (b, 0, 0)),
        pl.BlockSpec((tt, LANES), lambda b, t: (t, 0)),
        pl.BlockSpec((tt, LANES), lambda b, t: (t, 0)),
    ] + [_const_spec(w.shape) for w in wts] + [_const_spec(tb.shape) for tb in tables]
    out_specs = [
        tok_spec,
        pl.BlockSpec((tt * SUBLANES, LANES), lambda b, t: (b * nt + t, 0)),
        pl.BlockSpec((tt, LANES), lambda b, t: (b * nt + t, 0)),
        pl.BlockSpec((1, SUBLANES, D_LRU), lambda b, t: (b, 0, 0)),
        pl.BlockSpec((1, SUBLANES, D_LRU), lambda b, t: (b, 0, 0)),
        pl.BlockSpec((1, N_HEADS, DK, DV), lambda b, t: (b, 0, 0, 0)),
        pl.BlockSpec((SUBLANES, LANES), lambda b, t: (0, 0)),
    ]
    out_shape = [
        jax.ShapeDtypeStruct((n_tok, D_MODEL), F32),
        jax.ShapeDtypeStruct((n_tok * SUBLANES, LANES), F32),
        jax.ShapeDtypeStruct((n_tok, LANES), F32),
        jax.ShapeDtypeStruct((bsz, SUBLANES, D_LRU), F32),
        jax.ShapeDtypeStruct((bsz, SUBLANES, D_LRU), F32),
        jax.ShapeDtypeStruct((bsz, N_HEADS, DK, DV), F32),
        jax.ShapeDtypeStruct((SUBLANES, LANES), F32),
    ]
    return pl.pallas_call(
        _prompt_mixer_kernel,
        grid=(bsz, nt),
        in_specs=in_specs,
        out_specs=out_specs,
        out_shape=out_shape,
        scratch_shapes=[
            pltpu.VMEM((SUBLANES, D_LRU), F32),
            pltpu.VMEM((SUBLANES, D_LRU), F32),
            pltpu.VMEM((N_HEADS, DK, DV), F32),
            pltpu.VMEM((SUBLANES, LANES), F32),
        ],
        compiler_params=pltpu.CompilerParams(
            dimension_semantics=("arbitrary", "arbitrary"), vmem_limit_bytes=VMEM_LIMIT),
        name="prompt_mixer",
    )(x, mod3, cos2, sin2, *wts, *tables)


def _sample_mixer_kernel(x_ref, mod_ref, cos_ref, sin_ref, buf_ref, h0_ref, s0_ref, cnt_in_ref,
                         n_pre1_ref, n_post1_ref, n_pre2_ref,
                         w_in_ref, conv_w_ref, conv_b_ref, wg_ref, b_r_ref, b_i_ref, lam_ref,
                         gn_w_ref, w_out_ref, w_router_ref, b_router_ref,
                         smask_ref, wstate_ref, cross_ref, cdecay_ref,
                         x1_ref, pk_ref, route_ref, xr_out_ref, h_out_ref, ret_out_ref, cnt_out_ref,
                         cnt_scr):
    bb, ts, _ = x_ref.shape
    m = bb * ts

    @pl.when(pl.program_id(0) == 0)
    def _():
        cnt_scr[...] = cnt_in_ref[...]

    x3 = x_ref[...]
    mod3 = mod_ref[...]
    z = _in_proj(x3, mod3, n_pre1_ref, w_in_ref)
    xr = z[:, 0:D_LRU]
    yg = z[:, D_LRU:2 * D_LRU]
    q = z[:, 1024:1536]
    k = z[:, 1536:2048]
    v = z[:, 2048:2560]
    g = z[:, 2560:3072]
    xr_out_ref[...] = xr

    xr3 = xr.reshape(bb, ts, D_LRU)
    buf3 = buf_ref[...]
    tpos = lax.broadcasted_iota(jnp.int32, (bb, ts, D_LRU), 1)
    xc3 = jnp.broadcast_to(conv_b_ref[...].reshape(1, 1, D_LRU), (bb, ts, D_LRU))
    for j in range(CONV_W):
        back = CONV_W - 1 - j
        w_j = conv_w_ref[j:j + 1, :].reshape(1, 1, D_LRU)
        if back == 0:
            term = xr3
        else:
            cur = pltpu.roll(xr3, back, axis=1)
            up = CONV_W - 1 - back
            old = buf3 if up == 0 else pltpu.roll(buf3, ts - up, axis=1)
            term = jnp.where(tpos >= back, cur, old)
        xc3 = xc3 + term * w_j
    xc = xc3.reshape(m, D_LRU)

    a, b = _lru_coeffs(xc, wg_ref, b_r_ref, b_i_ref, lam_ref)
    a3 = a.reshape(bb, ts, D_LRU)
    b3 = b.reshape(bb, ts, D_LRU) + a3 * h0_ref[...]
    s = 1
    while s < ts:
        keep = tpos >= s
        a_sh = jnp.where(keep, pltpu.roll(a3, s, axis=1), 1.0)
        b_sh = jnp.where(keep, pltpu.roll(b3, s, axis=1), 0.0)
        b3 = a3 * b_sh + b3
        a3 = a3 * a_sh
        s *= 2
    hseq = b3.reshape(m, D_LRU)
    h_out_ref[...] = hseq
    out_a = hseq * jax.nn.gelu(yg, approximate=True)

    cos2 = cos_ref[...].reshape(1, ts, LANES)
    sin2 = sin_ref[...].reshape(1, ts, LANES)
    scale = DK ** -0.5
    o_heads = []
    for h in range(N_HEADS):
        hs = slice(h * DK, (h + 1) * DK)
        q3 = (_rope(q[:, hs].reshape(bb, ts, DK), cos2, sin2, 2) * scale).astype(BF16)
        k3 = _rope(k[:, hs].reshape(bb, ts, DK), cos2, sin2, 2)
        v3 = v[:, hs].reshape(bb, ts, DV).astype(BF16)
        q2 = q3.reshape(m, DK)
        k2 = k3.reshape(m, DK).astype(BF16)
        v2 = v3.reshape(m, DV)
        scores = lax.dot_general(q2, k2, (((1,), (1,)), ((), ())),
                                 preferred_element_type=F32) * smask_ref[h]
        inner = jnp.dot(scores.astype(BF16), v2, preferred_element_type=F32)
        s0h = s0_ref[:, h]
        cross = jnp.einsum('bid,bde->bie', q3, s0h.astype(BF16), preferred_element_type=F32)
        cross = cross * cross_ref[:, hs].reshape(1, ts, DV)
        kw3 = (k3 * wstate_ref[:, hs].reshape(1, ts, DK)).astype(BF16)
        kv = jnp.einsum('bjd,bje->bde', kw3, v3, preferred_element_type=F32)
        ret_out_ref[:, h] = cdecay_ref[:, hs].reshape(1, 1, DV) * s0h + kv
        o_heads.append(_group_norm(inner + cross.reshape(m, DV)))
    o = jnp.concatenate(o_heads, axis=1)
    out_b = o * gn_w_ref[...] * _silu(g)

    _post_mixer(x3, mod3, out_a, out_b, w_out_ref, n_post1_ref, n_pre2_ref, w_router_ref, b_router_ref,
                x1_ref, pk_ref, route_ref, cnt_scr)
    cnt_out_ref[...] = cnt_scr[...]


def _sample_mixer_call(x, mod3, cos2, sin2, buf8, h0p, s0, cnt_in, wts, tables):
    bsz, ts, _ = x.shape
    bb = SAMPLE_SEQS
    m = bb * ts
    n_tok = bsz * ts
    seq_spec = lambda w: pl.BlockSpec((bb, ts, w), lambda i: (i, 0, 0))
    tok_spec = lambda w: pl.BlockSpec((m, w), lambda i: (i, 0))
    in_specs = [
        seq_spec(D_MODEL),
        pl.BlockSpec((bb, 1, 6 * D_MODEL), lambda i: (i, 0, 0)),
        _const_spec(cos2.shape),
        _const_spec(sin2.shape),
        seq_spec(D_LRU),
        seq_spec(D_LRU),
        pl.BlockSpec((bb, N_HEADS, DK, DV), lambda i: (i, 0, 0, 0)),
        _const_spec(cnt_in.shape),
    ] + [_const_spec(w.shape) for w in wts] + [_const_spec(tb.shape) for tb in tables]
    out_specs = [
        tok_spec(D_MODEL),
        pl.BlockSpec((m * SUBLANES, LANES), lambda i: (i, 0)),
        tok_spec(LANES),
        tok_spec(D_LRU),
        tok_spec(D_LRU),
        pl.BlockSpec((bb, N_HEADS, DK, DV), lambda i: (i, 0, 0, 0)),
        _const_spec(cnt_in.shape),
    ]
    out_shape = [
        jax.ShapeDtypeStruct((n_tok, D_MODEL), F32),
        jax.ShapeDtypeStruct((n_tok * SUBLANES, LANES), F32),
        jax.ShapeDtypeStruct((n_tok, LANES), F32),
        jax.ShapeDtypeStruct((n_tok, D_LRU), F32),
        jax.ShapeDtypeStruct((n_tok, D_LRU), F32),
        jax.ShapeDtypeStruct((bsz, N_HEADS, DK, DV), F32),
        jax.ShapeDtypeStruct(cnt_in.shape, F32),
    ]
    return pl.pallas_call(
        _sample_mixer_kernel,
        grid=(bsz // bb,),
        in_specs=in_specs,
        out_specs=out_specs,
        out_shape=out_shape,
        scratch_shapes=[pltpu.VMEM((SUBLANES, LANES), F32)],
        compiler_params=pltpu.CompilerParams(
            dimension_semantics=("arbitrary",), vmem_limit_bytes=VMEM_LIMIT),
        name="sample_mixer",
    )(x, mod3, cos2, sin2, buf8, h0p, s0, cnt_in, *wts, *tables)


def _dispatch_kernel(dst_ref, cnt_ref, padcnt_ref, start_ref, srcp_ref, srcs_ref, out_ref, zrow_ref, sem):
    n_p = srcp_ref.shape[0] // SUBLANES
    n_s = srcs_ref.shape[0] // SUBLANES

    def slab(ref, i):
        return ref.at[pl.ds(pl.multiple_of(i * SUBLANES, SUBLANES), SUBLANES)]

    def row_copy(src_ref, i, d):
        return pltpu.make_async_copy(slab(src_ref, i), slab(out_ref, d), sem)

    def wait_rows(k):
        pltpu.make_async_copy(srcp_ref.at[pl.ds(0, k * SUBLANES)], out_ref.at[pl.ds(0, k * SUBLANES)], sem).wait()

    def stream(src_ref, n, base):
        groups = n // DMA_GROUP
        lag = DMA_LAG // DMA_GROUP

        def body(g, carry):
            first = g * DMA_GROUP
            slots = [dst_ref[base + first + j] for j in range(DMA_GROUP)]
            for j in range(DMA_GROUP):
                row_copy(src_ref, first + j, slots[j]).start()

            @pl.when(g >= lag)
            def _():
                wait_rows(DMA_GROUP)
            return carry
        lax.fori_loop(0, groups, body, 0)

        def drain(g, carry):
            wait_rows(DMA_GROUP)
            return carry
        lax.fori_loop(0, min(groups, lag), drain, 0)

    stream(srcp_ref, n_p, 0)
    stream(srcs_ref, n_s, n_p)

    zrow_ref[...] = jnp.zeros_like(zrow_ref)

    def per_class(c, carry):
        lo = start_ref[c] + cnt_ref[c]
        hi = start_ref[c] + padcnt_ref[c]

        def fill(r, carry2):
            pltpu.make_async_copy(zrow_ref.at[pl.ds(0, SUBLANES)], slab(out_ref, r), sem).start()
            return carry2
        lax.fori_loop(lo, hi, fill, 0)

        def done(r, carry2):
            wait_rows(1)
            return carry2
        lax.fori_loop(lo, hi, done, 0)
        return carry
    lax.fori_loop(0, N_CLASSES, per_class, 0)

    tile_rows = zrow_ref.shape[0]
    tm = tile_rows // SUBLANES
    used_tiles = (start_ref[N_CLASSES - 1] + padcnt_ref[N_CLASSES - 1]) // tm
    all_tiles = out_ref.shape[0] // tile_rows

    def tile_copy(t):
        return pltpu.make_async_copy(
            zrow_ref, out_ref.at[pl.ds(pl.multiple_of(t * tile_rows, tile_rows), tile_rows)], sem)

    def fill_tile(t, carry):
        tile_copy(t).start()
        return carry
    lax.fori_loop(used_tiles, all_tiles, fill_tile, 0)

    def done_tile(t, carry):
        tile_copy(t).wait()
        return carry
    lax.fori_loop(used_tiles, all_tiles, done_tile, 0)


def _dispatch_call(dst, cnt, padcnt, start, pk_p, pk_s, n_rows):
    return pl.pallas_call(
        _dispatch_kernel,
        grid_spec=pltpu.PrefetchScalarGridSpec(
            num_scalar_prefetch=4,
            grid=(1,),
            in_specs=[pl.BlockSpec(memory_space=pl.ANY), pl.BlockSpec(memory_space=pl.ANY)],
            out_specs=pl.BlockSpec(memory_space=pl.ANY),
            scratch_shapes=[pltpu.VMEM((MOE_TILE * SUBLANES, LANES), F32),
                            pltpu.SemaphoreType.DMA(())],
        ),
        out_shape=jax.ShapeDtypeStruct((n_rows * SUBLANES, LANES), F32),
        compiler_params=pltpu.CompilerParams(
            dimension_semantics=("arbitrary",), has_side_effects=True),
        name="moe_dispatch",
    )(dst, cnt, padcnt, start, pk_p, pk_s)


def _moe_kernel(ea_ref, eb_ref, valid_ref, xs_ref, w_router_ref, b_router_ref,
                wga_ref, wua_ref, wda_ref, wgb_ref, wub_ref, wdb_ref, f_ref):
    t = pl.program_id(0)

    @pl.when(valid_ref[t] == 1)
    def _():
        tm = xs_ref.shape[0] // SUBLANES
        x = jnp.concatenate([xs_ref[pl.ds(j, tm, stride=SUBLANES), :] for j in range(ROW_GROUPS)],
                            axis=1).astype(BF16)
        e_a = ea_ref[t]
        e_b = eb_ref[t]
        e_lo = (e_a // PER_GROUP) * PER_GROUP
        logits = jnp.dot(x, w_router_ref[...], preferred_element_type=F32) + b_router_ref[...]
        lane = lax.broadcasted_iota(jnp.int32, (tm, LANES), 1)
        pick = lambda p, idx: jnp.sum(jnp.where(lane == idx, p, 0.0), axis=-1, keepdims=True)
        p_group = _masked_softmax(logits, (lane >= N_EXPERTS) & (lane < N_EXPERTS + N_GROUPS))
        p_g = pick(p_group, N_EXPERTS + e_a // PER_GROUP)
        p_e = _masked_softmax(logits, (lane >= e_lo) & (lane < e_lo + PER_GROUP))
        w_a = pick(p_e, e_a)
        w_b = pick(p_e, e_b)
        wsum = w_a + w_b

        def expert(wg_ref, wu_ref, gate):
            hg = jnp.dot(x, wg_ref[0], preferred_element_type=F32)
            hu = jnp.dot(x, wu_ref[0], preferred_element_type=F32)
            return (_silu(hg) * hu * gate).astype(BF16)

        ha = expert(wga_ref, wua_ref, p_g * (w_a / wsum))
        hb = expert(wgb_ref, wub_ref, p_g * (w_b / wsum))
        f = (jnp.dot(ha, wda_ref[0], preferred_element_type=F32)
             + jnp.dot(hb, wdb_ref[0], preferred_element_type=F32))
        for j in range(D_MODEL // LANES):
            f_ref[pl.ds(j, tm, stride=SUBLANES), :] = f[:, j * LANES:(j + 1) * LANES]

    @pl.when(valid_ref[t] == 0)
    def _():
        f_ref[...] = jnp.zeros_like(f_ref)


def _moe_call(tile_ea, tile_eb, tile_valid, rows, w_router, b_router, wg, wu, wd):
    n_rows = rows.shape[0] // SUBLANES
    tm = MOE_TILE
    const = lambda a: pl.BlockSpec(a.shape, lambda t, ea, eb, v: (0,) * a.ndim)
    up = lambda sel: pl.BlockSpec((1, D_MODEL, D_EXPERT), lambda t, ea, eb, v: (sel(ea, eb)[t], 0, 0))
    down = lambda sel: pl.BlockSpec((1, D_EXPERT, D_MODEL), lambda t, ea, eb, v: (sel(ea, eb)[t], 0, 0))
    first = lambda ea, eb: ea
    second = lambda ea, eb: eb
    return pl.pallas_call(
        _moe_kernel,
        grid_spec=pltpu.PrefetchScalarGridSpec(
            num_scalar_prefetch=3,
            grid=(n_rows // tm,),
            in_specs=[
                pl.BlockSpec((tm * SUBLANES, LANES), lambda t, ea, eb, v: (t, 0)),
                const(w_router), const(b_router),
                up(first), up(first), down(first), up(second), up(second), down(second),
            ],
            out_specs=pl.BlockSpec((tm * SUBLANES, LANES), lambda t, ea, eb, v: (t, 0)),
        ),
        out_shape=jax.ShapeDtypeStruct((n_rows * SUBLANES, LANES), F32),
        compiler_params=pltpu.CompilerParams(
            dimension_semantics=("arbitrary",), vmem_limit_bytes=VMEM_LIMIT),
        name="moe_experts",
    )(tile_ea, tile_eb, tile_valid, rows, w_router, b_router, wg, wu, wd, wg, wu, wd)


def _combine_kernel(dst_ref, x1_ref, mod_ref, n_post2_ref, f_hbm, o_ref, fbuf, sem):
    i = pl.program_id(0)
    n = pl.num_programs(0)
    tf = x1_ref.shape[0]

    def slab(ref, r):
        return ref.at[pl.ds(pl.multiple_of(r * SUBLANES, SUBLANES), SUBLANES)]

    def issue(tile, slot):
        def body(g, carry):
            first = g * DMA_GROUP
            slots = [dst_ref[tile * tf + first + j] for j in range(DMA_GROUP)]
            for j in range(DMA_GROUP):
                pltpu.make_async_copy(
                    slab(f_hbm, slots[j]), slab(fbuf.at[slot], first + j), sem.at[slot]).start()
            return carry
        lax.fori_loop(0, tf // DMA_GROUP, body, 0)

    def finish(slot):
        pltpu.make_async_copy(f_hbm.at[pl.ds(0, tf * SUBLANES)], fbuf.at[slot], sem.at[slot]).wait()
        f = jnp.concatenate([fbuf[slot, pl.ds(j, tf, stride=SUBLANES), :] for j in range(D_MODEL // LANES)],
                            axis=1)
        mod3 = mod_ref[...]
        bb = mod3.shape[0]
        f3 = f.reshape(bb, tf // bb, D_MODEL)
        g2 = mod3[:, :, 5 * D_MODEL:6 * D_MODEL]
        x13 = x1_ref[...].reshape(f3.shape)
        out = x13 + g2 * (f3 * _rms_scale(f3) * n_post2_ref[...].reshape(1, 1, D_MODEL))
        o_ref[...] = out.reshape(o_ref.shape)

    @pl.when(i == 0)
    def _():
        issue(0, 0)

    for parity in range(2):
        @pl.when(i % 2 == parity)
        def _():
            @pl.when(i + 1 < n)
            def _():
                issue(i + 1, 1 - parity)
            finish(parity)


def _combine_call(dst, x1, mod3, n_post2, f_sorted, seqs_per_tile):
    n_tok = x1.shape[0]
    tf = FINAL_TILE
    tok_per_seq = n_tok // mod3.shape[0]
    if seqs_per_tile > 1:
        mod_map = lambda i, d: (i, 0, 0)
    else:
        mod_map = lambda i, d: ((i * tf) // tok_per_seq, 0, 0)
    return pl.pallas_call(
        _combine_kernel,
        grid_spec=pltpu.PrefetchScalarGridSpec(
            num_scalar_prefetch=1,
            grid=(n_tok // tf,),
            in_specs=[
                pl.BlockSpec((tf, D_MODEL), lambda i, d: (i, 0)),
                pl.BlockSpec((seqs_per_tile, 1, 6 * D_MODEL), mod_map),
                pl.BlockSpec((1, D_MODEL), lambda i, d: (0, 0)),
                pl.BlockSpec(memory_space=pl.ANY),
            ],
            out_specs=pl.BlockSpec((tf, D_MODEL), lambda i, d: (i, 0)),
            scratch_shapes=[pltpu.VMEM((2, tf * SUBLANES, LANES), F32), pltpu.SemaphoreType.DMA((2,))],
        ),
        out_shape=jax.ShapeDtypeStruct((n_tok, D_MODEL), F32),
        compiler_params=pltpu.CompilerParams(
            dimension_semantics=("arbitrary",), vmem_limit_bytes=VMEM_LIMIT),
        name="moe_combine",
    )(dst, x1, mod3, n_post2, f_sorted)


def _block_diag_gate(w_r, w_i):
    per_half = GATE_HALF // LRU_BLOCK
    halves = []
    for hb in range(D_LRU // GATE_HALF):
        blocks = []
        for w in (w_r, w_i):
            mat = jnp.zeros((GATE_HALF, GATE_HALF), F32)
            for n in range(per_half):
                lo = n * LRU_BLOCK
                mat = mat.at[lo:lo + LRU_BLOCK, lo:lo + LRU_BLOCK].set(w[hb * per_half + n])
            blocks.append(mat)
        halves.append(jnp.concatenate(blocks, axis=1))
    return jnp.stack(halves).astype(BF16)


def _rope_tables(pos):
    half = DK // 2
    inv = ROPE_BASE ** (-jnp.arange(half, dtype=F32) / half)
    ang = pos[:, None] * inv[None, :]
    cos = jnp.cos(ang)
    sin = jnp.sin(ang)
    return jnp.concatenate([cos, cos], axis=-1), jnp.concatenate([-sin, sin], axis=-1)


def _decay_tables(c):
    log_g = jnp.log1p(-jnp.exp2(-5.0 - jnp.arange(N_HEADS, dtype=F32)))
    idx = jnp.arange(c, dtype=F32)
    diff = idx[:, None] - idx[None, :]
    mask = jnp.where(diff[None] >= 0, jnp.exp(jnp.maximum(diff, 0.0)[None] * log_g[:, None, None]), 0.0)
    w_state = jnp.exp((c - 1.0 - idx)[None, :] * log_g[:, None])
    cross_decay = jnp.exp((idx + 1.0)[:, None] * log_g[None, :])
    chunk_decay = jnp.exp(c * log_g)
    wstate_full = jnp.repeat(w_state.T, DK, axis=1)
    cross_full = jnp.repeat(cross_decay, DV, axis=1)
    cdecay_full = jnp.repeat(chunk_decay, DV)[None, :]
    return mask, wstate_full, cross_full, cdecay_full


def kernel(x_prompt, x_sample, state_conv, state_lru, state_ret, c_prompt, c_sample, w_mod, b_mod, norm_pre_mix, norm_post_mix, norm_pre_ffn, norm_post_ffn, w_in, conv_w, conv_b, w_rgate, b_rgate, w_igate, b_igate, lru_lambda, ret_gn_w, w_out, w_router_group, b_router_group, w_router_expert, b_router_expert, w_exp_gate, w_exp_up, w_exp_down):
    bp, tp, _ = x_prompt.shape
    bs, ts, _ = x_sample.shape
    l = 0

    mod = _mod_call(jnp.concatenate([c_prompt, c_sample], axis=0), w_mod[l], b_mod[l][None, :])
    mod_p = mod[:bp][:, None, :]
    mod_s = mod[bp:][:, None, :]

    w_router = jnp.zeros((D_MODEL, LANES), F32)
    w_router = w_router.at[:, :N_EXPERTS].set(w_router_expert[l])
    w_router = w_router.at[:, N_EXPERTS:N_EXPERTS + N_GROUPS].set(w_router_group[l]).astype(BF16)
    b_router = jnp.zeros((1, LANES), F32)
    b_router = b_router.at[0, :N_EXPERTS].set(b_router_expert[l])
    b_router = b_router.at[0, N_EXPERTS:N_EXPERTS + N_GROUPS].set(b_router_group[l])

    row = lambda vec: vec.reshape(1, -1)
    wts = (row(norm_pre_mix[l]), row(norm_post_mix[l]), row(norm_pre_ffn[l]),
           w_in[l].astype(BF16), conv_w[l], row(conv_b[l]),
           _block_diag_gate(w_rgate[l], w_igate[l]),
           row(b_rgate[l]), row(b_igate[l]), row(lru_lambda[l]), row(ret_gn_w[l]),
           w_out[l].astype(BF16), w_router, b_router)

    cos_p, sin_p = _rope_tables(jnp.arange(tp, dtype=F32))
    x1_p, pk_p, route_p, conv_p8, lru_p8, ret_p, cnt_p = _prompt_mixer_call(
        x_prompt, mod_p, cos_p, sin_p, wts, _decay_tables(math.gcd(tp, RET_CHUNK)))

    cos_s, sin_s = _rope_tables(jnp.float32(PAST_LEN) + jnp.arange(ts, dtype=F32))
    mask8, wstate_s, cross_s, cdecay_s = _decay_tables(math.gcd(ts, RET_CHUNK))
    eye = jnp.eye(SAMPLE_SEQS, dtype=F32)
    smask = jnp.stack([jnp.kron(eye, mask8[h]) for h in range(N_HEADS)])
    buf8 = jnp.pad(state_conv[l], ((0, 0), (0, ts - (CONV_W - 1)), (0, 0)))
    h0p = jnp.pad(state_lru[l][:, None, :], ((0, 0), (0, ts - 1), (0, 0)))
    x1_s, pk_s, route_s, xr_s, h_s, ret_s, cnt_all = _sample_mixer_call(
        x_sample, mod_s, cos_s, sin_s, buf8, h0p, state_ret[l], cnt_p, wts,
        (smask, wstate_s, cross_s, cdecay_s))

    n_p = bp * tp
    n_tok = n_p + bs * ts
    tm = MOE_TILE
    max_tiles = n_tok // tm + N_CLASSES
    route = jnp.concatenate([route_p, route_s], axis=0)
    cls = route[:, 0].astype(jnp.int32)
    rank = route[:, 1].astype(jnp.int32)
    cnt = cnt_all[0, :N_CLASSES].astype(jnp.int32)
    ntile = (cnt + (tm - 1)) // tm
    padcnt = ntile * tm
    start = jnp.cumsum(padcnt) - padcnt
    dst = start[cls] + rank
    tile_end = jnp.cumsum(ntile)
    tile_ids = jnp.arange(max_tiles, dtype=jnp.int32)
    tile_valid = (tile_ids < tile_end[-1]).astype(jnp.int32)
    last_used = jnp.minimum(tile_ids, tile_end[-1] - 1)
    tile_cls = jnp.sum((last_used[:, None] >= tile_end[None, :]).astype(jnp.int32), axis=1)
    pair_a = jnp.array([0, 0, 0, 1, 1, 2], jnp.int32)
    pair_b = jnp.array([1, 2, 3, 2, 3, 3], jnp.int32)
    tile_ea = (tile_cls // N_PAIRS) * PER_GROUP + pair_a[tile_cls % N_PAIRS]
    tile_eb = (tile_cls // N_PAIRS) * PER_GROUP + pair_b[tile_cls % N_PAIRS]

    n_post2 = row(norm_post_ffn[l])
    wg = w_exp_gate[l].astype(BF16)
    wu = w_exp_up[l].astype(BF16)
    wd = w_exp_down[l].astype(BF16)
    rows = _dispatch_call(dst, cnt, padcnt, start, pk_p, pk_s, max_tiles * tm)
    f_sorted = _moe_call(tile_ea, tile_eb, tile_valid, rows, w_router, b_router, wg, wu, wd)
    y_p = _combine_call(dst[:n_p], x1_p, mod_p, n_post2, f_sorted, 1)
    y_s = _combine_call(dst[n_p:], x1_s, mod_s, n_post2, f_sorted, FINAL_TILE // ts)

    conv_p = conv_p8[:, SUBLANES - (CONV_W - 1):, :]
    lru_p = lru_p8[:, SUBLANES - 1, :]
    xr_s3 = xr_s.reshape(bs, ts, D_LRU)
    conv_s = xr_s3[:, ts - (CONV_W - 1):, :]
    lru_s = h_s.reshape(bs, ts, D_LRU)[:, ts - 1, :]
    return (y_p.reshape(bp, tp, D_MODEL), y_s.reshape(bs, ts, D_MODEL),
            conv_p[None], lru_p[None], ret_p[None],
            conv_s[None], lru_s[None], ret_s[None])
```

```python
import functools
import math

import jax
import jax.numpy as jnp
from jax import lax
from jax.experimental import pallas as pl
from jax.experimental.pallas import tpu as pltpu

F32 = jnp.float32
BF16 = jnp.bfloat16

D_MODEL = 1024
D_LRU = 512
D_RET = 512
N_LRU_BLOCKS = 8
LRU_BLOCK = D_LRU // N_LRU_BLOCKS
CONV_W = 4
LRU_C = 8.0
N_HEADS = 4
DK = 128
DV = 128
RET_CHUNK = 128
ROPE_BASE = 10000.0
D_IN_PROJ = 3072
N_GROUPS = 4
PER_GROUP = 4
N_EXPERTS = 16
D_EXPERT = 256
NORM_EPS = 1e-6
GN_EPS = 1e-5
PAST_LEN = 16384

SUBLANES = 8
LANES = 128
GATE_HALF = 256
VMEM_LIMIT = 56 * 1024 * 1024

PROMPT_TILE = 256
SAMPLE_SEQS = 16
MOE_TILE = 256
FINAL_TILE = 256
RING = 3
DMA_GROUP = 8

N_PAIRS = 6
N_CLASSES = N_GROUPS * N_PAIRS
ROW_GROUPS = D_MODEL // LANES


def _silu(x):
    return x * jax.nn.sigmoid(x)


def _rms_scale(x):
    return lax.rsqrt(jnp.mean(x * x, axis=-1, keepdims=True) + NORM_EPS)


def _masked_softmax(logits, mask):
    top = jnp.max(jnp.where(mask, logits, -jnp.inf), axis=-1, keepdims=True)
    e = jnp.where(mask, jnp.exp(logits - top), 0.0)
    return e / jnp.sum(e, axis=-1, keepdims=True)


def _mod_kernel(c_ref, w_ref, b_ref, o_ref):
    s = _silu(c_ref[...]).astype(BF16)
    o_ref[...] = jnp.dot(s, w_ref[...].astype(BF16), preferred_element_type=F32) + b_ref[...]


def _mod_call(c_all, w_mod, b_mod):
    rows = c_all.shape[0]
    ncol = w_mod.shape[1]
    blk = 1024
    return pl.pallas_call(
        _mod_kernel,
        grid=(ncol // blk,),
        in_specs=[
            pl.BlockSpec((rows, D_MODEL), lambda j: (0, 0)),
            pl.BlockSpec((D_MODEL, blk), lambda j: (0, j)),
            pl.BlockSpec((1, blk), lambda j: (0, j)),
        ],
        out_specs=pl.BlockSpec((rows, blk), lambda j: (0, j)),
        out_shape=jax.ShapeDtypeStruct((rows, ncol), F32),
        compiler_params=pltpu.CompilerParams(
            dimension_semantics=("arbitrary",), vmem_limit_bytes=VMEM_LIMIT),
        name="mod",
    )(c_all, w_mod, b_mod)


def _in_proj(x3, mod3, n_pre1_ref, w_in_ref):
    bb, tt, _ = x3.shape
    sh1 = mod3[:, :, 0:D_MODEL]
    sc1 = mod3[:, :, D_MODEL:2 * D_MODEL]
    n1 = n_pre1_ref[...].reshape(1, 1, D_MODEL)
    u = x3 * _rms_scale(x3) * n1 * (1.0 + sc1) + sh1
    u2d = u.reshape(bb * tt, D_MODEL).astype(BF16)
    return jnp.dot(u2d, w_in_ref[...], preferred_element_type=F32)


def _lru_coeffs(xc, wg_ref, b_r_ref, b_i_ref, lam_ref):
    xcb = xc.astype(BF16)
    g0 = jnp.dot(xcb[:, :GATE_HALF], wg_ref[0], preferred_element_type=F32)
    g1 = jnp.dot(xcb[:, GATE_HALF:], wg_ref[1], preferred_element_type=F32)
    r = jax.nn.sigmoid(jnp.concatenate([g0[:, :GATE_HALF], g1[:, :GATE_HALF]], axis=1) + b_r_ref[...])
    i = jax.nn.sigmoid(jnp.concatenate([g0[:, GATE_HALF:], g1[:, GATE_HALF:]], axis=1) + b_i_ref[...])
    lam = lam_ref[...]
    sp = jnp.maximum(-lam, 0.0) + jnp.log1p(jnp.exp(-jnp.abs(lam)))
    log_a = -LRU_C * r * sp
    a = jnp.exp(log_a)
    gain = jnp.sqrt(-jnp.tanh(log_a) * (1.0 + a * a))
    return a, gain * (i * xc)


def _rope(xh, cos2, sin2, lane_axis):
    return xh * cos2 + pltpu.roll(xh, DK // 2, axis=lane_axis) * sin2


def _group_norm(o):
    mu = jnp.mean(o, axis=-1, keepdims=True)
    d = o - mu
    var = jnp.mean(d * d, axis=-1, keepdims=True)
    return d * lax.rsqrt(var + GN_EPS)


def _post_mixer(x3, mod3, out_a, out_b, w_out_ref, n_post1_ref, n_pre2_ref, w_router_ref, b_router_ref,
                x1_ref, pk_ref, route_ref, cnt_scr):
    bb, tt, _ = x3.shape
    m = bb * tt
    y = (jnp.dot(out_a.astype(BF16), w_out_ref[0:D_LRU, :], preferred_element_type=F32)
         + jnp.dot(out_b.astype(BF16), w_out_ref[D_LRU:, :], preferred_element_type=F32))
    g1 = mod3[:, :, 2 * D_MODEL:3 * D_MODEL]
    sh2 = mod3[:, :, 3 * D_MODEL:4 * D_MODEL]
    sc2 = mod3[:, :, 4 * D_MODEL:5 * D_MODEL]
    y3 = y.reshape(bb, tt, D_MODEL)
    x1 = x3 + g1 * (y3 * _rms_scale(y3) * n_post1_ref[...].reshape(1, 1, D_MODEL))
    u2 = x1 * _rms_scale(x1) * n_pre2_ref[...].reshape(1, 1, D_MODEL) * (1.0 + sc2) + sh2
    x1_ref[...] = x1.reshape(m, D_MODEL)
    u2f = u2.reshape(m, D_MODEL)
    for j in range(ROW_GROUPS):
        pk_ref[pl.ds(j, m, stride=SUBLANES), :] = u2f[:, j * LANES:(j + 1) * LANES]
    u2b = u2f.astype(BF16)

    logits = jnp.dot(u2b, w_router_ref[...], preferred_element_type=F32) + b_router_ref[...]
    lane = lax.broadcasted_iota(jnp.int32, (m, LANES), 1)
    is_g = (lane >= N_EXPERTS) & (lane < N_EXPERTS + N_GROUPS)
    p_group = _masked_softmax(logits, is_g)
    p_g = jnp.max(p_group, axis=-1, keepdims=True)
    g_lane = jnp.min(jnp.where(is_g & (p_group == p_g), lane, LANES), axis=-1, keepdims=True)
    e_lo = (g_lane - N_EXPERTS) * PER_GROUP
    in_g = (lane >= e_lo) & (lane < e_lo + PER_GROUP)
    p_e = _masked_softmax(logits, in_g)
    pm = jnp.where(in_g, p_e, -1.0)
    w1 = jnp.max(pm, axis=-1, keepdims=True)
    i1 = jnp.min(jnp.where(pm == w1, lane, LANES), axis=-1, keepdims=True)
    pm2 = jnp.where(lane == i1, -1.0, pm)
    w2 = jnp.max(pm2, axis=-1, keepdims=True)
    i2 = jnp.min(jnp.where(pm2 == w2, lane, LANES), axis=-1, keepdims=True)
    a = jnp.minimum(i1, i2) - e_lo
    b = jnp.maximum(i1, i2) - e_lo
    pair = jnp.where(a == 0, b - 1, jnp.where(a == 1, b + 1, 5))
    cls = (g_lane - N_EXPERTS) * N_PAIRS + pair
    onehot = lane == cls
    r_i = lax.broadcasted_iota(jnp.int32, (m, m), 0)
    c_i = lax.broadcasted_iota(jnp.int32, (m, m), 1)
    earlier = jnp.where(r_i > c_i, 1.0, 0.0).astype(BF16)
    prefix = jnp.dot(earlier, jnp.where(onehot, 1.0, 0.0).astype(BF16), preferred_element_type=F32)
    run = cnt_scr[0:1, :]
    rank = jnp.sum(jnp.where(onehot, prefix + run, 0.0), axis=-1, keepdims=True)
    cnt_scr[...] = jnp.broadcast_to(
        run + jnp.sum(jnp.where(onehot, 1.0, 0.0), axis=0, keepdims=True), cnt_scr.shape)
    route = jnp.where(lane == 0, cls.astype(F32), jnp.where(lane == 1, rank, 0.0))
    route_ref[...] = jnp.transpose(route)[0:SUBLANES, :]


def _scan_rows(a, b):
    n = a.shape[0]
    row = lax.broadcasted_iota(jnp.int32, a.shape, 0)
    s = 1
    while s < n:
        keep = row >= s
        a_sh = jnp.where(keep, pltpu.roll(a, s, axis=0), 1.0)
        b_sh = jnp.where(keep, pltpu.roll(b, s, axis=0), 0.0)
        b = a * b_sh + b
        a = a * a_sh
        s *= 2
    return a, b


def _prompt_mixer_kernel(x_ref, mod_ref, cos_ref, sin_ref,
                         n_pre1_ref, n_post1_ref, n_pre2_ref,
                         w_in_ref, conv_w_ref, conv_b_ref, wg_ref, b_r_ref, b_i_ref, lam_ref,
                         gn_w_ref, w_out_ref, w_router_ref, b_router_ref,
                         mask_ref, wstate_ref, cross_ref, cdecay_ref,
                         x1_ref, pk_ref, route_ref, conv_out_ref, lru_out_ref, ret_out_ref, cnt_out_ref,
                         conv_scr, h_scr, s_scr, cnt_scr):
    t = pl.program_id(1)
    tt = x_ref.shape[1]

    @pl.when((pl.program_id(0) == 0) & (t == 0))
    def _():
        cnt_scr[...] = jnp.zeros_like(cnt_scr)

    @pl.when(t == 0)
    def _():
        conv_scr[...] = jnp.zeros_like(conv_scr)
        h_scr[...] = jnp.zeros_like(h_scr)
        s_scr[...] = jnp.zeros_like(s_scr)

    x3 = x_ref[...]
    mod3 = mod_ref[...]
    z = _in_proj(x3, mod3, n_pre1_ref, w_in_ref)
    xr = z[:, 0:D_LRU]
    yg = z[:, D_LRU:2 * D_LRU]
    q = z[:, 1024:1536]
    k = z[:, 1536:2048]
    v = z[:, 2048:2560]
    g = z[:, 2560:3072]

    xx = jnp.concatenate([conv_scr[...], xr], axis=0)
    xc = jnp.broadcast_to(conv_b_ref[...], (tt, D_LRU))
    for j in range(CONV_W):
        off = SUBLANES - (CONV_W - 1) + j
        xc = xc + xx[off:off + tt, :] * conv_w_ref[j:j + 1, :]
    conv_scr[...] = xr[tt - SUBLANES:, :]

    a, b = _lru_coeffs(xc, wg_ref, b_r_ref, b_i_ref, lam_ref)
    a_cum, b_cum = _scan_rows(a, b)
    hseq = b_cum + a_cum * h_scr[0:1, :]
    h_scr[...] = jnp.broadcast_to(hseq[tt - 1:tt, :], h_scr.shape)
    out_a = hseq * jax.nn.gelu(yg, approximate=True)

    cos2 = cos_ref[...]
    sin2 = sin_ref[...]
    scale = DK ** -0.5
    o_heads = []
    for h in range(N_HEADS):
        hs = slice(h * DK, (h + 1) * DK)
        qh = (_rope(q[:, hs], cos2, sin2, 1) * scale).astype(BF16)
        kh = _rope(k[:, hs], cos2, sin2, 1)
        vh = v[:, hs].astype(BF16)
        o_chunks = []
        for c in range(tt // RET_CHUNK):
            cs = slice(c * RET_CHUNK, (c + 1) * RET_CHUNK)
            qc = qh[cs]
            kc = kh[cs]
            vc = vh[cs]
            s_prev = s_scr[h]
            scores = lax.dot_general(qc, kc.astype(BF16), (((1,), (1,)), ((), ())),
                                     preferred_element_type=F32) * mask_ref[h]
            inner = jnp.dot(scores.astype(BF16), vc, preferred_element_type=F32)
            cross = jnp.dot(qc, s_prev.astype(BF16), preferred_element_type=F32) * cross_ref[:, hs]
            kw = (kc * wstate_ref[:, hs]).astype(BF16)
            kv = lax.dot_general(kw, vc, (((0,), (0,)), ((), ())), preferred_element_type=F32)
            s_scr[h] = cdecay_ref[:, hs] * s_prev + kv
            o_chunks.append(inner + cross)
        o_heads.append(_group_norm(jnp.concatenate(o_chunks, axis=0)))
    o = jnp.concatenate(o_heads, axis=1)
    out_b = o * gn_w_ref[...] * _silu(g)

    _post_mixer(x3, mod3, out_a, out_b, w_out_ref, n_post1_ref, n_pre2_ref, w_router_ref, b_router_ref,
                x1_ref, pk_ref, route_ref, cnt_scr)
    cnt_out_ref[...] = cnt_scr[...]

    @pl.when(t == pl.num_programs(1) - 1)
    def _():
        conv_out_ref[0] = xr[tt - SUBLANES:, :]
        lru_out_ref[0] = hseq[tt - SUBLANES:, :]
        ret_out_ref[0] = s_scr[...]


def _const_spec(shape):
    nd = len(shape)
    return pl.BlockSpec(shape, lambda *_: (0,) * nd)


def _prompt_mixer_call(x, mod3, cos2, sin2, wts, tables):
    bsz, seq, _ = x.shape
    tt = PROMPT_TILE
    nt = seq // tt
    n_tok = bsz * seq
    (n_pre1, n_post1, n_pre2, w_in, conv_w, conv_b, wg, b_r, b_i, lam, gn_w, w_out, w_router, b_router) = wts
    mask, wstate, cross, cdecay = tables
    tok_spec = pl.BlockSpec((tt, D_MODEL), lambda b, t: (b * nt + t, 0))
    in_specs = [
        pl.BlockSpec((1, tt, D_MODEL), lambda b, t: (b, t, 0)),
        pl.BlockSpec((1, 1, 6 * D_MODEL), lambda b, t: (b, 0, 0)),
        pl.BlockSpec((tt, LANES), lambda b, t: (t, 0)),
        pl.BlockSpec((tt, LANES), lambda b, t: (t, 0)),
    ] + [_const_spec(w.shape) for w in wts] + [_const_spec(tb.shape) for tb in tables]
    out_specs = [
        tok_spec,
        pl.BlockSpec((tt * SUBLANES, LANES), lambda b, t: (b * nt + t, 0)),
        pl.BlockSpec((SUBLANES, tt), lambda b, t: (0, b * nt + t)),
        pl.BlockSpec((1, SUBLANES, D_LRU), lambda b, t: (b, 0, 0)),
        pl.BlockSpec((1, SUBLANES, D_LRU), lambda b, t: (b, 0, 0)),
        pl.BlockSpec((1, N_HEADS, DK, DV), lambda b, t: (b, 0, 0, 0)),
        pl.BlockSpec((SUBLANES, LANES), lambda b, t: (0, 0)),
    ]
    out_shape = [
        jax.ShapeDtypeStruct((n_tok, D_MODEL), F32),
        jax.ShapeDtypeStruct((n_tok * SUBLANES, LANES), F32),
        jax.ShapeDtypeStruct((SUBLANES, n_tok), F32),
        jax.ShapeDtypeStruct((bsz, SUBLANES, D_LRU), F32),
        jax.ShapeDtypeStruct((bsz, SUBLANES, D_LRU), F32),
        jax.ShapeDtypeStruct((bsz, N_HEADS, DK, DV), F32),
        jax.ShapeDtypeStruct((SUBLANES, LANES), F32),
    ]
    return pl.pallas_call(
        _prompt_mixer_kernel,
        grid=(bsz, nt),
        in_specs=in_specs,
        out_specs=out_specs,
        out_shape=out_shape,
        scratch_shapes=[
            pltpu.VMEM((SUBLANES, D_LRU), F32),
            pltpu.VMEM((SUBLANES, D_LRU), F32),
            pltpu.VMEM((N_HEADS, DK, DV), F32),
            pltpu.VMEM((SUBLANES, LANES), F32),
        ],
        compiler_params=pltpu.CompilerParams(
            dimension_semantics=("arbitrary", "arbitrary"), vmem_limit_bytes=VMEM_LIMIT),
        name="prompt_mixer",
    )(x, mod3, cos2, sin2, *wts, *tables)


def _sample_mixer_kernel(x_ref, mod_ref, cos_ref, sin_ref, buf_ref, h0_ref, s0_ref, cnt_in_ref,
                         n_pre1_ref, n_post1_ref, n_pre2_ref,
                         w_in_ref, conv_w_ref, conv_b_ref, wg_ref, b_r_ref, b_i_ref, lam_ref,
                         gn_w_ref, w_out_ref, w_router_ref, b_router_ref,
                         smask_ref, wstate_ref, cross_ref, cdecay_ref,
                         x1_ref, pk_ref, route_ref, xr_out_ref, h_out_ref, ret_out_ref, cnt_out_ref,
                         cnt_scr):
    bb, ts, _ = x_ref.shape
    m = bb * ts

    @pl.when(pl.program_id(0) == 0)
    def _():
        cnt_scr[...] = cnt_in_ref[...]

    x3 = x_ref[...]
    mod3 = mod_ref[...]
    z = _in_proj(x3, mod3, n_pre1_ref, w_in_ref)
    xr = z[:, 0:D_LRU]
    yg = z[:, D_LRU:2 * D_LRU]
    q = z[:, 1024:1536]
    k = z[:, 1536:2048]
    v = z[:, 2048:2560]
    g = z[:, 2560:3072]
    xr_out_ref[...] = xr

    xr3 = xr.reshape(bb, ts, D_LRU)
    buf3 = buf_ref[...]
    tpos = lax.broadcasted_iota(jnp.int32, (bb, ts, D_LRU), 1)
    xc3 = jnp.broadcast_to(conv_b_ref[...].reshape(1, 1, D_LRU), (bb, ts, D_LRU))
    for j in range(CONV_W):
        back = CONV_W - 1 - j
        w_j = conv_w_ref[j:j + 1, :].reshape(1, 1, D_LRU)
        if back == 0:
            term = xr3
        else:
            cur = pltpu.roll(xr3, back, axis=1)
            up = CONV_W - 1 - back
            old = buf3 if up == 0 else pltpu.roll(buf3, ts - up, axis=1)
            term = jnp.where(tpos >= back, cur, old)
        xc3 = xc3 + term * w_j
    xc = xc3.reshape(m, D_LRU)

    a, b = _lru_coeffs(xc, wg_ref, b_r_ref, b_i_ref, lam_ref)
    a3 = a.reshape(bb, ts, D_LRU)
    b3 = b.reshape(bb, ts, D_LRU) + a3 * h0_ref[...]
    s = 1
    while s < ts:
        keep = tpos >= s
        a_sh = jnp.where(keep, pltpu.roll(a3, s, axis=1), 1.0)
        b_sh = jnp.where(keep, pltpu.roll(b3, s, axis=1), 0.0)
        b3 = a3 * b_sh + b3
        a3 = a3 * a_sh
        s *= 2
    hseq = b3.reshape(m, D_LRU)
    h_out_ref[...] = hseq
    out_a = hseq * jax.nn.gelu(yg, approximate=True)

    cos2 = cos_ref[...].reshape(1, ts, LANES)
    sin2 = sin_ref[...].reshape(1, ts, LANES)
    scale = DK ** -0.5
    o_heads = []
    for h in range(N_HEADS):
        hs = slice(h * DK, (h + 1) * DK)
        q3 = (_rope(q[:, hs].reshape(bb, ts, DK), cos2, sin2, 2) * scale).astype(BF16)
        k3 = _rope(k[:, hs].reshape(bb, ts, DK), cos2, sin2, 2)
        v3 = v[:, hs].reshape(bb, ts, DV).astype(BF16)
        q2 = q3.reshape(m, DK)
        k2 = k3.reshape(m, DK).astype(BF16)
        v2 = v3.reshape(m, DV)
        scores = lax.dot_general(q2, k2, (((1,), (1,)), ((), ())),
                                 preferred_element_type=F32) * smask_ref[h]
        inner = jnp.dot(scores.astype(BF16), v2, preferred_element_type=F32)
        s0h = s0_ref[:, h]
        cross = jnp.einsum('bid,bde->bie', q3, s0h.astype(BF16), preferred_element_type=F32)
        cross = cross * cross_ref[:, hs].reshape(1, ts, DV)
        kw3 = (k3 * wstate_ref[:, hs].reshape(1, ts, DK)).astype(BF16)
        kv = jnp.einsum('bjd,bje->bde', kw3, v3, preferred_element_type=F32)
        ret_out_ref[:, h] = cdecay_ref[:, hs].reshape(1, 1, DV) * s0h + kv
        o_heads.append(_group_norm(inner + cross.reshape(m, DV)))
    o = jnp.concatenate(o_heads, axis=1)
    out_b = o * gn_w_ref[...] * _silu(g)

    _post_mixer(x3, mod3, out_a, out_b, w_out_ref, n_post1_ref, n_pre2_ref, w_router_ref, b_router_ref,
                x1_ref, pk_ref, route_ref, cnt_scr)
    cnt_out_ref[...] = cnt_scr[...]


def _sample_mixer_call(x, mod3, cos2, sin2, buf8, h0p, s0, cnt_in, wts, tables):
    bsz, ts, _ = x.shape
    bb = SAMPLE_SEQS
    m = bb * ts
    n_tok = bsz * ts
    seq_spec = lambda w: pl.BlockSpec((bb, ts, w), lambda i: (i, 0, 0))
    tok_spec = lambda w: pl.BlockSpec((m, w), lambda i: (i, 0))
    in_specs = [
        seq_spec(D_MODEL),
        pl.BlockSpec((bb, 1, 6 * D_MODEL), lambda i: (i, 0, 0)),
        _const_spec(cos2.shape),
        _const_spec(sin2.shape),
        seq_spec(D_LRU),
        seq_spec(D_LRU),
        pl.BlockSpec((bb, N_HEADS, DK, DV), lambda i: (i, 0, 0, 0)),
        _const_spec(cnt_in.shape),
    ] + [_const_spec(w.shape) for w in wts] + [_const_spec(tb.shape) for tb in tables]
    out_specs = [
        tok_spec(D_MODEL),
        pl.BlockSpec((m * SUBLANES, LANES), lambda i: (i, 0)),
        pl.BlockSpec((SUBLANES, m), lambda i: (0, i)),
        tok_spec(D_LRU),
        tok_spec(D_LRU),
        pl.BlockSpec((bb, N_HEADS, DK, DV), lambda i: (i, 0, 0, 0)),
        _const_spec(cnt_in.shape),
    ]
    out_shape = [
        jax.ShapeDtypeStruct((n_tok, D_MODEL), F32),
        jax.ShapeDtypeStruct((n_tok * SUBLANES, LANES), F32),
        jax.ShapeDtypeStruct((SUBLANES, n_tok), F32),
        jax.ShapeDtypeStruct((n_tok, D_LRU), F32),
        jax.ShapeDtypeStruct((n_tok, D_LRU), F32),
        jax.ShapeDtypeStruct((bsz, N_HEADS, DK, DV), F32),
        jax.ShapeDtypeStruct(cnt_in.shape, F32),
    ]
    return pl.pallas_call(
        _sample_mixer_kernel,
        grid=(bsz // bb,),
        in_specs=in_specs,
        out_specs=out_specs,
        out_shape=out_shape,
        scratch_shapes=[pltpu.VMEM((SUBLANES, LANES), F32)],
        compiler_params=pltpu.CompilerParams(
            dimension_semantics=("arbitrary",), vmem_limit_bytes=VMEM_LIMIT),
        name="sample_mixer",
    )(x, mod3, cos2, sin2, buf8, h0p, s0, cnt_in, *wts, *tables)


def _slab(ref, r):
    return ref.at[pl.ds(pl.multiple_of(r * SUBLANES, SUBLANES), SUBLANES)]


def _dispatch_kernel(dst_ref, cnt_ref, padcnt_ref, start_ref, srcp_ref, srcs_ref, out_ref, ring, sem,
                     *, p_tiles):
    i = pl.program_id(0)
    n = pl.num_programs(0)
    tile_rows = ring.shape[1]
    td = tile_rows // SUBLANES

    def issue(slot):
        def body(g, carry):
            first = g * DMA_GROUP
            slots = [dst_ref[i * td + first + j] for j in range(DMA_GROUP)]
            for j in range(DMA_GROUP):
                pltpu.make_async_copy(
                    _slab(ring.at[slot], first + j), _slab(out_ref, slots[j]), sem.at[slot]
                ).start(priority=j % 2)
            return carry
        lax.fori_loop(0, td // DMA_GROUP, body, 0)

    def wait_tile(slot):
        pltpu.make_async_copy(ring.at[slot], out_ref.at[pl.ds(0, tile_rows)], sem.at[slot]).wait()

    def zero_fill(slot):
        ring[slot] = jnp.zeros((tile_rows, LANES), F32)
        zero_src = ring.at[slot]

        def per_class(c, carry):
            lo = start_ref[c] + cnt_ref[c]
            hi = start_ref[c] + padcnt_ref[c]

            def fill(r, carry2):
                pltpu.make_async_copy(_slab(zero_src, 0), _slab(out_ref, r), sem.at[slot]).start()
                return carry2
            lax.fori_loop(lo, hi, fill, 0)

            def done(r, carry2):
                pltpu.make_async_copy(_slab(zero_src, 0), _slab(out_ref, 0), sem.at[slot]).wait()
                return carry2
            lax.fori_loop(lo, hi, done, 0)
            return carry
        lax.fori_loop(0, N_CLASSES, per_class, 0)

        used_tiles = (start_ref[N_CLASSES - 1] + padcnt_ref[N_CLASSES - 1]) // td
        all_tiles = out_ref.shape[0] // tile_rows

        def tile_copy(t):
            return pltpu.make_async_copy(
                zero_src, out_ref.at[pl.ds(pl.multiple_of(t * tile_rows, tile_rows), tile_rows)], sem.at[slot])

        def fill_tile(t, carry):
            tile_copy(t).start()
            return carry
        lax.fori_loop(used_tiles, all_tiles, fill_tile, 0)

        def done_tile(t, carry):
            tile_copy(t).wait()
            return carry
        lax.fori_loop(used_tiles, all_tiles, done_tile, 0)

    for s in range(RING):
        @pl.when(i % RING == s)
        def _():
            @pl.when(i < p_tiles)
            def _():
                ring[s] = srcp_ref[...]

            @pl.when(i >= p_tiles)
            def _():
                ring[s] = srcs_ref[...]

            issue(s)

            @pl.when(i >= RING - 1)
            def _():
                wait_tile((s + 1) % RING)

            @pl.when(i == n - 1)
            def _():
                for back in range(RING - 2, -1, -1):
                    wait_tile((s - back) % RING)
                zero_fill(s)


def _dispatch_call(dst, cnt, padcnt, start, pk_p, pk_s, n_rows):
    td = MOE_TILE
    tile_rows = td * SUBLANES
    p_tiles = pk_p.shape[0] // tile_rows
    s_tiles = pk_s.shape[0] // tile_rows
    assert p_tiles + s_tiles >= RING
    return pl.pallas_call(
        functools.partial(_dispatch_kernel, p_tiles=p_tiles),
        grid_spec=pltpu.PrefetchScalarGridSpec(
            num_scalar_prefetch=4,
            grid=(p_tiles + s_tiles,),
            in_specs=[
                pl.BlockSpec((tile_rows, LANES), lambda i, *_: (jnp.minimum(i, p_tiles - 1), 0)),
                pl.BlockSpec((tile_rows, LANES), lambda i, *_: (jnp.maximum(i - p_tiles, 0), 0)),
            ],
            out_specs=pl.BlockSpec(memory_space=pl.ANY),
            scratch_shapes=[pltpu.VMEM((RING, tile_rows, LANES), F32),
                            pltpu.SemaphoreType.DMA((RING,))],
        ),
        out_shape=jax.ShapeDtypeStruct((n_rows * SUBLANES, LANES), F32),
        compiler_params=pltpu.CompilerParams(
            dimension_semantics=("arbitrary",), has_side_effects=True, vmem_limit_bytes=VMEM_LIMIT),
        name="moe_dispatch",
    )(dst, cnt, padcnt, start, pk_p, pk_s)


def _moe_kernel(ea_ref, eb_ref, valid_ref, xs_ref, w_router_ref, b_router_ref,
                wga_ref, wua_ref, wda_ref, wgb_ref, wub_ref, wdb_ref, f_ref):
    t = pl.program_id(0)

    @pl.when(valid_ref[t] == 1)
    def _():
        tm = xs_ref.shape[0] // SUBLANES
        x = jnp.concatenate([xs_ref[pl.ds(j, tm, stride=SUBLANES), :] for j in range(ROW_GROUPS)],
                            axis=1).astype(BF16)
        e_a = ea_ref[t]
        e_b = eb_ref[t]
        e_lo = (e_a // PER_GROUP) * PER_GROUP
        logits = jnp.dot(x, w_router_ref[...], preferred_element_type=F32) + b_router_ref[...]
        lane = lax.broadcasted_iota(jnp.int32, (tm, LANES), 1)
        pick = lambda p, idx: jnp.sum(jnp.where(lane == idx, p, 0.0), axis=-1, keepdims=True)
        p_group = _masked_softmax(logits, (lane >= N_EXPERTS) & (lane < N_EXPERTS + N_GROUPS))
        p_g = pick(p_group, N_EXPERTS + e_a // PER_GROUP)
        p_e = _masked_softmax(logits, (lane >= e_lo) & (lane < e_lo + PER_GROUP))
        w_a = pick(p_e, e_a)
        w_b = pick(p_e, e_b)
        wsum = w_a + w_b

        def expert(wg_ref, wu_ref, gate):
            hg = jnp.dot(x, wg_ref[0], preferred_element_type=F32)
            hu = jnp.dot(x, wu_ref[0], preferred_element_type=F32)
            return (_silu(hg) * hu * gate).astype(BF16)

        ha = expert(wga_ref, wua_ref, p_g * (w_a / wsum))
        hb = expert(wgb_ref, wub_ref, p_g * (w_b / wsum))
        f = (jnp.dot(ha, wda_ref[0], preferred_element_type=F32)
             + jnp.dot(hb, wdb_ref[0], preferred_element_type=F32))
        for j in range(D_MODEL // LANES):
            f_ref[pl.ds(j, tm, stride=SUBLANES), :] = f[:, j * LANES:(j + 1) * LANES]

    @pl.when(valid_ref[t] == 0)
    def _():
        f_ref[...] = jnp.zeros_like(f_ref)


def _moe_call(tile_ea, tile_eb, tile_valid, rows, w_router, b_router, wg, wu, wd):
    n_rows = rows.shape[0] // SUBLANES
    tm = MOE_TILE
    const = lambda a: pl.BlockSpec(a.shape, lambda t, ea, eb, v: (0,) * a.ndim)
    up = lambda sel: pl.BlockSpec((1, D_MODEL, D_EXPERT), lambda t, ea, eb, v: (sel(ea, eb)[t], 0, 0))
    down = lambda sel: pl.BlockSpec((1, D_EXPERT, D_MODEL), lambda t, ea, eb, v: (sel(ea, eb)[t], 0, 0))
    first = lambda ea, eb: ea
    second = lambda ea, eb: eb
    return pl.pallas_call(
        _moe_kernel,
        grid_spec=pltpu.PrefetchScalarGridSpec(
            num_scalar_prefetch=3,
            grid=(n_rows // tm,),
            in_specs=[
                pl.BlockSpec((tm * SUBLANES, LANES), lambda t, ea, eb, v: (t, 0)),
                const(w_router), const(b_router),
                up(first), up(first), down(first), up(second), up(second), down(second),
            ],
            out_specs=pl.BlockSpec((tm * SUBLANES, LANES), lambda t, ea, eb, v: (t, 0)),
        ),
        out_shape=jax.ShapeDtypeStruct((n_rows * SUBLANES, LANES), F32),
        compiler_params=pltpu.CompilerParams(
            dimension_semantics=("arbitrary",), vmem_limit_bytes=VMEM_LIMIT),
        name="moe_experts",
    )(tile_ea, tile_eb, tile_valid, rows, w_router, b_router, wg, wu, wd, wg, wu, wd)


def _combine_kernel(dst_ref, x1_ref, mod_ref, n_post2_ref, f_hbm, o_ref, fbuf, sem):
    i = pl.program_id(0)
    n = pl.num_programs(0)
    tf = x1_ref.shape[0]

    def issue(tile, slot):
        def body(g, carry):
            first = g * DMA_GROUP
            slots = [dst_ref[tile * tf + first + j] for j in range(DMA_GROUP)]
            for j in range(DMA_GROUP):
                pltpu.make_async_copy(
                    _slab(f_hbm, slots[j]), _slab(fbuf.at[slot], first + j), sem.at[slot]
                ).start(priority=j % 2)
            return carry
        lax.fori_loop(0, tf // DMA_GROUP, body, 0)

    def finish(slot):
        pltpu.make_async_copy(f_hbm.at[pl.ds(0, tf * SUBLANES)], fbuf.at[slot], sem.at[slot]).wait()
        f = jnp.concatenate([fbuf[slot, pl.ds(j, tf, stride=SUBLANES), :] for j in range(D_MODEL // LANES)],
                            axis=1)
        mod3 = mod_ref[...]
        bb = mod3.shape[0]
        f3 = f.reshape(bb, tf // bb, D_MODEL)
        g2 = mod3[:, :, 5 * D_MODEL:6 * D_MODEL]
        x13 = x1_ref[...].reshape(f3.shape)
        out = x13 + g2 * (f3 * _rms_scale(f3) * n_post2_ref[...].reshape(1, 1, D_MODEL))
        o_ref[...] = out.reshape(o_ref.shape)

    @pl.when(i == 0)
    def _():
        issue(0, 0)

    for parity in range(2):
        @pl.when(i % 2 == parity)
        def _():
            @pl.when(i + 1 < n)
            def _():
                issue(i + 1, 1 - parity)
            finish(parity)


def _combine_call(dst, x1, mod3, n_post2, f_sorted, seqs_per_tile):
    n_tok = x1.shape[0]
    tf = FINAL_TILE
    tok_per_seq = n_tok // mod3.shape[0]
    if seqs_per_tile > 1:
        mod_map = lambda i, d: (i, 0, 0)
    else:
        mod_map = lambda i, d: ((i * tf) // tok_per_seq, 0, 0)
    return pl.pallas_call(
        _combine_kernel,
        grid_spec=pltpu.PrefetchScalarGridSpec(
            num_scalar_prefetch=1,
            grid=(n_tok // tf,),
            in_specs=[
                pl.BlockSpec((tf, D_MODEL), lambda i, d: (i, 0)),
                pl.BlockSpec((seqs_per_tile, 1, 6 * D_MODEL), mod_map),
                pl.BlockSpec((1, D_MODEL), lambda i, d: (0, 0)),
                pl.BlockSpec(memory_space=pl.ANY),
            ],
            out_specs=pl.BlockSpec((tf, D_MODEL), lambda i, d: (i, 0)),
            scratch_shapes=[pltpu.VMEM((2, tf * SUBLANES, LANES), F32), pltpu.SemaphoreType.DMA((2,))],
        ),
        out_shape=jax.ShapeDtypeStruct((n_tok, D_MODEL), F32),
        compiler_params=pltpu.CompilerParams(
            dimension_semantics=("arbitrary",), vmem_limit_bytes=VMEM_LIMIT),
        name="moe_combine",
    )(dst, x1, mod3, n_post2, f_sorted)


def _block_diag_gate(w_r, w_i):
    per_half = GATE_HALF // LRU_BLOCK
    halves = []
    for hb in range(D_LRU // GATE_HALF):
        blocks = []
        for w in (w_r, w_i):
            mat = jnp.zeros((GATE_HALF, GATE_HALF), F32)
            for n in range(per_half):
                lo = n * LRU_BLOCK
                mat = mat.at[lo:lo + LRU_BLOCK, lo:lo + LRU_BLOCK].set(w[hb * per_half + n])
            blocks.append(mat)
        halves.append(jnp.concatenate(blocks, axis=1))
    return jnp.stack(halves).astype(BF16)


def _rope_tables(pos):
    half = DK // 2
    inv = ROPE_BASE ** (-jnp.arange(half, dtype=F32) / half)
    ang = pos[:, None] * inv[None, :]
    cos = jnp.cos(ang)
    sin = jnp.sin(ang)
    return jnp.concatenate([cos, cos], axis=-1), jnp.concatenate([-sin, sin], axis=-1)


def _decay_tables(c):
    log_g = jnp.log1p(-jnp.exp2(-5.0 - jnp.arange(N_HEADS, dtype=F32)))
    idx = jnp.arange(c, dtype=F32)
    diff = idx[:, None] - idx[None, :]
    mask = jnp.where(diff[None] >= 0, jnp.exp(jnp.maximum(diff, 0.0)[None] * log_g[:, None, None]), 0.0)
    w_state = jnp.exp((c - 1.0 - idx)[None, :] * log_g[:, None])
    cross_decay = jnp.exp((idx + 1.0)[:, None] * log_g[None, :])
    chunk_decay = jnp.exp(c * log_g)
    wstate_full = jnp.repeat(w_state.T, DK, axis=1)
    cross_full = jnp.repeat(cross_decay, DV, axis=1)
    cdecay_full = jnp.repeat(chunk_decay, DV)[None, :]
    return mask, wstate_full, cross_full, cdecay_full


def kernel(x_prompt, x_sample, state_conv, state_lru, state_ret, c_prompt, c_sample, w_mod, b_mod, norm_pre_mix, norm_post_mix, norm_pre_ffn, norm_post_ffn, w_in, conv_w, conv_b, w_rgate, b_rgate, w_igate, b_igate, lru_lambda, ret_gn_w, w_out, w_router_group, b_router_group, w_router_expert, b_router_expert, w_exp_gate, w_exp_up, w_exp_down):
    bp, tp, _ = x_prompt.shape
    bs, ts, _ = x_sample.shape
    l = 0

    mod = _mod_call(jnp.concatenate([c_prompt, c_sample], axis=0), w_mod[l], b_mod[l][None, :])
    mod_p = mod[:bp][:, None, :]
    mod_s = mod[bp:][:, None, :]

    w_router = jnp.zeros((D_MODEL, LANES), F32)
    w_router = w_router.at[:, :N_EXPERTS].set(w_router_expert[l])
    w_router = w_router.at[:, N_EXPERTS:N_EXPERTS + N_GROUPS].set(w_router_group[l]).astype(BF16)
    b_router = jnp.zeros((1, LANES), F32)
    b_router = b_router.at[0, :N_EXPERTS].set(b_router_expert[l])
    b_router = b_router.at[0, N_EXPERTS:N_EXPERTS + N_GROUPS].set(b_router_group[l])

    row = lambda vec: vec.reshape(1, -1)
    wts = (row(norm_pre_mix[l]), row(norm_post_mix[l]), row(norm_pre_ffn[l]),
           w_in[l].astype(BF16), conv_w[l], row(conv_b[l]),
           _block_diag_gate(w_rgate[l], w_igate[l]),
           row(b_rgate[l]), row(b_igate[l]), row(lru_lambda[l]), row(ret_gn_w[l]),
           w_out[l].astype(BF16), w_router, b_router)

    cos_p, sin_p = _rope_tables(jnp.arange(tp, dtype=F32))
    x1_p, pk_p, route_p, conv_p8, lru_p8, ret_p, cnt_p = _prompt_mixer_call(
        x_prompt, mod_p, cos_p, sin_p, wts, _decay_tables(math.gcd(tp, RET_CHUNK)))

    cos_s, sin_s = _rope_tables(jnp.float32(PAST_LEN) + jnp.arange(ts, dtype=F32))
    mask8, wstate_s, cross_s, cdecay_s = _decay_tables(math.gcd(ts, RET_CHUNK))
    eye = jnp.eye(SAMPLE_SEQS, dtype=F32)
    smask = jnp.stack([jnp.kron(eye, mask8[h]) for h in range(N_HEADS)])
    buf8 = jnp.pad(state_conv[l], ((0, 0), (0, ts - (CONV_W - 1)), (0, 0)))
    h0p = jnp.pad(state_lru[l][:, None, :], ((0, 0), (0, ts - 1), (0, 0)))
    x1_s, pk_s, route_s, xr_s, h_s, ret_s, cnt_all = _sample_mixer_call(
        x_sample, mod_s, cos_s, sin_s, buf8, h0p, state_ret[l], cnt_p, wts,
        (smask, wstate_s, cross_s, cdecay_s))

    n_p = bp * tp
    n_tok = n_p + bs * ts
    tm = MOE_TILE
    max_tiles = n_tok // tm + N_CLASSES
    route = jnp.concatenate([route_p[0:2], route_s[0:2]], axis=1).astype(jnp.int32)
    cls = route[0]
    rank = route[1]
    cnt = cnt_all[0, :N_CLASSES].astype(jnp.int32)
    ntile = (cnt + (tm - 1)) // tm
    padcnt = ntile * tm
    start = jnp.cumsum(padcnt) - padcnt
    dst = start[cls] + rank
    tile_end = jnp.cumsum(ntile)
    tile_ids = jnp.arange(max_tiles, dtype=jnp.int32)
    tile_valid = (tile_ids < tile_end[-1]).astype(jnp.int32)
    last_used = jnp.minimum(tile_ids, tile_end[-1] - 1)
    tile_cls = jnp.sum((last_used[:, None] >= tile_end[None, :]).astype(jnp.int32), axis=1)
    pair_a = jnp.array([0, 0, 0, 1, 1, 2], jnp.int32)
    pair_b = jnp.array([1, 2, 3, 2, 3, 3], jnp.int32)
    tile_ea = (tile_cls // N_PAIRS) * PER_GROUP + pair_a[tile_cls % N_PAIRS]
    tile_eb = (tile_cls // N_PAIRS) * PER_GROUP + pair_b[tile_cls % N_PAIRS]

    n_post2 = row(norm_post_ffn[l])
    wg = w_exp_gate[l].astype(BF16)
    wu = w_exp_up[l].astype(BF16)
    wd = w_exp_down[l].astype(BF16)
    rows = _dispatch_call(dst, cnt, padcnt, start, pk_p, pk_s, max_tiles * tm)
    f_sorted = _moe_call(tile_ea, tile_eb, tile_valid, rows, w_router, b_router, wg, wu, wd)
    y_p = _combine_call(dst[:n_p], x1_p, mod_p, n_post2, f_sorted, 1)
    y_s = _combine_call(dst[n_p:], x1_s, mod_s, n_post2, f_sorted, FINAL_TILE // ts)

    conv_p = conv_p8[:, SUBLANES - (CONV_W - 1):, :]
    lru_p = lru_p8[:, SUBLANES - 1, :]
    xr_s3 = xr_s.reshape(bs, ts, D_LRU)
    conv_s = xr_s3[:, ts - (CONV_W - 1):, :]
    lru_s = h_s.reshape(bs, ts, D_LRU)[:, ts - 1, :]
    return (y_p.reshape(bp, tp, D_MODEL), y_s.reshape(bs, ts, D_MODEL),
            conv_p[None], lru_p[None], ret_p[None],
            conv_s[None], lru_s[None], ret_s[None])
```

```python
import functools
import math

import jax
import jax.numpy as jnp
from jax import lax
from jax.experimental import pallas as pl
from jax.experimental.pallas import tpu as pltpu

F32 = jnp.float32
BF16 = jnp.bfloat16

D_MODEL = 1024
D_LRU = 512
D_RET = 512
N_LRU_BLOCKS = 8
LRU_BLOCK = D_LRU // N_LRU_BLOCKS
CONV_W = 4
LRU_C = 8.0
N_HEADS = 4
DK = 128
DV = 128
RET_CHUNK = 128
ROPE_BASE = 10000.0
D_IN_PROJ = 3072
N_GROUPS = 4
PER_GROUP = 4
N_EXPERTS = 16
D_EXPERT = 256
EXPM1_DIRECT_BELOW = -0.5
NORM_EPS = 1e-6
GN_EPS = 1e-5
PAST_LEN = 16384

SUBLANES = 8
LANES = 128
GATE_HALF = 256
VMEM_LIMIT = 56 * 1024 * 1024

PROMPT_TILE = 512
SAMPLE_SEQS = 16
MOE_TILE = 256
FINAL_TILE = 256
RING = 3
DMA_GROUP = 8

N_PAIRS = 6
N_CLASSES = N_GROUPS * N_PAIRS
ROW_GROUPS = D_MODEL // LANES


def _silu(x):
    return x * jax.nn.sigmoid(x)


def _rms_scale(x):
    return lax.rsqrt(jnp.mean(x * x, axis=-1, keepdims=True) + NORM_EPS)


def _masked_softmax(logits, mask):
    top = jnp.max(jnp.where(mask, logits, -jnp.inf), axis=-1, keepdims=True)
    e = jnp.where(mask, jnp.exp(logits - top), 0.0)
    return e / jnp.sum(e, axis=-1, keepdims=True)


def _mod_kernel(c_ref, w_ref, b_ref, o_ref):
    s = _silu(c_ref[...]).astype(BF16)
    o_ref[...] = jnp.dot(s, w_ref[...].astype(BF16), preferred_element_type=F32) + b_ref[...]


def _mod_call(c_all, w_mod, b_mod):
    rows = c_all.shape[0]
    ncol = w_mod.shape[1]
    blk = 1024
    return pl.pallas_call(
        _mod_kernel,
        grid=(ncol // blk,),
        in_specs=[
            pl.BlockSpec((rows, D_MODEL), lambda j: (0, 0)),
            pl.BlockSpec((D_MODEL, blk), lambda j: (0, j)),
            pl.BlockSpec((1, blk), lambda j: (0, j)),
        ],
        out_specs=pl.BlockSpec((rows, blk), lambda j: (0, j)),
        out_shape=jax.ShapeDtypeStruct((rows, ncol), F32),
        compiler_params=pltpu.CompilerParams(
            dimension_semantics=("arbitrary",), vmem_limit_bytes=VMEM_LIMIT),
        name="mod",
    )(c_all, w_mod, b_mod)


def _in_proj(x3, mod3, n_pre1_ref, w_in_ref):
    bb, tt, _ = x3.shape
    sh1 = mod3[:, :, 0:D_MODEL]
    sc1 = mod3[:, :, D_MODEL:2 * D_MODEL]
    coef = n_pre1_ref[...].reshape(1, 1, D_MODEL) * (1.0 + sc1)
    u = (x3 * _rms_scale(x3)) * coef + sh1
    u2d = u.reshape(bb * tt, D_MODEL).astype(BF16)
    return jnp.dot(u2d, w_in_ref[...], preferred_element_type=F32)


def _lru_coeffs(xc, wg_ref, b_r_ref, b_i_ref, lam_ref):
    xcb = xc.astype(BF16)
    g0 = jnp.dot(xcb[:, :GATE_HALF], wg_ref[0], preferred_element_type=F32)
    g1 = jnp.dot(xcb[:, GATE_HALF:], wg_ref[1], preferred_element_type=F32)
    r = jax.nn.sigmoid(jnp.concatenate([g0[:, :GATE_HALF], g1[:, :GATE_HALF]], axis=1) + b_r_ref[...])
    i = jax.nn.sigmoid(jnp.concatenate([g0[:, GATE_HALF:], g1[:, GATE_HALF:]], axis=1) + b_i_ref[...])
    lam = lam_ref[...]
    sp = jnp.maximum(-lam, 0.0) + jnp.log1p(jnp.exp(-jnp.abs(lam)))
    log_a = -LRU_C * r * sp
    a = jnp.exp(log_a)
    y = 2.0 * log_a
    a2 = a * a
    d = a2 - 1.0
    small = d * y / jnp.log(a2)
    em1 = jnp.where(y < EXPM1_DIRECT_BELOW, d, jnp.where(d == 0.0, y, small))
    gain = jnp.sqrt(-em1)
    return a, gain * (i * xc)


def _rope(xh, cos2, sin2, lane_axis):
    return xh * cos2 + pltpu.roll(xh, DK // 2, axis=lane_axis) * sin2


def _group_norm(o):
    mu = jnp.mean(o, axis=-1, keepdims=True)
    d = o - mu
    var = jnp.mean(d * d, axis=-1, keepdims=True)
    return d * lax.rsqrt(var + GN_EPS)


def _post_mixer(x3, mod3, out_a, out_b, w_out_ref, n_post1_ref, n_pre2_ref, w_router_ref, b_router_ref,
                x1_ref, pk_ref, route_ref, cnt_scr):
    bb, tt, _ = x3.shape
    m = bb * tt
    y = (jnp.dot(out_a.astype(BF16), w_out_ref[0:D_LRU, :], preferred_element_type=F32)
         + jnp.dot(out_b.astype(BF16), w_out_ref[D_LRU:, :], preferred_element_type=F32))
    g1 = mod3[:, :, 2 * D_MODEL:3 * D_MODEL]
    sh2 = mod3[:, :, 3 * D_MODEL:4 * D_MODEL]
    sc2 = mod3[:, :, 4 * D_MODEL:5 * D_MODEL]
    y3 = y.reshape(bb, tt, D_MODEL)
    x1 = x3 + (y3 * _rms_scale(y3)) * (g1 * n_post1_ref[...].reshape(1, 1, D_MODEL))
    u2 = (x1 * _rms_scale(x1)) * (n_pre2_ref[...].reshape(1, 1, D_MODEL) * (1.0 + sc2)) + sh2
    x1_ref[...] = x1.reshape(m, D_MODEL)
    u2f = u2.reshape(m, D_MODEL)
    for j in range(ROW_GROUPS):
        pk_ref[pl.ds(j, m, stride=SUBLANES), :] = u2f[:, j * LANES:(j + 1) * LANES]
    u2b = u2f.astype(BF16)

    logits = jnp.dot(u2b, w_router_ref[...], preferred_element_type=F32) + b_router_ref[...]
    lane = lax.broadcasted_iota(jnp.int32, (m, LANES), 1)
    is_g = (lane >= N_EXPERTS) & (lane < N_EXPERTS + N_GROUPS)
    p_group = _masked_softmax(logits, is_g)
    p_g = jnp.max(p_group, axis=-1, keepdims=True)
    g_lane = jnp.min(jnp.where(is_g & (p_group == p_g), lane, LANES), axis=-1, keepdims=True)
    e_lo = (g_lane - N_EXPERTS) * PER_GROUP
    in_g = (lane >= e_lo) & (lane < e_lo + PER_GROUP)
    p_e = _masked_softmax(logits, in_g)
    pm = jnp.where(in_g, p_e, -1.0)
    w1 = jnp.max(pm, axis=-1, keepdims=True)
    i1 = jnp.min(jnp.where(pm == w1, lane, LANES), axis=-1, keepdims=True)
    pm2 = jnp.where(lane == i1, -1.0, pm)
    w2 = jnp.max(pm2, axis=-1, keepdims=True)
    i2 = jnp.min(jnp.where(pm2 == w2, lane, LANES), axis=-1, keepdims=True)
    a = jnp.minimum(i1, i2) - e_lo
    b = jnp.maximum(i1, i2) - e_lo
    pair = jnp.where(a == 0, b - 1, jnp.where(a == 1, b + 1, 5))
    cls = (g_lane - N_EXPERTS) * N_PAIRS + pair
    onehot = lane == cls
    r_i = lax.broadcasted_iota(jnp.int32, (m, m), 0)
    c_i = lax.broadcasted_iota(jnp.int32, (m, m), 1)
    earlier = jnp.where(r_i > c_i, 1.0, 0.0).astype(BF16)
    prefix = jnp.dot(earlier, jnp.where(onehot, 1.0, 0.0).astype(BF16), preferred_element_type=F32)
    run = cnt_scr[0:1, :]
    rank = jnp.sum(jnp.where(onehot, prefix + run, 0.0), axis=-1, keepdims=True)
    cnt_scr[...] = jnp.broadcast_to(
        run + jnp.sum(jnp.where(onehot, 1.0, 0.0), axis=0, keepdims=True), cnt_scr.shape)
    route = jnp.where(lane == 0, cls.astype(F32), jnp.where(lane == 1, rank, 0.0))
    route_ref[...] = jnp.transpose(route)[0:SUBLANES, :]


def _group_scan(a3, b3):
    tpos = lax.broadcasted_iota(jnp.int32, a3.shape, 1)
    s = 1
    while s < a3.shape[1]:
        keep = tpos >= s
        a_sh = jnp.where(keep, pltpu.roll(a3, s, axis=1), 1.0)
        b_sh = jnp.where(keep, pltpu.roll(b3, s, axis=1), 0.0)
        b3 = a3 * b_sh + b3
        a3 = a3 * a_sh
        s *= 2
    return a3, b3


def _scan_rows(a, b, h0):
    n, c = a.shape
    groups = n // SUBLANES
    a3, b3 = _group_scan(a.reshape(groups, SUBLANES, c), b.reshape(groups, SUBLANES, c))
    carry = h0
    out = []
    for g in range(groups):
        hg = b3[g] + a3[g] * carry
        out.append(hg)
        carry = hg[SUBLANES - 1:SUBLANES, :]
    return jnp.concatenate(out, axis=0)


def _prompt_mixer_kernel(x_ref, mod_ref, cos_ref, sin_ref,
                         n_pre1_ref, n_post1_ref, n_pre2_ref,
                         w_in_ref, conv_w_ref, conv_b_ref, wg_ref, b_r_ref, b_i_ref, lam_ref,
                         gn_w_ref, w_out_ref, w_router_ref, b_router_ref,
                         mask_ref, wstate_ref, cross_ref, cdecay_ref,
                         x1_ref, pk_ref, route_ref, conv_out_ref, lru_out_ref, ret_out_ref, cnt_out_ref,
                         conv_scr, h_scr, s_scr, cnt_scr):
    t = pl.program_id(1)
    tt = x_ref.shape[1]

    @pl.when((pl.program_id(0) == 0) & (t == 0))
    def _():
        cnt_scr[...] = jnp.zeros_like(cnt_scr)

    @pl.when(t == 0)
    def _():
        conv_scr[...] = jnp.zeros_like(conv_scr)
        h_scr[...] = jnp.zeros_like(h_scr)
        s_scr[...] = jnp.zeros_like(s_scr)

    x3 = x_ref[...]
    mod3 = mod_ref[...]
    z = _in_proj(x3, mod3, n_pre1_ref, w_in_ref)
    xr = z[:, 0:D_LRU]
    yg = z[:, D_LRU:2 * D_LRU]
    q = z[:, 1024:1536]
    k = z[:, 1536:2048]
    v = z[:, 2048:2560]
    g = z[:, 2560:3072]

    groups = tt // SUBLANES
    xr3 = xr.reshape(groups, SUBLANES, D_LRU)
    tpos = lax.broadcasted_iota(jnp.int32, xr3.shape, 1)
    tail = conv_scr[...]
    xc3 = jnp.broadcast_to(conv_b_ref[...].reshape(1, 1, D_LRU), xr3.shape)
    for j in range(CONV_W):
        back = CONV_W - 1 - j
        w_j = conv_w_ref[j:j + 1, :].reshape(1, 1, D_LRU)
        if back == 0:
            term = xr3
        else:
            cur = pltpu.roll(xr3, back, axis=1)
            first = pltpu.roll(tail, back, axis=0).reshape(1, SUBLANES, D_LRU)
            prev = jnp.concatenate([first, cur[:groups - 1]], axis=0)
            term = jnp.where(tpos >= back, cur, prev)
        xc3 = xc3 + term * w_j
    xc = xc3.reshape(tt, D_LRU)
    conv_scr[...] = xr[tt - SUBLANES:, :]

    a, b = _lru_coeffs(xc, wg_ref, b_r_ref, b_i_ref, lam_ref)
    hseq = _scan_rows(a, b, h_scr[0:1, :])
    h_scr[...] = jnp.broadcast_to(hseq[tt - 1:tt, :], h_scr.shape)
    out_a = hseq * jax.nn.gelu(yg, approximate=True)

    cos2 = cos_ref[...]
    sin2 = sin_ref[...]
    scale = DK ** -0.5
    o_heads = []
    for h in range(N_HEADS):
        hs = slice(h * DK, (h + 1) * DK)
        qh = (_rope(q[:, hs], cos2, sin2, 1) * scale).astype(BF16)
        kh = _rope(k[:, hs], cos2, sin2, 1)
        vh = v[:, hs].astype(BF16)
        o_chunks = []
        for c in range(tt // RET_CHUNK):
            cs = slice(c * RET_CHUNK, (c + 1) * RET_CHUNK)
            qc = qh[cs]
            kc = kh[cs]
            vc = vh[cs]
            s_prev = s_scr[h]
            scores = lax.dot_general(qc, kc.astype(BF16), (((1,), (1,)), ((), ())),
                                     preferred_element_type=F32) * mask_ref[h]
            inner = jnp.dot(scores.astype(BF16), vc, preferred_element_type=F32)
            cross = jnp.dot(qc, s_prev.astype(BF16), preferred_element_type=F32) * cross_ref[:, hs]
            kw = (kc * wstate_ref[:, hs]).astype(BF16)
            kv = lax.dot_general(kw, vc, (((0,), (0,)), ((), ())), preferred_element_type=F32)
            s_scr[h] = cdecay_ref[:, hs] * s_prev + kv
            o_chunks.append(inner + cross)
        o_heads.append(_group_norm(jnp.concatenate(o_chunks, axis=0)))
    o = jnp.concatenate(o_heads, axis=1)
    out_b = o * gn_w_ref[...] * _silu(g)

    _post_mixer(x3, mod3, out_a, out_b, w_out_ref, n_post1_ref, n_pre2_ref, w_router_ref, b_router_ref,
                x1_ref, pk_ref, route_ref, cnt_scr)
    cnt_out_ref[...] = cnt_scr[...]

    @pl.when(t == pl.num_programs(1) - 1)
    def _():
        conv_out_ref[0] = xr[tt - SUBLANES:, :]
        lru_out_ref[0] = hseq[tt - SUBLANES:, :]
        ret_out_ref[0] = s_scr[...]


def _const_spec(shape):
    nd = len(shape)
    return pl.BlockSpec(shape, lambda *_: (0,) * nd)


def _prompt_mixer_call(x, mod3, cos2, sin2, wts, tables):
    bsz, seq, _ = x.shape
    tt = PROMPT_TILE
    nt = seq // tt
    n_tok = bsz * seq
    (n_pre1, n_post1, n_pre2, w_in, conv_w, conv_b, wg, b_r, b_i, lam, gn_w, w_out, w_router, b_router) = wts
    mask, wstate, cross, cdecay = tables
    tok_spec = pl.BlockSpec((tt, D_MODEL), lambda b, t: (b * nt + t, 0))
    in_specs = [
        pl.BlockSpec((1, tt, D_MODEL), lambda b, t: (b, t, 0)),
        pl.BlockSpec((1, 1, 6 * D_MODEL), lambda b, t: (b, 0, 0)),
        pl.BlockSpec((tt, LANES), lambda b, t: (t, 0)),
        pl.BlockSpec((tt, LANES), lambda b, t: (t, 0)),
    ] + [_const_spec(w.shape) for w in wts] + [_const_spec(tb.shape) for tb in tables]
    out_specs = [
        tok_spec,
        pl.BlockSpec((tt * SUBLANES, LANES), lambda b, t: (b * nt + t, 0)),
        pl.BlockSpec((SUBLANES, tt), lambda b, t: (0, b * nt + t)),
        pl.BlockSpec((1, SUBLANES, D_LRU), lambda b, t: (b, 0, 0)),
        pl.BlockSpec((1, SUBLANES, D_LRU), lambda b, t: (b, 0, 0)),
        pl.BlockSpec((1, N_HEADS, DK, DV), lambda b, t: (b, 0, 0, 0)),
        pl.BlockSpec((SUBLANES, LANES), lambda b, t: (0, 0)),
    ]
    out_shape = [
        jax.ShapeDtypeStruct((n_tok, D_MODEL), F32),
        jax.ShapeDtypeStruct((n_tok * SUBLANES, LANES), F32),
        jax.ShapeDtypeStruct((SUBLANES, n_tok), F32),
        jax.ShapeDtypeStruct((bsz, SUBLANES, D_LRU), F32),
        jax.ShapeDtypeStruct((bsz, SUBLANES, D_LRU), F32),
        jax.ShapeDtypeStruct((bsz, N_HEADS, DK, DV), F32),
        jax.ShapeDtypeStruct((SUBLANES, LANES), F32),
    ]
    return pl.pallas_call(
        _prompt_mixer_kernel,
        grid=(bsz, nt),
        in_specs=in_specs,
        out_specs=out_specs,
        out_shape=out_shape,
        scratch_shapes=[
            pltpu.VMEM((SUBLANES, D_LRU), F32),
            pltpu.VMEM((SUBLANES, D_LRU), F32),
            pltpu.VMEM((N_HEADS, DK, DV), F32),
            pltpu.VMEM((SUBLANES, LANES), F32),
        ],
        compiler_params=pltpu.CompilerParams(
            dimension_semantics=("arbitrary", "arbitrary"), vmem_limit_bytes=VMEM_LIMIT),
        name="prompt_mixer",
    )(x, mod3, cos2, sin2, *wts, *tables)


def _sample_mixer_kernel(x_ref, mod_ref, cos_ref, sin_ref, buf_ref, h0_ref, s0_ref, cnt_in_ref,
                         n_pre1_ref, n_post1_ref, n_pre2_ref,
                         w_in_ref, conv_w_ref, conv_b_ref, wg_ref, b_r_ref, b_i_ref, lam_ref,
                         gn_w_ref, w_out_ref, w_router_ref, b_router_ref,
                         smask_ref, wstate_ref, cross_ref, cdecay_ref,
                         x1_ref, pk_ref, route_ref, xr_out_ref, h_out_ref, ret_out_ref, cnt_out_ref,
                         cnt_scr):
    bb, ts, _ = x_ref.shape
    m = bb * ts

    @pl.when(pl.program_id(0) == 0)
    def _():
        cnt_scr[...] = cnt_in_ref[...]

    x3 = x_ref[...]
    mod3 = mod_ref[...]
    z = _in_proj(x3, mod3, n_pre1_ref, w_in_ref)
    xr = z[:, 0:D_LRU]
    yg = z[:, D_LRU:2 * D_LRU]
    q = z[:, 1024:1536]
    k = z[:, 1536:2048]
    v = z[:, 2048:2560]
    g = z[:, 2560:3072]
    xr_out_ref[...] = xr

    xr3 = xr.reshape(bb, ts, D_LRU)
    buf3 = buf_ref[...]
    tpos = lax.broadcasted_iota(jnp.int32, (bb, ts, D_LRU), 1)
    xc3 = jnp.broadcast_to(conv_b_ref[...].reshape(1, 1, D_LRU), (bb, ts, D_LRU))
    for j in range(CONV_W):
        back = CONV_W - 1 - j
        w_j = conv_w_ref[j:j + 1, :].reshape(1, 1, D_LRU)
        if back == 0:
            term = xr3
        else:
            cur = pltpu.roll(xr3, back, axis=1)
            up = CONV_W - 1 - back
            old = buf3 if up == 0 else pltpu.roll(buf3, ts - up, axis=1)
            term = jnp.where(tpos >= back, cur, old)
        xc3 = xc3 + term * w_j
    xc = xc3.reshape(m, D_LRU)

    a, b = _lru_coeffs(xc, wg_ref, b_r_ref, b_i_ref, lam_ref)
    a3 = a.reshape(bb, ts, D_LRU)
    b3 = b.reshape(bb, ts, D_LRU) + a3 * h0_ref[...]
    _, h3 = _group_scan(a3, b3)
    hseq = h3.reshape(m, D_LRU)
    h_out_ref[...] = hseq
    out_a = hseq * jax.nn.gelu(yg, approximate=True)

    cos2 = cos_ref[...].reshape(1, ts, LANES)
    sin2 = sin_ref[...].reshape(1, ts, LANES)
    scale = DK ** -0.5
    o_heads = []
    for h in range(N_HEADS):
        hs = slice(h * DK, (h + 1) * DK)
        q3 = (_rope(q[:, hs].reshape(bb, ts, DK), cos2, sin2, 2) * scale).astype(BF16)
        k3 = _rope(k[:, hs].reshape(bb, ts, DK), cos2, sin2, 2)
        v3 = v[:, hs].reshape(bb, ts, DV).astype(BF16)
        q2 = q3.reshape(m, DK)
        k2 = k3.reshape(m, DK).astype(BF16)
        v2 = v3.reshape(m, DV)
        scores = lax.dot_general(q2, k2, (((1,), (1,)), ((), ())),
                                 preferred_element_type=F32) * smask_ref[h]
        inner = jnp.dot(scores.astype(BF16), v2, preferred_element_type=F32)
        s0h = s0_ref[:, h]
        cross = jnp.einsum('bid,bde->bie', q3, s0h.astype(BF16), preferred_element_type=F32)
        cross = cross * cross_ref[:, hs].reshape(1, ts, DV)
        kw3 = (k3 * wstate_ref[:, hs].reshape(1, ts, DK)).astype(BF16)
        kv = jnp.einsum('bjd,bje->bde', kw3, v3, preferred_element_type=F32)
        ret_out_ref[:, h] = cdecay_ref[:, hs].reshape(1, 1, DV) * s0h + kv
        o_heads.append(_group_norm(inner + cross.reshape(m, DV)))
    o = jnp.concatenate(o_heads, axis=1)
    out_b = o * gn_w_ref[...] * _silu(g)

    _post_mixer(x3, mod3, out_a, out_b, w_out_ref, n_post1_ref, n_pre2_ref, w_router_ref, b_router_ref,
                x1_ref, pk_ref, route_ref, cnt_scr)
    cnt_out_ref[...] = cnt_scr[...]


def _sample_mixer_call(x, mod3, cos2, sin2, buf8, h0p, s0, cnt_in, wts, tables):
    bsz, ts, _ = x.shape
    bb = SAMPLE_SEQS
    m = bb * ts
    n_tok = bsz * ts
    seq_spec = lambda w: pl.BlockSpec((bb, ts, w), lambda i: (i, 0, 0))
    tok_spec = lambda w: pl.BlockSpec((m, w), lambda i: (i, 0))
    in_specs = [
        seq_spec(D_MODEL),
        pl.BlockSpec((bb, 1, 6 * D_MODEL), lambda i: (i, 0, 0)),
        _const_spec(cos2.shape),
        _const_spec(sin2.shape),
        seq_spec(D_LRU),
        seq_spec(D_LRU),
        pl.BlockSpec((bb, N_HEADS, DK, DV), lambda i: (i, 0, 0, 0)),
        _const_spec(cnt_in.shape),
    ] + [_const_spec(w.shape) for w in wts] + [_const_spec(tb.shape) for tb in tables]
    out_specs = [
        tok_spec(D_MODEL),
        pl.BlockSpec((m * SUBLANES, LANES), lambda i: (i, 0)),
        pl.BlockSpec((SUBLANES, m), lambda i: (0, i)),
        tok_spec(D_LRU),
        tok_spec(D_LRU),
        pl.BlockSpec((bb, N_HEADS, DK, DV), lambda i: (i, 0, 0, 0)),
        _const_spec(cnt_in.shape),
    ]
    out_shape = [
        jax.ShapeDtypeStruct((n_tok, D_MODEL), F32),
        jax.ShapeDtypeStruct((n_tok * SUBLANES, LANES), F32),
        jax.ShapeDtypeStruct((SUBLANES, n_tok), F32),
        jax.ShapeDtypeStruct((n_tok, D_LRU), F32),
        jax.ShapeDtypeStruct((n_tok, D_LRU), F32),
        jax.ShapeDtypeStruct((bsz, N_HEADS, DK, DV), F32),
        jax.ShapeDtypeStruct(cnt_in.shape, F32),
    ]
    return pl.pallas_call(
        _sample_mixer_kernel,
        grid=(bsz // bb,),
        in_specs=in_specs,
        out_specs=out_specs,
        out_shape=out_shape,
        scratch_shapes=[pltpu.VMEM((SUBLANES, LANES), F32)],
        compiler_params=pltpu.CompilerParams(
            dimension_semantics=("arbitrary",), vmem_limit_bytes=VMEM_LIMIT),
        name="sample_mixer",
    )(x, mod3, cos2, sin2, buf8, h0p, s0, cnt_in, *wts, *tables)


def _slab(ref, r):
    return ref.at[pl.ds(pl.multiple_of(r * SUBLANES, SUBLANES), SUBLANES)]


def _dispatch_kernel(dst_ref, cnt_ref, padcnt_ref, start_ref, srcp_ref, srcs_ref, out_ref, ring, sem,
                     *, p_tiles):
    i = pl.program_id(0)
    n = pl.num_programs(0)
    tile_rows = ring.shape[1]
    td = tile_rows // SUBLANES

    def issue(slot):
        def body(g, carry):
            first = g * DMA_GROUP
            slots = [dst_ref[i * td + first + j] for j in range(DMA_GROUP)]
            for j in range(DMA_GROUP):
                pltpu.make_async_copy(
                    _slab(ring.at[slot], first + j), _slab(out_ref, slots[j]), sem.at[slot]
                ).start(priority=j % 2)
            return carry
        lax.fori_loop(0, td // DMA_GROUP, body, 0)

    def wait_tile(slot):
        pltpu.make_async_copy(ring.at[slot], out_ref.at[pl.ds(0, tile_rows)], sem.at[slot]).wait()

    def zero_fill(slot):
        ring[slot] = jnp.zeros((tile_rows, LANES), F32)
        zero_src = ring.at[slot]

        def per_class(c, carry):
            lo = start_ref[c] + cnt_ref[c]
            hi = start_ref[c] + padcnt_ref[c]

            def fill(r, carry2):
                pltpu.make_async_copy(_slab(zero_src, 0), _slab(out_ref, r), sem.at[slot]).start()
                return carry2
            lax.fori_loop(lo, hi, fill, 0)

            def done(r, carry2):
                pltpu.make_async_copy(_slab(zero_src, 0), _slab(out_ref, 0), sem.at[slot]).wait()
                return carry2
            lax.fori_loop(lo, hi, done, 0)
            return carry
        lax.fori_loop(0, N_CLASSES, per_class, 0)

        used_tiles = (start_ref[N_CLASSES - 1] + padcnt_ref[N_CLASSES - 1]) // td
        all_tiles = out_ref.shape[0] // tile_rows

        def tile_copy(t):
            return pltpu.make_async_copy(
                zero_src, out_ref.at[pl.ds(pl.multiple_of(t * tile_rows, tile_rows), tile_rows)], sem.at[slot])

        def fill_tile(t, carry):
            tile_copy(t).start()
            return carry
        lax.fori_loop(used_tiles, all_tiles, fill_tile, 0)

        def done_tile(t, carry):
            tile_copy(t).wait()
            return carry
        lax.fori_loop(used_tiles, all_tiles, done_tile, 0)

    for s in range(RING):
        @pl.when(i % RING == s)
        def _():
            @pl.when(i < p_tiles)
            def _():
                ring[s] = srcp_ref[...]

            @pl.when(i >= p_tiles)
            def _():
                ring[s] = srcs_ref[...]

            issue(s)

            @pl.when(i >= RING - 1)
            def _():
                wait_tile((s + 1) % RING)

            @pl.when(i == n - 1)
            def _():
                for back in range(RING - 2, -1, -1):
                    wait_tile((s - back) % RING)
                zero_fill(s)


def _dispatch_call(dst, cnt, padcnt, start, pk_p, pk_s, n_rows):
    td = MOE_TILE
    tile_rows = td * SUBLANES
    p_tiles = pk_p.shape[0] // tile_rows
    s_tiles = pk_s.shape[0] // tile_rows
    assert p_tiles + s_tiles >= RING
    return pl.pallas_call(
        functools.partial(_dispatch_kernel, p_tiles=p_tiles),
        grid_spec=pltpu.PrefetchScalarGridSpec(
            num_scalar_prefetch=4,
            grid=(p_tiles + s_tiles,),
            in_specs=[
                pl.BlockSpec((tile_rows, LANES), lambda i, *_: (jnp.minimum(i, p_tiles - 1), 0)),
                pl.BlockSpec((tile_rows, LANES), lambda i, *_: (jnp.maximum(i - p_tiles, 0), 0)),
            ],
            out_specs=pl.BlockSpec(memory_space=pl.ANY),
            scratch_shapes=[pltpu.VMEM((RING, tile_rows, LANES), F32),
                            pltpu.SemaphoreType.DMA((RING,))],
        ),
        out_shape=jax.ShapeDtypeStruct((n_rows * SUBLANES, LANES), F32),
        compiler_params=pltpu.CompilerParams(
            dimension_semantics=("arbitrary",), has_side_effects=True, vmem_limit_bytes=VMEM_LIMIT),
        name="moe_dispatch",
    )(dst, cnt, padcnt, start, pk_p, pk_s)


def _moe_kernel(ea_ref, eb_ref, valid_ref, xs_ref, w_router_ref, b_router_ref,
                wga_ref, wua_ref, wda_ref, wgb_ref, wub_ref, wdb_ref, f_ref):
    t = pl.program_id(0)

    @pl.when(valid_ref[t] == 1)
    def _():
        tm = xs_ref.shape[0] // SUBLANES
        x = jnp.concatenate([xs_ref[pl.ds(j, tm, stride=SUBLANES), :] for j in range(ROW_GROUPS)],
                            axis=1).astype(BF16)
        e_a = ea_ref[t]
        e_b = eb_ref[t]
        e_lo = (e_a // PER_GROUP) * PER_GROUP
        logits = jnp.dot(x, w_router_ref[...], preferred_element_type=F32) + b_router_ref[...]
        lane = lax.broadcasted_iota(jnp.int32, (tm, LANES), 1)
        pick = lambda p, idx: jnp.sum(jnp.where(lane == idx, p, 0.0), axis=-1, keepdims=True)
        p_group = _masked_softmax(logits, (lane >= N_EXPERTS) & (lane < N_EXPERTS + N_GROUPS))
        p_g = pick(p_group, N_EXPERTS + e_a // PER_GROUP)
        p_e = _masked_softmax(logits, (lane >= e_lo) & (lane < e_lo + PER_GROUP))
        w_a = pick(p_e, e_a)
        w_b = pick(p_e, e_b)
        wsum = w_a + w_b

        def expert(wg_ref, wu_ref, gate):
            hg = jnp.dot(x, wg_ref[0], preferred_element_type=F32)
            hu = jnp.dot(x, wu_ref[0], preferred_element_type=F32)
            return (_silu(hg) * hu * gate).astype(BF16)

        ha = expert(wga_ref, wua_ref, p_g * (w_a / wsum))
        hb = expert(wgb_ref, wub_ref, p_g * (w_b / wsum))
        f = (jnp.dot(ha, wda_ref[0], preferred_element_type=F32)
             + jnp.dot(hb, wdb_ref[0], preferred_element_type=F32))
        for j in range(D_MODEL // LANES):
            f_ref[pl.ds(j, tm, stride=SUBLANES), :] = f[:, j * LANES:(j + 1) * LANES]

    @pl.when(valid_ref[t] == 0)
    def _():
        f_ref[...] = jnp.zeros_like(f_ref)


def _moe_call(tile_ea, tile_eb, tile_valid, rows, w_router, b_router, wg, wu, wd):
    n_rows = rows.shape[0] // SUBLANES
    tm = MOE_TILE
    const = lambda a: pl.BlockSpec(a.shape, lambda t, ea, eb, v: (0,) * a.ndim)
    up = lambda sel: pl.BlockSpec((1, D_MODEL, D_EXPERT), lambda t, ea, eb, v: (sel(ea, eb)[t], 0, 0))
    down = lambda sel: pl.BlockSpec((1, D_EXPERT, D_MODEL), lambda t, ea, eb, v: (sel(ea, eb)[t], 0, 0))
    first = lambda ea, eb: ea
    second = lambda ea, eb: eb
    return pl.pallas_call(
        _moe_kernel,
        grid_spec=pltpu.PrefetchScalarGridSpec(
            num_scalar_prefetch=3,
            grid=(n_rows // tm,),
            in_specs=[
                pl.BlockSpec((tm * SUBLANES, LANES), lambda t, ea, eb, v: (t, 0)),
                const(w_router), const(b_router),
                up(first), up(first), down(first), up(second), up(second), down(second),
            ],
            out_specs=pl.BlockSpec((tm * SUBLANES, LANES), lambda t, ea, eb, v: (t, 0)),
        ),
        out_shape=jax.ShapeDtypeStruct((n_rows * SUBLANES, LANES), F32),
        compiler_params=pltpu.CompilerParams(
            dimension_semantics=("arbitrary",), vmem_limit_bytes=VMEM_LIMIT),
        name="moe_experts",
    )(tile_ea, tile_eb, tile_valid, rows, w_router, b_router, wg, wu, wd, wg, wu, wd)


def _combine_kernel(dst_ref, x1_ref, mod_ref, n_post2_ref, f_hbm, o_ref, fbuf, sem):
    i = pl.program_id(0)
    n = pl.num_programs(0)
    tf = x1_ref.shape[0]

    def issue(tile, slot):
        def body(g, carry):
            first = g * DMA_GROUP
            slots = [dst_ref[tile * tf + first + j] for j in range(DMA_GROUP)]
            for j in range(DMA_GROUP):
                pltpu.make_async_copy(
                    _slab(f_hbm, slots[j]), _slab(fbuf.at[slot], first + j), sem.at[slot]
                ).start(priority=j % 2)
            return carry
        lax.fori_loop(0, tf // DMA_GROUP, body, 0)

    def finish(slot):
        pltpu.make_async_copy(f_hbm.at[pl.ds(0, tf * SUBLANES)], fbuf.at[slot], sem.at[slot]).wait()
        f = jnp.concatenate([fbuf[slot, pl.ds(j, tf, stride=SUBLANES), :] for j in range(D_MODEL // LANES)],
                            axis=1)
        mod3 = mod_ref[...]
        bb = mod3.shape[0]
        f3 = f.reshape(bb, tf // bb, D_MODEL)
        g2 = mod3[:, :, 5 * D_MODEL:6 * D_MODEL]
        x13 = x1_ref[...].reshape(f3.shape)
        out = x13 + (f3 * _rms_scale(f3)) * (g2 * n_post2_ref[...].reshape(1, 1, D_MODEL))
        o_ref[...] = out.reshape(o_ref.shape)

    @pl.when(i == 0)
    def _():
        issue(0, 0)

    for parity in range(2):
        @pl.when(i % 2 == parity)
        def _():
            @pl.when(i + 1 < n)
            def _():
                issue(i + 1, 1 - parity)
            finish(parity)


def _combine_call(dst, x1, mod3, n_post2, f_sorted, seqs_per_tile):
    n_tok = x1.shape[0]
    tf = FINAL_TILE
    tok_per_seq = n_tok // mod3.shape[0]
    if seqs_per_tile > 1:
        mod_map = lambda i, d: (i, 0, 0)
    else:
        mod_map = lambda i, d: ((i * tf) // tok_per_seq, 0, 0)
    return pl.pallas_call(
        _combine_kernel,
        grid_spec=pltpu.PrefetchScalarGridSpec(
            num_scalar_prefetch=1,
            grid=(n_tok // tf,),
            in_specs=[
                pl.BlockSpec((tf, D_MODEL), lambda i, d: (i, 0)),
                pl.BlockSpec((seqs_per_tile, 1, 6 * D_MODEL), mod_map),
                pl.BlockSpec((1, D_MODEL), lambda i, d: (0, 0)),
                pl.BlockSpec(memory_space=pl.ANY),
            ],
            out_specs=pl.BlockSpec((tf, D_MODEL), lambda i, d: (i, 0)),
            scratch_shapes=[pltpu.VMEM((2, tf * SUBLANES, LANES), F32), pltpu.SemaphoreType.DMA((2,))],
        ),
        out_shape=jax.ShapeDtypeStruct((n_tok, D_MODEL), F32),
        compiler_params=pltpu.CompilerParams(
            dimension_semantics=("arbitrary",), vmem_limit_bytes=VMEM_LIMIT),
        name="moe_combine",
    )(dst, x1, mod3, n_post2, f_sorted)


def _block_diag_gate(w_r, w_i):
    per_half = GATE_HALF // LRU_BLOCK
    halves = []
    for hb in range(D_LRU // GATE_HALF):
        blocks = []
        for w in (w_r, w_i):
            mat = jnp.zeros((GATE_HALF, GATE_HALF), F32)
            for n in range(per_half):
                lo = n * LRU_BLOCK
                mat = mat.at[lo:lo + LRU_BLOCK, lo:lo + LRU_BLOCK].set(w[hb * per_half + n])
            blocks.append(mat)
        halves.append(jnp.concatenate(blocks, axis=1))
    return jnp.stack(halves).astype(BF16)


def _rope_tables(pos):
    half = DK // 2
    inv = ROPE_BASE ** (-jnp.arange(half, dtype=F32) / half)
    ang = pos[:, None] * inv[None, :]
    cos = jnp.cos(ang)
    sin = jnp.sin(ang)
    return jnp.concatenate([cos, cos], axis=-1), jnp.concatenate([-sin, sin], axis=-1)


def _decay_tables(c):
    log_g = jnp.log1p(-jnp.exp2(-5.0 - jnp.arange(N_HEADS, dtype=F32)))
    idx = jnp.arange(c, dtype=F32)
    diff = idx[:, None] - idx[None, :]
    mask = jnp.where(diff[None] >= 0, jnp.exp(jnp.maximum(diff, 0.0)[None] * log_g[:, None, None]), 0.0)
    w_state = jnp.exp((c - 1.0 - idx)[None, :] * log_g[:, None])
    cross_decay = jnp.exp((idx + 1.0)[:, None] * log_g[None, :])
    chunk_decay = jnp.exp(c * log_g)
    wstate_full = jnp.repeat(w_state.T, DK, axis=1)
    cross_full = jnp.repeat(cross_decay, DV, axis=1)
    cdecay_full = jnp.repeat(chunk_decay, DV)[None, :]
    return mask, wstate_full, cross_full, cdecay_full


def kernel(x_prompt, x_sample, state_conv, state_lru, state_ret, c_prompt, c_sample, w_mod, b_mod, norm_pre_mix, norm_post_mix, norm_pre_ffn, norm_post_ffn, w_in, conv_w, conv_b, w_rgate, b_rgate, w_igate, b_igate, lru_lambda, ret_gn_w, w_out, w_router_group, b_router_group, w_router_expert, b_router_expert, w_exp_gate, w_exp_up, w_exp_down):
    bp, tp, _ = x_prompt.shape
    bs, ts, _ = x_sample.shape
    l = 0

    mod = _mod_call(jnp.concatenate([c_prompt, c_sample], axis=0), w_mod[l], b_mod[l][None, :])
    mod_p = mod[:bp][:, None, :]
    mod_s = mod[bp:][:, None, :]

    w_router = jnp.zeros((D_MODEL, LANES), F32)
    w_router = w_router.at[:, :N_EXPERTS].set(w_router_expert[l])
    w_router = w_router.at[:, N_EXPERTS:N_EXPERTS + N_GROUPS].set(w_router_group[l]).astype(BF16)
    b_router = jnp.zeros((1, LANES), F32)
    b_router = b_router.at[0, :N_EXPERTS].set(b_router_expert[l])
    b_router = b_router.at[0, N_EXPERTS:N_EXPERTS + N_GROUPS].set(b_router_group[l])

    row = lambda vec: vec.reshape(1, -1)
    wts = (row(norm_pre_mix[l]), row(norm_post_mix[l]), row(norm_pre_ffn[l]),
           w_in[l].astype(BF16), conv_w[l], row(conv_b[l]),
           _block_diag_gate(w_rgate[l], w_igate[l]),
           row(b_rgate[l]), row(b_igate[l]), row(lru_lambda[l]), row(ret_gn_w[l]),
           w_out[l].astype(BF16), w_router, b_router)

    cos_p, sin_p = _rope_tables(jnp.arange(tp, dtype=F32))
    x1_p, pk_p, route_p, conv_p8, lru_p8, ret_p, cnt_p = _prompt_mixer_call(
        x_prompt, mod_p, cos_p, sin_p, wts, _decay_tables(math.gcd(tp, RET_CHUNK)))

    cos_s, sin_s = _rope_tables(jnp.float32(PAST_LEN) + jnp.arange(ts, dtype=F32))
    mask8, wstate_s, cross_s, cdecay_s = _decay_tables(math.gcd(ts, RET_CHUNK))
    eye = jnp.eye(SAMPLE_SEQS, dtype=F32)
    smask = jnp.stack([jnp.kron(eye, mask8[h]) for h in range(N_HEADS)])
    buf8 = jnp.pad(state_conv[l], ((0, 0), (0, ts - (CONV_W - 1)), (0, 0)))
    h0p = jnp.pad(state_lru[l][:, None, :], ((0, 0), (0, ts - 1), (0, 0)))
    x1_s, pk_s, route_s, xr_s, h_s, ret_s, cnt_all = _sample_mixer_call(
        x_sample, mod_s, cos_s, sin_s, buf8, h0p, state_ret[l], cnt_p, wts,
        (smask, wstate_s, cross_s, cdecay_s))

    n_p = bp * tp
    n_tok = n_p + bs * ts
    tm = MOE_TILE
    max_tiles = n_tok // tm + N_CLASSES
    route = jnp.concatenate([route_p[0:2], route_s[0:2]], axis=1).astype(jnp.int32)
    cls = route[0]
    rank = route[1]
    cnt = cnt_all[0, :N_CLASSES].astype(jnp.int32)
    ntile = (cnt + (tm - 1)) // tm
    padcnt = ntile * tm
    start = jnp.cumsum(padcnt) - padcnt
    dst = start[cls] + rank
    tile_end = jnp.cumsum(ntile)
    tile_ids = jnp.arange(max_tiles, dtype=jnp.int32)
    tile_valid = (tile_ids < tile_end[-1]).astype(jnp.int32)
    last_used = jnp.minimum(tile_ids, tile_end[-1] - 1)
    tile_cls = jnp.sum((last_used[:, None] >= tile_end[None, :]).astype(jnp.int32), axis=1)
    pair_a = jnp.array([0, 0, 0, 1, 1, 2], jnp.int32)
    pair_b = jnp.array([1, 2, 3, 2, 3, 3], jnp.int32)
    tile_ea = (tile_cls // N_PAIRS) * PER_GROUP + pair_a[tile_cls % N_PAIRS]
    tile_eb = (tile_cls // N_PAIRS) * PER_GROUP + pair_b[tile_cls % N_PAIRS]

    n_post2 = row(norm_post_ffn[l])
    wg = w_exp_gate[l].astype(BF16)
    wu = w_exp_up[l].astype(BF16)
    wd = w_exp_down[l].astype(BF16)
    rows = _dispatch_call(dst, cnt, padcnt, start, pk_p, pk_s, max_tiles * tm)
    f_sorted = _moe_call(tile_ea, tile_eb, tile_valid, rows, w_router, b_router, wg, wu, wd)
    y_p = _combine_call(dst[:n_p], x1_p, mod_p, n_post2, f_sorted, 1)
    y_s = _combine_call(dst[n_p:], x1_s, mod_s, n_post2, f_sorted, FINAL_TILE // ts)

    conv_p = conv_p8[:, SUBLANES - (CONV_W - 1):, :]
    lru_p = lru_p8[:, SUBLANES - 1, :]
    xr_s3 = xr_s.reshape(bs, ts, D_LRU)
    conv_s = xr_s3[:, ts - (CONV_W - 1):, :]
    lru_s = h_s.reshape(bs, ts, D_LRU)[:, ts - 1, :]
    return (y_p.reshape(bp, tp, D_MODEL), y_s.reshape(bs, ts, D_MODEL),
            conv_p[None], lru_p[None], ret_p[None],
            conv_s[None], lru_s[None], ret_s[None])
```

```python
import functools
import math

import jax
import jax.numpy as jnp
from jax import lax
from jax.experimental import pallas as pl
from jax.experimental.pallas import tpu as pltpu

F32 = jnp.float32
BF16 = jnp.bfloat16

D_MODEL = 1024
D_LRU = 512
D_RET = 512
N_LRU_BLOCKS = 8
LRU_BLOCK = D_LRU // N_LRU_BLOCKS
CONV_W = 4
LRU_C = 8.0
N_HEADS = 4
DK = 128
DV = 128
RET_CHUNK = 128
ROPE_BASE = 10000.0
D_IN_PROJ = 3072
N_GROUPS = 4
PER_GROUP = 4
N_EXPERTS = 16
D_EXPERT = 256
EXPM1_DIRECT_BELOW = -0.5
NORM_EPS = 1e-6
GN_EPS = 1e-5
PAST_LEN = 16384

SUBLANES = 8
LANES = 128
GATE_HALF = 256
VMEM_LIMIT = 56 * 1024 * 1024

PROMPT_TILE = 512
SAMPLE_SEQS = 16
MOE_TILE = 256
FINAL_TILE = 256
RING = 3
DMA_GROUP = 8

N_PAIRS = 6
N_CLASSES = N_GROUPS * N_PAIRS
ROW_GROUPS = D_MODEL // LANES


def _silu(x):
    return x * jax.nn.sigmoid(x)


def _rms_scale(x):
    return lax.rsqrt(jnp.mean(x * x, axis=-1, keepdims=True) + NORM_EPS)


def _masked_softmax(logits, mask):
    top = jnp.max(jnp.where(mask, logits, -jnp.inf), axis=-1, keepdims=True)
    e = jnp.where(mask, jnp.exp(logits - top), 0.0)
    return e / jnp.sum(e, axis=-1, keepdims=True)


def _mod_kernel(c_ref, w_ref, b_ref, o_ref):
    s = _silu(c_ref[...]).astype(BF16)
    o_ref[...] = jnp.dot(s, w_ref[...].astype(BF16), preferred_element_type=F32) + b_ref[...]


def _mod_call(c_all, w_mod, b_mod):
    rows = c_all.shape[0]
    ncol = w_mod.shape[1]
    blk = 1024
    return pl.pallas_call(
        _mod_kernel,
        grid=(ncol // blk,),
        in_specs=[
            pl.BlockSpec((rows, D_MODEL), lambda j: (0, 0)),
            pl.BlockSpec((D_MODEL, blk), lambda j: (0, j)),
            pl.BlockSpec((1, blk), lambda j: (0, j)),
        ],
        out_specs=pl.BlockSpec((rows, blk), lambda j: (0, j)),
        out_shape=jax.ShapeDtypeStruct((rows, ncol), F32),
        compiler_params=pltpu.CompilerParams(
            dimension_semantics=("arbitrary",), vmem_limit_bytes=VMEM_LIMIT),
        name="mod",
    )(c_all, w_mod, b_mod)


def _in_proj(x3, mod3, n_pre1_ref, w_in_ref):
    bb, tt, _ = x3.shape
    sh1 = mod3[:, :, 0:D_MODEL]
    sc1 = mod3[:, :, D_MODEL:2 * D_MODEL]
    coef = n_pre1_ref[...].reshape(1, 1, D_MODEL) * (1.0 + sc1)
    u = (x3 * _rms_scale(x3)) * coef + sh1
    u2d = u.reshape(bb * tt, D_MODEL).astype(BF16)
    return [jnp.dot(u2d, w_in_ref[:, c * D_LRU:(c + 1) * D_LRU], preferred_element_type=F32)
            for c in range(D_IN_PROJ // D_LRU)]


def _lru_coeffs(xc, wg_ref, b_r_ref, b_i_ref, lam_ref):
    xcb = xc.astype(BF16)
    g0 = jnp.dot(xcb[:, :GATE_HALF], wg_ref[0], preferred_element_type=F32)
    g1 = jnp.dot(xcb[:, GATE_HALF:], wg_ref[1], preferred_element_type=F32)
    r = jax.nn.sigmoid(jnp.concatenate([g0[:, :GATE_HALF], g1[:, :GATE_HALF]], axis=1) + b_r_ref[...])
    i = jax.nn.sigmoid(jnp.concatenate([g0[:, GATE_HALF:], g1[:, GATE_HALF:]], axis=1) + b_i_ref[...])
    lam = lam_ref[...]
    sp = jnp.maximum(-lam, 0.0) + jnp.log1p(jnp.exp(-jnp.abs(lam)))
    log_a = -LRU_C * r * sp
    a = jnp.exp(log_a)
    y = 2.0 * log_a
    a2 = a * a
    d = a2 - 1.0
    small = d * y / jnp.log(a2)
    em1 = jnp.where(y < EXPM1_DIRECT_BELOW, d, jnp.where(d == 0.0, y, small))
    gain = jnp.sqrt(-em1)
    return a, gain * (i * xc)


def _rope(xh, cos2, sin2, lane_axis):
    return xh * cos2 + pltpu.roll(xh, DK // 2, axis=lane_axis) * sin2


def _group_norm(o):
    mu = jnp.mean(o, axis=-1, keepdims=True)
    d = o - mu
    var = jnp.mean(d * d, axis=-1, keepdims=True)
    return d * lax.rsqrt(var + GN_EPS)


def _post_mixer(x3, mod3, out_a, out_b, w_out_ref, n_post1_ref, n_pre2_ref, w_router_ref, b_router_ref,
                x1_ref, pk_ref, route_ref, tilecnt_ref, cnt_scr):
    bb, tt, _ = x3.shape
    m = bb * tt
    y = (jnp.dot(out_a.astype(BF16), w_out_ref[0:D_LRU, :], preferred_element_type=F32)
         + jnp.dot(out_b.astype(BF16), w_out_ref[D_LRU:, :], preferred_element_type=F32))
    g1 = mod3[:, :, 2 * D_MODEL:3 * D_MODEL]
    sh2 = mod3[:, :, 3 * D_MODEL:4 * D_MODEL]
    sc2 = mod3[:, :, 4 * D_MODEL:5 * D_MODEL]
    y3 = y.reshape(bb, tt, D_MODEL)
    x1 = x3 + (y3 * _rms_scale(y3)) * (g1 * n_post1_ref[...].reshape(1, 1, D_MODEL))
    u2 = (x1 * _rms_scale(x1)) * (n_pre2_ref[...].reshape(1, 1, D_MODEL) * (1.0 + sc2)) + sh2
    x1_ref[...] = x1.reshape(m, D_MODEL)
    u2b = u2.reshape(m, D_MODEL).astype(BF16)

    logits = jnp.dot(u2b, w_router_ref[...], preferred_element_type=F32) + b_router_ref[...]
    lane = lax.broadcasted_iota(jnp.int32, (m, LANES), 1)
    is_g = (lane >= N_EXPERTS) & (lane < N_EXPERTS + N_GROUPS)
    p_group = _masked_softmax(logits, is_g)
    p_g = jnp.max(p_group, axis=-1, keepdims=True)
    g_lane = jnp.min(jnp.where(is_g & (p_group == p_g), lane, LANES), axis=-1, keepdims=True)
    e_lo = (g_lane - N_EXPERTS) * PER_GROUP
    in_g = (lane >= e_lo) & (lane < e_lo + PER_GROUP)
    p_e = _masked_softmax(logits, in_g)
    pm = jnp.where(in_g, p_e, -1.0)
    w1 = jnp.max(pm, axis=-1, keepdims=True)
    i1 = jnp.min(jnp.where(pm == w1, lane, LANES), axis=-1, keepdims=True)
    pm2 = jnp.where(lane == i1, -1.0, pm)
    w2 = jnp.max(pm2, axis=-1, keepdims=True)
    i2 = jnp.min(jnp.where(pm2 == w2, lane, LANES), axis=-1, keepdims=True)
    a = jnp.minimum(i1, i2) - e_lo
    b = jnp.maximum(i1, i2) - e_lo
    pair = jnp.where(a == 0, b - 1, jnp.where(a == 1, b + 1, 5))
    cls = (g_lane - N_EXPERTS) * N_PAIRS + pair
    onehot = lane == cls
    r_i = lax.broadcasted_iota(jnp.int32, (m, m), 0)
    c_i = lax.broadcasted_iota(jnp.int32, (m, m), 1)
    earlier = jnp.where(r_i > c_i, 1.0, 0.0).astype(BF16)
    prefix = jnp.dot(earlier, jnp.where(onehot, 1.0, 0.0).astype(BF16), preferred_element_type=F32)
    run = cnt_scr[0:1, :]
    rank = jnp.sum(jnp.where(onehot, prefix + run, 0.0), axis=-1, keepdims=True)
    tile_cnt = jnp.broadcast_to(jnp.sum(jnp.where(onehot, 1.0, 0.0), axis=0, keepdims=True), cnt_scr.shape)
    cnt_scr[...] = run + tile_cnt
    tilecnt_ref[...] = tile_cnt
    lane8 = lax.broadcasted_iota(jnp.int32, tile_cnt.shape, 1)
    below = tile_cnt
    s = 1
    while s < LANES:
        below = below + jnp.where(lane8 >= s, pltpu.roll(below, s, axis=1), 0.0)
        s *= 2
    below = below - tile_cnt
    local = jnp.sum(jnp.where(onehot, prefix + below[0:1, :], 0.0), axis=-1, keepdims=True)
    route = jnp.where(lane == 0, cls.astype(F32),
                      jnp.where(lane == 1, rank, jnp.where(lane == 2, local, 0.0)))
    route_t = jnp.transpose(route)
    route_ref[...] = route_t[0:SUBLANES, :]
    place = jnp.where(r_i == route_t[2:3, :].astype(jnp.int32), 1.0, 0.0).astype(BF16)
    u2s = jnp.dot(place, u2b, preferred_element_type=F32)
    for j in range(ROW_GROUPS):
        pk_ref[pl.ds(j, m, stride=SUBLANES), :] = u2s[:, j * LANES:(j + 1) * LANES]


def _group_scan(a3, b3):
    tpos = lax.broadcasted_iota(jnp.int32, a3.shape, 1)
    s = 1
    while s < a3.shape[1]:
        keep = tpos >= s
        a_sh = jnp.where(keep, pltpu.roll(a3, s, axis=1), 1.0)
        b_sh = jnp.where(keep, pltpu.roll(b3, s, axis=1), 0.0)
        b3 = a3 * b_sh + b3
        a3 = a3 * a_sh
        s *= 2
    return a3, b3


def _scan_rows(a, b, h0):
    n, c = a.shape
    groups = n // SUBLANES
    a3, b3 = _group_scan(a.reshape(groups, SUBLANES, c), b.reshape(groups, SUBLANES, c))
    carry = h0
    out = []
    for g in range(groups):
        hg = b3[g] + a3[g] * carry
        out.append(hg)
        carry = hg[SUBLANES - 1:SUBLANES, :]
    return jnp.concatenate(out, axis=0)


def _prompt_mixer_kernel(x_ref, mod_ref, cos_ref, sin_ref,
                         n_pre1_ref, n_post1_ref, n_pre2_ref,
                         w_in_ref, conv_w_ref, conv_b_ref, wg_ref, b_r_ref, b_i_ref, lam_ref,
                         gn_w_ref, w_out_ref, w_router_ref, b_router_ref,
                         mask_ref, wstate_ref, cross_ref, cdecay_ref,
                         x1_ref, pk_ref, route_ref, tilecnt_ref, conv_out_ref, lru_out_ref, ret_out_ref,
                         cnt_out_ref,
                         conv_scr, h_scr, s_scr, cnt_scr):
    t = pl.program_id(1)
    tt = x_ref.shape[1]

    @pl.when((pl.program_id(0) == 0) & (t == 0))
    def _():
        cnt_scr[...] = jnp.zeros_like(cnt_scr)

    @pl.when(t == 0)
    def _():
        conv_scr[...] = jnp.zeros_like(conv_scr)
        h_scr[...] = jnp.zeros_like(h_scr)
        s_scr[...] = jnp.zeros_like(s_scr)

    x3 = x_ref[...]
    mod3 = mod_ref[...]
    xr, yg, q, k, v, g = _in_proj(x3, mod3, n_pre1_ref, w_in_ref)

    groups = tt // SUBLANES
    xr3 = xr.reshape(groups, SUBLANES, D_LRU)
    tpos = lax.broadcasted_iota(jnp.int32, xr3.shape, 1)
    tail = conv_scr[...]
    xc3 = jnp.broadcast_to(conv_b_ref[...].reshape(1, 1, D_LRU), xr3.shape)
    for j in range(CONV_W):
        back = CONV_W - 1 - j
        w_j = conv_w_ref[j:j + 1, :].reshape(1, 1, D_LRU)
        if back == 0:
            term = xr3
        else:
            cur = pltpu.roll(xr3, back, axis=1)
            first = pltpu.roll(tail, back, axis=0).reshape(1, SUBLANES, D_LRU)
            prev = jnp.concatenate([first, cur[:groups - 1]], axis=0)
            term = jnp.where(tpos >= back, cur, prev)
        xc3 = xc3 + term * w_j
    xc = xc3.reshape(tt, D_LRU)
    conv_scr[...] = xr[tt - SUBLANES:, :]

    a, b = _lru_coeffs(xc, wg_ref, b_r_ref, b_i_ref, lam_ref)
    hseq = _scan_rows(a, b, h_scr[0:1, :])
    h_scr[...] = jnp.broadcast_to(hseq[tt - 1:tt, :], h_scr.shape)
    out_a = hseq * jax.nn.gelu(yg, approximate=True)

    cos2 = cos_ref[...]
    sin2 = sin_ref[...]
    scale = DK ** -0.5
    o_heads = []
    for h in range(N_HEADS):
        hs = slice(h * DK, (h + 1) * DK)
        qh = (_rope(q[:, hs], cos2, sin2, 1) * scale).astype(BF16)
        kh = _rope(k[:, hs], cos2, sin2, 1)
        vh = v[:, hs].astype(BF16)
        o_chunks = []
        for c in range(tt // RET_CHUNK):
            cs = slice(c * RET_CHUNK, (c + 1) * RET_CHUNK)
            qc = qh[cs]
            kc = kh[cs]
            vc = vh[cs]
            s_prev = s_scr[h]
            scores = lax.dot_general(qc, kc.astype(BF16), (((1,), (1,)), ((), ())),
                                     preferred_element_type=F32) * mask_ref[h]
            inner = jnp.dot(scores.astype(BF16), vc, preferred_element_type=F32)
            cross = jnp.dot(qc, s_prev.astype(BF16), preferred_element_type=F32) * cross_ref[:, hs]
            kw = (kc * wstate_ref[:, hs]).astype(BF16)
            kv = lax.dot_general(kw, vc, (((0,), (0,)), ((), ())), preferred_element_type=F32)
            s_scr[h] = cdecay_ref[:, hs] * s_prev + kv
            o_chunks.append(inner + cross)
        o_heads.append(_group_norm(jnp.concatenate(o_chunks, axis=0)))
    o = jnp.concatenate(o_heads, axis=1)
    out_b = o * gn_w_ref[...] * _silu(g)

    _post_mixer(x3, mod3, out_a, out_b, w_out_ref, n_post1_ref, n_pre2_ref, w_router_ref, b_router_ref,
                x1_ref, pk_ref, route_ref, tilecnt_ref, cnt_scr)
    cnt_out_ref[...] = cnt_scr[...]

    @pl.when(t == pl.num_programs(1) - 1)
    def _():
        conv_out_ref[0] = xr[tt - SUBLANES:, :]
        lru_out_ref[0] = hseq[tt - SUBLANES:, :]
        ret_out_ref[0] = s_scr[...]


def _const_spec(shape):
    nd = len(shape)
    return pl.BlockSpec(shape, lambda *_: (0,) * nd)


def _prompt_mixer_call(x, mod3, cos2, sin2, wts, tables):
    bsz, seq, _ = x.shape
    tt = PROMPT_TILE
    nt = seq // tt
    n_tok = bsz * seq
    (n_pre1, n_post1, n_pre2, w_in, conv_w, conv_b, wg, b_r, b_i, lam, gn_w, w_out, w_router, b_router) = wts
    mask, wstate, cross, cdecay = tables
    tok_spec = pl.BlockSpec((tt, D_MODEL), lambda b, t: (b * nt + t, 0))
    in_specs = [
        pl.BlockSpec((1, tt, D_MODEL), lambda b, t: (b, t, 0)),
        pl.BlockSpec((1, 1, 6 * D_MODEL), lambda b, t: (b, 0, 0)),
        pl.BlockSpec((tt, LANES), lambda b, t: (t, 0)),
        pl.BlockSpec((tt, LANES), lambda b, t: (t, 0)),
    ] + [_const_spec(w.shape) for w in wts] + [_const_spec(tb.shape) for tb in tables]
    out_specs = [
        tok_spec,
        pl.BlockSpec((tt * SUBLANES, LANES), lambda b, t: (b * nt + t, 0)),
        pl.BlockSpec((SUBLANES, tt), lambda b, t: (0, b * nt + t)),
        pl.BlockSpec((SUBLANES, LANES), lambda b, t: (b * nt + t, 0)),
        pl.BlockSpec((1, SUBLANES, D_LRU), lambda b, t: (b, 0, 0)),
        pl.BlockSpec((1, SUBLANES, D_LRU), lambda b, t: (b, 0, 0)),
        pl.BlockSpec((1, N_HEADS, DK, DV), lambda b, t: (b, 0, 0, 0)),
        pl.BlockSpec((SUBLANES, LANES), lambda b, t: (0, 0)),
    ]
    out_shape = [
        jax.ShapeDtypeStruct((n_tok, D_MODEL), F32),
        jax.ShapeDtypeStruct((n_tok * SUBLANES, LANES), F32),
        jax.ShapeDtypeStruct((SUBLANES, n_tok), F32),
        jax.ShapeDtypeStruct((bsz * nt * SUBLANES, LANES), F32),
        jax.ShapeDtypeStruct((bsz, SUBLANES, D_LRU), F32),
        jax.ShapeDtypeStruct((bsz, SUBLANES, D_LRU), F32),
        jax.ShapeDtypeStruct((bsz, N_HEADS, DK, DV), F32),
        jax.ShapeDtypeStruct((SUBLANES, LANES), F32),
    ]
    return pl.pallas_call(
        _prompt_mixer_kernel,
        grid=(bsz, nt),
        in_specs=in_specs,
        out_specs=out_specs,
        out_shape=out_shape,
        scratch_shapes=[
            pltpu.VMEM((SUBLANES, D_LRU), F32),
            pltpu.VMEM((SUBLANES, D_LRU), F32),
            pltpu.VMEM((N_HEADS, DK, DV), F32),
            pltpu.VMEM((SUBLANES, LANES), F32),
        ],
        compiler_params=pltpu.CompilerParams(
            dimension_semantics=("arbitrary", "arbitrary"), vmem_limit_bytes=VMEM_LIMIT),
        name="prompt_mixer",
    )(x, mod3, cos2, sin2, *wts, *tables)


def _sample_mixer_kernel(x_ref, mod_ref, cos_ref, sin_ref, buf_ref, h0_ref, s0_ref, cnt_in_ref,
                         n_pre1_ref, n_post1_ref, n_pre2_ref,
                         w_in_ref, conv_w_ref, conv_b_ref, wg_ref, b_r_ref, b_i_ref, lam_ref,
                         gn_w_ref, w_out_ref, w_router_ref, b_router_ref,
                         smask_ref, wstate_ref, cross_ref, cdecay_ref,
                         x1_ref, pk_ref, route_ref, tilecnt_ref, xr_out_ref, h_out_ref, ret_out_ref,
                         cnt_out_ref,
                         cnt_scr):
    bb, ts, _ = x_ref.shape
    m = bb * ts

    @pl.when(pl.program_id(0) == 0)
    def _():
        cnt_scr[...] = cnt_in_ref[...]

    x3 = x_ref[...]
    mod3 = mod_ref[...]
    xr, yg, q, k, v, g = _in_proj(x3, mod3, n_pre1_ref, w_in_ref)
    xr_out_ref[...] = xr

    xr3 = xr.reshape(bb, ts, D_LRU)
    buf3 = buf_ref[...]
    tpos = lax.broadcasted_iota(jnp.int32, (bb, ts, D_LRU), 1)
    xc3 = jnp.broadcast_to(conv_b_ref[...].reshape(1, 1, D_LRU), (bb, ts, D_LRU))
    for j in range(CONV_W):
        back = CONV_W - 1 - j
        w_j = conv_w_ref[j:j + 1, :].reshape(1, 1, D_LRU)
        if back == 0:
            term = xr3
        else:
            cur = pltpu.roll(xr3, back, axis=1)
            up = CONV_W - 1 - back
            old = buf3 if up == 0 else pltpu.roll(buf3, ts - up, axis=1)
            term = jnp.where(tpos >= back, cur, old)
        xc3 = xc3 + term * w_j
    xc = xc3.reshape(m, D_LRU)

    a, b = _lru_coeffs(xc, wg_ref, b_r_ref, b_i_ref, lam_ref)
    a3 = a.reshape(bb, ts, D_LRU)
    b3 = b.reshape(bb, ts, D_LRU) + a3 * h0_ref[...]
    _, h3 = _group_scan(a3, b3)
    hseq = h3.reshape(m, D_LRU)
    h_out_ref[...] = hseq
    out_a = hseq * jax.nn.gelu(yg, approximate=True)

    cos2 = cos_ref[...].reshape(1, ts, LANES)
    sin2 = sin_ref[...].reshape(1, ts, LANES)
    scale = DK ** -0.5
    o_heads = []
    for h in range(N_HEADS):
        hs = slice(h * DK, (h + 1) * DK)
        q3 = (_rope(q[:, hs].reshape(bb, ts, DK), cos2, sin2, 2) * scale).astype(BF16)
        k3 = _rope(k[:, hs].reshape(bb, ts, DK), cos2, sin2, 2)
        v3 = v[:, hs].reshape(bb, ts, DV).astype(BF16)
        q2 = q3.reshape(m, DK)
        k2 = k3.reshape(m, DK).astype(BF16)
        v2 = v3.reshape(m, DV)
        scores = lax.dot_general(q2, k2, (((1,), (1,)), ((), ())),
                                 preferred_element_type=F32) * smask_ref[h]
        inner = jnp.dot(scores.astype(BF16), v2, preferred_element_type=F32)
        s0h = s0_ref[:, h]
        cross = jnp.einsum('bid,bde->bie', q3, s0h.astype(BF16), preferred_element_type=F32)
        cross = cross * cross_ref[:, hs].reshape(1, ts, DV)
        kw3 = (k3 * wstate_ref[:, hs].reshape(1, ts, DK)).astype(BF16)
        kv = jnp.einsum('bjd,bje->bde', kw3, v3, preferred_element_type=F32)
        ret_out_ref[:, h] = cdecay_ref[:, hs].reshape(1, 1, DV) * s0h + kv
        o_heads.append(_group_norm(inner + cross.reshape(m, DV)))
    o = jnp.concatenate(o_heads, axis=1)
    out_b = o * gn_w_ref[...] * _silu(g)

    _post_mixer(x3, mod3, out_a, out_b, w_out_ref, n_post1_ref, n_pre2_ref, w_router_ref, b_router_ref,
                x1_ref, pk_ref, route_ref, tilecnt_ref, cnt_scr)
    cnt_out_ref[...] = cnt_scr[...]


def _sample_mixer_call(x, mod3, cos2, sin2, buf8, h0p, s0, cnt_in, wts, tables):
    bsz, ts, _ = x.shape
    bb = SAMPLE_SEQS
    m = bb * ts
    n_tok = bsz * ts
    seq_spec = lambda w: pl.BlockSpec((bb, ts, w), lambda i: (i, 0, 0))
    tok_spec = lambda w: pl.BlockSpec((m, w), lambda i: (i, 0))
    in_specs = [
        seq_spec(D_MODEL),
        pl.BlockSpec((bb, 1, 6 * D_MODEL), lambda i: (i, 0, 0)),
        _const_spec(cos2.shape),
        _const_spec(sin2.shape),
        seq_spec(D_LRU),
        seq_spec(D_LRU),
        pl.BlockSpec((bb, N_HEADS, DK, DV), lambda i: (i, 0, 0, 0)),
        _const_spec(cnt_in.shape),
    ] + [_const_spec(w.shape) for w in wts] + [_const_spec(tb.shape) for tb in tables]
    out_specs = [
        tok_spec(D_MODEL),
        pl.BlockSpec((m * SUBLANES, LANES), lambda i: (i, 0)),
        pl.BlockSpec((SUBLANES, m), lambda i: (0, i)),
        pl.BlockSpec((SUBLANES, LANES), lambda i: (i, 0)),
        tok_spec(D_LRU),
        tok_spec(D_LRU),
        pl.BlockSpec((bb, N_HEADS, DK, DV), lambda i: (i, 0, 0, 0)),
        _const_spec(cnt_in.shape),
    ]
    out_shape = [
        jax.ShapeDtypeStruct((n_tok, D_MODEL), F32),
        jax.ShapeDtypeStruct((n_tok * SUBLANES, LANES), F32),
        jax.ShapeDtypeStruct((SUBLANES, n_tok), F32),
        jax.ShapeDtypeStruct((bsz // bb * SUBLANES, LANES), F32),
        jax.ShapeDtypeStruct((n_tok, D_LRU), F32),
        jax.ShapeDtypeStruct((n_tok, D_LRU), F32),
        jax.ShapeDtypeStruct((bsz, N_HEADS, DK, DV), F32),
        jax.ShapeDtypeStruct(cnt_in.shape, F32),
    ]
    return pl.pallas_call(
        _sample_mixer_kernel,
        grid=(bsz // bb,),
        in_specs=in_specs,
        out_specs=out_specs,
        out_shape=out_shape,
        scratch_shapes=[pltpu.VMEM((SUBLANES, LANES), F32)],
        compiler_params=pltpu.CompilerParams(
            dimension_semantics=("arbitrary",), vmem_limit_bytes=VMEM_LIMIT),
        name="sample_mixer",
    )(x, mod3, cos2, sin2, buf8, h0p, s0, cnt_in, *wts, *tables)


def _slab(ref, r):
    return ref.at[pl.ds(pl.multiple_of(r * SUBLANES, SUBLANES), SUBLANES)]


def _dispatch_kernel(len_ref, src_ref, dst_ref, cnt_ref, padcnt_ref, start_ref, srcp_ref, srcs_ref, out_ref,
                     ring, sem, *, p_tiles):
    i = pl.program_id(0)
    n = pl.num_programs(0)
    p_rows = srcp_ref.shape[0]
    s_rows = srcs_ref.shape[0]
    moe_rows = MOE_TILE * SUBLANES

    def rows(ref, first, count):
        return ref.at[pl.ds(pl.multiple_of(first * SUBLANES, SUBLANES), count * SUBLANES)]

    def issue(slot, tile_rows):
        lens = [len_ref[i * N_CLASSES + c] for c in range(N_CLASSES)]
        srcs = [src_ref[i * N_CLASSES + c] for c in range(N_CLASSES)]
        dsts = [dst_ref[i * N_CLASSES + c] for c in range(N_CLASSES)]
        pieces = []
        piece = tile_rows // SUBLANES
        while piece >= 1:
            pieces.append(piece)
            piece //= 2
        for c in range(N_CLASSES):
            done = 0
            for piece in pieces:
                take = lens[c] & piece

                @pl.when(take != 0)
                def _(done=done, piece=piece, c=c):
                    pltpu.make_async_copy(
                        rows(ring.at[slot], srcs[c] + done, piece), rows(out_ref, dsts[c] + done, piece),
                        sem.at[slot]).start(priority=c % 2)
                done = done + take

    def wait_tile(slot, step):
        @pl.when(step < p_tiles)
        def _():
            pltpu.make_async_copy(ring.at[slot], out_ref.at[pl.ds(0, p_rows)], sem.at[slot]).wait()

        @pl.when(step >= p_tiles)
        def _():
            pltpu.make_async_copy(ring.at[slot, pl.ds(0, s_rows)], out_ref.at[pl.ds(0, s_rows)],
                                  sem.at[slot]).wait()

    def zero_fill(slot):
        ring[slot] = jnp.zeros((p_rows, LANES), F32)
        zero_src = ring.at[slot]

        def per_class(c, carry):
            lo = start_ref[c] + cnt_ref[c]
            hi = start_ref[c] + padcnt_ref[c]

            def fill(r, carry2):
                pltpu.make_async_copy(_slab(zero_src, 0), _slab(out_ref, r), sem.at[slot]).start()
                return carry2
            lax.fori_loop(lo, hi, fill, 0)

            def done(r, carry2):
                pltpu.make_async_copy(_slab(zero_src, 0), _slab(out_ref, 0), sem.at[slot]).wait()
                return carry2
            lax.fori_loop(lo, hi, done, 0)
            return carry
        lax.fori_loop(0, N_CLASSES, per_class, 0)

        used_tiles = (start_ref[N_CLASSES - 1] + padcnt_ref[N_CLASSES - 1]) // MOE_TILE
        all_tiles = out_ref.shape[0] // moe_rows

        def tile_copy(t):
            return pltpu.make_async_copy(
                zero_src.at[pl.ds(0, moe_rows)],
                out_ref.at[pl.ds(pl.multiple_of(t * moe_rows, moe_rows), moe_rows)], sem.at[slot])

        def fill_tile(t, carry):
            tile_copy(t).start()
            return carry
        lax.fori_loop(used_tiles, all_tiles, fill_tile, 0)

        def done_tile(t, carry):
            tile_copy(t).wait()
            return carry
        lax.fori_loop(used_tiles, all_tiles, done_tile, 0)

    for s in range(RING):
        @pl.when(i % RING == s)
        def _():
            @pl.when(i < p_tiles)
            def _():
                ring[s] = srcp_ref[...]
                issue(s, p_rows)

            @pl.when(i >= p_tiles)
            def _():
                ring[s, 0:s_rows, :] = srcs_ref[...]
                issue(s, s_rows)

            @pl.when(i >= RING - 1)
            def _():
                wait_tile((s + 1) % RING, i - (RING - 1))

            @pl.when(i == n - 1)
            def _():
                for back in range(RING - 2, -1, -1):
                    wait_tile((s - back) % RING, i - back)
                zero_fill(s)


def _dispatch_call(seg_len, seg_src, seg_dst, cnt, padcnt, start, pk_p, pk_s, p_tile, s_tile, n_rows):
    p_rows = p_tile * SUBLANES
    s_rows = s_tile * SUBLANES
    p_tiles = pk_p.shape[0] // p_rows
    s_tiles = pk_s.shape[0] // s_rows
    assert p_tiles + s_tiles >= RING and p_rows >= s_rows and p_rows >= MOE_TILE * SUBLANES
    return pl.pallas_call(
        functools.partial(_dispatch_kernel, p_tiles=p_tiles),
        grid_spec=pltpu.PrefetchScalarGridSpec(
            num_scalar_prefetch=6,
            grid=(p_tiles + s_tiles,),
            in_specs=[
                pl.BlockSpec((p_rows, LANES), lambda i, *_: (jnp.minimum(i, p_tiles - 1), 0)),
                pl.BlockSpec((s_rows, LANES), lambda i, *_: (jnp.maximum(i - p_tiles, 0), 0)),
            ],
            out_specs=pl.BlockSpec(memory_space=pl.ANY),
            scratch_shapes=[pltpu.VMEM((RING, p_rows, LANES), F32),
                            pltpu.SemaphoreType.DMA((RING,))],
        ),
        out_shape=jax.ShapeDtypeStruct((n_rows * SUBLANES, LANES), F32),
        compiler_params=pltpu.CompilerParams(
            dimension_semantics=("arbitrary",), has_side_effects=True, vmem_limit_bytes=VMEM_LIMIT),
        name="moe_dispatch",
    )(seg_len, seg_src, seg_dst, cnt, padcnt, start, pk_p, pk_s)


def _moe_kernel(ea_ref, eb_ref, valid_ref, xs_ref, w_router_ref, b_router_ref,
                wga_ref, wua_ref, wda_ref, wgb_ref, wub_ref, wdb_ref, f_ref):
    t = pl.program_id(0)

    @pl.when(valid_ref[t] == 1)
    def _():
        tm = xs_ref.shape[0] // SUBLANES
        x = jnp.concatenate([xs_ref[pl.ds(j, tm, stride=SUBLANES), :] for j in range(ROW_GROUPS)],
                            axis=1).astype(BF16)
        e_a = ea_ref[t]
        e_b = eb_ref[t]
        e_lo = (e_a // PER_GROUP) * PER_GROUP
        logits = jnp.dot(x, w_router_ref[...], preferred_element_type=F32) + b_router_ref[...]
        lane = lax.broadcasted_iota(jnp.int32, (tm, LANES), 1)
        pick = lambda p, idx: jnp.sum(jnp.where(lane == idx, p, 0.0), axis=-1, keepdims=True)
        p_group = _masked_softmax(logits, (lane >= N_EXPERTS) & (lane < N_EXPERTS + N_GROUPS))
        p_g = pick(p_group, N_EXPERTS + e_a // PER_GROUP)
        p_e = _masked_softmax(logits, (lane >= e_lo) & (lane < e_lo + PER_GROUP))
        w_a = pick(p_e, e_a)
        w_b = pick(p_e, e_b)
        wsum = w_a + w_b

        def expert(wg_ref, wu_ref, gate):
            hg = jnp.dot(x, wg_ref[0], preferred_element_type=F32)
            hu = jnp.dot(x, wu_ref[0], preferred_element_type=F32)
            return (_silu(hg) * hu * gate).astype(BF16)

        ha = expert(wga_ref, wua_ref, p_g * (w_a / wsum))
        hb = expert(wgb_ref, wub_ref, p_g * (w_b / wsum))
        for n in range(D_MODEL // GATE_HALF):
            cols = slice(n * GATE_HALF, (n + 1) * GATE_HALF)
            f = (jnp.dot(ha, wda_ref[0, :, cols], preferred_element_type=F32)
                 + jnp.dot(hb, wdb_ref[0, :, cols], preferred_element_type=F32))
            for jj in range(GATE_HALF // LANES):
                j = n * (GATE_HALF // LANES) + jj
                f_ref[pl.ds(j, tm, stride=SUBLANES), :] = f[:, jj * LANES:(jj + 1) * LANES]

    @pl.when(valid_ref[t] == 0)
    def _():
        f_ref[...] = jnp.zeros_like(f_ref)


def _moe_call(tile_ea, tile_eb, tile_valid, rows, w_router, b_router, wg, wu, wd):
    n_rows = rows.shape[0] // SUBLANES
    tm = MOE_TILE
    const = lambda a: pl.BlockSpec(a.shape, lambda t, ea, eb, v: (0,) * a.ndim)
    up = lambda sel: pl.BlockSpec((1, D_MODEL, D_EXPERT), lambda t, ea, eb, v: (sel(ea, eb)[t], 0, 0))
    down = lambda sel: pl.BlockSpec((1, D_EXPERT, D_MODEL), lambda t, ea, eb, v: (sel(ea, eb)[t], 0, 0))
    first = lambda ea, eb: ea
    second = lambda ea, eb: eb
    return pl.pallas_call(
        _moe_kernel,
        grid_spec=pltpu.PrefetchScalarGridSpec(
            num_scalar_prefetch=3,
            grid=(n_rows // tm,),
            in_specs=[
                pl.BlockSpec((tm * SUBLANES, LANES), lambda t, ea, eb, v: (t, 0)),
                const(w_router), const(b_router),
                up(first), up(first), down(first), up(second), up(second), down(second),
            ],
            out_specs=pl.BlockSpec((tm * SUBLANES, LANES), lambda t, ea, eb, v: (t, 0)),
        ),
        out_shape=jax.ShapeDtypeStruct((n_rows * SUBLANES, LANES), F32),
        compiler_params=pltpu.CompilerParams(
            dimension_semantics=("arbitrary",), vmem_limit_bytes=VMEM_LIMIT),
        name="moe_experts",
    )(tile_ea, tile_eb, tile_valid, rows, w_router, b_router, wg, wu, wd, wg, wu, wd)


def _combine_kernel(dst_ref, x1_ref, mod_ref, n_post2_ref, f_hbm, o_ref, fbuf, sem):
    i = pl.program_id(0)
    n = pl.num_programs(0)
    tf = x1_ref.shape[0]

    def issue(tile, slot):
        def body(g, carry):
            first = g * DMA_GROUP
            slots = [dst_ref[tile * tf + first + j] for j in range(DMA_GROUP)]
            for j in range(DMA_GROUP):
                pltpu.make_async_copy(
                    _slab(f_hbm, slots[j]), _slab(fbuf.at[slot], first + j), sem.at[slot]
                ).start(priority=j % 2)
            return carry
        lax.fori_loop(0, tf // DMA_GROUP, body, 0)

    def finish(slot):
        pltpu.make_async_copy(f_hbm.at[pl.ds(0, tf * SUBLANES)], fbuf.at[slot], sem.at[slot]).wait()
        f = jnp.concatenate([fbuf[slot, pl.ds(j, tf, stride=SUBLANES), :] for j in range(D_MODEL // LANES)],
                            axis=1)
        mod3 = mod_ref[...]
        bb = mod3.shape[0]
        f3 = f.reshape(bb, tf // bb, D_MODEL)
        g2 = mod3[:, :, 5 * D_MODEL:6 * D_MODEL]
        x13 = x1_ref[...].reshape(f3.shape)
        out = x13 + (f3 * _rms_scale(f3)) * (g2 * n_post2_ref[...].reshape(1, 1, D_MODEL))
        o_ref[...] = out.reshape(o_ref.shape)

    @pl.when(i == 0)
    def _():
        issue(0, 0)

    for parity in range(2):
        @pl.when(i % 2 == parity)
        def _():
            @pl.when(i + 1 < n)
            def _():
                issue(i + 1, 1 - parity)
            finish(parity)


def _combine_call(dst, x1, mod3, n_post2, f_sorted, seqs_per_tile):
    n_tok = x1.shape[0]
    tf = FINAL_TILE
    tok_per_seq = n_tok // mod3.shape[0]
    if seqs_per_tile > 1:
        mod_map = lambda i, d: (i, 0, 0)
    else:
        mod_map = lambda i, d: ((i * tf) // tok_per_seq, 0, 0)
    return pl.pallas_call(
        _combine_kernel,
        grid_spec=pltpu.PrefetchScalarGridSpec(
            num_scalar_prefetch=1,
            grid=(n_tok // tf,),
            in_specs=[
                pl.BlockSpec((tf, D_MODEL), lambda i, d: (i, 0)),
                pl.BlockSpec((seqs_per_tile, 1, 6 * D_MODEL), mod_map),
                pl.BlockSpec((1, D_MODEL), lambda i, d: (0, 0)),
                pl.BlockSpec(memory_space=pl.ANY),
            ],
            out_specs=pl.BlockSpec((tf, D_MODEL), lambda i, d: (i, 0)),
            scratch_shapes=[pltpu.VMEM((2, tf * SUBLANES, LANES), F32), pltpu.SemaphoreType.DMA((2,))],
        ),
        out_shape=jax.ShapeDtypeStruct((n_tok, D_MODEL), F32),
        compiler_params=pltpu.CompilerParams(
            dimension_semantics=("arbitrary",), vmem_limit_bytes=VMEM_LIMIT),
        name="moe_combine",
    )(dst, x1, mod3, n_post2, f_sorted)


def _block_diag_gate(w_r, w_i):
    per_half = GATE_HALF // LRU_BLOCK
    halves = []
    for hb in range(D_LRU // GATE_HALF):
        blocks = []
        for w in (w_r, w_i):
            mat = jnp.zeros((GATE_HALF, GATE_HALF), F32)
            for n in range(per_half):
                lo = n * LRU_BLOCK
                mat = mat.at[lo:lo + LRU_BLOCK, lo:lo + LRU_BLOCK].set(w[hb * per_half + n])
            blocks.append(mat)
        halves.append(jnp.concatenate(blocks, axis=1))
    return jnp.stack(halves).astype(BF16)


def _rope_tables(pos):
    half = DK // 2
    inv = ROPE_BASE ** (-jnp.arange(half, dtype=F32) / half)
    ang = pos[:, None] * inv[None, :]
    cos = jnp.cos(ang)
    sin = jnp.sin(ang)
    return jnp.concatenate([cos, cos], axis=-1), jnp.concatenate([-sin, sin], axis=-1)


def _decay_tables(c):
    log_g = jnp.log1p(-jnp.exp2(-5.0 - jnp.arange(N_HEADS, dtype=F32)))
    idx = jnp.arange(c, dtype=F32)
    diff = idx[:, None] - idx[None, :]
    mask = jnp.where(diff[None] >= 0, jnp.exp(jnp.maximum(diff, 0.0)[None] * log_g[:, None, None]), 0.0)
    w_state = jnp.exp((c - 1.0 - idx)[None, :] * log_g[:, None])
    cross_decay = jnp.exp((idx + 1.0)[:, None] * log_g[None, :])
    chunk_decay = jnp.exp(c * log_g)
    wstate_full = jnp.repeat(w_state.T, DK, axis=1)
    cross_full = jnp.repeat(cross_decay, DV, axis=1)
    cdecay_full = jnp.repeat(chunk_decay, DV)[None, :]
    return mask, wstate_full, cross_full, cdecay_full


def kernel(x_prompt, x_sample, state_conv, state_lru, state_ret, c_prompt, c_sample, w_mod, b_mod, norm_pre_mix, norm_post_mix, norm_pre_ffn, norm_post_ffn, w_in, conv_w, conv_b, w_rgate, b_rgate, w_igate, b_igate, lru_lambda, ret_gn_w, w_out, w_router_group, b_router_group, w_router_expert, b_router_expert, w_exp_gate, w_exp_up, w_exp_down):
    bp, tp, _ = x_prompt.shape
    bs, ts, _ = x_sample.shape
    l = 0

    mod = _mod_call(jnp.concatenate([c_prompt, c_sample], axis=0), w_mod[l], b_mod[l][None, :])
    mod_p = mod[:bp][:, None, :]
    mod_s = mod[bp:][:, None, :]

    w_router = jnp.zeros((D_MODEL, LANES), F32)
    w_router = w_router.at[:, :N_EXPERTS].set(w_router_expert[l])
    w_router = w_router.at[:, N_EXPERTS:N_EXPERTS + N_GROUPS].set(w_router_group[l]).astype(BF16)
    b_router = jnp.zeros((1, LANES), F32)
    b_router = b_router.at[0, :N_EXPERTS].set(b_router_expert[l])
    b_router = b_router.at[0, N_EXPERTS:N_EXPERTS + N_GROUPS].set(b_router_group[l])

    row = lambda vec: vec.reshape(1, -1)
    wts = (row(norm_pre_mix[l]), row(norm_post_mix[l]), row(norm_pre_ffn[l]),
           w_in[l].astype(BF16), conv_w[l], row(conv_b[l]),
           _block_diag_gate(w_rgate[l], w_igate[l]),
           row(b_rgate[l]), row(b_igate[l]), row(lru_lambda[l]), row(ret_gn_w[l]),
           w_out[l].astype(BF16), w_router, b_router)

    cos_p, sin_p = _rope_tables(jnp.arange(tp, dtype=F32))
    x1_p, pk_p, route_p, tilecnt_p, conv_p8, lru_p8, ret_p, cnt_p = _prompt_mixer_call(
        x_prompt, mod_p, cos_p, sin_p, wts, _decay_tables(math.gcd(tp, RET_CHUNK)))

    cos_s, sin_s = _rope_tables(jnp.float32(PAST_LEN) + jnp.arange(ts, dtype=F32))
    mask8, wstate_s, cross_s, cdecay_s = _decay_tables(math.gcd(ts, RET_CHUNK))
    eye = jnp.eye(SAMPLE_SEQS, dtype=F32)
    smask = jnp.stack([jnp.kron(eye, mask8[h]) for h in range(N_HEADS)])
    buf8 = jnp.pad(state_conv[l], ((0, 0), (0, ts - (CONV_W - 1)), (0, 0)))
    h0p = jnp.pad(state_lru[l][:, None, :], ((0, 0), (0, ts - 1), (0, 0)))
    x1_s, pk_s, route_s, tilecnt_s, xr_s, h_s, ret_s, cnt_all = _sample_mixer_call(
        x_sample, mod_s, cos_s, sin_s, buf8, h0p, state_ret[l], cnt_p, wts,
        (smask, wstate_s, cross_s, cdecay_s))

    n_p = bp * tp
    n_tok = n_p + bs * ts
    tm = MOE_TILE
    max_tiles = n_tok // tm + N_CLASSES
    route = jnp.concatenate([route_p[0:2], route_s[0:2]], axis=1).astype(jnp.int32)
    cls = route[0]
    rank = route[1]
    cnt = cnt_all[0, :N_CLASSES].astype(jnp.int32)
    ntile = (cnt + (tm - 1)) // tm
    padcnt = ntile * tm
    start = jnp.cumsum(padcnt) - padcnt
    dst = start[cls] + rank
    seg_len = jnp.concatenate([tilecnt_p[::SUBLANES, :N_CLASSES], tilecnt_s[::SUBLANES, :N_CLASSES]],
                              axis=0).astype(jnp.int32)
    seg_src = jnp.cumsum(seg_len, axis=1) - seg_len
    seg_dst = start[None, :] + jnp.cumsum(seg_len, axis=0) - seg_len
    tile_end = jnp.cumsum(ntile)
    tile_ids = jnp.arange(max_tiles, dtype=jnp.int32)
    tile_valid = (tile_ids < tile_end[-1]).astype(jnp.int32)
    last_used = jnp.minimum(tile_ids, tile_end[-1] - 1)
    tile_cls = jnp.sum((last_used[:, None] >= tile_end[None, :]).astype(jnp.int32), axis=1)
    pair_a = jnp.array([0, 0, 0, 1, 1, 2], jnp.int32)
    pair_b = jnp.array([1, 2, 3, 2, 3, 3], jnp.int32)
    tile_ea = (tile_cls // N_PAIRS) * PER_GROUP + pair_a[tile_cls % N_PAIRS]
    tile_eb = (tile_cls // N_PAIRS) * PER_GROUP + pair_b[tile_cls % N_PAIRS]

    n_post2 = row(norm_post_ffn[l])
    wg = w_exp_gate[l].astype(BF16)
    wu = w_exp_up[l].astype(BF16)
    wd = w_exp_down[l].astype(BF16)
    rows = _dispatch_call(seg_len.reshape(-1), seg_src.reshape(-1), seg_dst.reshape(-1), cnt, padcnt, start,
                          pk_p, pk_s, PROMPT_TILE, SAMPLE_SEQS * ts, max_tiles * tm)
    f_sorted = _moe_call(tile_ea, tile_eb, tile_valid, rows, w_router, b_router, wg, wu, wd)
    y_p = _combine_call(dst[:n_p], x1_p, mod_p, n_post2, f_sorted, 1)
    y_s = _combine_call(dst[n_p:], x1_s, mod_s, n_post2, f_sorted, FINAL_TILE // ts)

    conv_p = conv_p8[:, SUBLANES - (CONV_W - 1):, :]
    lru_p = lru_p8[:, SUBLANES - 1, :]
    xr_s3 = xr_s.reshape(bs, ts, D_LRU)
    conv_s = xr_s3[:, ts - (CONV_W - 1):, :]
    lru_s = h_s.reshape(bs, ts, D_LRU)[:, ts - 1, :]
    return (y_p.reshape(bp, tp, D_MODEL), y_s.reshape(bs, ts, D_MODEL),
            conv_p[None], lru_p[None], ret_p[None],
            conv_s[None], lru_s[None], ret_s[None])
```

```python
import math

import jax
import jax.numpy as jnp
from jax import lax
from jax.experimental import pallas as pl
from jax.experimental.pallas import tpu as pltpu

F32 = jnp.float32
BF16 = jnp.bfloat16

D_MODEL = 1024
D_LRU = 512
D_RET = 512
N_LRU_BLOCKS = 8
LRU_BLOCK = D_LRU // N_LRU_BLOCKS
CONV_W = 4
LRU_C = 8.0
N_HEADS = 4
DK = 128
DV = 128
RET_CHUNK = 128
ROPE_BASE = 10000.0
D_IN_PROJ = 3072
N_GROUPS = 4
PER_GROUP = 4
N_EXPERTS = 16
D_EXPERT = 256
EXPM1_DIRECT_BELOW = -0.5
NORM_EPS = 1e-6
GN_EPS = 1e-5
PAST_LEN = 16384

SUBLANES = 8
LANES = 128
GATE_HALF = 256
VMEM_LIMIT = 56 * 1024 * 1024

PROMPT_TILE = 512
SAMPLE_SEQS = 16
MOE_TILE = 256
FINAL_TILE = 256
DMA_GROUP = 8

N_PAIRS = 6
N_CLASSES = N_GROUPS * N_PAIRS
ROW_GROUPS = D_MODEL // LANES


def _silu(x):
    return x * jax.nn.sigmoid(x)


def _rms_scale(x):
    return lax.rsqrt(jnp.mean(x * x, axis=-1, keepdims=True) + NORM_EPS)


def _masked_softmax(logits, mask):
    top = jnp.max(jnp.where(mask, logits, -jnp.inf), axis=-1, keepdims=True)
    e = jnp.where(mask, jnp.exp(logits - top), 0.0)
    return e / jnp.sum(e, axis=-1, keepdims=True)


def _mod_kernel(c_ref, w_ref, b_ref, o_ref):
    s = _silu(c_ref[...]).astype(BF16)
    o_ref[...] = jnp.dot(s, w_ref[...].astype(BF16), preferred_element_type=F32) + b_ref[...]


def _mod_call(c_all, w_mod, b_mod):
    rows = c_all.shape[0]
    ncol = w_mod.shape[1]
    blk = 1024
    return pl.pallas_call(
        _mod_kernel,
        grid=(ncol // blk,),
        in_specs=[
            pl.BlockSpec((rows, D_MODEL), lambda j: (0, 0)),
            pl.BlockSpec((D_MODEL, blk), lambda j: (0, j)),
            pl.BlockSpec((1, blk), lambda j: (0, j)),
        ],
        out_specs=pl.BlockSpec((rows, blk), lambda j: (0, j)),
        out_shape=jax.ShapeDtypeStruct((rows, ncol), F32),
        compiler_params=pltpu.CompilerParams(
            dimension_semantics=("arbitrary",), vmem_limit_bytes=VMEM_LIMIT),
        name="mod",
    )(c_all, w_mod, b_mod)


def _in_proj(x3, mod3, n_pre1_ref, w_in_ref):
    bb, tt, _ = x3.shape
    sh1 = mod3[:, :, 0:D_MODEL]
    sc1 = mod3[:, :, D_MODEL:2 * D_MODEL]
    coef = n_pre1_ref[...].reshape(1, 1, D_MODEL) * (1.0 + sc1)
    u = (x3 * _rms_scale(x3)) * coef + sh1
    u2d = u.reshape(bb * tt, D_MODEL).astype(BF16)
    return [jnp.dot(u2d, w_in_ref[:, c * D_LRU:(c + 1) * D_LRU], preferred_element_type=F32)
            for c in range(D_IN_PROJ // D_LRU)]


def _lru_coeffs(xc, wg_ref, b_r_ref, b_i_ref, lam_ref):
    xcb = xc.astype(BF16)
    g0 = jnp.dot(xcb[:, :GATE_HALF], wg_ref[0], preferred_element_type=F32)
    g1 = jnp.dot(xcb[:, GATE_HALF:], wg_ref[1], preferred_element_type=F32)
    r = jax.nn.sigmoid(jnp.concatenate([g0[:, :GATE_HALF], g1[:, :GATE_HALF]], axis=1) + b_r_ref[...])
    i = jax.nn.sigmoid(jnp.concatenate([g0[:, GATE_HALF:], g1[:, GATE_HALF:]], axis=1) + b_i_ref[...])
    lam = lam_ref[...]
    sp = jnp.maximum(-lam, 0.0) + jnp.log1p(jnp.exp(-jnp.abs(lam)))
    log_a = -LRU_C * r * sp
    a = jnp.exp(log_a)
    y = 2.0 * log_a
    a2 = a * a
    d = a2 - 1.0
    small = d * y / jnp.log(a2)
    em1 = jnp.where(y < EXPM1_DIRECT_BELOW, d, jnp.where(d == 0.0, y, small))
    gain = jnp.sqrt(-em1)
    return a, gain * (i * xc)


def _rope(xh, cos2, sin2, lane_axis):
    return xh * cos2 + pltpu.roll(xh, DK // 2, axis=lane_axis) * sin2


def _group_norm(o):
    mu = jnp.mean(o, axis=-1, keepdims=True)
    d = o - mu
    var = jnp.mean(d * d, axis=-1, keepdims=True)
    return d * lax.rsqrt(var + GN_EPS)


def _post_mixer(x3, mod3, out_a, out_b, w_out_ref, n_post1_ref, n_pre2_ref, w_router_ref, b_router_ref,
                x1_ref, pk_ref, route_ref, cnt_scr):
    bb, tt, _ = x3.shape
    m = bb * tt
    y = (jnp.dot(out_a.astype(BF16), w_out_ref[0:D_LRU, :], preferred_element_type=F32)
         + jnp.dot(out_b.astype(BF16), w_out_ref[D_LRU:, :], preferred_element_type=F32))
    g1 = mod3[:, :, 2 * D_MODEL:3 * D_MODEL]
    sh2 = mod3[:, :, 3 * D_MODEL:4 * D_MODEL]
    sc2 = mod3[:, :, 4 * D_MODEL:5 * D_MODEL]
    y3 = y.reshape(bb, tt, D_MODEL)
    x1 = x3 + (y3 * _rms_scale(y3)) * (g1 * n_post1_ref[...].reshape(1, 1, D_MODEL))
    u2 = (x1 * _rms_scale(x1)) * (n_pre2_ref[...].reshape(1, 1, D_MODEL) * (1.0 + sc2)) + sh2
    x1_ref[...] = x1.reshape(m, D_MODEL)
    u2f = u2.reshape(m, D_MODEL)
    for j in range(ROW_GROUPS):
        pk_ref[pl.ds(j, m, stride=SUBLANES), :] = u2f[:, j * LANES:(j + 1) * LANES]
    u2b = u2f.astype(BF16)

    logits = jnp.dot(u2b, w_router_ref[...], preferred_element_type=F32) + b_router_ref[...]
    lane = lax.broadcasted_iota(jnp.int32, (m, LANES), 1)
    is_g = (lane >= N_EXPERTS) & (lane < N_EXPERTS + N_GROUPS)
    p_group = _masked_softmax(logits, is_g)
    p_g = jnp.max(p_group, axis=-1, keepdims=True)
    g_lane = jnp.min(jnp.where(is_g & (p_group == p_g), lane, LANES), axis=-1, keepdims=True)
    e_lo = (g_lane - N_EXPERTS) * PER_GROUP
    in_g = (lane >= e_lo) & (lane < e_lo + PER_GROUP)
    p_e = _masked_softmax(logits, in_g)
    pm = jnp.where(in_g, p_e, -1.0)
    w1 = jnp.max(pm, axis=-1, keepdims=True)
    i1 = jnp.min(jnp.where(pm == w1, lane, LANES), axis=-1, keepdims=True)
    pm2 = jnp.where(lane == i1, -1.0, pm)
    w2 = jnp.max(pm2, axis=-1, keepdims=True)
    i2 = jnp.min(jnp.where(pm2 == w2, lane, LANES), axis=-1, keepdims=True)
    a = jnp.minimum(i1, i2) - e_lo
    b = jnp.maximum(i1, i2) - e_lo
    pair = jnp.where(a == 0, b - 1, jnp.where(a == 1, b + 1, 5))
    cls = (g_lane - N_EXPERTS) * N_PAIRS + pair
    onehot = lane == cls
    r_i = lax.broadcasted_iota(jnp.int32, (m, m), 0)
    c_i = lax.broadcasted_iota(jnp.int32, (m, m), 1)
    earlier = jnp.where(r_i > c_i, 1.0, 0.0).astype(BF16)
    prefix = jnp.dot(earlier, jnp.where(onehot, 1.0, 0.0).astype(BF16), preferred_element_type=F32)
    run = cnt_scr[0:1, :]
    rank = jnp.sum(jnp.where(onehot, prefix + run, 0.0), axis=-1, keepdims=True)
    cnt_scr[...] = jnp.broadcast_to(
        run + jnp.sum(jnp.where(onehot, 1.0, 0.0), axis=0, keepdims=True), cnt_scr.shape)
    route = jnp.where(lane == 0, cls.astype(F32), jnp.where(lane == 1, rank, 0.0))
    route_ref[...] = jnp.transpose(route)[0:SUBLANES, :]


def _group_scan(a3, b3):
    tpos = lax.broadcasted_iota(jnp.int32, a3.shape, 1)
    s = 1
    while s < a3.shape[1]:
        keep = tpos >= s
        a_sh = jnp.where(keep, pltpu.roll(a3, s, axis=1), 1.0)
        b_sh = jnp.where(keep, pltpu.roll(b3, s, axis=1), 0.0)
        b3 = a3 * b_sh + b3
        a3 = a3 * a_sh
        s *= 2
    return a3, b3


def _scan_rows(a, b, h0):
    n, c = a.shape
    groups = n // SUBLANES
    a3, b3 = _group_scan(a.reshape(groups, SUBLANES, c), b.reshape(groups, SUBLANES, c))
    carry = h0
    out = []
    for g in range(groups):
        hg = b3[g] + a3[g] * carry
        out.append(hg)
        carry = hg[SUBLANES - 1:SUBLANES, :]
    return jnp.concatenate(out, axis=0)


def _prompt_mixer_kernel(x_ref, mod_ref, cos_ref, sin_ref,
                         n_pre1_ref, n_post1_ref, n_pre2_ref,
                         w_in_ref, conv_w_ref, conv_b_ref, wg_ref, b_r_ref, b_i_ref, lam_ref,
                         gn_w_ref, w_out_ref, w_router_ref, b_router_ref,
                         mask_ref, wstate_ref, cross_ref, cdecay_ref,
                         x1_ref, pk_ref, route_ref, conv_out_ref, lru_out_ref, ret_out_ref, cnt_out_ref,
                         conv_scr, h_scr, s_scr, cnt_scr):
    t = pl.program_id(1)
    tt = x_ref.shape[1]

    @pl.when((pl.program_id(0) == 0) & (t == 0))
    def _():
        cnt_scr[...] = jnp.zeros_like(cnt_scr)

    @pl.when(t == 0)
    def _():
        conv_scr[...] = jnp.zeros_like(conv_scr)
        h_scr[...] = jnp.zeros_like(h_scr)
        s_scr[...] = jnp.zeros_like(s_scr)

    x3 = x_ref[...]
    mod3 = mod_ref[...]
    xr, yg, q, k, v, g = _in_proj(x3, mod3, n_pre1_ref, w_in_ref)

    groups = tt // SUBLANES
    xr3 = xr.reshape(groups, SUBLANES, D_LRU)
    tpos = lax.broadcasted_iota(jnp.int32, xr3.shape, 1)
    tail = conv_scr[...]
    xc3 = jnp.broadcast_to(conv_b_ref[...].reshape(1, 1, D_LRU), xr3.shape)
    for j in range(CONV_W):
        back = CONV_W - 1 - j
        w_j = conv_w_ref[j:j + 1, :].reshape(1, 1, D_LRU)
        if back == 0:
            term = xr3
        else:
            cur = pltpu.roll(xr3, back, axis=1)
            first = pltpu.roll(tail, back, axis=0).reshape(1, SUBLANES, D_LRU)
            prev = jnp.concatenate([first, cur[:groups - 1]], axis=0)
            term = jnp.where(tpos >= back, cur, prev)
        xc3 = xc3 + term * w_j
    xc = xc3.reshape(tt, D_LRU)
    conv_scr[...] = xr[tt - SUBLANES:, :]

    a, b = _lru_coeffs(xc, wg_ref, b_r_ref, b_i_ref, lam_ref)
    hseq = _scan_rows(a, b, h_scr[0:1, :])
    h_scr[...] = jnp.broadcast_to(hseq[tt - 1:tt, :], h_scr.shape)
    out_a = hseq * jax.nn.gelu(yg, approximate=True)

    cos2 = cos_ref[...]
    sin2 = sin_ref[...]
    scale = DK ** -0.5
    o_heads = []
    for h in range(N_HEADS):
        hs = slice(h * DK, (h + 1) * DK)
        qh = (_rope(q[:, hs], cos2, sin2, 1) * scale).astype(BF16)
        kh = _rope(k[:, hs], cos2, sin2, 1)
        vh = v[:, hs].astype(BF16)
        o_chunks = []
        for c in range(tt // RET_CHUNK):
            cs = slice(c * RET_CHUNK, (c + 1) * RET_CHUNK)
            qc = qh[cs]
            kc = kh[cs]
            vc = vh[cs]
            s_prev = s_scr[h]
            scores = lax.dot_general(qc, kc.astype(BF16), (((1,), (1,)), ((), ())),
                                     preferred_element_type=F32) * mask_ref[h]
            inner = jnp.dot(scores.astype(BF16), vc, preferred_element_type=F32)
            cross = jnp.dot(qc, s_prev.astype(BF16), preferred_element_type=F32) * cross_ref[:, hs]
            kw = (kc * wstate_ref[:, hs]).astype(BF16)
            kv = lax.dot_general(kw, vc, (((0,), (0,)), ((), ())), preferred_element_type=F32)
            s_scr[h] = cdecay_ref[:, hs] * s_prev + kv
            o_chunks.append(inner + cross)
        o_heads.append(_group_norm(jnp.concatenate(o_chunks, axis=0)))
    o = jnp.concatenate(o_heads, axis=1)
    out_b = o * gn_w_ref[...] * _silu(g)

    _post_mixer(x3, mod3, out_a, out_b, w_out_ref, n_post1_ref, n_pre2_ref, w_router_ref, b_router_ref,
                x1_ref, pk_ref, route_ref, cnt_scr)
    cnt_out_ref[...] = cnt_scr[...]

    @pl.when(t == pl.num_programs(1) - 1)
    def _():
        conv_out_ref[0] = xr[tt - SUBLANES:, :]
        lru_out_ref[0] = hseq[tt - SUBLANES:, :]
        ret_out_ref[0] = s_scr[...]


def _const_spec(shape):
    nd = len(shape)
    return pl.BlockSpec(shape, lambda *_: (0,) * nd)


def _prompt_mixer_call(x, mod3, cos2, sin2, wts, tables, n_all):
    bsz, seq, _ = x.shape
    tt = PROMPT_TILE
    nt = seq // tt
    n_tok = bsz * seq
    (n_pre1, n_post1, n_pre2, w_in, conv_w, conv_b, wg, b_r, b_i, lam, gn_w, w_out, w_router, b_router) = wts
    mask, wstate, cross, cdecay = tables
    tok_spec = pl.BlockSpec((tt, D_MODEL), lambda b, t: (b * nt + t, 0))
    in_specs = [
        pl.BlockSpec((1, tt, D_MODEL), lambda b, t: (b, t, 0)),
        pl.BlockSpec((1, 1, 6 * D_MODEL), lambda b, t: (b, 0, 0)),
        pl.BlockSpec((tt, LANES), lambda b, t: (t, 0)),
        pl.BlockSpec((tt, LANES), lambda b, t: (t, 0)),
    ] + [_const_spec(w.shape) for w in wts] + [_const_spec(tb.shape) for tb in tables]
    out_specs = [
        tok_spec,
        pl.BlockSpec((tt * SUBLANES, LANES), lambda b, t: (b * nt + t, 0)),
        pl.BlockSpec((SUBLANES, tt), lambda b, t: (0, b * nt + t)),
        pl.BlockSpec((1, SUBLANES, D_LRU), lambda b, t: (b, 0, 0)),
        pl.BlockSpec((1, SUBLANES, D_LRU), lambda b, t: (b, 0, 0)),
        pl.BlockSpec((1, N_HEADS, DK, DV), lambda b, t: (b, 0, 0, 0)),
        pl.BlockSpec((SUBLANES, LANES), lambda b, t: (0, 0)),
    ]
    out_shape = [
        jax.ShapeDtypeStruct((n_tok, D_MODEL), F32),
        jax.ShapeDtypeStruct((n_all * SUBLANES, LANES), F32),
        jax.ShapeDtypeStruct((SUBLANES, n_tok), F32),
        jax.ShapeDtypeStruct((bsz, SUBLANES, D_LRU), F32),
        jax.ShapeDtypeStruct((bsz, SUBLANES, D_LRU), F32),
        jax.ShapeDtypeStruct((bsz, N_HEADS, DK, DV), F32),
        jax.ShapeDtypeStruct((SUBLANES, LANES), F32),
    ]
    return pl.pallas_call(
        _prompt_mixer_kernel,
        grid=(bsz, nt),
        in_specs=in_specs,
        out_specs=out_specs,
        out_shape=out_shape,
        scratch_shapes=[
            pltpu.VMEM((SUBLANES, D_LRU), F32),
            pltpu.VMEM((SUBLANES, D_LRU), F32),
            pltpu.VMEM((N_HEADS, DK, DV), F32),
            pltpu.VMEM((SUBLANES, LANES), F32),
        ],
        compiler_params=pltpu.CompilerParams(
            dimension_semantics=("arbitrary", "arbitrary"), vmem_limit_bytes=VMEM_LIMIT),
        name="prompt_mixer",
    )(x, mod3, cos2, sin2, *wts, *tables)


def _sample_mixer_kernel(x_ref, mod_ref, cos_ref, sin_ref, buf_ref, h0_ref, s0_ref, cnt_in_ref,
                         n_pre1_ref, n_post1_ref, n_pre2_ref,
                         w_in_ref, conv_w_ref, conv_b_ref, wg_ref, b_r_ref, b_i_ref, lam_ref,
                         gn_w_ref, w_out_ref, w_router_ref, b_router_ref,
                         smask_ref, wstate_ref, cross_ref, cdecay_ref, pk_all_ref,
                         x1_ref, pk_ref, route_ref, xr_out_ref, h_out_ref, ret_out_ref, cnt_out_ref,
                         cnt_scr):
    bb, ts, _ = x_ref.shape
    m = bb * ts

    @pl.when(pl.program_id(0) == 0)
    def _():
        cnt_scr[...] = cnt_in_ref[...]

    x3 = x_ref[...]
    mod3 = mod_ref[...]
    xr, yg, q, k, v, g = _in_proj(x3, mod3, n_pre1_ref, w_in_ref)
    xr_out_ref[...] = xr

    xr3 = xr.reshape(bb, ts, D_LRU)
    buf3 = buf_ref[...]
    tpos = lax.broadcasted_iota(jnp.int32, (bb, ts, D_LRU), 1)
    xc3 = jnp.broadcast_to(conv_b_ref[...].reshape(1, 1, D_LRU), (bb, ts, D_LRU))
    for j in range(CONV_W):
        back = CONV_W - 1 - j
        w_j = conv_w_ref[j:j + 1, :].reshape(1, 1, D_LRU)
        if back == 0:
            term = xr3
        else:
            cur = pltpu.roll(xr3, back, axis=1)
            up = CONV_W - 1 - back
            old = buf3 if up == 0 else pltpu.roll(buf3, ts - up, axis=1)
            term = jnp.where(tpos >= back, cur, old)
        xc3 = xc3 + term * w_j
    xc = xc3.reshape(m, D_LRU)

    a, b = _lru_coeffs(xc, wg_ref, b_r_ref, b_i_ref, lam_ref)
    a3 = a.reshape(bb, ts, D_LRU)
    b3 = b.reshape(bb, ts, D_LRU) + a3 * h0_ref[...]
    _, h3 = _group_scan(a3, b3)
    hseq = h3.reshape(m, D_LRU)
    h_out_ref[...] = hseq
    out_a = hseq * jax.nn.gelu(yg, approximate=True)

    cos2 = cos_ref[...].reshape(1, ts, LANES)
    sin2 = sin_ref[...].reshape(1, ts, LANES)
    scale = DK ** -0.5
    o_heads = []
    for h in range(N_HEADS):
        hs = slice(h * DK, (h + 1) * DK)
        q3 = (_rope(q[:, hs].reshape(bb, ts, DK), cos2, sin2, 2) * scale).astype(BF16)
        k3 = _rope(k[:, hs].reshape(bb, ts, DK), cos2, sin2, 2)
        v3 = v[:, hs].reshape(bb, ts, DV).astype(BF16)
        q2 = q3.reshape(m, DK)
        k2 = k3.reshape(m, DK).astype(BF16)
        v2 = v3.reshape(m, DV)
        scores = lax.dot_general(q2, k2, (((1,), (1,)), ((), ())),
                                 preferred_element_type=F32) * smask_ref[h]
        inner = jnp.dot(scores.astype(BF16), v2, preferred_element_type=F32)
        s0h = s0_ref[:, h]
        cross = jnp.einsum('bid,bde->bie', q3, s0h.astype(BF16), preferred_element_type=F32)
        cross = cross * cross_ref[:, hs].reshape(1, ts, DV)
        kw3 = (k3 * wstate_ref[:, hs].reshape(1, ts, DK)).astype(BF16)
        kv = jnp.einsum('bjd,bje->bde', kw3, v3, preferred_element_type=F32)
        ret_out_ref[:, h] = cdecay_ref[:, hs].reshape(1, 1, DV) * s0h + kv
        o_heads.append(_group_norm(inner + cross.reshape(m, DV)))
    o = jnp.concatenate(o_heads, axis=1)
    out_b = o * gn_w_ref[...] * _silu(g)

    _post_mixer(x3, mod3, out_a, out_b, w_out_ref, n_post1_ref, n_pre2_ref, w_router_ref, b_router_ref,
                x1_ref, pk_ref, route_ref, cnt_scr)
    cnt_out_ref[...] = cnt_scr[...]


def _sample_mixer_call(x, mod3, cos2, sin2, buf8, h0p, s0, cnt_in, wts, tables, pk_all):
    bsz, ts, _ = x.shape
    bb = SAMPLE_SEQS
    m = bb * ts
    n_tok = bsz * ts
    first_blk = pk_all.shape[0] // (m * SUBLANES) - bsz // bb
    seq_spec = lambda w: pl.BlockSpec((bb, ts, w), lambda i: (i, 0, 0))
    tok_spec = lambda w: pl.BlockSpec((m, w), lambda i: (i, 0))
    in_specs = [
        seq_spec(D_MODEL),
        pl.BlockSpec((bb, 1, 6 * D_MODEL), lambda i: (i, 0, 0)),
        _const_spec(cos2.shape),
        _const_spec(sin2.shape),
        seq_spec(D_LRU),
        seq_spec(D_LRU),
        pl.BlockSpec((bb, N_HEADS, DK, DV), lambda i: (i, 0, 0, 0)),
        _const_spec(cnt_in.shape),
    ] + [_const_spec(w.shape) for w in wts] + [_const_spec(tb.shape) for tb in tables] + [
        pl.BlockSpec(memory_space=pl.ANY)]
    out_specs = [
        tok_spec(D_MODEL),
        pl.BlockSpec((m * SUBLANES, LANES), lambda i: (first_blk + i, 0)),
        pl.BlockSpec((SUBLANES, m), lambda i: (0, i)),
        tok_spec(D_LRU),
        tok_spec(D_LRU),
        pl.BlockSpec((bb, N_HEADS, DK, DV), lambda i: (i, 0, 0, 0)),
        _const_spec(cnt_in.shape),
    ]
    out_shape = [
        jax.ShapeDtypeStruct((n_tok, D_MODEL), F32),
        jax.ShapeDtypeStruct(pk_all.shape, F32),
        jax.ShapeDtypeStruct((SUBLANES, n_tok), F32),
        jax.ShapeDtypeStruct((n_tok, D_LRU), F32),
        jax.ShapeDtypeStruct((n_tok, D_LRU), F32),
        jax.ShapeDtypeStruct((bsz, N_HEADS, DK, DV), F32),
        jax.ShapeDtypeStruct(cnt_in.shape, F32),
    ]
    return pl.pallas_call(
        _sample_mixer_kernel,
        grid=(bsz // bb,),
        in_specs=in_specs,
        out_specs=out_specs,
        out_shape=out_shape,
        input_output_aliases={len(in_specs) - 1: 1},
        scratch_shapes=[pltpu.VMEM((SUBLANES, LANES), F32)],
        compiler_params=pltpu.CompilerParams(
            dimension_semantics=("arbitrary",), vmem_limit_bytes=VMEM_LIMIT),
        name="sample_mixer",
    )(x, mod3, cos2, sin2, buf8, h0p, s0, cnt_in, *wts, *tables, pk_all)


def _slab(ref, r):
    return ref.at[pl.ds(pl.multiple_of(r * SUBLANES, SUBLANES), SUBLANES)]


def _inverse_kernel(dst_ref, inv_ref):
    n_slots = inv_ref.shape[0]
    n_tok = dst_ref.shape[0]

    def clear(g, carry):
        for j in range(DMA_GROUP):
            inv_ref[g * DMA_GROUP + j] = 0
        return carry
    lax.fori_loop(0, n_slots // DMA_GROUP, clear, 0)

    def place(g, carry):
        first = g * DMA_GROUP
        slots = [dst_ref[first + j] for j in range(DMA_GROUP)]
        for j in range(DMA_GROUP):
            inv_ref[slots[j]] = first + j
        return carry
    lax.fori_loop(0, n_tok // DMA_GROUP, place, 0)


def _inverse_call(dst, n_slots):
    return pl.pallas_call(
        _inverse_kernel,
        grid_spec=pltpu.PrefetchScalarGridSpec(
            num_scalar_prefetch=1,
            grid=(1,),
            in_specs=[],
            out_specs=pl.BlockSpec(memory_space=pltpu.SMEM),
        ),
        out_shape=jax.ShapeDtypeStruct((n_slots,), jnp.int32),
        compiler_params=pltpu.CompilerParams(dimension_semantics=("arbitrary",)),
        name="moe_inverse",
    )(dst)


def _experts_kernel(ea_ref, eb_ref, valid_ref, inv_ref, pk_hbm, w_router_ref, b_router_ref,
                    wga_ref, wua_ref, wda_ref, wgb_ref, wub_ref, wdb_ref, f_ref, xbuf, sem):
    t = pl.program_id(0)
    n = pl.num_programs(0)
    tm = f_ref.shape[0] // SUBLANES

    def issue(tile, slot):
        def body(g, carry):
            first = g * DMA_GROUP
            toks = [inv_ref[tile * tm + first + j] for j in range(DMA_GROUP)]
            for j in range(DMA_GROUP):
                pltpu.make_async_copy(
                    _slab(pk_hbm, toks[j]), _slab(xbuf.at[slot], first + j), sem.at[slot]
                ).start(priority=j % 2)
            return carry
        lax.fori_loop(0, tm // DMA_GROUP, body, 0)

    def compute(slot):
        pltpu.make_async_copy(pk_hbm.at[pl.ds(0, tm * SUBLANES)], xbuf.at[slot], sem.at[slot]).wait()
        x = jnp.concatenate([xbuf[slot, pl.ds(j, tm, stride=SUBLANES), :] for j in range(ROW_GROUPS)],
                            axis=1).astype(BF16)
        e_a = ea_ref[t]
        e_b = eb_ref[t]
        e_lo = (e_a // PER_GROUP) * PER_GROUP
        logits = jnp.dot(x, w_router_ref[...], preferred_element_type=F32) + b_router_ref[...]
        lane = lax.broadcasted_iota(jnp.int32, (tm, LANES), 1)
        pick = lambda p, idx: jnp.sum(jnp.where(lane == idx, p, 0.0), axis=-1, keepdims=True)
        p_group = _masked_softmax(logits, (lane >= N_EXPERTS) & (lane < N_EXPERTS + N_GROUPS))
        p_g = pick(p_group, N_EXPERTS + e_a // PER_GROUP)
        p_e = _masked_softmax(logits, (lane >= e_lo) & (lane < e_lo + PER_GROUP))
        w_a = pick(p_e, e_a)
        w_b = pick(p_e, e_b)
        wsum = w_a + w_b

        def expert(wg_ref, wu_ref, gate):
            hg = jnp.dot(x, wg_ref[0], preferred_element_type=F32)
            hu = jnp.dot(x, wu_ref[0], preferred_element_type=F32)
            return (_silu(hg) * hu * gate).astype(BF16)

        ha = expert(wga_ref, wua_ref, p_g * (w_a / wsum))
        hb = expert(wgb_ref, wub_ref, p_g * (w_b / wsum))
        for c in range(D_MODEL // GATE_HALF):
            cols = slice(c * GATE_HALF, (c + 1) * GATE_HALF)
            f = (jnp.dot(ha, wda_ref[0, :, cols], preferred_element_type=F32)
                 + jnp.dot(hb, wdb_ref[0, :, cols], preferred_element_type=F32))
            for jj in range(GATE_HALF // LANES):
                j = c * (GATE_HALF // LANES) + jj
                f_ref[pl.ds(j, tm, stride=SUBLANES), :] = f[:, jj * LANES:(jj + 1) * LANES]

    @pl.when((t == 0) & (valid_ref[0] == 1))
    def _():
        issue(0, 0)

    nxt = jnp.minimum(t + 1, n - 1)
    for parity in range(2):
        @pl.when(t % 2 == parity)
        def _():
            @pl.when((t + 1 < n) & (valid_ref[nxt] == 1))
            def _():
                issue(t + 1, 1 - parity)

            @pl.when(valid_ref[t] == 1)
            def _():
                compute(parity)

    @pl.when(valid_ref[t] == 0)
    def _():
        f_ref[...] = jnp.zeros_like(f_ref)


def _experts_call(tile_ea, tile_eb, tile_valid, inv, pk_all, w_router, b_router, wg, wu, wd):
    n_rows = inv.shape[0]
    tm = MOE_TILE
    const = lambda a: pl.BlockSpec(a.shape, lambda t, *_: (0,) * a.ndim)
    up = lambda sel: pl.BlockSpec((1, D_MODEL, D_EXPERT), lambda t, ea, eb, v, inv: (sel(ea, eb)[t], 0, 0))
    down = lambda sel: pl.BlockSpec((1, D_EXPERT, D_MODEL), lambda t, ea, eb, v, inv: (sel(ea, eb)[t], 0, 0))
    first = lambda ea, eb: ea
    second = lambda ea, eb: eb
    return pl.pallas_call(
        _experts_kernel,
        grid_spec=pltpu.PrefetchScalarGridSpec(
            num_scalar_prefetch=4,
            grid=(n_rows // tm,),
            in_specs=[
                pl.BlockSpec(memory_space=pl.ANY),
                const(w_router), const(b_router),
                up(first), up(first), down(first), up(second), up(second), down(second),
            ],
            out_specs=pl.BlockSpec((tm * SUBLANES, LANES), lambda t, *_: (t, 0)),
            scratch_shapes=[pltpu.VMEM((2, tm * SUBLANES, LANES), F32), pltpu.SemaphoreType.DMA((2,))],
        ),
        out_shape=jax.ShapeDtypeStruct((n_rows * SUBLANES, LANES), F32),
        compiler_params=pltpu.CompilerParams(
            dimension_semantics=("arbitrary",), vmem_limit_bytes=VMEM_LIMIT),
        name="moe_experts",
    )(tile_ea, tile_eb, tile_valid, inv, pk_all, w_router, b_router, wg, wu, wd, wg, wu, wd)


def _combine_kernel(dst_ref, x1_ref, mod_ref, n_post2_ref, f_hbm, o_ref, fbuf, sem):
    i = pl.program_id(0)
    n = pl.num_programs(0)
    tf = x1_ref.shape[0]

    def issue(tile, slot):
        def body(g, carry):
            first = g * DMA_GROUP
            slots = [dst_ref[tile * tf + first + j] for j in range(DMA_GROUP)]
            for j in range(DMA_GROUP):
                pltpu.make_async_copy(
                    _slab(f_hbm, slots[j]), _slab(fbuf.at[slot], first + j), sem.at[slot]
                ).start(priority=j % 2)
            return carry
        lax.fori_loop(0, tf // DMA_GROUP, body, 0)

    def finish(slot):
        pltpu.make_async_copy(f_hbm.at[pl.ds(0, tf * SUBLANES)], fbuf.at[slot], sem.at[slot]).wait()
        f = jnp.concatenate([fbuf[slot, pl.ds(j, tf, stride=SUBLANES), :] for j in range(D_MODEL // LANES)],
                            axis=1)
        mod3 = mod_ref[...]
        bb = mod3.shape[0]
        f3 = f.reshape(bb, tf // bb, D_MODEL)
        g2 = mod3[:, :, 5 * D_MODEL:6 * D_MODEL]
        x13 = x1_ref[...].reshape(f3.shape)
        out = x13 + (f3 * _rms_scale(f3)) * (g2 * n_post2_ref[...].reshape(1, 1, D_MODEL))
        o_ref[...] = out.reshape(o_ref.shape)

    @pl.when(i == 0)
    def _():
        issue(0, 0)

    for parity in range(2):
        @pl.when(i % 2 == parity)
        def _():
            @pl.when(i + 1 < n)
            def _():
                issue(i + 1, 1 - parity)
            finish(parity)


def _combine_call(dst, x1, mod3, n_post2, f_sorted, seqs_per_tile):
    n_tok = x1.shape[0]
    tf = FINAL_TILE
    tok_per_seq = n_tok // mod3.shape[0]
    if seqs_per_tile > 1:
        mod_map = lambda i, d: (i, 0, 0)
    else:
        mod_map = lambda i, d: ((i * tf) // tok_per_seq, 0, 0)
    return pl.pallas_call(
        _combine_kernel,
        grid_spec=pltpu.PrefetchScalarGridSpec(
            num_scalar_prefetch=1,
            grid=(n_tok // tf,),
            in_specs=[
                pl.BlockSpec((tf, D_MODEL), lambda i, d: (i, 0)),
                pl.BlockSpec((seqs_per_tile, 1, 6 * D_MODEL), mod_map),
                pl.BlockSpec((1, D_MODEL), lambda i, d: (0, 0)),
                pl.BlockSpec(memory_space=pl.ANY),
            ],
            out_specs=pl.BlockSpec((tf, D_MODEL), lambda i, d: (i, 0)),
            scratch_shapes=[pltpu.VMEM((2, tf * SUBLANES, LANES), F32), pltpu.SemaphoreType.DMA((2,))],
        ),
        out_shape=jax.ShapeDtypeStruct((n_tok, D_MODEL), F32),
        compiler_params=pltpu.CompilerParams(
            dimension_semantics=("arbitrary",), vmem_limit_bytes=VMEM_LIMIT),
        name="moe_combine",
    )(dst, x1, mod3, n_post2, f_sorted)


def _block_diag_gate(w_r, w_i):
    per_half = GATE_HALF // LRU_BLOCK
    halves = []
    for hb in range(D_LRU // GATE_HALF):
        blocks = []
        for w in (w_r, w_i):
            mat = jnp.zeros((GATE_HALF, GATE_HALF), F32)
            for n in range(per_half):
                lo = n * LRU_BLOCK
                mat = mat.at[lo:lo + LRU_BLOCK, lo:lo + LRU_BLOCK].set(w[hb * per_half + n])
            blocks.append(mat)
        halves.append(jnp.concatenate(blocks, axis=1))
    return jnp.stack(halves).astype(BF16)


def _rope_tables(pos):
    half = DK // 2
    inv = ROPE_BASE ** (-jnp.arange(half, dtype=F32) / half)
    ang = pos[:, None] * inv[None, :]
    cos = jnp.cos(ang)
    sin = jnp.sin(ang)
    return jnp.concatenate([cos, cos], axis=-1), jnp.concatenate([-sin, sin], axis=-1)


def _decay_tables(c):
    log_g = jnp.log1p(-jnp.exp2(-5.0 - jnp.arange(N_HEADS, dtype=F32)))
    idx = jnp.arange(c, dtype=F32)
    diff = idx[:, None] - idx[None, :]
    mask = jnp.where(diff[None] >= 0, jnp.exp(jnp.maximum(diff, 0.0)[None] * log_g[:, None, None]), 0.0)
    w_state = jnp.exp((c - 1.0 - idx)[None, :] * log_g[:, None])
    cross_decay = jnp.exp((idx + 1.0)[:, None] * log_g[None, :])
    chunk_decay = jnp.exp(c * log_g)
    wstate_full = jnp.repeat(w_state.T, DK, axis=1)
    cross_full = jnp.repeat(cross_decay, DV, axis=1)
    cdecay_full = jnp.repeat(chunk_decay, DV)[None, :]
    return mask, wstate_full, cross_full, cdecay_full


def kernel(x_prompt, x_sample, state_conv, state_lru, state_ret, c_prompt, c_sample, w_mod, b_mod, norm_pre_mix, norm_post_mix, norm_pre_ffn, norm_post_ffn, w_in, conv_w, conv_b, w_rgate, b_rgate, w_igate, b_igate, lru_lambda, ret_gn_w, w_out, w_router_group, b_router_group, w_router_expert, b_router_expert, w_exp_gate, w_exp_up, w_exp_down):
    bp, tp, _ = x_prompt.shape
    bs, ts, _ = x_sample.shape
    l = 0

    mod = _mod_call(jnp.concatenate([c_prompt, c_sample], axis=0), w_mod[l], b_mod[l][None, :])
    mod_p = mod[:bp][:, None, :]
    mod_s = mod[bp:][:, None, :]

    w_router = jnp.zeros((D_MODEL, LANES), F32)
    w_router = w_router.at[:, :N_EXPERTS].set(w_router_expert[l])
    w_router = w_router.at[:, N_EXPERTS:N_EXPERTS + N_GROUPS].set(w_router_group[l]).astype(BF16)
    b_router = jnp.zeros((1, LANES), F32)
    b_router = b_router.at[0, :N_EXPERTS].set(b_router_expert[l])
    b_router = b_router.at[0, N_EXPERTS:N_EXPERTS + N_GROUPS].set(b_router_group[l])

    row = lambda vec: vec.reshape(1, -1)
    wts = (row(norm_pre_mix[l]), row(norm_post_mix[l]), row(norm_pre_ffn[l]),
           w_in[l].astype(BF16), conv_w[l], row(conv_b[l]),
           _block_diag_gate(w_rgate[l], w_igate[l]),
           row(b_rgate[l]), row(b_igate[l]), row(lru_lambda[l]), row(ret_gn_w[l]),
           w_out[l].astype(BF16), w_router, b_router)

    cos_p, sin_p = _rope_tables(jnp.arange(tp, dtype=F32))
    x1_p, pk_p, route_p, conv_p8, lru_p8, ret_p, cnt_p = _prompt_mixer_call(
        x_prompt, mod_p, cos_p, sin_p, wts, _decay_tables(math.gcd(tp, RET_CHUNK)), bp * tp + bs * ts)

    cos_s, sin_s = _rope_tables(jnp.float32(PAST_LEN) + jnp.arange(ts, dtype=F32))
    mask8, wstate_s, cross_s, cdecay_s = _decay_tables(math.gcd(ts, RET_CHUNK))
    eye = jnp.eye(SAMPLE_SEQS, dtype=F32)
    smask = jnp.stack([jnp.kron(eye, mask8[h]) for h in range(N_HEADS)])
    buf8 = jnp.pad(state_conv[l], ((0, 0), (0, ts - (CONV_W - 1)), (0, 0)))
    h0p = jnp.pad(state_lru[l][:, None, :], ((0, 0), (0, ts - 1), (0, 0)))
    x1_s, pk_all, route_s, xr_s, h_s, ret_s, cnt_all = _sample_mixer_call(
        x_sample, mod_s, cos_s, sin_s, buf8, h0p, state_ret[l], cnt_p, wts,
        (smask, wstate_s, cross_s, cdecay_s), pk_p)

    n_p = bp * tp
    n_tok = n_p + bs * ts
    tm = MOE_TILE
    max_tiles = n_tok // tm + N_CLASSES
    route = jnp.concatenate([route_p[0:2], route_s[0:2]], axis=1).astype(jnp.int32)
    cls = route[0]
    rank = route[1]
    cnt = cnt_all[0, :N_CLASSES].astype(jnp.int32)
    ntile = (cnt + (tm - 1)) // tm
    padcnt = ntile * tm
    start = jnp.cumsum(padcnt) - padcnt
    dst = start[cls] + rank
    tile_end = jnp.cumsum(ntile)
    tile_ids = jnp.arange(max_tiles, dtype=jnp.int32)
    tile_valid = (tile_ids < tile_end[-1]).astype(jnp.int32)
    last_used = jnp.minimum(tile_ids, tile_end[-1] - 1)
    tile_cls = jnp.sum((last_used[:, None] >= tile_end[None, :]).astype(jnp.int32), axis=1)
    pair_a = jnp.array([0, 0, 0, 1, 1, 2], jnp.int32)
    pair_b = jnp.array([1, 2, 3, 2, 3, 3], jnp.int32)
    tile_ea = (tile_cls // N_PAIRS) * PER_GROUP + pair_a[tile_cls % N_PAIRS]
    tile_eb = (tile_cls // N_PAIRS) * PER_GROUP + pair_b[tile_cls % N_PAIRS]

    n_post2 = row(norm_post_ffn[l])
    wg = w_exp_gate[l].astype(BF16)
    wu = w_exp_up[l].astype(BF16)
    wd = w_exp_down[l].astype(BF16)
    inv = _inverse_call(dst, max_tiles * tm)
    f_sorted = _experts_call(tile_ea, tile_eb, tile_valid, inv, pk_all, w_router, b_router, wg, wu, wd)
    y_p = _combine_call(dst[:n_p], x1_p, mod_p, n_post2, f_sorted, 1)
    y_s = _combine_call(dst[n_p:], x1_s, mod_s, n_post2, f_sorted, FINAL_TILE // ts)

    conv_p = conv_p8[:, SUBLANES - (CONV_W - 1):, :]
    lru_p = lru_p8[:, SUBLANES - 1, :]
    xr_s3 = xr_s.reshape(bs, ts, D_LRU)
    conv_s = xr_s3[:, ts - (CONV_W - 1):, :]
    lru_s = h_s.reshape(bs, ts, D_LRU)[:, ts - 1, :]
    return (y_p.reshape(bp, tp, D_MODEL), y_s.reshape(bs, ts, D_MODEL),
            conv_p[None], lru_p[None], ret_p[None],
            conv_s[None], lru_s[None], ret_s[None])
```

```python
import functools
import math

import jax
import jax.numpy as jnp
from jax import lax
from jax.experimental import pallas as pl
from jax.experimental.pallas import tpu as pltpu

F32 = jnp.float32
BF16 = jnp.bfloat16

D_MODEL = 1024
D_LRU = 512
D_RET = 512
N_LRU_BLOCKS = 8
LRU_BLOCK = D_LRU // N_LRU_BLOCKS
CONV_W = 4
LRU_C = 8.0
N_HEADS = 4
DK = 128
DV = 128
RET_CHUNK = 128
ROPE_BASE = 10000.0
D_IN_PROJ = 3072
N_GROUPS = 4
PER_GROUP = 4
N_EXPERTS = 16
D_EXPERT = 256
EXPM1_DIRECT_BELOW = -0.5
NORM_EPS = 1e-6
GN_EPS = 1e-5
PAST_LEN = 16384

SUBLANES = 8
LANES = 128
GATE_HALF = 256
VMEM_LIMIT = 56 * 1024 * 1024

PROMPT_TILE = 512
SAMPLE_SEQS = 16
MOE_TILE = 256
FINAL_TILE = 512
DISPATCH_TILE = 512
RING = 3
DMA_GROUP = 8

N_PAIRS = 6
N_CLASSES = N_GROUPS * N_PAIRS
ROW_GROUPS = D_MODEL // LANES


def _silu(x):
    return x * jax.nn.sigmoid(x)


def _rms_scale(x):
    return lax.rsqrt(jnp.mean(x * x, axis=-1, keepdims=True) + NORM_EPS)


def _masked_softmax(logits, mask):
    top = jnp.max(jnp.where(mask, logits, -jnp.inf), axis=-1, keepdims=True)
    e = jnp.where(mask, jnp.exp(logits - top), 0.0)
    return e / jnp.sum(e, axis=-1, keepdims=True)


def _mod_kernel(c_ref, w_ref, b_ref, o_ref):
    s = _silu(c_ref[...]).astype(BF16)
    o_ref[...] = jnp.dot(s, w_ref[...].astype(BF16), preferred_element_type=F32) + b_ref[...]


def _mod_call(c_all, w_mod, b_mod):
    rows = c_all.shape[0]
    ncol = w_mod.shape[1]
    blk = 1024
    return pl.pallas_call(
        _mod_kernel,
        grid=(ncol // blk,),
        in_specs=[
            pl.BlockSpec((rows, D_MODEL), lambda j: (0, 0)),
            pl.BlockSpec((D_MODEL, blk), lambda j: (0, j)),
            pl.BlockSpec((1, blk), lambda j: (0, j)),
        ],
        out_specs=pl.BlockSpec((rows, blk), lambda j: (0, j)),
        out_shape=jax.ShapeDtypeStruct((rows, ncol), F32),
        compiler_params=pltpu.CompilerParams(
            dimension_semantics=("arbitrary",), vmem_limit_bytes=VMEM_LIMIT),
        name="mod",
    )(c_all, w_mod, b_mod)


def _in_proj(x3, mod3, n_pre1_ref, w_in_ref):
    bb, tt, _ = x3.shape
    sh1 = mod3[:, :, 0:D_MODEL]
    sc1 = mod3[:, :, D_MODEL:2 * D_MODEL]
    coef = n_pre1_ref[...].reshape(1, 1, D_MODEL) * (1.0 + sc1)
    u = (x3 * _rms_scale(x3)) * coef + sh1
    u2d = u.reshape(bb * tt, D_MODEL).astype(BF16)
    return jnp.dot(u2d, w_in_ref[...], preferred_element_type=F32)


def _lru_coeffs(xc, wg_ref, b_r_ref, b_i_ref, lam_ref):
    xcb = xc.astype(BF16)
    g0 = jnp.dot(xcb[:, :GATE_HALF], wg_ref[0], preferred_element_type=F32)
    g1 = jnp.dot(xcb[:, GATE_HALF:], wg_ref[1], preferred_element_type=F32)
    r = jax.nn.sigmoid(jnp.concatenate([g0[:, :GATE_HALF], g1[:, :GATE_HALF]], axis=1) + b_r_ref[...])
    i = jax.nn.sigmoid(jnp.concatenate([g0[:, GATE_HALF:], g1[:, GATE_HALF:]], axis=1) + b_i_ref[...])
    lam = lam_ref[...]
    sp = jnp.maximum(-lam, 0.0) + jnp.log1p(jnp.exp(-jnp.abs(lam)))
    log_a = -LRU_C * r * sp
    a = jnp.exp(log_a)
    y = 2.0 * log_a
    a2 = a * a
    d = a2 - 1.0
    small = d * y / jnp.log(a2)
    em1 = jnp.where(y < EXPM1_DIRECT_BELOW, d, jnp.where(d == 0.0, y, small))
    gain = jnp.sqrt(-em1)
    return a, gain * (i * xc)


def _rope(xh, cos2, sin2, lane_axis):
    return xh * cos2 + pltpu.roll(xh, DK // 2, axis=lane_axis) * sin2


def _group_norm(o):
    mu = jnp.mean(o, axis=-1, keepdims=True)
    d = o - mu
    var = jnp.mean(d * d, axis=-1, keepdims=True)
    return d * lax.rsqrt(var + GN_EPS)


def _post_mixer(x3, mod3, out_a, out_b, w_out_ref, n_post1_ref, n_pre2_ref, w_router_ref, b_router_ref,
                x1_ref, pk_ref, route_ref, cnt_scr):
    bb, tt, _ = x3.shape
    m = bb * tt
    y = (jnp.dot(out_a.astype(BF16), w_out_ref[0:D_LRU, :], preferred_element_type=F32)
         + jnp.dot(out_b.astype(BF16), w_out_ref[D_LRU:, :], preferred_element_type=F32))
    g1 = mod3[:, :, 2 * D_MODEL:3 * D_MODEL]
    sh2 = mod3[:, :, 3 * D_MODEL:4 * D_MODEL]
    sc2 = mod3[:, :, 4 * D_MODEL:5 * D_MODEL]
    y3 = y.reshape(bb, tt, D_MODEL)
    x1 = x3 + (y3 * _rms_scale(y3)) * (g1 * n_post1_ref[...].reshape(1, 1, D_MODEL))
    u2 = (x1 * _rms_scale(x1)) * (n_pre2_ref[...].reshape(1, 1, D_MODEL) * (1.0 + sc2)) + sh2
    x1_ref[...] = x1.reshape(m, D_MODEL)
    u2f = u2.reshape(m, D_MODEL)
    for j in range(ROW_GROUPS):
        pk_ref[pl.ds(j, m, stride=SUBLANES), :] = u2f[:, j * LANES:(j + 1) * LANES]
    u2b = u2f.astype(BF16)

    logits = jnp.dot(u2b, w_router_ref[...], preferred_element_type=F32) + b_router_ref[...]
    lane = lax.broadcasted_iota(jnp.int32, (m, LANES), 1)
    is_g = (lane >= N_EXPERTS) & (lane < N_EXPERTS + N_GROUPS)
    p_group = _masked_softmax(logits, is_g)
    p_g = jnp.max(p_group, axis=-1, keepdims=True)
    g_lane = jnp.min(jnp.where(is_g & (p_group == p_g), lane, LANES), axis=-1, keepdims=True)
    e_lo = (g_lane - N_EXPERTS) * PER_GROUP
    in_g = (lane >= e_lo) & (lane < e_lo + PER_GROUP)
    p_e = _masked_softmax(logits, in_g)
    pm = jnp.where(in_g, p_e, -1.0)
    w1 = jnp.max(pm, axis=-1, keepdims=True)
    i1 = jnp.min(jnp.where(pm == w1, lane, LANES), axis=-1, keepdims=True)
    pm2 = jnp.where(lane == i1, -1.0, pm)
    w2 = jnp.max(pm2, axis=-1, keepdims=True)
    i2 = jnp.min(jnp.where(pm2 == w2, lane, LANES), axis=-1, keepdims=True)
    a = jnp.minimum(i1, i2) - e_lo
    b = jnp.maximum(i1, i2) - e_lo
    pair = jnp.where(a == 0, b - 1, jnp.where(a == 1, b + 1, 5))
    cls = (g_lane - N_EXPERTS) * N_PAIRS + pair
    onehot = lane == cls
    r_i = lax.broadcasted_iota(jnp.int32, (m, m), 0)
    c_i = lax.broadcasted_iota(jnp.int32, (m, m), 1)
    earlier = jnp.where(r_i > c_i, 1.0, 0.0).astype(BF16)
    prefix = jnp.dot(earlier, jnp.where(onehot, 1.0, 0.0).astype(BF16), preferred_element_type=F32)
    run = cnt_scr[0:1, :]
    rank = jnp.sum(jnp.where(onehot, prefix + run, 0.0), axis=-1, keepdims=True)
    cnt_scr[...] = jnp.broadcast_to(
        run + jnp.sum(jnp.where(onehot, 1.0, 0.0), axis=0, keepdims=True), cnt_scr.shape)
    route = jnp.where(lane == 0, cls.astype(F32), jnp.where(lane == 1, rank, 0.0))
    route_ref[...] = jnp.transpose(route)[0:SUBLANES, :]


def _group_scan(a3, b3):
    tpos = lax.broadcasted_iota(jnp.int32, a3.shape, 1)
    s = 1
    while s < a3.shape[1]:
        keep = tpos >= s
        a_sh = jnp.where(keep, pltpu.roll(a3, s, axis=1), 1.0)
        b_sh = jnp.where(keep, pltpu.roll(b3, s, axis=1), 0.0)
        b3 = a3 * b_sh + b3
        a3 = a3 * a_sh
        s *= 2
    return a3, b3


def _scan_rows(a, b, h0):
    n, c = a.shape
    groups = n // SUBLANES
    a3, b3 = _group_scan(a.reshape(groups, SUBLANES, c), b.reshape(groups, SUBLANES, c))
    carry = h0
    out = []
    for g in range(groups):
        hg = b3[g] + a3[g] * carry
        out.append(hg)
        carry = hg[SUBLANES - 1:SUBLANES, :]
    return jnp.concatenate(out, axis=0)


def _prompt_mixer_kernel(x_ref, mod_ref, cos_ref, sin_ref,
                         n_pre1_ref, n_post1_ref, n_pre2_ref,
                         w_in_ref, conv_w_ref, conv_b_ref, wg_ref, b_r_ref, b_i_ref, lam_ref,
                         gn_w_ref, w_out_ref, w_router_ref, b_router_ref,
                         mask_ref, wstate_ref, cross_ref, cdecay_ref,
                         x1_ref, pk_ref, route_ref, conv_out_ref, lru_out_ref, ret_out_ref, cnt_out_ref,
                         conv_scr, h_scr, s_scr, cnt_scr):
    t = pl.program_id(1)
    tt = x_ref.shape[1]

    @pl.when((pl.program_id(0) == 0) & (t == 0))
    def _():
        cnt_scr[...] = jnp.zeros_like(cnt_scr)

    @pl.when(t == 0)
    def _():
        conv_scr[...] = jnp.zeros_like(conv_scr)
        h_scr[...] = jnp.zeros_like(h_scr)
        s_scr[...] = jnp.zeros_like(s_scr)

    x3 = x_ref[...]
    mod3 = mod_ref[...]
    z = _in_proj(x3, mod3, n_pre1_ref, w_in_ref)
    xr = z[:, 0:D_LRU]
    yg = z[:, D_LRU:2 * D_LRU]
    q = z[:, 1024:1536]
    k = z[:, 1536:2048]
    v = z[:, 2048:2560]
    g = z[:, 2560:3072]

    groups = tt // SUBLANES
    xr3 = xr.reshape(groups, SUBLANES, D_LRU)
    tpos = lax.broadcasted_iota(jnp.int32, xr3.shape, 1)
    tail = conv_scr[...]
    xc3 = jnp.broadcast_to(conv_b_ref[...].reshape(1, 1, D_LRU), xr3.shape)
    for j in range(CONV_W):
        back = CONV_W - 1 - j
        w_j = conv_w_ref[j:j + 1, :].reshape(1, 1, D_LRU)
        if back == 0:
            term = xr3
        else:
            cur = pltpu.roll(xr3, back, axis=1)
            first = pltpu.roll(tail, back, axis=0).reshape(1, SUBLANES, D_LRU)
            prev = jnp.concatenate([first, cur[:groups - 1]], axis=0)
            term = jnp.where(tpos >= back, cur, prev)
        xc3 = xc3 + term * w_j
    xc = xc3.reshape(tt, D_LRU)
    conv_scr[...] = xr[tt - SUBLANES:, :]

    a, b = _lru_coeffs(xc, wg_ref, b_r_ref, b_i_ref, lam_ref)
    hseq = _scan_rows(a, b, h_scr[0:1, :])
    h_scr[...] = jnp.broadcast_to(hseq[tt - 1:tt, :], h_scr.shape)
    out_a = hseq * jax.nn.gelu(yg, approximate=True)

    cos2 = cos_ref[...]
    sin2 = sin_ref[...]
    scale = DK ** -0.5
    o_heads = []
    for h in range(N_HEADS):
        hs = slice(h * DK, (h + 1) * DK)
        qh = (_rope(q[:, hs], cos2, sin2, 1) * scale).astype(BF16)
        kh = _rope(k[:, hs], cos2, sin2, 1)
        vh = v[:, hs].astype(BF16)
        o_chunks = []
        for c in range(tt // RET_CHUNK):
            cs = slice(c * RET_CHUNK, (c + 1) * RET_CHUNK)
            qc = qh[cs]
            kc = kh[cs]
            vc = vh[cs]
            s_prev = s_scr[h]
            scores = lax.dot_general(qc, kc.astype(BF16), (((1,), (1,)), ((), ())),
                                     preferred_element_type=F32) * mask_ref[h]
            inner = jnp.dot(scores.astype(BF16), vc, preferred_element_type=F32)
            cross = jnp.dot(qc, s_prev.astype(BF16), preferred_element_type=F32) * cross_ref[:, hs]
            kw = (kc * wstate_ref[:, hs]).astype(BF16)
            kv = lax.dot_general(kw, vc, (((0,), (0,)), ((), ())), preferred_element_type=F32)
            s_scr[h] = cdecay_ref[:, hs] * s_prev + kv
            o_chunks.append(inner + cross)
        o_heads.append(_group_norm(jnp.concatenate(o_chunks, axis=0)))
    o = jnp.concatenate(o_heads, axis=1)
    out_b = o * gn_w_ref[...] * _silu(g)

    _post_mixer(x3, mod3, out_a, out_b, w_out_ref, n_post1_ref, n_pre2_ref, w_router_ref, b_router_ref,
                x1_ref, pk_ref, route_ref, cnt_scr)
    cnt_out_ref[...] = cnt_scr[...]

    @pl.when(t == pl.num_programs(1) - 1)
    def _():
        conv_out_ref[0] = xr[tt - SUBLANES:, :]
        lru_out_ref[0] = hseq[tt - SUBLANES:, :]
        ret_out_ref[0] = s_scr[...]


def _const_spec(shape):
    nd = len(shape)
    return pl.BlockSpec(shape, lambda *_: (0,) * nd)


def _prompt_mixer_call(x, mod3, cos2, sin2, wts, tables):
    bsz, seq, _ = x.shape
    tt = PROMPT_TILE
    nt = seq // tt
    n_tok = bsz * seq
    (n_pre1, n_post1, n_pre2, w_in, conv_w, conv_b, wg, b_r, b_i, lam, gn_w, w_out, w_router, b_router) = wts
    mask, wstate, cross, cdecay = tables
    tok_spec = pl.BlockSpec((tt, D_MODEL), lambda b, t: (b * nt + t, 0))
    in_specs = [
        pl.BlockSpec((1, tt, D_MODEL), lambda b, t: (b, t, 0)),
        pl.BlockSpec((1, 1, 6 * D_MODEL), lambda b, t: (b, 0, 0)),
        pl.BlockSpec((tt, LANES), lambda b, t: (t, 0)),
        pl.BlockSpec((tt, LANES), lambda b, t: (t, 0)),
    ] + [_const_spec(w.shape) for w in wts] + [_const_spec(tb.shape) for tb in tables]
    out_specs = [
        tok_spec,
        pl.BlockSpec((tt * SUBLANES, LANES), lambda b, t: (b * nt + t, 0)),
        pl.BlockSpec((SUBLANES, tt), lambda b, t: (0, b * nt + t)),
        pl.BlockSpec((1, SUBLANES, D_LRU), lambda b, t: (b, 0, 0)),
        pl.BlockSpec((1, SUBLANES, D_LRU), lambda b, t: (b, 0, 0)),
        pl.BlockSpec((1, N_HEADS, DK, DV), lambda b, t: (b, 0, 0, 0)),
        pl.BlockSpec((SUBLANES, LANES), lambda b, t: (0, 0)),
    ]
    out_shape = [
        jax.ShapeDtypeStruct((n_tok, D_MODEL), F32),
        jax.ShapeDtypeStruct((n_tok * SUBLANES, LANES), F32),
        jax.ShapeDtypeStruct((SUBLANES, n_tok), F32),
        jax.ShapeDtypeStruct((bsz, SUBLANES, D_LRU), F32),
        jax.ShapeDtypeStruct((bsz, SUBLANES, D_LRU), F32),
        jax.ShapeDtypeStruct((bsz, N_HEADS, DK, DV), F32),
        jax.ShapeDtypeStruct((SUBLANES, LANES), F32),
    ]
    return pl.pallas_call(
        _prompt_mixer_kernel,
        grid=(bsz, nt),
        in_specs=in_specs,
        out_specs=out_specs,
        out_shape=out_shape,
        scratch_shapes=[
            pltpu.VMEM((SUBLANES, D_LRU), F32),
            pltpu.VMEM((SUBLANES, D_LRU), F32),
            pltpu.VMEM((N_HEADS, DK, DV), F32),
            pltpu.VMEM((SUBLANES, LANES), F32),
        ],
        compiler_params=pltpu.CompilerParams(
            dimension_semantics=("arbitrary", "arbitrary"), vmem_limit_bytes=VMEM_LIMIT),
        name="prompt_mixer",
    )(x, mod3, cos2, sin2, *wts, *tables)


def _sample_mixer_kernel(x_ref, mod_ref, cos_ref, sin_ref, buf_ref, h0_ref, s0_ref, cnt_in_ref,
                         n_pre1_ref, n_post1_ref, n_pre2_ref,
                         w_in_ref, conv_w_ref, conv_b_ref, wg_ref, b_r_ref, b_i_ref, lam_ref,
                         gn_w_ref, w_out_ref, w_router_ref, b_router_ref,
                         smask_ref, wstate_ref, cross_ref, cdecay_ref,
                         x1_ref, pk_ref, route_ref, xr_out_ref, h_out_ref, ret_out_ref, cnt_out_ref,
                         cnt_scr):
    bb, ts, _ = x_ref.shape
    m = bb * ts

    @pl.when(pl.program_id(0) == 0)
    def _():
        cnt_scr[...] = cnt_in_ref[...]

    x3 = x_ref[...]
    mod3 = mod_ref[...]
    z = _in_proj(x3, mod3, n_pre1_ref, w_in_ref)
    xr = z[:, 0:D_LRU]
    yg = z[:, D_LRU:2 * D_LRU]
    q = z[:, 1024:1536]
    k = z[:, 1536:2048]
    v = z[:, 2048:2560]
    g = z[:, 2560:3072]
    xr_out_ref[...] = xr

    xr3 = xr.reshape(bb, ts, D_LRU)
    buf3 = buf_ref[...]
    tpos = lax.broadcasted_iota(jnp.int32, (bb, ts, D_LRU), 1)
    xc3 = jnp.broadcast_to(conv_b_ref[...].reshape(1, 1, D_LRU), (bb, ts, D_LRU))
    for j in range(CONV_W):
        back = CONV_W - 1 - j
        w_j = conv_w_ref[j:j + 1, :].reshape(1, 1, D_LRU)
        if back == 0:
            term = xr3
        else:
            cur = pltpu.roll(xr3, back, axis=1)
            up = CONV_W - 1 - back
            old = buf3 if up == 0 else pltpu.roll(buf3, ts - up, axis=1)
            term = jnp.where(tpos >= back, cur, old)
        xc3 = xc3 + term * w_j
    xc = xc3.reshape(m, D_LRU)

    a, b = _lru_coeffs(xc, wg_ref, b_r_ref, b_i_ref, lam_ref)
    a3 = a.reshape(bb, ts, D_LRU)
    b3 = b.reshape(bb, ts, D_LRU) + a3 * h0_ref[...]
    _, h3 = _group_scan(a3, b3)
    hseq = h3.reshape(m, D_LRU)
    h_out_ref[...] = hseq
    out_a = hseq * jax.nn.gelu(yg, approximate=True)

    cos2 = cos_ref[...].reshape(1, ts, LANES)
    sin2 = sin_ref[...].reshape(1, ts, LANES)
    scale = DK ** -0.5
    o_heads = []
    for h in range(N_HEADS):
        hs = slice(h * DK, (h + 1) * DK)
        q3 = (_rope(q[:, hs].reshape(bb, ts, DK), cos2, sin2, 2) * scale).astype(BF16)
        k3 = _rope(k[:, hs].reshape(bb, ts, DK), cos2, sin2, 2)
        v3 = v[:, hs].reshape(bb, ts, DV).astype(BF16)
        q2 = q3.reshape(m, DK)
        k2 = k3.reshape(m, DK).astype(BF16)
        v2 = v3.reshape(m, DV)
        scores = lax.dot_general(q2, k2, (((1,), (1,)), ((), ())),
                                 preferred_element_type=F32) * smask_ref[h]
        inner = jnp.dot(scores.astype(BF16), v2, preferred_element_type=F32)
        s0h = s0_ref[:, h]
        cross = jnp.einsum('bid,bde->bie', q3, s0h.astype(BF16), preferred_element_type=F32)
        cross = cross * cross_ref[:, hs].reshape(1, ts, DV)
        kw3 = (k3 * wstate_ref[:, hs].reshape(1, ts, DK)).astype(BF16)
        kv = jnp.einsum('bjd,bje->bde', kw3, v3, preferred_element_type=F32)
        ret_out_ref[:, h] = cdecay_ref[:, hs].reshape(1, 1, DV) * s0h + kv
        o_heads.append(_group_norm(inner + cross.reshape(m, DV)))
    o = jnp.concatenate(o_heads, axis=1)
    out_b = o * gn_w_ref[...] * _silu(g)

    _post_mixer(x3, mod3, out_a, out_b, w_out_ref, n_post1_ref, n_pre2_ref, w_router_ref, b_router_ref,
                x1_ref, pk_ref, route_ref, cnt_scr)
    cnt_out_ref[...] = cnt_scr[...]


def _sample_mixer_call(x, mod3, cos2, sin2, buf8, h0p, s0, cnt_in, wts, tables):
    bsz, ts, _ = x.shape
    bb = SAMPLE_SEQS
    m = bb * ts
    n_tok = bsz * ts
    seq_spec = lambda w: pl.BlockSpec((bb, ts, w), lambda i: (i, 0, 0))
    tok_spec = lambda w: pl.BlockSpec((m, w), lambda i: (i, 0))
    in_specs = [
        seq_spec(D_MODEL),
        pl.BlockSpec((bb, 1, 6 * D_MODEL), lambda i: (i, 0, 0)),
        _const_spec(cos2.shape),
        _const_spec(sin2.shape),
        seq_spec(D_LRU),
        seq_spec(D_LRU),
        pl.BlockSpec((bb, N_HEADS, DK, DV), lambda i: (i, 0, 0, 0)),
        _const_spec(cnt_in.shape),
    ] + [_const_spec(w.shape) for w in wts] + [_const_spec(tb.shape) for tb in tables]
    out_specs = [
        tok_spec(D_MODEL),
        pl.BlockSpec((m * SUBLANES, LANES), lambda i: (i, 0)),
        pl.BlockSpec((SUBLANES, m), lambda i: (0, i)),
        tok_spec(D_LRU),
        tok_spec(D_LRU),
        pl.BlockSpec((bb, N_HEADS, DK, DV), lambda i: (i, 0, 0, 0)),
        _const_spec(cnt_in.shape),
    ]
    out_shape = [
        jax.ShapeDtypeStruct((n_tok, D_MODEL), F32),
        jax.ShapeDtypeStruct((n_tok * SUBLANES, LANES), F32),
        jax.ShapeDtypeStruct((SUBLANES, n_tok), F32),
        jax.ShapeDtypeStruct((n_tok, D_LRU), F32),
        jax.ShapeDtypeStruct((n_tok, D_LRU), F32),
        jax.ShapeDtypeStruct((bsz, N_HEADS, DK, DV), F32),
        jax.ShapeDtypeStruct(cnt_in.shape, F32),
    ]
    return pl.pallas_call(
        _sample_mixer_kernel,
        grid=(bsz // bb,),
        in_specs=in_specs,
        out_specs=out_specs,
        out_shape=out_shape,
        scratch_shapes=[pltpu.VMEM((SUBLANES, LANES), F32)],
        compiler_params=pltpu.CompilerParams(
            dimension_semantics=("arbitrary",), vmem_limit_bytes=VMEM_LIMIT),
        name="sample_mixer",
    )(x, mod3, cos2, sin2, buf8, h0p, s0, cnt_in, *wts, *tables)


def _slab(ref, r):
    return ref.at[pl.ds(pl.multiple_of(r * SUBLANES, SUBLANES), SUBLANES)]


def _dispatch_kernel(cls_ref, rank_ref, cnt_ref, padcnt_ref, start_ref, srcp_ref, srcs_ref, out_ref, ring, sem,
                     *, p_tiles):
    i = pl.program_id(0)
    n = pl.num_programs(0)
    tile_rows = ring.shape[1]
    td = tile_rows // SUBLANES
    moe_rows = MOE_TILE * SUBLANES

    def issue(slot):
        def body(g, carry):
            first = i * td + g * DMA_GROUP
            slots = [start_ref[cls_ref[first + j]] + rank_ref[first + j] for j in range(DMA_GROUP)]
            first = g * DMA_GROUP
            for j in range(DMA_GROUP):
                pltpu.make_async_copy(
                    _slab(ring.at[slot], first + j), _slab(out_ref, slots[j]), sem.at[slot]
                ).start(priority=j % 2)
            return carry
        lax.fori_loop(0, td // DMA_GROUP, body, 0)

    def wait_tile(slot):
        pltpu.make_async_copy(ring.at[slot], out_ref.at[pl.ds(0, tile_rows)], sem.at[slot]).wait()

    def zero_fill(slot):
        ring[slot] = jnp.zeros((tile_rows, LANES), F32)
        zero_src = ring.at[slot]

        def per_class(c, carry):
            lo = start_ref[c] + cnt_ref[c]
            hi = start_ref[c] + padcnt_ref[c]

            def fill(r, carry2):
                pltpu.make_async_copy(_slab(zero_src, 0), _slab(out_ref, r), sem.at[slot]).start()
                return carry2
            lax.fori_loop(lo, hi, fill, 0)

            def done(r, carry2):
                pltpu.make_async_copy(_slab(zero_src, 0), _slab(out_ref, 0), sem.at[slot]).wait()
                return carry2
            lax.fori_loop(lo, hi, done, 0)
            return carry
        lax.fori_loop(0, N_CLASSES, per_class, 0)

        used_tiles = (start_ref[N_CLASSES - 1] + padcnt_ref[N_CLASSES - 1]) // MOE_TILE
        all_tiles = out_ref.shape[0] // moe_rows

        def tile_copy(t):
            return pltpu.make_async_copy(
                zero_src.at[pl.ds(0, moe_rows)],
                out_ref.at[pl.ds(pl.multiple_of(t * moe_rows, moe_rows), moe_rows)], sem.at[slot])

        def fill_tile(t, carry):
            tile_copy(t).start()
            return carry
        lax.fori_loop(used_tiles, all_tiles, fill_tile, 0)

        def done_tile(t, carry):
            tile_copy(t).wait()
            return carry
        lax.fori_loop(used_tiles, all_tiles, done_tile, 0)

    for s in range(RING):
        @pl.when(i % RING == s)
        def _():
            @pl.when(i < p_tiles)
            def _():
                ring[s] = srcp_ref[...]

            @pl.when(i >= p_tiles)
            def _():
                ring[s] = srcs_ref[...]

            issue(s)

            @pl.when(i >= RING - 1)
            def _():
                wait_tile((s + 1) % RING)

            @pl.when(i == n - 1)
            def _():
                for back in range(RING - 2, -1, -1):
                    wait_tile((s - back) % RING)
                zero_fill(s)


def _dispatch_call(cls, rank, cnt, padcnt, start, pk_p, pk_s, n_rows):
    td = DISPATCH_TILE
    tile_rows = td * SUBLANES
    p_tiles = pk_p.shape[0] // tile_rows
    s_tiles = pk_s.shape[0] // tile_rows
    assert p_tiles + s_tiles >= RING and td >= MOE_TILE
    return pl.pallas_call(
        functools.partial(_dispatch_kernel, p_tiles=p_tiles),
        grid_spec=pltpu.PrefetchScalarGridSpec(
            num_scalar_prefetch=5,
            grid=(p_tiles + s_tiles,),
            in_specs=[
                pl.BlockSpec((tile_rows, LANES), lambda i, *_: (jnp.minimum(i, p_tiles - 1), 0)),
                pl.BlockSpec((tile_rows, LANES), lambda i, *_: (jnp.maximum(i - p_tiles, 0), 0)),
            ],
            out_specs=pl.BlockSpec(memory_space=pl.ANY),
            scratch_shapes=[pltpu.VMEM((RING, tile_rows, LANES), F32),
                            pltpu.SemaphoreType.DMA((RING,))],
        ),
        out_shape=jax.ShapeDtypeStruct((n_rows * SUBLANES, LANES), F32),
        compiler_params=pltpu.CompilerParams(
            dimension_semantics=("arbitrary",), has_side_effects=True, vmem_limit_bytes=VMEM_LIMIT),
        name="moe_dispatch",
    )(cls, rank, cnt, padcnt, start, pk_p, pk_s)


def _moe_kernel(ea_ref, eb_ref, valid_ref, xs_ref, w_router_ref, b_router_ref,
                wga_ref, wua_ref, wda_ref, wgb_ref, wub_ref, wdb_ref, f_ref):
    t = pl.program_id(0)

    @pl.when(valid_ref[t] == 1)
    def _():
        tm = xs_ref.shape[0] // SUBLANES
        x = jnp.concatenate([xs_ref[pl.ds(j, tm, stride=SUBLANES), :] for j in range(ROW_GROUPS)],
                            axis=1).astype(BF16)
        e_a = ea_ref[t]
        e_b = eb_ref[t]
        e_lo = (e_a // PER_GROUP) * PER_GROUP
        logits = jnp.dot(x, w_router_ref[...], preferred_element_type=F32) + b_router_ref[...]
        lane = lax.broadcasted_iota(jnp.int32, (tm, LANES), 1)
        pick = lambda p, idx: jnp.sum(jnp.where(lane == idx, p, 0.0), axis=-1, keepdims=True)
        p_group = _masked_softmax(logits, (lane >= N_EXPERTS) & (lane < N_EXPERTS + N_GROUPS))
        p_g = pick(p_group, N_EXPERTS + e_a // PER_GROUP)
        p_e = _masked_softmax(logits, (lane >= e_lo) & (lane < e_lo + PER_GROUP))
        w_a = pick(p_e, e_a)
        w_b = pick(p_e, e_b)
        wsum = w_a + w_b

        def expert(wg_ref, wu_ref, gate):
            hg = jnp.dot(x, wg_ref[0], preferred_element_type=F32)
            hu = jnp.dot(x, wu_ref[0], preferred_element_type=F32)
            return (_silu(hg) * hu * gate).astype(BF16)

        ha = expert(wga_ref, wua_ref, p_g * (w_a / wsum))
        hb = expert(wgb_ref, wub_ref, p_g * (w_b / wsum))
        for c in range(D_MODEL // GATE_HALF):
            cols = slice(c * GATE_HALF, (c + 1) * GATE_HALF)
            f = (jnp.dot(ha, wda_ref[0, :, cols], preferred_element_type=F32)
                 + jnp.dot(hb, wdb_ref[0, :, cols], preferred_element_type=F32))
            for jj in range(GATE_HALF // LANES):
                j = c * (GATE_HALF // LANES) + jj
                f_ref[pl.ds(j, tm, stride=SUBLANES), :] = f[:, jj * LANES:(jj + 1) * LANES]

    @pl.when(valid_ref[t] == 0)
    def _():
        f_ref[...] = jnp.zeros_like(f_ref)


def _moe_call(tile_ea, tile_eb, tile_valid, rows, w_router, b_router, wg, wu, wd):
    n_rows = rows.shape[0] // SUBLANES
    tm = MOE_TILE
    const = lambda a: pl.BlockSpec(a.shape, lambda t, ea, eb, v: (0,) * a.ndim)
    up = lambda sel: pl.BlockSpec((1, D_MODEL, D_EXPERT), lambda t, ea, eb, v: (sel(ea, eb)[t], 0, 0))
    down = lambda sel: pl.BlockSpec((1, D_EXPERT, D_MODEL), lambda t, ea, eb, v: (sel(ea, eb)[t], 0, 0))
    first = lambda ea, eb: ea
    second = lambda ea, eb: eb
    return pl.pallas_call(
        _moe_kernel,
        grid_spec=pltpu.PrefetchScalarGridSpec(
            num_scalar_prefetch=3,
            grid=(n_rows // tm,),
            in_specs=[
                pl.BlockSpec((tm * SUBLANES, LANES), lambda t, ea, eb, v: (t, 0)),
                const(w_router), const(b_router),
                up(first), up(first), down(first), up(second), up(second), down(second),
            ],
            out_specs=pl.BlockSpec((tm * SUBLANES, LANES), lambda t, ea, eb, v: (t, 0)),
        ),
        out_shape=jax.ShapeDtypeStruct((n_rows * SUBLANES, LANES), F32),
        compiler_params=pltpu.CompilerParams(
            dimension_semantics=("arbitrary",), vmem_limit_bytes=VMEM_LIMIT),
        name="moe_experts",
    )(tile_ea, tile_eb, tile_valid, rows, w_router, b_router, wg, wu, wd, wg, wu, wd)


def _combine_kernel(cls_ref, rank_ref, start_ref, x1_ref, mod_ref, n_post2_ref, f_hbm, o_ref, fbuf, sem):
    i = pl.program_id(0)
    n = pl.num_programs(0)
    tf = x1_ref.shape[0]

    def issue(tile, slot):
        def body(g, carry):
            first = tile * tf + g * DMA_GROUP
            slots = [start_ref[cls_ref[first + j]] + rank_ref[first + j] for j in range(DMA_GROUP)]
            first = g * DMA_GROUP
            for j in range(DMA_GROUP):
                pltpu.make_async_copy(
                    _slab(f_hbm, slots[j]), _slab(fbuf.at[slot], first + j), sem.at[slot]
                ).start(priority=j % 2)
            return carry
        lax.fori_loop(0, tf // DMA_GROUP, body, 0)

    def finish(slot):
        pltpu.make_async_copy(f_hbm.at[pl.ds(0, tf * SUBLANES)], fbuf.at[slot], sem.at[slot]).wait()
        f = jnp.concatenate([fbuf[slot, pl.ds(j, tf, stride=SUBLANES), :] for j in range(D_MODEL // LANES)],
                            axis=1)
        mod3 = mod_ref[...]
        bb = mod3.shape[0]
        f3 = f.reshape(bb, tf // bb, D_MODEL)
        g2 = mod3[:, :, 5 * D_MODEL:6 * D_MODEL]
        x13 = x1_ref[...].reshape(f3.shape)
        out = x13 + (f3 * _rms_scale(f3)) * (g2 * n_post2_ref[...].reshape(1, 1, D_MODEL))
        o_ref[...] = out.reshape(o_ref.shape)

    @pl.when(i == 0)
    def _():
        issue(0, 0)

    for parity in range(2):
        @pl.when(i % 2 == parity)
        def _():
            @pl.when(i + 1 < n)
            def _():
                issue(i + 1, 1 - parity)
            finish(parity)


def _combine_call(cls, rank, start, x1, mod3, n_post2, f_sorted, seqs_per_tile):
    n_tok = x1.shape[0]
    tf = FINAL_TILE
    tok_per_seq = n_tok // mod3.shape[0]
    if seqs_per_tile > 1:
        mod_map = lambda i, *_: (i, 0, 0)
    else:
        mod_map = lambda i, *_: ((i * tf) // tok_per_seq, 0, 0)
    return pl.pallas_call(
        _combine_kernel,
        grid_spec=pltpu.PrefetchScalarGridSpec(
            num_scalar_prefetch=3,
            grid=(n_tok // tf,),
            in_specs=[
                pl.BlockSpec((tf, D_MODEL), lambda i, *_: (i, 0)),
                pl.BlockSpec((seqs_per_tile, 1, 6 * D_MODEL), mod_map),
                pl.BlockSpec((1, D_MODEL), lambda i, *_: (0, 0)),
                pl.BlockSpec(memory_space=pl.ANY),
            ],
            out_specs=pl.BlockSpec((tf, D_MODEL), lambda i, *_: (i, 0)),
            scratch_shapes=[pltpu.VMEM((2, tf * SUBLANES, LANES), F32), pltpu.SemaphoreType.DMA((2,))],
        ),
        out_shape=jax.ShapeDtypeStruct((n_tok, D_MODEL), F32),
        compiler_params=pltpu.CompilerParams(
            dimension_semantics=("arbitrary",), vmem_limit_bytes=VMEM_LIMIT),
        name="moe_combine",
    )(cls, rank, start, x1, mod3, n_post2, f_sorted)


def _block_diag_gate(w_r, w_i):
    per_half = GATE_HALF // LRU_BLOCK
    halves = []
    for hb in range(D_LRU // GATE_HALF):
        blocks = []
        for w in (w_r, w_i):
            mat = jnp.zeros((GATE_HALF, GATE_HALF), F32)
            for n in range(per_half):
                lo = n * LRU_BLOCK
                mat = mat.at[lo:lo + LRU_BLOCK, lo:lo + LRU_BLOCK].set(w[hb * per_half + n])
            blocks.append(mat)
        halves.append(jnp.concatenate(blocks, axis=1))
    return jnp.stack(halves).astype(BF16)


def _rope_tables(pos):
    half = DK // 2
    inv = ROPE_BASE ** (-jnp.arange(half, dtype=F32) / half)
    ang = pos[:, None] * inv[None, :]
    cos = jnp.cos(ang)
    sin = jnp.sin(ang)
    return jnp.concatenate([cos, cos], axis=-1), jnp.concatenate([-sin, sin], axis=-1)


def _decay_tables(c):
    log_g = jnp.log1p(-jnp.exp2(-5.0 - jnp.arange(N_HEADS, dtype=F32)))
    idx = jnp.arange(c, dtype=F32)
    diff = idx[:, None] - idx[None, :]
    mask = jnp.where(diff[None] >= 0, jnp.exp(jnp.maximum(diff, 0.0)[None] * log_g[:, None, None]), 0.0)
    w_state = jnp.exp((c - 1.0 - idx)[None, :] * log_g[:, None])
    cross_decay = jnp.exp((idx + 1.0)[:, None] * log_g[None, :])
    chunk_decay = jnp.exp(c * log_g)
    wstate_full = jnp.repeat(w_state.T, DK, axis=1)
    cross_full = jnp.repeat(cross_decay, DV, axis=1)
    cdecay_full = jnp.repeat(chunk_decay, DV)[None, :]
    return mask, wstate_full, cross_full, cdecay_full


def kernel(x_prompt, x_sample, state_conv, state_lru, state_ret, c_prompt, c_sample, w_mod, b_mod, norm_pre_mix, norm_post_mix, norm_pre_ffn, norm_post_ffn, w_in, conv_w, conv_b, w_rgate, b_rgate, w_igate, b_igate, lru_lambda, ret_gn_w, w_out, w_router_group, b_router_group, w_router_expert, b_router_expert, w_exp_gate, w_exp_up, w_exp_down):
    bp, tp, _ = x_prompt.shape
    bs, ts, _ = x_sample.shape
    l = 0

    mod = _mod_call(jnp.concatenate([c_prompt, c_sample], axis=0), w_mod[l], b_mod[l][None, :])
    mod_p = mod[:bp][:, None, :]
    mod_s = mod[bp:][:, None, :]

    w_router = jnp.zeros((D_MODEL, LANES), F32)
    w_router = w_router.at[:, :N_EXPERTS].set(w_router_expert[l])
    w_router = w_router.at[:, N_EXPERTS:N_EXPERTS + N_GROUPS].set(w_router_group[l]).astype(BF16)
    b_router = jnp.zeros((1, LANES), F32)
    b_router = b_router.at[0, :N_EXPERTS].set(b_router_expert[l])
    b_router = b_router.at[0, N_EXPERTS:N_EXPERTS + N_GROUPS].set(b_router_group[l])

    row = lambda vec: vec.reshape(1, -1)
    wts = (row(norm_pre_mix[l]), row(norm_post_mix[l]), row(norm_pre_ffn[l]),
           w_in[l].astype(BF16), conv_w[l], row(conv_b[l]),
           _block_diag_gate(w_rgate[l], w_igate[l]),
           row(b_rgate[l]), row(b_igate[l]), row(lru_lambda[l]), row(ret_gn_w[l]),
           w_out[l].astype(BF16), w_router, b_router)

    cos_p, sin_p = _rope_tables(jnp.arange(tp, dtype=F32))
    x1_p, pk_p, route_p, conv_p8, lru_p8, ret_p, cnt_p = _prompt_mixer_call(
        x_prompt, mod_p, cos_p, sin_p, wts, _decay_tables(math.gcd(tp, RET_CHUNK)))

    cos_s, sin_s = _rope_tables(jnp.float32(PAST_LEN) + jnp.arange(ts, dtype=F32))
    mask8, wstate_s, cross_s, cdecay_s = _decay_tables(math.gcd(ts, RET_CHUNK))
    eye = jnp.eye(SAMPLE_SEQS, dtype=F32)
    smask = jnp.stack([jnp.kron(eye, mask8[h]) for h in range(N_HEADS)])
    buf8 = jnp.pad(state_conv[l], ((0, 0), (0, ts - (CONV_W - 1)), (0, 0)))
    h0p = jnp.pad(state_lru[l][:, None, :], ((0, 0), (0, ts - 1), (0, 0)))
    x1_s, pk_s, route_s, xr_s, h_s, ret_s, cnt_all = _sample_mixer_call(
        x_sample, mod_s, cos_s, sin_s, buf8, h0p, state_ret[l], cnt_p, wts,
        (smask, wstate_s, cross_s, cdecay_s))

    n_p = bp * tp
    n_tok = n_p + bs * ts
    tm = MOE_TILE
    max_tiles = n_tok // tm + N_CLASSES
    route = jnp.concatenate([route_p[0:2], route_s[0:2]], axis=1).astype(jnp.int32)
    cls = route[0]
    rank = route[1]
    cnt = cnt_all[0, :N_CLASSES].astype(jnp.int32)
    ntile = (cnt + (tm - 1)) // tm
    padcnt = ntile * tm
    start = jnp.cumsum(padcnt) - padcnt
    tile_end = jnp.cumsum(ntile)
    tile_ids = jnp.arange(max_tiles, dtype=jnp.int32)
    tile_valid = (tile_ids < tile_end[-1]).astype(jnp.int32)
    last_used = jnp.minimum(tile_ids, tile_end[-1] - 1)
    tile_cls = jnp.sum((last_used[:, None] >= tile_end[None, :]).astype(jnp.int32), axis=1)
    pair_a = jnp.array([0, 0, 0, 1, 1, 2], jnp.int32)
    pair_b = jnp.array([1, 2, 3, 2, 3, 3], jnp.int32)
    tile_ea = (tile_cls // N_PAIRS) * PER_GROUP + pair_a[tile_cls % N_PAIRS]
    tile_eb = (tile_cls // N_PAIRS) * PER_GROUP + pair_b[tile_cls % N_PAIRS]

    n_post2 = row(norm_post_ffn[l])
    wg = w_exp_gate[l].astype(BF16)
    wu = w_exp_up[l].astype(BF16)
    wd = w_exp_down[l].astype(BF16)
    rows = _dispatch_call(cls, rank, cnt, padcnt, start, pk_p, pk_s, max_tiles * tm)
    f_sorted = _moe_call(tile_ea, tile_eb, tile_valid, rows, w_router, b_router, wg, wu, wd)
    y_p = _combine_call(cls[:n_p], rank[:n_p], start, x1_p, mod_p, n_post2, f_sorted, 1)
    y_s = _combine_call(cls[n_p:], rank[n_p:], start, x1_s, mod_s, n_post2, f_sorted, FINAL_TILE // ts)

    conv_p = conv_p8[:, SUBLANES - (CONV_W - 1):, :]
    lru_p = lru_p8[:, SUBLANES - 1, :]
    xr_s3 = xr_s.reshape(bs, ts, D_LRU)
    conv_s = xr_s3[:, ts - (CONV_W - 1):, :]
    lru_s = h_s.reshape(bs, ts, D_LRU)[:, ts - 1, :]
    return (y_p.reshape(bp, tp, D_MODEL), y_s.reshape(bs, ts, D_MODEL),
            conv_p[None], lru_p[None], ret_p[None],
            conv_s[None], lru_s[None], ret_s[None])
```

```python
import functools
import math

import jax
import jax.numpy as jnp
from jax import lax
from jax.experimental import pallas as pl
from jax.experimental.pallas import tpu as pltpu

F32 = jnp.float32
BF16 = jnp.bfloat16

D_MODEL = 1024
D_LRU = 512
D_RET = 512
N_LRU_BLOCKS = 8
LRU_BLOCK = D_LRU // N_LRU_BLOCKS
CONV_W = 4
LRU_C = 8.0
N_HEADS = 4
DK = 128
DV = 128
RET_CHUNK = 128
ROPE_BASE = 10000.0
D_IN_PROJ = 3072
N_GROUPS = 4
PER_GROUP = 4
N_EXPERTS = 16
D_EXPERT = 256
EXPM1_DIRECT_BELOW = -0.5
NORM_EPS = 1e-6
GN_EPS = 1e-5
PAST_LEN = 16384

SUBLANES = 8
LANES = 128
GATE_HALF = 256
VMEM_LIMIT = 56 * 1024 * 1024

PROMPT_TILE = 512
SAMPLE_SEQS = 16
MOE_TILE = 256
FINAL_TILE = 512
DISPATCH_TILE = 512
RING = 3
DMA_GROUP = 8

N_PAIRS = 6
N_CLASSES = N_GROUPS * N_PAIRS
ROW_GROUPS = D_MODEL // LANES


def _silu(x):
    return x * jax.nn.sigmoid(x)


def _rms_scale(x):
    return lax.rsqrt(jnp.mean(x * x, axis=-1, keepdims=True) + NORM_EPS)


def _masked_softmax(logits, mask):
    top = jnp.max(jnp.where(mask, logits, -jnp.inf), axis=-1, keepdims=True)
    e = jnp.where(mask, jnp.exp(logits - top), 0.0)
    return e / jnp.sum(e, axis=-1, keepdims=True)


def _mod_kernel(c_ref, w_ref, b_ref, o_ref):
    s = _silu(c_ref[...]).astype(BF16)
    o_ref[...] = jnp.dot(s, w_ref[...].astype(BF16), preferred_element_type=F32) + b_ref[...]


def _mod_call(c_all, w_mod, b_mod):
    rows = c_all.shape[0]
    ncol = w_mod.shape[1]
    blk = 1024
    return pl.pallas_call(
        _mod_kernel,
        grid=(ncol // blk,),
        in_specs=[
            pl.BlockSpec((rows, D_MODEL), lambda j: (0, 0)),
            pl.BlockSpec((D_MODEL, blk), lambda j: (0, j)),
            pl.BlockSpec((1, blk), lambda j: (0, j)),
        ],
        out_specs=pl.BlockSpec((rows, blk), lambda j: (0, j)),
        out_shape=jax.ShapeDtypeStruct((rows, ncol), F32),
        compiler_params=pltpu.CompilerParams(
            dimension_semantics=("arbitrary",), vmem_limit_bytes=VMEM_LIMIT),
        name="mod",
    )(c_all, w_mod, b_mod)


def _in_proj(x3, mod3, n_pre1_ref, w_in_ref):
    bb, tt, _ = x3.shape
    sh1 = mod3[:, :, 0:D_MODEL]
    sc1 = mod3[:, :, D_MODEL:2 * D_MODEL]
    coef = n_pre1_ref[...].reshape(1, 1, D_MODEL) * (1.0 + sc1)
    u = (x3 * _rms_scale(x3)) * coef + sh1
    u2d = u.reshape(bb * tt, D_MODEL).astype(BF16)
    return jnp.dot(u2d, w_in_ref[...], preferred_element_type=F32)


def _lru_coeffs(xc, wg_ref, b_r_ref, b_i_ref, lam_ref):
    xcb = xc.astype(BF16)
    g0 = jnp.dot(xcb[:, :GATE_HALF], wg_ref[0], preferred_element_type=F32)
    g1 = jnp.dot(xcb[:, GATE_HALF:], wg_ref[1], preferred_element_type=F32)
    r = jax.nn.sigmoid(jnp.concatenate([g0[:, :GATE_HALF], g1[:, :GATE_HALF]], axis=1) + b_r_ref[...])
    i = jax.nn.sigmoid(jnp.concatenate([g0[:, GATE_HALF:], g1[:, GATE_HALF:]], axis=1) + b_i_ref[...])
    lam = lam_ref[...]
    sp = jnp.maximum(-lam, 0.0) + jnp.log1p(jnp.exp(-jnp.abs(lam)))
    log_a = -LRU_C * r * sp
    a = jnp.exp(log_a)
    y = 2.0 * log_a
    a2 = a * a
    d = a2 - 1.0
    small = d * y / jnp.log(a2)
    em1 = jnp.where(y < EXPM1_DIRECT_BELOW, d, jnp.where(d == 0.0, y, small))
    gain = jnp.sqrt(-em1)
    return a, gain * (i * xc)


def _rope(xh, cos2, sin2, lane_axis):
    return xh * cos2 + pltpu.roll(xh, DK // 2, axis=lane_axis) * sin2


def _group_norm(o):
    mu = jnp.mean(o, axis=-1, keepdims=True)
    d = o - mu
    var = jnp.mean(d * d, axis=-1, keepdims=True)
    return d * lax.rsqrt(var + GN_EPS)


def _post_mixer(x3, mod3, out_a, out_b, w_out_ref, n_post1_ref, n_pre2_ref, w_router_ref, b_router_ref,
                x1_ref, pk_ref, route_ref, cnt_scr):
    bb, tt, _ = x3.shape
    m = bb * tt
    y = (jnp.dot(out_a.astype(BF16), w_out_ref[0:D_LRU, :], preferred_element_type=F32)
         + jnp.dot(out_b.astype(BF16), w_out_ref[D_LRU:, :], preferred_element_type=F32))
    g1 = mod3[:, :, 2 * D_MODEL:3 * D_MODEL]
    sh2 = mod3[:, :, 3 * D_MODEL:4 * D_MODEL]
    sc2 = mod3[:, :, 4 * D_MODEL:5 * D_MODEL]
    y3 = y.reshape(bb, tt, D_MODEL)
    x1 = x3 + (y3 * _rms_scale(y3)) * (g1 * n_post1_ref[...].reshape(1, 1, D_MODEL))
    u2 = (x1 * _rms_scale(x1)) * (n_pre2_ref[...].reshape(1, 1, D_MODEL) * (1.0 + sc2)) + sh2
    x1_ref[...] = x1.reshape(m, D_MODEL)
    u2b = u2.reshape(m, D_MODEL).astype(BF16)
    pk_ref[...] = u2b

    logits = jnp.dot(u2b, w_router_ref[...], preferred_element_type=F32) + b_router_ref[...]
    lane = lax.broadcasted_iota(jnp.int32, (m, LANES), 1)
    is_g = (lane >= N_EXPERTS) & (lane < N_EXPERTS + N_GROUPS)
    p_group = _masked_softmax(logits, is_g)
    p_g = jnp.max(p_group, axis=-1, keepdims=True)
    g_lane = jnp.min(jnp.where(is_g & (p_group == p_g), lane, LANES), axis=-1, keepdims=True)
    e_lo = (g_lane - N_EXPERTS) * PER_GROUP
    in_g = (lane >= e_lo) & (lane < e_lo + PER_GROUP)
    p_e = _masked_softmax(logits, in_g)
    pm = jnp.where(in_g, p_e, -1.0)
    w1 = jnp.max(pm, axis=-1, keepdims=True)
    i1 = jnp.min(jnp.where(pm == w1, lane, LANES), axis=-1, keepdims=True)
    pm2 = jnp.where(lane == i1, -1.0, pm)
    w2 = jnp.max(pm2, axis=-1, keepdims=True)
    i2 = jnp.min(jnp.where(pm2 == w2, lane, LANES), axis=-1, keepdims=True)
    a = jnp.minimum(i1, i2) - e_lo
    b = jnp.maximum(i1, i2) - e_lo
    pair = jnp.where(a == 0, b - 1, jnp.where(a == 1, b + 1, 5))
    cls = (g_lane - N_EXPERTS) * N_PAIRS + pair
    onehot = lane == cls
    r_i = lax.broadcasted_iota(jnp.int32, (m, m), 0)
    c_i = lax.broadcasted_iota(jnp.int32, (m, m), 1)
    earlier = jnp.where(r_i > c_i, 1.0, 0.0).astype(BF16)
    prefix = jnp.dot(earlier, jnp.where(onehot, 1.0, 0.0).astype(BF16), preferred_element_type=F32)
    run = cnt_scr[0:1, :]
    rank = jnp.sum(jnp.where(onehot, prefix + run, 0.0), axis=-1, keepdims=True)
    cnt_scr[...] = jnp.broadcast_to(
        run + jnp.sum(jnp.where(onehot, 1.0, 0.0), axis=0, keepdims=True), cnt_scr.shape)
    route = jnp.where(lane == 0, cls.astype(F32), jnp.where(lane == 1, rank, 0.0))
    route_ref[...] = jnp.transpose(route)[0:SUBLANES, :]


def _group_scan(a3, b3):
    tpos = lax.broadcasted_iota(jnp.int32, a3.shape, 1)
    s = 1
    while s < a3.shape[1]:
        keep = tpos >= s
        a_sh = jnp.where(keep, pltpu.roll(a3, s, axis=1), 1.0)
        b_sh = jnp.where(keep, pltpu.roll(b3, s, axis=1), 0.0)
        b3 = a3 * b_sh + b3
        a3 = a3 * a_sh
        s *= 2
    return a3, b3


def _scan_rows(a, b, h0):
    n, c = a.shape
    groups = n // SUBLANES
    a3, b3 = _group_scan(a.reshape(groups, SUBLANES, c), b.reshape(groups, SUBLANES, c))
    carry = h0
    out = []
    for g in range(groups):
        hg = b3[g] + a3[g] * carry
        out.append(hg)
        carry = hg[SUBLANES - 1:SUBLANES, :]
    return jnp.concatenate(out, axis=0)


def _prompt_mixer_kernel(x_ref, mod_ref, cos_ref, sin_ref,
                         n_pre1_ref, n_post1_ref, n_pre2_ref,
                         w_in_ref, conv_w_ref, conv_b_ref, wg_ref, b_r_ref, b_i_ref, lam_ref,
                         gn_w_ref, w_out_ref, w_router_ref, b_router_ref,
                         mask_ref, wstate_ref, cross_ref, cdecay_ref,
                         x1_ref, pk_ref, route_ref, conv_out_ref, lru_out_ref, ret_out_ref, cnt_out_ref,
                         conv_scr, h_scr, s_scr, cnt_scr):
    t = pl.program_id(1)
    tt = x_ref.shape[1]

    @pl.when((pl.program_id(0) == 0) & (t == 0))
    def _():
        cnt_scr[...] = jnp.zeros_like(cnt_scr)

    @pl.when(t == 0)
    def _():
        conv_scr[...] = jnp.zeros_like(conv_scr)
        h_scr[...] = jnp.zeros_like(h_scr)
        s_scr[...] = jnp.zeros_like(s_scr)

    x3 = x_ref[...]
    mod3 = mod_ref[...]
    z = _in_proj(x3, mod3, n_pre1_ref, w_in_ref)
    xr = z[:, 0:D_LRU]
    yg = z[:, D_LRU:2 * D_LRU]
    q = z[:, 1024:1536]
    k = z[:, 1536:2048]
    v = z[:, 2048:2560]
    g = z[:, 2560:3072]

    groups = tt // SUBLANES
    xr3 = xr.reshape(groups, SUBLANES, D_LRU)
    tpos = lax.broadcasted_iota(jnp.int32, xr3.shape, 1)
    tail = conv_scr[...]
    xc3 = jnp.broadcast_to(conv_b_ref[...].reshape(1, 1, D_LRU), xr3.shape)
    for j in range(CONV_W):
        back = CONV_W - 1 - j
        w_j = conv_w_ref[j:j + 1, :].reshape(1, 1, D_LRU)
        if back == 0:
            term = xr3
        else:
            cur = pltpu.roll(xr3, back, axis=1)
            first = pltpu.roll(tail, back, axis=0).reshape(1, SUBLANES, D_LRU)
            prev = jnp.concatenate([first, cur[:groups - 1]], axis=0)
            term = jnp.where(tpos >= back, cur, prev)
        xc3 = xc3 + term * w_j
    xc = xc3.reshape(tt, D_LRU)
    conv_scr[...] = xr[tt - SUBLANES:, :]

    a, b = _lru_coeffs(xc, wg_ref, b_r_ref, b_i_ref, lam_ref)
    hseq = _scan_rows(a, b, h_scr[0:1, :])
    h_scr[...] = jnp.broadcast_to(hseq[tt - 1:tt, :], h_scr.shape)
    out_a = hseq * jax.nn.gelu(yg, approximate=True)

    cos2 = cos_ref[...]
    sin2 = sin_ref[...]
    scale = DK ** -0.5
    o_heads = []
    for h in range(N_HEADS):
        hs = slice(h * DK, (h + 1) * DK)
        qh = (_rope(q[:, hs], cos2, sin2, 1) * scale).astype(BF16)
        kh = _rope(k[:, hs], cos2, sin2, 1)
        vh = v[:, hs].astype(BF16)
        o_chunks = []
        for c in range(tt // RET_CHUNK):
            cs = slice(c * RET_CHUNK, (c + 1) * RET_CHUNK)
            qc = qh[cs]
            kc = kh[cs]
            vc = vh[cs]
            s_prev = s_scr[h]
            scores = lax.dot_general(qc, kc.astype(BF16), (((1,), (1,)), ((), ())),
                                     preferred_element_type=F32) * mask_ref[h]
            inner = jnp.dot(scores.astype(BF16), vc, preferred_element_type=F32)
            cross = jnp.dot(qc, s_prev.astype(BF16), preferred_element_type=F32) * cross_ref[:, hs]
            kw = (kc * wstate_ref[:, hs]).astype(BF16)
            kv = lax.dot_general(kw, vc, (((0,), (0,)), ((), ())), preferred_element_type=F32)
            s_scr[h] = cdecay_ref[:, hs] * s_prev + kv
            o_chunks.append(inner + cross)
        o_heads.append(_group_norm(jnp.concatenate(o_chunks, axis=0)))
    o = jnp.concatenate(o_heads, axis=1)
    out_b = o * gn_w_ref[...] * _silu(g)

    _post_mixer(x3, mod3, out_a, out_b, w_out_ref, n_post1_ref, n_pre2_ref, w_router_ref, b_router_ref,
                x1_ref, pk_ref, route_ref, cnt_scr)
    cnt_out_ref[...] = cnt_scr[...]

    @pl.when(t == pl.num_programs(1) - 1)
    def _():
        conv_out_ref[0] = xr[tt - SUBLANES:, :]
        lru_out_ref[0] = hseq[tt - SUBLANES:, :]
        ret_out_ref[0] = s_scr[...]


def _const_spec(shape):
    nd = len(shape)
    return pl.BlockSpec(shape, lambda *_: (0,) * nd)


def _prompt_mixer_call(x, mod3, cos2, sin2, wts, tables):
    bsz, seq, _ = x.shape
    tt = PROMPT_TILE
    nt = seq // tt
    n_tok = bsz * seq
    (n_pre1, n_post1, n_pre2, w_in, conv_w, conv_b, wg, b_r, b_i, lam, gn_w, w_out, w_router, b_router) = wts
    mask, wstate, cross, cdecay = tables
    tok_spec = pl.BlockSpec((tt, D_MODEL), lambda b, t: (b * nt + t, 0))
    in_specs = [
        pl.BlockSpec((1, tt, D_MODEL), lambda b, t: (b, t, 0)),
        pl.BlockSpec((1, 1, 6 * D_MODEL), lambda b, t: (b, 0, 0)),
        pl.BlockSpec((tt, LANES), lambda b, t: (t, 0)),
        pl.BlockSpec((tt, LANES), lambda b, t: (t, 0)),
    ] + [_const_spec(w.shape) for w in wts] + [_const_spec(tb.shape) for tb in tables]
    out_specs = [
        tok_spec,
        tok_spec,
        pl.BlockSpec((SUBLANES, tt), lambda b, t: (0, b * nt + t)),
        pl.BlockSpec((1, SUBLANES, D_LRU), lambda b, t: (b, 0, 0)),
        pl.BlockSpec((1, SUBLANES, D_LRU), lambda b, t: (b, 0, 0)),
        pl.BlockSpec((1, N_HEADS, DK, DV), lambda b, t: (b, 0, 0, 0)),
        pl.BlockSpec((SUBLANES, LANES), lambda b, t: (0, 0)),
    ]
    out_shape = [
        jax.ShapeDtypeStruct((n_tok, D_MODEL), F32),
        jax.ShapeDtypeStruct((n_tok, D_MODEL), BF16),
        jax.ShapeDtypeStruct((SUBLANES, n_tok), F32),
        jax.ShapeDtypeStruct((bsz, SUBLANES, D_LRU), F32),
        jax.ShapeDtypeStruct((bsz, SUBLANES, D_LRU), F32),
        jax.ShapeDtypeStruct((bsz, N_HEADS, DK, DV), F32),
        jax.ShapeDtypeStruct((SUBLANES, LANES), F32),
    ]
    return pl.pallas_call(
        _prompt_mixer_kernel,
        grid=(bsz, nt),
        in_specs=in_specs,
        out_specs=out_specs,
        out_shape=out_shape,
        scratch_shapes=[
            pltpu.VMEM((SUBLANES, D_LRU), F32),
            pltpu.VMEM((SUBLANES, D_LRU), F32),
            pltpu.VMEM((N_HEADS, DK, DV), F32),
            pltpu.VMEM((SUBLANES, LANES), F32),
        ],
        compiler_params=pltpu.CompilerParams(
            dimension_semantics=("arbitrary", "arbitrary"), vmem_limit_bytes=VMEM_LIMIT),
        name="prompt_mixer",
    )(x, mod3, cos2, sin2, *wts, *tables)


def _sample_mixer_kernel(x_ref, mod_ref, cos_ref, sin_ref, buf_ref, h0_ref, s0_ref, cnt_in_ref,
                         n_pre1_ref, n_post1_ref, n_pre2_ref,
                         w_in_ref, conv_w_ref, conv_b_ref, wg_ref, b_r_ref, b_i_ref, lam_ref,
                         gn_w_ref, w_out_ref, w_router_ref, b_router_ref,
                         smask_ref, wstate_ref, cross_ref, cdecay_ref,
                         x1_ref, pk_ref, route_ref, xr_out_ref, h_out_ref, ret_out_ref, cnt_out_ref,
                         cnt_scr):
    bb, ts, _ = x_ref.shape
    m = bb * ts

    @pl.when(pl.program_id(0) == 0)
    def _():
        cnt_scr[...] = cnt_in_ref[...]

    x3 = x_ref[...]
    mod3 = mod_ref[...]
    z = _in_proj(x3, mod3, n_pre1_ref, w_in_ref)
    xr = z[:, 0:D_LRU]
    yg = z[:, D_LRU:2 * D_LRU]
    q = z[:, 1024:1536]
    k = z[:, 1536:2048]
    v = z[:, 2048:2560]
    g = z[:, 2560:3072]
    xr_out_ref[...] = xr

    xr3 = xr.reshape(bb, ts, D_LRU)
    buf3 = buf_ref[...]
    tpos = lax.broadcasted_iota(jnp.int32, (bb, ts, D_LRU), 1)
    xc3 = jnp.broadcast_to(conv_b_ref[...].reshape(1, 1, D_LRU), (bb, ts, D_LRU))
    for j in range(CONV_W):
        back = CONV_W - 1 - j
        w_j = conv_w_ref[j:j + 1, :].reshape(1, 1, D_LRU)
        if back == 0:
            term = xr3
        else:
            cur = pltpu.roll(xr3, back, axis=1)
            up = CONV_W - 1 - back
            old = buf3 if up == 0 else pltpu.roll(buf3, ts - up, axis=1)
            term = jnp.where(tpos >= back, cur, old)
        xc3 = xc3 + term * w_j
    xc = xc3.reshape(m, D_LRU)

    a, b = _lru_coeffs(xc, wg_ref, b_r_ref, b_i_ref, lam_ref)
    a3 = a.reshape(bb, ts, D_LRU)
    b3 = b.reshape(bb, ts, D_LRU) + a3 * h0_ref[...]
    _, h3 = _group_scan(a3, b3)
    hseq = h3.reshape(m, D_LRU)
    h_out_ref[...] = hseq
    out_a = hseq * jax.nn.gelu(yg, approximate=True)

    cos2 = cos_ref[...].reshape(1, ts, LANES)
    sin2 = sin_ref[...].reshape(1, ts, LANES)
    scale = DK ** -0.5
    o_heads = []
    for h in range(N_HEADS):
        hs = slice(h * DK, (h + 1) * DK)
        q3 = (_rope(q[:, hs].reshape(bb, ts, DK), cos2, sin2, 2) * scale).astype(BF16)
        k3 = _rope(k[:, hs].reshape(bb, ts, DK), cos2, sin2, 2)
        v3 = v[:, hs].reshape(bb, ts, DV).astype(BF16)
        q2 = q3.reshape(m, DK)
        k2 = k3.reshape(m, DK).astype(BF16)
        v2 = v3.reshape(m, DV)
        scores = lax.dot_general(q2, k2, (((1,), (1,)), ((), ())),
                                 preferred_element_type=F32) * smask_ref[h]
        inner = jnp.dot(scores.astype(BF16), v2, preferred_element_type=F32)
        s0h = s0_ref[:, h]
        cross = jnp.einsum('bid,bde->bie', q3, s0h.astype(BF16), preferred_element_type=F32)
        cross = cross * cross_ref[:, hs].reshape(1, ts, DV)
        kw3 = (k3 * wstate_ref[:, hs].reshape(1, ts, DK)).astype(BF16)
        kv = jnp.einsum('bjd,bje->bde', kw3, v3, preferred_element_type=F32)
        ret_out_ref[:, h] = cdecay_ref[:, hs].reshape(1, 1, DV) * s0h + kv
        o_heads.append(_group_norm(inner + cross.reshape(m, DV)))
    o = jnp.concatenate(o_heads, axis=1)
    out_b = o * gn_w_ref[...] * _silu(g)

    _post_mixer(x3, mod3, out_a, out_b, w_out_ref, n_post1_ref, n_pre2_ref, w_router_ref, b_router_ref,
                x1_ref, pk_ref, route_ref, cnt_scr)
    cnt_out_ref[...] = cnt_scr[...]


def _sample_mixer_call(x, mod3, cos2, sin2, buf8, h0p, s0, cnt_in, wts, tables):
    bsz, ts, _ = x.shape
    bb = SAMPLE_SEQS
    m = bb * ts
    n_tok = bsz * ts
    seq_spec = lambda w: pl.BlockSpec((bb, ts, w), lambda i: (i, 0, 0))
    tok_spec = lambda w: pl.BlockSpec((m, w), lambda i: (i, 0))
    in_specs = [
        seq_spec(D_MODEL),
        pl.BlockSpec((bb, 1, 6 * D_MODEL), lambda i: (i, 0, 0)),
        _const_spec(cos2.shape),
        _const_spec(sin2.shape),
        seq_spec(D_LRU),
        seq_spec(D_LRU),
        pl.BlockSpec((bb, N_HEADS, DK, DV), lambda i: (i, 0, 0, 0)),
        _const_spec(cnt_in.shape),
    ] + [_const_spec(w.shape) for w in wts] + [_const_spec(tb.shape) for tb in tables]
    out_specs = [
        tok_spec(D_MODEL),
        tok_spec(D_MODEL),
        pl.BlockSpec((SUBLANES, m), lambda i: (0, i)),
        tok_spec(D_LRU),
        tok_spec(D_LRU),
        pl.BlockSpec((bb, N_HEADS, DK, DV), lambda i: (i, 0, 0, 0)),
        _const_spec(cnt_in.shape),
    ]
    out_shape = [
        jax.ShapeDtypeStruct((n_tok, D_MODEL), F32),
        jax.ShapeDtypeStruct((n_tok, D_MODEL), BF16),
        jax.ShapeDtypeStruct((SUBLANES, n_tok), F32),
        jax.ShapeDtypeStruct((n_tok, D_LRU), F32),
        jax.ShapeDtypeStruct((n_tok, D_LRU), F32),
        jax.ShapeDtypeStruct((bsz, N_HEADS, DK, DV), F32),
        jax.ShapeDtypeStruct(cnt_in.shape, F32),
    ]
    return pl.pallas_call(
        _sample_mixer_kernel,
        grid=(bsz // bb,),
        in_specs=in_specs,
        out_specs=out_specs,
        out_shape=out_shape,
        scratch_shapes=[pltpu.VMEM((SUBLANES, LANES), F32)],
        compiler_params=pltpu.CompilerParams(
            dimension_semantics=("arbitrary",), vmem_limit_bytes=VMEM_LIMIT),
        name="sample_mixer",
    )(x, mod3, cos2, sin2, buf8, h0p, s0, cnt_in, *wts, *tables)


def _slab(ref, r):
    return ref.at[pl.ds(pl.multiple_of(r * SUBLANES, SUBLANES), SUBLANES)]


def _dispatch_kernel(cls_ref, rank_ref, cnt_ref, padcnt_ref, start_ref, srcp_ref, srcs_ref, out_ref, ring, sem,
                     *, p_tiles):
    i = pl.program_id(0)
    n = pl.num_programs(0)
    tile_rows = ring.shape[1]
    td = tile_rows // SUBLANES
    moe_rows = MOE_TILE * SUBLANES

    def stage(src_ref, slot):
        src = src_ref[...].astype(F32)
        for j in range(ROW_GROUPS):
            ring[slot, pl.ds(j, td, stride=SUBLANES), :] = src[:, j * LANES:(j + 1) * LANES]

    def issue(slot):
        def body(g, carry):
            first = i * td + g * DMA_GROUP
            slots = [start_ref[cls_ref[first + j]] + rank_ref[first + j] for j in range(DMA_GROUP)]
            first = g * DMA_GROUP
            for j in range(DMA_GROUP):
                pltpu.make_async_copy(
                    _slab(ring.at[slot], first + j), _slab(out_ref, slots[j]), sem.at[slot]
                ).start(priority=j % 2)
            return carry
        lax.fori_loop(0, td // DMA_GROUP, body, 0)

    def wait_tile(slot):
        pltpu.make_async_copy(ring.at[slot], out_ref.at[pl.ds(0, tile_rows)], sem.at[slot]).wait()

    def zero_fill(slot):
        ring[slot] = jnp.zeros((tile_rows, LANES), F32)
        zero_src = ring.at[slot]

        def per_class(c, carry):
            lo = start_ref[c] + cnt_ref[c]
            hi = start_ref[c] + padcnt_ref[c]

            def fill(r, carry2):
                pltpu.make_async_copy(_slab(zero_src, 0), _slab(out_ref, r), sem.at[slot]).start()
                return carry2
            lax.fori_loop(lo, hi, fill, 0)

            def done(r, carry2):
                pltpu.make_async_copy(_slab(zero_src, 0), _slab(out_ref, 0), sem.at[slot]).wait()
                return carry2
            lax.fori_loop(lo, hi, done, 0)
            return carry
        lax.fori_loop(0, N_CLASSES, per_class, 0)

        used_tiles = (start_ref[N_CLASSES - 1] + padcnt_ref[N_CLASSES - 1]) // MOE_TILE
        all_tiles = out_ref.shape[0] // moe_rows

        def tile_copy(t):
            return pltpu.make_async_copy(
                zero_src.at[pl.ds(0, moe_rows)],
                out_ref.at[pl.ds(pl.multiple_of(t * moe_rows, moe_rows), moe_rows)], sem.at[slot])

        def fill_tile(t, carry):
            tile_copy(t).start()
            return carry
        lax.fori_loop(used_tiles, all_tiles, fill_tile, 0)

        def done_tile(t, carry):
            tile_copy(t).wait()
            return carry
        lax.fori_loop(used_tiles, all_tiles, done_tile, 0)

    for s in range(RING):
        @pl.when(i % RING == s)
        def _():
            @pl.when(i < p_tiles)
            def _():
                stage(srcp_ref, s)

            @pl.when(i >= p_tiles)
            def _():
                stage(srcs_ref, s)

            issue(s)

            @pl.when(i >= RING - 1)
            def _():
                wait_tile((s + 1) % RING)

            @pl.when(i == n - 1)
            def _():
                for back in range(RING - 2, -1, -1):
                    wait_tile((s - back) % RING)
                zero_fill(s)


def _dispatch_call(cls, rank, cnt, padcnt, start, pk_p, pk_s, n_rows):
    td = DISPATCH_TILE
    tile_rows = td * SUBLANES
    p_tiles = pk_p.shape[0] // td
    s_tiles = pk_s.shape[0] // td
    assert p_tiles + s_tiles >= RING and td >= MOE_TILE
    return pl.pallas_call(
        functools.partial(_dispatch_kernel, p_tiles=p_tiles),
        grid_spec=pltpu.PrefetchScalarGridSpec(
            num_scalar_prefetch=5,
            grid=(p_tiles + s_tiles,),
            in_specs=[
                pl.BlockSpec((td, D_MODEL), lambda i, *_: (jnp.minimum(i, p_tiles - 1), 0)),
                pl.BlockSpec((td, D_MODEL), lambda i, *_: (jnp.maximum(i - p_tiles, 0), 0)),
            ],
            out_specs=pl.BlockSpec(memory_space=pl.ANY),
            scratch_shapes=[pltpu.VMEM((RING, tile_rows, LANES), F32),
                            pltpu.SemaphoreType.DMA((RING,))],
        ),
        out_shape=jax.ShapeDtypeStruct((n_rows * SUBLANES, LANES), F32),
        compiler_params=pltpu.CompilerParams(
            dimension_semantics=("arbitrary",), has_side_effects=True, vmem_limit_bytes=VMEM_LIMIT),
        name="moe_dispatch",
    )(cls, rank, cnt, padcnt, start, pk_p, pk_s)


def _moe_kernel(ea_ref, eb_ref, valid_ref, xs_ref, w_router_ref, b_router_ref,
                wga_ref, wua_ref, wda_ref, wgb_ref, wub_ref, wdb_ref, f_ref):
    t = pl.program_id(0)

    @pl.when(valid_ref[t] == 1)
    def _():
        tm = xs_ref.shape[0] // SUBLANES
        x = jnp.concatenate([xs_ref[pl.ds(j, tm, stride=SUBLANES), :] for j in range(ROW_GROUPS)],
                            axis=1).astype(BF16)
        e_a = ea_ref[t]
        e_b = eb_ref[t]
        e_lo = (e_a // PER_GROUP) * PER_GROUP
        logits = jnp.dot(x, w_router_ref[...], preferred_element_type=F32) + b_router_ref[...]
        lane = lax.broadcasted_iota(jnp.int32, (tm, LANES), 1)
        pick = lambda p, idx: jnp.sum(jnp.where(lane == idx, p, 0.0), axis=-1, keepdims=True)
        p_group = _masked_softmax(logits, (lane >= N_EXPERTS) & (lane < N_EXPERTS + N_GROUPS))
        p_g = pick(p_group, N_EXPERTS + e_a // PER_GROUP)
        p_e = _masked_softmax(logits, (lane >= e_lo) & (lane < e_lo + PER_GROUP))
        w_a = pick(p_e, e_a)
        w_b = pick(p_e, e_b)
        wsum = w_a + w_b

        def expert(wg_ref, wu_ref, gate):
            hg = jnp.dot(x, wg_ref[0].astype(BF16), preferred_element_type=F32)
            hu = jnp.dot(x, wu_ref[0].astype(BF16), preferred_element_type=F32)
            return (_silu(hg) * hu * gate).astype(BF16)

        ha = expert(wga_ref, wua_ref, p_g * (w_a / wsum))
        hb = expert(wgb_ref, wub_ref, p_g * (w_b / wsum))
        for c in range(D_MODEL // GATE_HALF):
            cols = slice(c * GATE_HALF, (c + 1) * GATE_HALF)
            f = (jnp.dot(ha, wda_ref[0, :, cols].astype(BF16), preferred_element_type=F32)
                 + jnp.dot(hb, wdb_ref[0, :, cols].astype(BF16), preferred_element_type=F32))
            for jj in range(GATE_HALF // LANES):
                j = c * (GATE_HALF // LANES) + jj
                f_ref[pl.ds(j, tm, stride=SUBLANES), :] = f[:, jj * LANES:(jj + 1) * LANES]

    @pl.when(valid_ref[t] == 0)
    def _():
        f_ref[...] = jnp.zeros_like(f_ref)


def _moe_call(tile_ea, tile_eb, tile_valid, rows, w_router, b_router, wg, wu, wd):
    n_rows = rows.shape[0] // SUBLANES
    tm = MOE_TILE
    const = lambda a: pl.BlockSpec(a.shape, lambda t, ea, eb, v: (0,) * a.ndim)
    up = lambda sel: pl.BlockSpec((1, D_MODEL, D_EXPERT), lambda t, ea, eb, v: (sel(ea, eb)[t], 0, 0))
    down = lambda sel: pl.BlockSpec((1, D_EXPERT, D_MODEL), lambda t, ea, eb, v: (sel(ea, eb)[t], 0, 0))
    first = lambda ea, eb: ea
    second = lambda ea, eb: eb
    return pl.pallas_call(
        _moe_kernel,
        grid_spec=pltpu.PrefetchScalarGridSpec(
            num_scalar_prefetch=3,
            grid=(n_rows // tm,),
            in_specs=[
                pl.BlockSpec((tm * SUBLANES, LANES), lambda t, ea, eb, v: (t, 0)),
                const(w_router), const(b_router),
                up(first), up(first), down(first), up(second), up(second), down(second),
            ],
            out_specs=pl.BlockSpec((tm * SUBLANES, LANES), lambda t, ea, eb, v: (t, 0)),
        ),
        out_shape=jax.ShapeDtypeStruct((n_rows * SUBLANES, LANES), F32),
        compiler_params=pltpu.CompilerParams(
            dimension_semantics=("arbitrary",), vmem_limit_bytes=VMEM_LIMIT),
        name="moe_experts",
    )(tile_ea, tile_eb, tile_valid, rows, w_router, b_router, wg, wu, wd, wg, wu, wd)


def _combine_kernel(cls_ref, rank_ref, start_ref, x1_ref, mod_ref, n_post2_ref, f_hbm, o_ref, fbuf, sem):
    i = pl.program_id(0)
    n = pl.num_programs(0)
    tf = x1_ref.shape[0]

    def issue(tile, slot):
        def body(g, carry):
            first = tile * tf + g * DMA_GROUP
            slots = [start_ref[cls_ref[first + j]] + rank_ref[first + j] for j in range(DMA_GROUP)]
            first = g * DMA_GROUP
            for j in range(DMA_GROUP):
                pltpu.make_async_copy(
                    _slab(f_hbm, slots[j]), _slab(fbuf.at[slot], first + j), sem.at[slot]
                ).start(priority=j % 2)
            return carry
        lax.fori_loop(0, tf // DMA_GROUP, body, 0)

    def finish(slot):
        pltpu.make_async_copy(f_hbm.at[pl.ds(0, tf * SUBLANES)], fbuf.at[slot], sem.at[slot]).wait()
        f = jnp.concatenate([fbuf[slot, pl.ds(j, tf, stride=SUBLANES), :] for j in range(D_MODEL // LANES)],
                            axis=1)
        mod3 = mod_ref[...]
        bb = mod3.shape[0]
        f3 = f.reshape(bb, tf // bb, D_MODEL)
        g2 = mod3[:, :, 5 * D_MODEL:6 * D_MODEL]
        x13 = x1_ref[...].reshape(f3.shape)
        out = x13 + (f3 * _rms_scale(f3)) * (g2 * n_post2_ref[...].reshape(1, 1, D_MODEL))
        o_ref[...] = out.reshape(o_ref.shape)

    @pl.when(i == 0)
    def _():
        issue(0, 0)

    for parity in range(2):
        @pl.when(i % 2 == parity)
        def _():
            @pl.when(i + 1 < n)
            def _():
                issue(i + 1, 1 - parity)
            finish(parity)


def _combine_call(cls, rank, start, x1, mod3, n_post2, f_sorted, seqs_per_tile):
    n_tok = x1.shape[0]
    tf = FINAL_TILE
    tok_per_seq = n_tok // mod3.shape[0]
    if seqs_per_tile > 1:
        mod_map = lambda i, *_: (i, 0, 0)
    else:
        mod_map = lambda i, *_: ((i * tf) // tok_per_seq, 0, 0)
    return pl.pallas_call(
        _combine_kernel,
        grid_spec=pltpu.PrefetchScalarGridSpec(
            num_scalar_prefetch=3,
            grid=(n_tok // tf,),
            in_specs=[
                pl.BlockSpec((tf, D_MODEL), lambda i, *_: (i, 0)),
                pl.BlockSpec((seqs_per_tile, 1, 6 * D_MODEL), mod_map),
                pl.BlockSpec((1, D_MODEL), lambda i, *_: (0, 0)),
                pl.BlockSpec(memory_space=pl.ANY),
            ],
            out_specs=pl.BlockSpec((tf, D_MODEL), lambda i, *_: (i, 0)),
            scratch_shapes=[pltpu.VMEM((2, tf * SUBLANES, LANES), F32), pltpu.SemaphoreType.DMA((2,))],
        ),
        out_shape=jax.ShapeDtypeStruct((n_tok, D_MODEL), F32),
        compiler_params=pltpu.CompilerParams(
            dimension_semantics=("arbitrary",), vmem_limit_bytes=VMEM_LIMIT),
        name="moe_combine",
    )(cls, rank, start, x1, mod3, n_post2, f_sorted)


def _block_diag_gate(w_r, w_i):
    per_half = GATE_HALF // LRU_BLOCK
    halves = []
    for hb in range(D_LRU // GATE_HALF):
        blocks = []
        for w in (w_r, w_i):
            mat = jnp.zeros((GATE_HALF, GATE_HALF), F32)
            for n in range(per_half):
                lo = n * LRU_BLOCK
                mat = mat.at[lo:lo + LRU_BLOCK, lo:lo + LRU_BLOCK].set(w[hb * per_half + n])
            blocks.append(mat)
        halves.append(jnp.concatenate(blocks, axis=1))
    return jnp.stack(halves).astype(BF16)


def _rope_tables(pos):
    half = DK // 2
    inv = ROPE_BASE ** (-jnp.arange(half, dtype=F32) / half)
    ang = pos[:, None] * inv[None, :]
    cos = jnp.cos(ang)
    sin = jnp.sin(ang)
    return jnp.concatenate([cos, cos], axis=-1), jnp.concatenate([-sin, sin], axis=-1)


def _decay_tables(c):
    log_g = jnp.log1p(-jnp.exp2(-5.0 - jnp.arange(N_HEADS, dtype=F32)))
    idx = jnp.arange(c, dtype=F32)
    diff = idx[:, None] - idx[None, :]
    mask = jnp.where(diff[None] >= 0, jnp.exp(jnp.maximum(diff, 0.0)[None] * log_g[:, None, None]), 0.0)
    w_state = jnp.exp((c - 1.0 - idx)[None, :] * log_g[:, None])
    cross_decay = jnp.exp((idx + 1.0)[:, None] * log_g[None, :])
    chunk_decay = jnp.exp(c * log_g)
    wstate_full = jnp.repeat(w_state.T, DK, axis=1)
    cross_full = jnp.repeat(cross_decay, DV, axis=1)
    cdecay_full = jnp.repeat(chunk_decay, DV)[None, :]
    return mask, wstate_full, cross_full, cdecay_full


def kernel(x_prompt, x_sample, state_conv, state_lru, state_ret, c_prompt, c_sample, w_mod, b_mod, norm_pre_mix, norm_post_mix, norm_pre_ffn, norm_post_ffn, w_in, conv_w, conv_b, w_rgate, b_rgate, w_igate, b_igate, lru_lambda, ret_gn_w, w_out, w_router_group, b_router_group, w_router_expert, b_router_expert, w_exp_gate, w_exp_up, w_exp_down):
    bp, tp, _ = x_prompt.shape
    bs, ts, _ = x_sample.shape
    l = 0

    mod = _mod_call(jnp.concatenate([c_prompt, c_sample], axis=0), w_mod[l], b_mod[l][None, :])
    mod_p = mod[:bp][:, None, :]
    mod_s = mod[bp:][:, None, :]

    w_router = jnp.zeros((D_MODEL, LANES), F32)
    w_router = w_router.at[:, :N_EXPERTS].set(w_router_expert[l])
    w_router = w_router.at[:, N_EXPERTS:N_EXPERTS + N_GROUPS].set(w_router_group[l]).astype(BF16)
    b_router = jnp.zeros((1, LANES), F32)
    b_router = b_router.at[0, :N_EXPERTS].set(b_router_expert[l])
    b_router = b_router.at[0, N_EXPERTS:N_EXPERTS + N_GROUPS].set(b_router_group[l])

    row = lambda vec: vec.reshape(1, -1)
    wts = (row(norm_pre_mix[l]), row(norm_post_mix[l]), row(norm_pre_ffn[l]),
           w_in[l].astype(BF16), conv_w[l], row(conv_b[l]),
           _block_diag_gate(w_rgate[l], w_igate[l]),
           row(b_rgate[l]), row(b_igate[l]), row(lru_lambda[l]), row(ret_gn_w[l]),
           w_out[l].astype(BF16), w_router, b_router)

    cos_p, sin_p = _rope_tables(jnp.arange(tp, dtype=F32))
    x1_p, pk_p, route_p, conv_p8, lru_p8, ret_p, cnt_p = _prompt_mixer_call(
        x_prompt, mod_p, cos_p, sin_p, wts, _decay_tables(math.gcd(tp, RET_CHUNK)))

    cos_s, sin_s = _rope_tables(jnp.float32(PAST_LEN) + jnp.arange(ts, dtype=F32))
    mask8, wstate_s, cross_s, cdecay_s = _decay_tables(math.gcd(ts, RET_CHUNK))
    eye = jnp.eye(SAMPLE_SEQS, dtype=F32)
    smask = jnp.stack([jnp.kron(eye, mask8[h]) for h in range(N_HEADS)])
    buf8 = jnp.pad(state_conv[l], ((0, 0), (0, ts - (CONV_W - 1)), (0, 0)))
    h0p = jnp.pad(state_lru[l][:, None, :], ((0, 0), (0, ts - 1), (0, 0)))
    x1_s, pk_s, route_s, xr_s, h_s, ret_s, cnt_all = _sample_mixer_call(
        x_sample, mod_s, cos_s, sin_s, buf8, h0p, state_ret[l], cnt_p, wts,
        (smask, wstate_s, cross_s, cdecay_s))

    n_p = bp * tp
    n_tok = n_p + bs * ts
    tm = MOE_TILE
    max_tiles = n_tok // tm + N_CLASSES
    route = jnp.concatenate([route_p[0:2], route_s[0:2]], axis=1).astype(jnp.int32)
    cls = route[0]
    rank = route[1]
    cnt = cnt_all[0, :N_CLASSES].astype(jnp.int32)
    ntile = (cnt + (tm - 1)) // tm
    padcnt = ntile * tm
    start = jnp.cumsum(padcnt) - padcnt
    tile_end = jnp.cumsum(ntile)
    tile_ids = jnp.arange(max_tiles, dtype=jnp.int32)
    tile_valid = (tile_ids < tile_end[-1]).astype(jnp.int32)
    last_used = jnp.minimum(tile_ids, tile_end[-1] - 1)
    tile_cls = jnp.sum((last_used[:, None] >= tile_end[None, :]).astype(jnp.int32), axis=1)
    pair_a = jnp.array([0, 0, 0, 1, 1, 2], jnp.int32)
    pair_b = jnp.array([1, 2, 3, 2, 3, 3], jnp.int32)
    tile_ea = (tile_cls // N_PAIRS) * PER_GROUP + pair_a[tile_cls % N_PAIRS]
    tile_eb = (tile_cls // N_PAIRS) * PER_GROUP + pair_b[tile_cls % N_PAIRS]

    n_post2 = row(norm_post_ffn[l])
    wg = w_exp_gate[l]
    wu = w_exp_up[l]
    wd = w_exp_down[l]
    rows = _dispatch_call(cls, rank, cnt, padcnt, start, pk_p, pk_s, max_tiles * tm)
    f_sorted = _moe_call(tile_ea, tile_eb, tile_valid, rows, w_router, b_router, wg, wu, wd)
    y_p = _combine_call(cls[:n_p], rank[:n_p], start, x1_p, mod_p, n_post2, f_sorted, 1)
    y_s = _combine_call(cls[n_p:], rank[n_p:], start, x1_s, mod_s, n_post2, f_sorted, FINAL_TILE // ts)

    conv_p = conv_p8[:, SUBLANES - (CONV_W - 1):, :]
    lru_p = lru_p8[:, SUBLANES - 1, :]
    xr_s3 = xr_s.reshape(bs, ts, D_LRU)
    conv_s = xr_s3[:, ts - (CONV_W - 1):, :]
    lru_s = h_s.reshape(bs, ts, D_LRU)[:, ts - 1, :]
    return (y_p.reshape(bp, tp, D_MODEL), y_s.reshape(bs, ts, D_MODEL),
            conv_p[None], lru_p[None], ret_p[None],
            conv_s[None], lru_s[None], ret_s[None])
```

```python
import functools
import math

import jax
import jax.numpy as jnp
from jax import lax
from jax.experimental import pallas as pl
from jax.experimental.pallas import tpu as pltpu

F32 = jnp.float32
BF16 = jnp.bfloat16

D_MODEL = 1024
D_LRU = 512
D_RET = 512
N_LRU_BLOCKS = 8
LRU_BLOCK = D_LRU // N_LRU_BLOCKS
CONV_W = 4
LRU_C = 8.0
N_HEADS = 4
DK = 128
DV = 128
RET_CHUNK = 128
ROPE_BASE = 10000.0
D_IN_PROJ = 3072
N_GROUPS = 4
PER_GROUP = 4
N_EXPERTS = 16
D_EXPERT = 256
EXPM1_DIRECT_BELOW = -0.5
NORM_EPS = 1e-6
GN_EPS = 1e-5
PAST_LEN = 16384

SUBLANES = 8
LANES = 128
GATE_HALF = 256
VMEM_LIMIT = 56 * 1024 * 1024

PROMPT_TILE = 1024
SAMPLE_SEQS = 16
MOE_TILE = 256
FINAL_TILE = 512
DISPATCH_TILE = 512
RING = 3
DMA_GROUP = 8

N_PAIRS = 6
N_CLASSES = N_GROUPS * N_PAIRS
ROW_GROUPS = D_MODEL // LANES


def _silu(x):
    return x * jax.nn.sigmoid(x)


def _rms_scale(x):
    return lax.rsqrt(jnp.mean(x * x, axis=-1, keepdims=True) + NORM_EPS)


def _masked_softmax(logits, mask):
    top = jnp.max(jnp.where(mask, logits, -jnp.inf), axis=-1, keepdims=True)
    e = jnp.where(mask, jnp.exp(logits - top), 0.0)
    return e / jnp.sum(e, axis=-1, keepdims=True)


def _mod_kernel(c_ref, w_ref, b_ref, o_ref):
    s = _silu(c_ref[...]).astype(BF16)
    o_ref[...] = jnp.dot(s, w_ref[...].astype(BF16), preferred_element_type=F32) + b_ref[...]


def _mod_call(c_all, w_mod, b_mod):
    rows = c_all.shape[0]
    ncol = w_mod.shape[1]
    blk = 1024
    return pl.pallas_call(
        _mod_kernel,
        grid=(ncol // blk,),
        in_specs=[
            pl.BlockSpec((rows, D_MODEL), lambda j: (0, 0)),
            pl.BlockSpec((D_MODEL, blk), lambda j: (0, j)),
            pl.BlockSpec((1, blk), lambda j: (0, j)),
        ],
        out_specs=pl.BlockSpec((rows, blk), lambda j: (0, j)),
        out_shape=jax.ShapeDtypeStruct((rows, ncol), F32),
        compiler_params=pltpu.CompilerParams(
            dimension_semantics=("arbitrary",), vmem_limit_bytes=VMEM_LIMIT),
        name="mod",
    )(c_all, w_mod, b_mod)


def _in_proj(x3, mod3, n_pre1_ref, w_in_ref):
    bb, tt, _ = x3.shape
    sh1 = mod3[:, :, 0:D_MODEL]
    sc1 = mod3[:, :, D_MODEL:2 * D_MODEL]
    coef = n_pre1_ref[...].reshape(1, 1, D_MODEL) * (1.0 + sc1)
    u = (x3 * _rms_scale(x3)) * coef + sh1
    u2d = u.reshape(bb * tt, D_MODEL).astype(BF16)
    return jnp.dot(u2d, w_in_ref[...], preferred_element_type=F32)


def _lru_coeffs(xc, wg_ref, b_r_ref, b_i_ref, lam_ref):
    xcb = xc.astype(BF16)
    g0 = jnp.dot(xcb[:, :GATE_HALF], wg_ref[0], preferred_element_type=F32)
    g1 = jnp.dot(xcb[:, GATE_HALF:], wg_ref[1], preferred_element_type=F32)
    r = jax.nn.sigmoid(jnp.concatenate([g0[:, :GATE_HALF], g1[:, :GATE_HALF]], axis=1) + b_r_ref[...])
    i = jax.nn.sigmoid(jnp.concatenate([g0[:, GATE_HALF:], g1[:, GATE_HALF:]], axis=1) + b_i_ref[...])
    lam = lam_ref[...]
    sp = jnp.maximum(-lam, 0.0) + jnp.log1p(jnp.exp(-jnp.abs(lam)))
    log_a = -LRU_C * r * sp
    a = jnp.exp(log_a)
    y = 2.0 * log_a
    a2 = a * a
    d = a2 - 1.0
    small = d * y / jnp.log(a2)
    em1 = jnp.where(y < EXPM1_DIRECT_BELOW, d, jnp.where(d == 0.0, y, small))
    gain = jnp.sqrt(-em1)
    return a, gain * (i * xc)


def _rope(xh, cos2, sin2, lane_axis):
    return xh * cos2 + pltpu.roll(xh, DK // 2, axis=lane_axis) * sin2


def _group_norm(o):
    mu = jnp.mean(o, axis=-1, keepdims=True)
    d = o - mu
    var = jnp.mean(d * d, axis=-1, keepdims=True)
    return d * lax.rsqrt(var + GN_EPS)


def _post_mixer(x3, mod3, out_a, out_b, w_out_ref, n_post1_ref, n_pre2_ref, w_router_ref, b_router_ref,
                x1_ref, pk_ref, route_ref, cnt_scr):
    bb, tt, _ = x3.shape
    m = bb * tt
    y = (jnp.dot(out_a.astype(BF16), w_out_ref[0:D_LRU, :], preferred_element_type=F32)
         + jnp.dot(out_b.astype(BF16), w_out_ref[D_LRU:, :], preferred_element_type=F32))
    g1 = mod3[:, :, 2 * D_MODEL:3 * D_MODEL]
    sh2 = mod3[:, :, 3 * D_MODEL:4 * D_MODEL]
    sc2 = mod3[:, :, 4 * D_MODEL:5 * D_MODEL]
    y3 = y.reshape(bb, tt, D_MODEL)
    x1 = x3 + (y3 * _rms_scale(y3)) * (g1 * n_post1_ref[...].reshape(1, 1, D_MODEL))
    u2 = (x1 * _rms_scale(x1)) * (n_pre2_ref[...].reshape(1, 1, D_MODEL) * (1.0 + sc2)) + sh2
    x1_ref[...] = x1.reshape(m, D_MODEL)
    u2f = u2.reshape(m, D_MODEL)
    for j in range(ROW_GROUPS):
        pk_ref[pl.ds(j, m, stride=SUBLANES), :] = u2f[:, j * LANES:(j + 1) * LANES]
    u2b = u2f.astype(BF16)

    logits = jnp.dot(u2b, w_router_ref[...], preferred_element_type=F32) + b_router_ref[...]
    lane = lax.broadcasted_iota(jnp.int32, (m, LANES), 1)
    is_g = (lane >= N_EXPERTS) & (lane < N_EXPERTS + N_GROUPS)
    p_group = _masked_softmax(logits, is_g)
    p_g = jnp.max(p_group, axis=-1, keepdims=True)
    g_lane = jnp.min(jnp.where(is_g & (p_group == p_g), lane, LANES), axis=-1, keepdims=True)
    e_lo = (g_lane - N_EXPERTS) * PER_GROUP
    in_g = (lane >= e_lo) & (lane < e_lo + PER_GROUP)
    p_e = _masked_softmax(logits, in_g)
    pm = jnp.where(in_g, p_e, -1.0)
    w1 = jnp.max(pm, axis=-1, keepdims=True)
    i1 = jnp.min(jnp.where(pm == w1, lane, LANES), axis=-1, keepdims=True)
    pm2 = jnp.where(lane == i1, -1.0, pm)
    w2 = jnp.max(pm2, axis=-1, keepdims=True)
    i2 = jnp.min(jnp.where(pm2 == w2, lane, LANES), axis=-1, keepdims=True)
    a = jnp.minimum(i1, i2) - e_lo
    b = jnp.maximum(i1, i2) - e_lo
    pair = jnp.where(a == 0, b - 1, jnp.where(a == 1, b + 1, 5))
    cls = (g_lane - N_EXPERTS) * N_PAIRS + pair
    onehot = lane == cls
    r_i = lax.broadcasted_iota(jnp.int32, (m, m), 0)
    c_i = lax.broadcasted_iota(jnp.int32, (m, m), 1)
    earlier = jnp.where(r_i > c_i, 1.0, 0.0).astype(BF16)
    prefix = jnp.dot(earlier, jnp.where(onehot, 1.0, 0.0).astype(BF16), preferred_element_type=F32)
    run = cnt_scr[0:1, :]
    rank = jnp.sum(jnp.where(onehot, prefix + run, 0.0), axis=-1, keepdims=True)
    cnt_scr[...] = jnp.broadcast_to(
        run + jnp.sum(jnp.where(onehot, 1.0, 0.0), axis=0, keepdims=True), cnt_scr.shape)
    route = jnp.where(lane == 0, cls.astype(F32), jnp.where(lane == 1, rank, 0.0))
    route_ref[...] = jnp.transpose(route)[0:SUBLANES, :]


def _group_scan(a3, b3):
    tpos = lax.broadcasted_iota(jnp.int32, a3.shape, 1)
    s = 1
    while s < a3.shape[1]:
        keep = tpos >= s
        a_sh = jnp.where(keep, pltpu.roll(a3, s, axis=1), 1.0)
        b_sh = jnp.where(keep, pltpu.roll(b3, s, axis=1), 0.0)
        b3 = a3 * b_sh + b3
        a3 = a3 * a_sh
        s *= 2
    return a3, b3


def _scan_rows(a, b, h0):
    n, c = a.shape
    groups = n // SUBLANES
    a3, b3 = _group_scan(a.reshape(groups, SUBLANES, c), b.reshape(groups, SUBLANES, c))
    carry = h0
    out = []
    for g in range(groups):
        hg = b3[g] + a3[g] * carry
        out.append(hg)
        carry = hg[SUBLANES - 1:SUBLANES, :]
    return jnp.concatenate(out, axis=0)


def _prompt_mixer_kernel(x_ref, mod_ref, cos_ref, sin_ref,
                         n_pre1_ref, n_post1_ref, n_pre2_ref,
                         w_in_ref, conv_w_ref, conv_b_ref, wg_ref, b_r_ref, b_i_ref, lam_ref,
                         gn_w_ref, w_out_ref, w_router_ref, b_router_ref,
                         mask_ref, wstate_ref, cross_ref, cdecay_ref,
                         x1_ref, pk_ref, route_ref, conv_out_ref, lru_out_ref, ret_out_ref, cnt_out_ref,
                         conv_scr, h_scr, s_scr, cnt_scr):
    t = pl.program_id(1)
    tt = x_ref.shape[1]

    @pl.when((pl.program_id(0) == 0) & (t == 0))
    def _():
        cnt_scr[...] = jnp.zeros_like(cnt_scr)

    @pl.when(t == 0)
    def _():
        conv_scr[...] = jnp.zeros_like(conv_scr)
        h_scr[...] = jnp.zeros_like(h_scr)
        s_scr[...] = jnp.zeros_like(s_scr)

    x3 = x_ref[...]
    mod3 = mod_ref[...]
    z = _in_proj(x3, mod3, n_pre1_ref, w_in_ref)
    xr = z[:, 0:D_LRU]
    yg = z[:, D_LRU:2 * D_LRU]
    q = z[:, 1024:1536]
    k = z[:, 1536:2048]
    v = z[:, 2048:2560]
    g = z[:, 2560:3072]

    groups = tt // SUBLANES
    xr3 = xr.reshape(groups, SUBLANES, D_LRU)
    tpos = lax.broadcasted_iota(jnp.int32, xr3.shape, 1)
    tail = conv_scr[...]
    xc3 = jnp.broadcast_to(conv_b_ref[...].reshape(1, 1, D_LRU), xr3.shape)
    for j in range(CONV_W):
        back = CONV_W - 1 - j
        w_j = conv_w_ref[j:j + 1, :].reshape(1, 1, D_LRU)
        if back == 0:
            term = xr3
        else:
            cur = pltpu.roll(xr3, back, axis=1)
            first = pltpu.roll(tail, back, axis=0).reshape(1, SUBLANES, D_LRU)
            prev = jnp.concatenate([first, cur[:groups - 1]], axis=0)
            term = jnp.where(tpos >= back, cur, prev)
        xc3 = xc3 + term * w_j
    xc = xc3.reshape(tt, D_LRU)
    conv_scr[...] = xr[tt - SUBLANES:, :]

    a, b = _lru_coeffs(xc, wg_ref, b_r_ref, b_i_ref, lam_ref)
    hseq = _scan_rows(a, b, h_scr[0:1, :])
    h_scr[...] = jnp.broadcast_to(hseq[tt - 1:tt, :], h_scr.shape)
    out_a = hseq * jax.nn.gelu(yg, approximate=True)

    cos2 = cos_ref[...]
    sin2 = sin_ref[...]
    scale = DK ** -0.5
    o_heads = []
    for h in range(N_HEADS):
        hs = slice(h * DK, (h + 1) * DK)
        qh = (_rope(q[:, hs], cos2, sin2, 1) * scale).astype(BF16)
        kh = _rope(k[:, hs], cos2, sin2, 1)
        vh = v[:, hs].astype(BF16)
        o_chunks = []
        for c in range(tt // RET_CHUNK):
            cs = slice(c * RET_CHUNK, (c + 1) * RET_CHUNK)
            qc = qh[cs]
            kc = kh[cs]
            vc = vh[cs]
            s_prev = s_scr[h]
            scores = lax.dot_general(qc, kc.astype(BF16), (((1,), (1,)), ((), ())),
                                     preferred_element_type=F32) * mask_ref[h]
            inner = jnp.dot(scores.astype(BF16), vc, preferred_element_type=F32)
            cross = jnp.dot(qc, s_prev.astype(BF16), preferred_element_type=F32) * cross_ref[:, hs]
            kw = (kc * wstate_ref[:, hs]).astype(BF16)
            kv = lax.dot_general(kw, vc, (((0,), (0,)), ((), ())), preferred_element_type=F32)
            s_scr[h] = cdecay_ref[:, hs] * s_prev + kv
            o_chunks.append(inner + cross)
        o_heads.append(_group_norm(jnp.concatenate(o_chunks, axis=0)))
    o = jnp.concatenate(o_heads, axis=1)
    out_b = o * gn_w_ref[...] * _silu(g)

    _post_mixer(x3, mod3, out_a, out_b, w_out_ref, n_post1_ref, n_pre2_ref, w_router_ref, b_router_ref,
                x1_ref, pk_ref, route_ref, cnt_scr)
    cnt_out_ref[...] = cnt_scr[...]

    @pl.when(t == pl.num_programs(1) - 1)
    def _():
        conv_out_ref[0] = xr[tt - SUBLANES:, :]
        lru_out_ref[0] = hseq[tt - SUBLANES:, :]
        ret_out_ref[0] = s_scr[...]


def _const_spec(shape):
    nd = len(shape)
    return pl.BlockSpec(shape, lambda *_: (0,) * nd)


def _prompt_mixer_call(x, mod3, cos2, sin2, wts, tables):
    bsz, seq, _ = x.shape
    tt = PROMPT_TILE
    nt = seq // tt
    n_tok = bsz * seq
    (n_pre1, n_post1, n_pre2, w_in, conv_w, conv_b, wg, b_r, b_i, lam, gn_w, w_out, w_router, b_router) = wts
    mask, wstate, cross, cdecay = tables
    tok_spec = pl.BlockSpec((tt, D_MODEL), lambda b, t: (b * nt + t, 0))
    in_specs = [
        pl.BlockSpec((1, tt, D_MODEL), lambda b, t: (b, t, 0)),
        pl.BlockSpec((1, 1, 6 * D_MODEL), lambda b, t: (b, 0, 0)),
        pl.BlockSpec((tt, LANES), lambda b, t: (t, 0)),
        pl.BlockSpec((tt, LANES), lambda b, t: (t, 0)),
    ] + [_const_spec(w.shape) for w in wts] + [_const_spec(tb.shape) for tb in tables]
    out_specs = [
        tok_spec,
        pl.BlockSpec((tt * SUBLANES, LANES), lambda b, t: (b * nt + t, 0)),
        pl.BlockSpec((SUBLANES, tt), lambda b, t: (0, b * nt + t)),
        pl.BlockSpec((1, SUBLANES, D_LRU), lambda b, t: (b, 0, 0)),
        pl.BlockSpec((1, SUBLANES, D_LRU), lambda b, t: (b, 0, 0)),
        pl.BlockSpec((1, N_HEADS, DK, DV), lambda b, t: (b, 0, 0, 0)),
        pl.BlockSpec((SUBLANES, LANES), lambda b, t: (0, 0)),
    ]
    out_shape = [
        jax.ShapeDtypeStruct((n_tok, D_MODEL), F32),
        jax.ShapeDtypeStruct((n_tok * SUBLANES, LANES), F32),
        jax.ShapeDtypeStruct((SUBLANES, n_tok), F32),
        jax.ShapeDtypeStruct((bsz, SUBLANES, D_LRU), F32),
        jax.ShapeDtypeStruct((bsz, SUBLANES, D_LRU), F32),
        jax.ShapeDtypeStruct((bsz, N_HEADS, DK, DV), F32),
        jax.ShapeDtypeStruct((SUBLANES, LANES), F32),
    ]
    return pl.pallas_call(
        _prompt_mixer_kernel,
        grid=(bsz, nt),
        in_specs=in_specs,
        out_specs=out_specs,
        out_shape=out_shape,
        scratch_shapes=[
            pltpu.VMEM((SUBLANES, D_LRU), F32),
            pltpu.VMEM((SUBLANES, D_LRU), F32),
            pltpu.VMEM((N_HEADS, DK, DV), F32),
            pltpu.VMEM((SUBLANES, LANES), F32),
        ],
        compiler_params=pltpu.CompilerParams(
            dimension_semantics=("arbitrary", "arbitrary"), vmem_limit_bytes=VMEM_LIMIT),
        name="prompt_mixer",
    )(x, mod3, cos2, sin2, *wts, *tables)


def _sample_mixer_kernel(x_ref, mod_ref, cos_ref, sin_ref, buf_ref, h0_ref, s0_ref, cnt_in_ref,
                         n_pre1_ref, n_post1_ref, n_pre2_ref,
                         w_in_ref, conv_w_ref, conv_b_ref, wg_ref, b_r_ref, b_i_ref, lam_ref,
                         gn_w_ref, w_out_ref, w_router_ref, b_router_ref,
                         smask_ref, wstate_ref, cross_ref, cdecay_ref,
                         x1_ref, pk_ref, route_ref, xr_out_ref, h_out_ref, ret_out_ref, cnt_out_ref,
                         cnt_scr):
    bb, ts, _ = x_ref.shape
    m = bb * ts

    @pl.when(pl.program_id(0) == 0)
    def _():
        cnt_scr[...] = cnt_in_ref[...]

    x3 = x_ref[...]
    mod3 = mod_ref[...]
    z = _in_proj(x3, mod3, n_pre1_ref, w_in_ref)
    xr = z[:, 0:D_LRU]
    yg = z[:, D_LRU:2 * D_LRU]
    q = z[:, 1024:1536]
    k = z[:, 1536:2048]
    v = z[:, 2048:2560]
    g = z[:, 2560:3072]
    xr_out_ref[...] = xr

    xr3 = xr.reshape(bb, ts, D_LRU)
    buf3 = buf_ref[...]
    tpos = lax.broadcasted_iota(jnp.int32, (bb, ts, D_LRU), 1)
    xc3 = jnp.broadcast_to(conv_b_ref[...].reshape(1, 1, D_LRU), (bb, ts, D_LRU))
    for j in range(CONV_W):
        back = CONV_W - 1 - j
        w_j = conv_w_ref[j:j + 1, :].reshape(1, 1, D_LRU)
        if back == 0:
            term = xr3
        else:
            cur = pltpu.roll(xr3, back, axis=1)
            up = CONV_W - 1 - back
            old = buf3 if up == 0 else pltpu.roll(buf3, ts - up, axis=1)
            term = jnp.where(tpos >= back, cur, old)
        xc3 = xc3 + term * w_j
    xc = xc3.reshape(m, D_LRU)

    a, b = _lru_coeffs(xc, wg_ref, b_r_ref, b_i_ref, lam_ref)
    a3 = a.reshape(bb, ts, D_LRU)
    b3 = b.reshape(bb, ts, D_LRU) + a3 * h0_ref[...]
    _, h3 = _group_scan(a3, b3)
    hseq = h3.reshape(m, D_LRU)
    h_out_ref[...] = hseq
    out_a = hseq * jax.nn.gelu(yg, approximate=True)

    cos2 = cos_ref[...].reshape(1, ts, LANES)
    sin2 = sin_ref[...].reshape(1, ts, LANES)
    scale = DK ** -0.5
    o_heads = []
    for h in range(N_HEADS):
        hs = slice(h * DK, (h + 1) * DK)
        q3 = (_rope(q[:, hs].reshape(bb, ts, DK), cos2, sin2, 2) * scale).astype(BF16)
        k3 = _rope(k[:, hs].reshape(bb, ts, DK), cos2, sin2, 2)
        v3 = v[:, hs].reshape(bb, ts, DV).astype(BF16)
        q2 = q3.reshape(m, DK)
        k2 = k3.reshape(m, DK).astype(BF16)
        v2 = v3.reshape(m, DV)
        scores = lax.dot_general(q2, k2, (((1,), (1,)), ((), ())),
                                 preferred_element_type=F32) * smask_ref[h]
        inner = jnp.dot(scores.astype(BF16), v2, preferred_element_type=F32)
        s0h = s0_ref[:, h]
        cross = jnp.einsum('bid,bde->bie', q3, s0h.astype(BF16), preferred_element_type=F32)
        cross = cross * cross_ref[:, hs].reshape(1, ts, DV)
        kw3 = (k3 * wstate_ref[:, hs].reshape(1, ts, DK)).astype(BF16)
        kv = jnp.einsum('bjd,bje->bde', kw3, v3, preferred_element_type=F32)
        ret_out_ref[:, h] = cdecay_ref[:, hs].reshape(1, 1, DV) * s0h + kv
        o_heads.append(_group_norm(inner + cross.reshape(m, DV)))
    o = jnp.concatenate(o_heads, axis=1)
    out_b = o * gn_w_ref[...] * _silu(g)

    _post_mixer(x3, mod3, out_a, out_b, w_out_ref, n_post1_ref, n_pre2_ref, w_router_ref, b_router_ref,
                x1_ref, pk_ref, route_ref, cnt_scr)
    cnt_out_ref[...] = cnt_scr[...]


def _sample_mixer_call(x, mod3, cos2, sin2, buf8, h0p, s0, cnt_in, wts, tables):
    bsz, ts, _ = x.shape
    bb = SAMPLE_SEQS
    m = bb * ts
    n_tok = bsz * ts
    seq_spec = lambda w: pl.BlockSpec((bb, ts, w), lambda i: (i, 0, 0))
    tok_spec = lambda w: pl.BlockSpec((m, w), lambda i: (i, 0))
    in_specs = [
        seq_spec(D_MODEL),
        pl.BlockSpec((bb, 1, 6 * D_MODEL), lambda i: (i, 0, 0)),
        _const_spec(cos2.shape),
        _const_spec(sin2.shape),
        seq_spec(D_LRU),
        seq_spec(D_LRU),
        pl.BlockSpec((bb, N_HEADS, DK, DV), lambda i: (i, 0, 0, 0)),
        _const_spec(cnt_in.shape),
    ] + [_const_spec(w.shape) for w in wts] + [_const_spec(tb.shape) for tb in tables]
    out_specs = [
        tok_spec(D_MODEL),
        pl.BlockSpec((m * SUBLANES, LANES), lambda i: (i, 0)),
        pl.BlockSpec((SUBLANES, m), lambda i: (0, i)),
        tok_spec(D_LRU),
        tok_spec(D_LRU),
        pl.BlockSpec((bb, N_HEADS, DK, DV), lambda i: (i, 0, 0, 0)),
        _const_spec(cnt_in.shape),
    ]
    out_shape = [
        jax.ShapeDtypeStruct((n_tok, D_MODEL), F32),
        jax.ShapeDtypeStruct((n_tok * SUBLANES, LANES), F32),
        jax.ShapeDtypeStruct((SUBLANES, n_tok), F32),
        jax.ShapeDtypeStruct((n_tok, D_LRU), F32),
        jax.ShapeDtypeStruct((n_tok, D_LRU), F32),
        jax.ShapeDtypeStruct((bsz, N_HEADS, DK, DV), F32),
        jax.ShapeDtypeStruct(cnt_in.shape, F32),
    ]
    return pl.pallas_call(
        _sample_mixer_kernel,
        grid=(bsz // bb,),
        in_specs=in_specs,
        out_specs=out_specs,
        out_shape=out_shape,
        scratch_shapes=[pltpu.VMEM((SUBLANES, LANES), F32)],
        compiler_params=pltpu.CompilerParams(
            dimension_semantics=("arbitrary",), vmem_limit_bytes=VMEM_LIMIT),
        name="sample_mixer",
    )(x, mod3, cos2, sin2, buf8, h0p, s0, cnt_in, *wts, *tables)


def _slab(ref, r):
    return ref.at[pl.ds(pl.multiple_of(r * SUBLANES, SUBLANES), SUBLANES)]


def _dispatch_kernel(cls_ref, rank_ref, cnt_ref, padcnt_ref, start_ref, srcp_ref, srcs_ref, out_ref, ring, sem,
                     *, p_tiles):
    i = pl.program_id(0)
    n = pl.num_programs(0)
    tile_rows = ring.shape[1]
    td = tile_rows // SUBLANES
    moe_rows = MOE_TILE * SUBLANES

    def issue(slot):
        def body(g, carry):
            first = i * td + g * DMA_GROUP
            slots = [start_ref[cls_ref[first + j]] + rank_ref[first + j] for j in range(DMA_GROUP)]
            first = g * DMA_GROUP
            for j in range(DMA_GROUP):
                pltpu.make_async_copy(
                    _slab(ring.at[slot], first + j), _slab(out_ref, slots[j]), sem.at[slot]
                ).start(priority=j % 2)
            return carry
        lax.fori_loop(0, td // DMA_GROUP, body, 0)

    def wait_tile(slot):
        pltpu.make_async_copy(ring.at[slot], out_ref.at[pl.ds(0, tile_rows)], sem.at[slot]).wait()

    def zero_fill(slot):
        ring[slot] = jnp.zeros((tile_rows, LANES), F32)
        zero_src = ring.at[slot]

        def per_class(c, carry):
            lo = start_ref[c] + cnt_ref[c]
            hi = start_ref[c] + padcnt_ref[c]

            def fill(r, carry2):
                pltpu.make_async_copy(_slab(zero_src, 0), _slab(out_ref, r), sem.at[slot]).start()
                return carry2
            lax.fori_loop(lo, hi, fill, 0)

            def done(r, carry2):
                pltpu.make_async_copy(_slab(zero_src, 0), _slab(out_ref, 0), sem.at[slot]).wait()
                return carry2
            lax.fori_loop(lo, hi, done, 0)
            return carry
        lax.fori_loop(0, N_CLASSES, per_class, 0)

        used_tiles = (start_ref[N_CLASSES - 1] + padcnt_ref[N_CLASSES - 1]) // MOE_TILE
        all_tiles = out_ref.shape[0] // moe_rows

        def tile_copy(t):
            return pltpu.make_async_copy(
                zero_src.at[pl.ds(0, moe_rows)],
                out_ref.at[pl.ds(pl.multiple_of(t * moe_rows, moe_rows), moe_rows)], sem.at[slot])

        def fill_tile(t, carry):
            tile_copy(t).start()
            return carry
        lax.fori_loop(used_tiles, all_tiles, fill_tile, 0)

        def done_tile(t, carry):
            tile_copy(t).wait()
            return carry
        lax.fori_loop(used_tiles, all_tiles, done_tile, 0)

    for s in range(RING):
        @pl.when(i % RING == s)
        def _():
            @pl.when(i < p_tiles)
            def _():
                ring[s] = srcp_ref[...]

            @pl.when(i >= p_tiles)
            def _():
                ring[s] = srcs_ref[...]

            issue(s)

            @pl.when(i >= RING - 1)
            def _():
                wait_tile((s + 1) % RING)

            @pl.when(i == n - 1)
            def _():
                for back in range(RING - 2, -1, -1):
                    wait_tile((s - back) % RING)
                zero_fill(s)


def _dispatch_call(cls, rank, cnt, padcnt, start, pk_p, pk_s, n_rows):
    td = DISPATCH_TILE
    tile_rows = td * SUBLANES
    p_tiles = pk_p.shape[0] // tile_rows
    s_tiles = pk_s.shape[0] // tile_rows
    assert p_tiles + s_tiles >= RING and td >= MOE_TILE
    return pl.pallas_call(
        functools.partial(_dispatch_kernel, p_tiles=p_tiles),
        grid_spec=pltpu.PrefetchScalarGridSpec(
            num_scalar_prefetch=5,
            grid=(p_tiles + s_tiles,),
            in_specs=[
                pl.BlockSpec((tile_rows, LANES), lambda i, *_: (jnp.minimum(i, p_tiles - 1), 0)),
                pl.BlockSpec((tile_rows, LANES), lambda i, *_: (jnp.maximum(i - p_tiles, 0), 0)),
            ],
            out_specs=pl.BlockSpec(memory_space=pl.ANY),
            scratch_shapes=[pltpu.VMEM((RING, tile_rows, LANES), F32),
                            pltpu.SemaphoreType.DMA((RING,))],
        ),
        out_shape=jax.ShapeDtypeStruct((n_rows * SUBLANES, LANES), F32),
        compiler_params=pltpu.CompilerParams(
            dimension_semantics=("arbitrary",), has_side_effects=True, vmem_limit_bytes=VMEM_LIMIT),
        name="moe_dispatch",
    )(cls, rank, cnt, padcnt, start, pk_p, pk_s)


def _moe_kernel(ea_ref, eb_ref, valid_ref, xs_ref, w_router_ref, b_router_ref,
                wga_ref, wua_ref, wda_ref, wgb_ref, wub_ref, wdb_ref, f_ref):
    t = pl.program_id(0)

    @pl.when(valid_ref[t] == 1)
    def _():
        tm = xs_ref.shape[0] // SUBLANES
        x = jnp.concatenate([xs_ref[pl.ds(j, tm, stride=SUBLANES), :] for j in range(ROW_GROUPS)],
                            axis=1).astype(BF16)
        e_a = ea_ref[t]
        e_b = eb_ref[t]
        e_lo = (e_a // PER_GROUP) * PER_GROUP
        logits = jnp.dot(x, w_router_ref[...], preferred_element_type=F32) + b_router_ref[...]
        lane = lax.broadcasted_iota(jnp.int32, (tm, LANES), 1)
        pick = lambda p, idx: jnp.sum(jnp.where(lane == idx, p, 0.0), axis=-1, keepdims=True)
        p_group = _masked_softmax(logits, (lane >= N_EXPERTS) & (lane < N_EXPERTS + N_GROUPS))
        p_g = pick(p_group, N_EXPERTS + e_a // PER_GROUP)
        p_e = _masked_softmax(logits, (lane >= e_lo) & (lane < e_lo + PER_GROUP))
        w_a = pick(p_e, e_a)
        w_b = pick(p_e, e_b)
        wsum = w_a + w_b

        def expert(wg_ref, wu_ref, gate):
            hg = jnp.dot(x, wg_ref[0].astype(BF16), preferred_element_type=F32)
            hu = jnp.dot(x, wu_ref[0].astype(BF16), preferred_element_type=F32)
            return (_silu(hg) * hu * gate).astype(BF16)

        ha = expert(wga_ref, wua_ref, p_g * (w_a / wsum))
        hb = expert(wgb_ref, wub_ref, p_g * (w_b / wsum))
        for c in range(D_MODEL // GATE_HALF):
            cols = slice(c * GATE_HALF, (c + 1) * GATE_HALF)
            f = (jnp.dot(ha, wda_ref[0, :, cols].astype(BF16), preferred_element_type=F32)
                 + jnp.dot(hb, wdb_ref[0, :, cols].astype(BF16), preferred_element_type=F32))
            for jj in range(GATE_HALF // LANES):
                j = c * (GATE_HALF // LANES) + jj
                f_ref[pl.ds(j, tm, stride=SUBLANES), :] = f[:, jj * LANES:(jj + 1) * LANES]

    @pl.when(valid_ref[t] == 0)
    def _():
        f_ref[...] = jnp.zeros_like(f_ref)


def _moe_call(tile_ea, tile_eb, tile_valid, rows, w_router, b_router, wg, wu, wd):
    n_rows = rows.shape[0] // SUBLANES
    tm = MOE_TILE
    const = lambda a: pl.BlockSpec(a.shape, lambda t, ea, eb, v: (0,) * a.ndim)
    up = lambda sel: pl.BlockSpec((1, D_MODEL, D_EXPERT), lambda t, ea, eb, v: (sel(ea, eb)[t], 0, 0))
    down = lambda sel: pl.BlockSpec((1, D_EXPERT, D_MODEL), lambda t, ea, eb, v: (sel(ea, eb)[t], 0, 0))
    first = lambda ea, eb: ea
    second = lambda ea, eb: eb
    return pl.pallas_call(
        _moe_kernel,
        grid_spec=pltpu.PrefetchScalarGridSpec(
            num_scalar_prefetch=3,
            grid=(n_rows // tm,),
            in_specs=[
                pl.BlockSpec((tm * SUBLANES, LANES), lambda t, ea, eb, v: (t, 0)),
                const(w_router), const(b_router),
                up(first), up(first), down(first), up(second), up(second), down(second),
            ],
            out_specs=pl.BlockSpec((tm * SUBLANES, LANES), lambda t, ea, eb, v: (t, 0)),
        ),
        out_shape=jax.ShapeDtypeStruct((n_rows * SUBLANES, LANES), F32),
        compiler_params=pltpu.CompilerParams(
            dimension_semantics=("arbitrary",), vmem_limit_bytes=VMEM_LIMIT),
        name="moe_experts",
    )(tile_ea, tile_eb, tile_valid, rows, w_router, b_router, wg, wu, wd, wg, wu, wd)


def _combine_kernel(cls_ref, rank_ref, start_ref, x1_ref, mod_ref, n_post2_ref, f_hbm, o_ref, fbuf, sem):
    i = pl.program_id(0)
    n = pl.num_programs(0)
    tf = x1_ref.shape[0]

    def issue(tile, slot):
        def body(g, carry):
            first = tile * tf + g * DMA_GROUP
            slots = [start_ref[cls_ref[first + j]] + rank_ref[first + j] for j in range(DMA_GROUP)]
            first = g * DMA_GROUP
            for j in range(DMA_GROUP):
                pltpu.make_async_copy(
                    _slab(f_hbm, slots[j]), _slab(fbuf.at[slot], first + j), sem.at[slot]
                ).start(priority=j % 2)
            return carry
        lax.fori_loop(0, tf // DMA_GROUP, body, 0)

    def finish(slot):
        pltpu.make_async_copy(f_hbm.at[pl.ds(0, tf * SUBLANES)], fbuf.at[slot], sem.at[slot]).wait()
        f = jnp.concatenate([fbuf[slot, pl.ds(j, tf, stride=SUBLANES), :] for j in range(D_MODEL // LANES)],
                            axis=1)
        mod3 = mod_ref[...]
        bb = mod3.shape[0]
        f3 = f.reshape(bb, tf // bb, D_MODEL)
        g2 = mod3[:, :, 5 * D_MODEL:6 * D_MODEL]
        x13 = x1_ref[...].reshape(f3.shape)
        out = x13 + (f3 * _rms_scale(f3)) * (g2 * n_post2_ref[...].reshape(1, 1, D_MODEL))
        o_ref[...] = out.reshape(o_ref.shape)

    @pl.when(i == 0)
    def _():
        issue(0, 0)

    for parity in range(2):
        @pl.when(i % 2 == parity)
        def _():
            @pl.when(i + 1 < n)
            def _():
                issue(i + 1, 1 - parity)
            finish(parity)


def _combine_call(cls, rank, start, x1, mod3, n_post2, f_sorted, seqs_per_tile):
    n_tok = x1.shape[0]
    tf = FINAL_TILE
    tok_per_seq = n_tok // mod3.shape[0]
    if seqs_per_tile > 1:
        mod_map = lambda i, *_: (i, 0, 0)
    else:
        mod_map = lambda i, *_: ((i * tf) // tok_per_seq, 0, 0)
    return pl.pallas_call(
        _combine_kernel,
        grid_spec=pltpu.PrefetchScalarGridSpec(
            num_scalar_prefetch=3,
            grid=(n_tok // tf,),
            in_specs=[
                pl.BlockSpec((tf, D_MODEL), lambda i, *_: (i, 0)),
                pl.BlockSpec((seqs_per_tile, 1, 6 * D_MODEL), mod_map),
                pl.BlockSpec((1, D_MODEL), lambda i, *_: (0, 0)),
                pl.BlockSpec(memory_space=pl.ANY),
            ],
            out_specs=pl.BlockSpec((tf, D_MODEL), lambda i, *_: (i, 0)),
            scratch_shapes=[pltpu.VMEM((2, tf * SUBLANES, LANES), F32), pltpu.SemaphoreType.DMA((2,))],
        ),
        out_shape=jax.ShapeDtypeStruct((n_tok, D_MODEL), F32),
        compiler_params=pltpu.CompilerParams(
            dimension_semantics=("arbitrary",), vmem_limit_bytes=VMEM_LIMIT),
        name="moe_combine",
    )(cls, rank, start, x1, mod3, n_post2, f_sorted)


def _block_diag_gate(w_r, w_i):
    per_half = GATE_HALF // LRU_BLOCK
    halves = []
    for hb in range(D_LRU // GATE_HALF):
        blocks = []
        for w in (w_r, w_i):
            mat = jnp.zeros((GATE_HALF, GATE_HALF), F32)
            for n in range(per_half):
                lo = n * LRU_BLOCK
                mat = mat.at[lo:lo + LRU_BLOCK, lo:lo + LRU_BLOCK].set(w[hb * per_half + n])
            blocks.append(mat)
        halves.append(jnp.concatenate(blocks, axis=1))
    return jnp.stack(halves).astype(BF16)


def _rope_tables(pos):
    half = DK // 2
    inv = ROPE_BASE ** (-jnp.arange(half, dtype=F32) / half)
    ang = pos[:, None] * inv[None, :]
    cos = jnp.cos(ang)
    sin = jnp.sin(ang)
    return jnp.concatenate([cos, cos], axis=-1), jnp.concatenate([-sin, sin], axis=-1)


def _decay_tables(c):
    log_g = jnp.log1p(-jnp.exp2(-5.0 - jnp.arange(N_HEADS, dtype=F32)))
    idx = jnp.arange(c, dtype=F32)
    diff = idx[:, None] - idx[None, :]
    mask = jnp.where(diff[None] >= 0, jnp.exp(jnp.maximum(diff, 0.0)[None] * log_g[:, None, None]), 0.0)
    w_state = jnp.exp((c - 1.0 - idx)[None, :] * log_g[:, None])
    cross_decay = jnp.exp((idx + 1.0)[:, None] * log_g[None, :])
    chunk_decay = jnp.exp(c * log_g)
    wstate_full = jnp.repeat(w_state.T, DK, axis=1)
    cross_full = jnp.repeat(cross_decay, DV, axis=1)
    cdecay_full = jnp.repeat(chunk_decay, DV)[None, :]
    return mask, wstate_full, cross_full, cdecay_full


def kernel(x_prompt, x_sample, state_conv, state_lru, state_ret, c_prompt, c_sample, w_mod, b_mod, norm_pre_mix, norm_post_mix, norm_pre_ffn, norm_post_ffn, w_in, conv_w, conv_b, w_rgate, b_rgate, w_igate, b_igate, lru_lambda, ret_gn_w, w_out, w_router_group, b_router_group, w_router_expert, b_router_expert, w_exp_gate, w_exp_up, w_exp_down):
    bp, tp, _ = x_prompt.shape
    bs, ts, _ = x_sample.shape
    l = 0

    mod = _mod_call(jnp.concatenate([c_prompt, c_sample], axis=0), w_mod[l], b_mod[l][None, :])
    mod_p = mod[:bp][:, None, :]
    mod_s = mod[bp:][:, None, :]

    w_router = jnp.zeros((D_MODEL, LANES), F32)
    w_router = w_router.at[:, :N_EXPERTS].set(w_router_expert[l])
    w_router = w_router.at[:, N_EXPERTS:N_EXPERTS + N_GROUPS].set(w_router_group[l]).astype(BF16)
    b_router = jnp.zeros((1, LANES), F32)
    b_router = b_router.at[0, :N_EXPERTS].set(b_router_expert[l])
    b_router = b_router.at[0, N_EXPERTS:N_EXPERTS + N_GROUPS].set(b_router_group[l])

    row = lambda vec: vec.reshape(1, -1)
    wts = (row(norm_pre_mix[l]), row(norm_post_mix[l]), row(norm_pre_ffn[l]),
           w_in[l].astype(BF16), conv_w[l], row(conv_b[l]),
           _block_diag_gate(w_rgate[l], w_igate[l]),
           row(b_rgate[l]), row(b_igate[l]), row(lru_lambda[l]), row(ret_gn_w[l]),
           w_out[l].astype(BF16), w_router, b_router)

    cos_p, sin_p = _rope_tables(jnp.arange(tp, dtype=F32))
    x1_p, pk_p, route_p, conv_p8, lru_p8, ret_p, cnt_p = _prompt_mixer_call(
        x_prompt, mod_p, cos_p, sin_p, wts, _decay_tables(math.gcd(tp, RET_CHUNK)))

    cos_s, sin_s = _rope_tables(jnp.float32(PAST_LEN) + jnp.arange(ts, dtype=F32))
    mask8, wstate_s, cross_s, cdecay_s = _decay_tables(math.gcd(ts, RET_CHUNK))
    eye = jnp.eye(SAMPLE_SEQS, dtype=F32)
    smask = jnp.stack([jnp.kron(eye, mask8[h]) for h in range(N_HEADS)])
    buf8 = jnp.pad(state_conv[l], ((0, 0), (0, ts - (CONV_W - 1)), (0, 0)))
    h0p = jnp.pad(state_lru[l][:, None, :], ((0, 0), (0, ts - 1), (0, 0)))
    x1_s, pk_s, route_s, xr_s, h_s, ret_s, cnt_all = _sample_mixer_call(
        x_sample, mod_s, cos_s, sin_s, buf8, h0p, state_ret[l], cnt_p, wts,
        (smask, wstate_s, cross_s, cdecay_s))

    n_p = bp * tp
    n_tok = n_p + bs * ts
    tm = MOE_TILE
    max_tiles = n_tok // tm + N_CLASSES
    route = jnp.concatenate([route_p[0:2], route_s[0:2]], axis=1).astype(jnp.int32)
    cls = route[0]
    rank = route[1]
    cnt = cnt_all[0, :N_CLASSES].astype(jnp.int32)
    ntile = (cnt + (tm - 1)) // tm
    padcnt = ntile * tm
    start = jnp.cumsum(padcnt) - padcnt
    tile_end = jnp.cumsum(ntile)
    tile_ids = jnp.arange(max_tiles, dtype=jnp.int32)
    tile_valid = (tile_ids < tile_end[-1]).astype(jnp.int32)
    last_used = jnp.minimum(tile_ids, tile_end[-1] - 1)
    tile_cls = jnp.sum((last_used[:, None] >= tile_end[None, :]).astype(jnp.int32), axis=1)
    pair_a = jnp.array([0, 0, 0, 1, 1, 2], jnp.int32)
    pair_b = jnp.array([1, 2, 3, 2, 3, 3], jnp.int32)
    tile_ea = (tile_cls // N_PAIRS) * PER_GROUP + pair_a[tile_cls % N_PAIRS]
    tile_eb = (tile_cls // N_PAIRS) * PER_GROUP + pair_b[tile_cls % N_PAIRS]

    n_post2 = row(norm_post_ffn[l])
    wg = w_exp_gate[l]
    wu = w_exp_up[l]
    wd = w_exp_down[l]
    rows = _dispatch_call(cls, rank, cnt, padcnt, start, pk_p, pk_s, max_tiles * tm)
    f_sorted = _moe_call(tile_ea, tile_eb, tile_valid, rows, w_router, b_router, wg, wu, wd)
    y_p = _combine_call(cls[:n_p], rank[:n_p], start, x1_p, mod_p, n_post2, f_sorted, 1)
    y_s = _combine_call(cls[n_p:], rank[n_p:], start, x1_s, mod_s, n_post2, f_sorted, FINAL_TILE // ts)

    conv_p = conv_p8[:, SUBLANES - (CONV_W - 1):, :]
    lru_p = lru_p8[:, SUBLANES - 1, :]
    xr_s3 = xr_s.reshape(bs, ts, D_LRU)
    conv_s = xr_s3[:, ts - (CONV_W - 1):, :]
    lru_s = h_s.reshape(bs, ts, D_LRU)[:, ts - 1, :]
    return (y_p.reshape(bp, tp, D_MODEL), y_s.reshape(bs, ts, D_MODEL),
            conv_p[None], lru_p[None], ret_p[None],
            conv_s[None], lru_s[None], ret_s[None])
```

```python
import functools
import math

import jax
import jax.numpy as jnp
import numpy as np
from jax import lax
from jax.experimental import pallas as pl
from jax.experimental.pallas import tpu as pltpu

F32 = jnp.float32
BF16 = jnp.bfloat16

D_MODEL = 1024
D_LRU = 512
D_RET = 512
N_LRU_BLOCKS = 8
LRU_BLOCK = D_LRU // N_LRU_BLOCKS
CONV_W = 4
LRU_C = 8.0
N_HEADS = 4
DK = 128
DV = 128
RET_CHUNK = 128
ROPE_BASE = 10000.0
D_IN_PROJ = 3072
N_GROUPS = 4
PER_GROUP = 4
N_EXPERTS = 16
D_EXPERT = 256
EXPM1_DIRECT_BELOW = -0.5
NORM_EPS = 1e-6
GN_EPS = 1e-5
PAST_LEN = 16384

SUBLANES = 8
LANES = 128
GATE_HALF = 256
VMEM_LIMIT = 56 * 1024 * 1024

PROMPT_TILE = 512
SAMPLE_SEQS = 16
MOE_TILE = 256
FINAL_TILE = 512
DISPATCH_TILE = 512
RING = 3
DMA_GROUP = 8

N_PAIRS = 6
N_CLASSES = N_GROUPS * N_PAIRS
ROW_GROUPS = D_MODEL // LANES


def _silu(x):
    return x * jax.nn.sigmoid(x)


def _rms_scale(x):
    return lax.rsqrt(jnp.mean(x * x, axis=-1, keepdims=True) + NORM_EPS)


def _masked_softmax(logits, mask):
    top = jnp.max(jnp.where(mask, logits, -jnp.inf), axis=-1, keepdims=True)
    e = jnp.where(mask, jnp.exp(logits - top), 0.0)
    return e / jnp.sum(e, axis=-1, keepdims=True)


def _mod_kernel(c_ref, w_ref, b_ref, o_ref):
    s = _silu(c_ref[...]).astype(BF16)
    o_ref[...] = jnp.dot(s, w_ref[...].astype(BF16), preferred_element_type=F32) + b_ref[...]


def _mod_call(c_all, w_mod, b_mod):
    rows = c_all.shape[0]
    ncol = w_mod.shape[1]
    blk = 1024
    return pl.pallas_call(
        _mod_kernel,
        grid=(ncol // blk,),
        in_specs=[
            pl.BlockSpec((rows, D_MODEL), lambda j: (0, 0)),
            pl.BlockSpec((D_MODEL, blk), lambda j: (0, j)),
            pl.BlockSpec((1, blk), lambda j: (0, j)),
        ],
        out_specs=pl.BlockSpec((rows, blk), lambda j: (0, j)),
        out_shape=jax.ShapeDtypeStruct((rows, ncol), F32),
        compiler_params=pltpu.CompilerParams(
            dimension_semantics=("arbitrary",), vmem_limit_bytes=VMEM_LIMIT),
        name="mod",
    )(c_all, w_mod, b_mod)


def _in_proj(x3, mod3, n_pre1_ref, w_in_ref):
    bb, tt, _ = x3.shape
    sh1 = mod3[:, :, 0:D_MODEL]
    sc1 = mod3[:, :, D_MODEL:2 * D_MODEL]
    coef = n_pre1_ref[...].reshape(1, 1, D_MODEL) * (1.0 + sc1)
    u = (x3 * _rms_scale(x3)) * coef + sh1
    u2d = u.reshape(bb * tt, D_MODEL).astype(BF16)
    return jnp.dot(u2d, w_in_ref[...], preferred_element_type=F32)


def _lru_coeffs(xc, wg_ref, b_r_ref, b_i_ref, lam_ref):
    xcb = xc.astype(BF16)
    g0 = jnp.dot(xcb[:, :GATE_HALF], wg_ref[0], preferred_element_type=F32)
    g1 = jnp.dot(xcb[:, GATE_HALF:], wg_ref[1], preferred_element_type=F32)
    r = jax.nn.sigmoid(jnp.concatenate([g0[:, :GATE_HALF], g1[:, :GATE_HALF]], axis=1) + b_r_ref[...])
    i = jax.nn.sigmoid(jnp.concatenate([g0[:, GATE_HALF:], g1[:, GATE_HALF:]], axis=1) + b_i_ref[...])
    lam = lam_ref[...]
    sp = jnp.maximum(-lam, 0.0) + jnp.log1p(jnp.exp(-jnp.abs(lam)))
    log_a = -LRU_C * r * sp
    a = jnp.exp(log_a)
    y = 2.0 * log_a
    a2 = a * a
    d = a2 - 1.0
    small = d * y / jnp.log(a2)
    em1 = jnp.where(y < EXPM1_DIRECT_BELOW, d, jnp.where(d == 0.0, y, small))
    gain = jnp.sqrt(-em1)
    return a, gain * (i * xc)


def _rope(xh, cos2, sin2, lane_axis):
    return xh * cos2 + pltpu.roll(xh, DK // 2, axis=lane_axis) * sin2


def _group_norm(o):
    mu = jnp.mean(o, axis=-1, keepdims=True)
    d = o - mu
    var = jnp.mean(d * d, axis=-1, keepdims=True)
    return d * lax.rsqrt(var + GN_EPS)


def _post_mixer(x3, mod3, out_a, out_b, w_out_ref, n_post1_ref, n_pre2_ref, w_router_ref, b_router_ref,
                x1_ref, pk_ref, route_ref, cnt_scr):
    bb, tt, _ = x3.shape
    m = bb * tt
    y = (jnp.dot(out_a.astype(BF16), w_out_ref[0:D_LRU, :], preferred_element_type=F32)
         + jnp.dot(out_b.astype(BF16), w_out_ref[D_LRU:, :], preferred_element_type=F32))
    g1 = mod3[:, :, 2 * D_MODEL:3 * D_MODEL]
    sh2 = mod3[:, :, 3 * D_MODEL:4 * D_MODEL]
    sc2 = mod3[:, :, 4 * D_MODEL:5 * D_MODEL]
    y3 = y.reshape(bb, tt, D_MODEL)
    x1 = x3 + (y3 * _rms_scale(y3)) * (g1 * n_post1_ref[...].reshape(1, 1, D_MODEL))
    u2 = (x1 * _rms_scale(x1)) * (n_pre2_ref[...].reshape(1, 1, D_MODEL) * (1.0 + sc2)) + sh2
    x1_ref[...] = x1.reshape(m, D_MODEL)
    u2f = u2.reshape(m, D_MODEL)
    for j in range(ROW_GROUPS):
        pk_ref[pl.ds(j, m, stride=SUBLANES), :] = u2f[:, j * LANES:(j + 1) * LANES]
    u2b = u2f.astype(BF16)

    logits = jnp.dot(u2b, w_router_ref[...], preferred_element_type=F32) + b_router_ref[...]
    lane = lax.broadcasted_iota(jnp.int32, (m, LANES), 1)
    is_g = (lane >= N_EXPERTS) & (lane < N_EXPERTS + N_GROUPS)
    p_group = _masked_softmax(logits, is_g)
    p_g = jnp.max(p_group, axis=-1, keepdims=True)
    g_lane = jnp.min(jnp.where(is_g & (p_group == p_g), lane, LANES), axis=-1, keepdims=True)
    e_lo = (g_lane - N_EXPERTS) * PER_GROUP
    in_g = (lane >= e_lo) & (lane < e_lo + PER_GROUP)
    p_e = _masked_softmax(logits, in_g)
    pm = jnp.where(in_g, p_e, -1.0)
    w1 = jnp.max(pm, axis=-1, keepdims=True)
    i1 = jnp.min(jnp.where(pm == w1, lane, LANES), axis=-1, keepdims=True)
    pm2 = jnp.where(lane == i1, -1.0, pm)
    w2 = jnp.max(pm2, axis=-1, keepdims=True)
    i2 = jnp.min(jnp.where(pm2 == w2, lane, LANES), axis=-1, keepdims=True)
    a = jnp.minimum(i1, i2) - e_lo
    b = jnp.maximum(i1, i2) - e_lo
    pair = jnp.where(a == 0, b - 1, jnp.where(a == 1, b + 1, 5))
    cls = (g_lane - N_EXPERTS) * N_PAIRS + pair
    onehot = lane == cls
    r_i = lax.broadcasted_iota(jnp.int32, (m, m), 0)
    c_i = lax.broadcasted_iota(jnp.int32, (m, m), 1)
    earlier = jnp.where(r_i > c_i, 1.0, 0.0).astype(BF16)
    prefix = jnp.dot(earlier, jnp.where(onehot, 1.0, 0.0).astype(BF16), preferred_element_type=F32)
    run = cnt_scr[0:1, :]
    rank = jnp.sum(jnp.where(onehot, prefix + run, 0.0), axis=-1, keepdims=True)
    cnt_scr[...] = jnp.broadcast_to(
        run + jnp.sum(jnp.where(onehot, 1.0, 0.0), axis=0, keepdims=True), cnt_scr.shape)
    route = jnp.where(lane == 0, cls.astype(F32), jnp.where(lane == 1, rank, 0.0))
    route_ref[...] = jnp.transpose(route)[0:SUBLANES, :]


def _group_scan(a3, b3):
    tpos = lax.broadcasted_iota(jnp.int32, a3.shape, 1)
    s = 1
    while s < a3.shape[1]:
        keep = tpos >= s
        a_sh = jnp.where(keep, pltpu.roll(a3, s, axis=1), 1.0)
        b_sh = jnp.where(keep, pltpu.roll(b3, s, axis=1), 0.0)
        b3 = a3 * b_sh + b3
        a3 = a3 * a_sh
        s *= 2
    return a3, b3


def _scan_rows(a, b, h0):
    n, c = a.shape
    groups = n // SUBLANES
    a3, b3 = _group_scan(a.reshape(groups, SUBLANES, c), b.reshape(groups, SUBLANES, c))
    carry = h0
    out = []
    for g in range(groups):
        hg = b3[g] + a3[g] * carry
        out.append(hg)
        carry = hg[SUBLANES - 1:SUBLANES, :]
    return jnp.concatenate(out, axis=0)


def _prompt_mixer_kernel(x_ref, mod_ref, cos_ref, sin_ref,
                         n_pre1_ref, n_post1_ref, n_pre2_ref,
                         w_in_ref, conv_w_ref, conv_b_ref, wg_ref, b_r_ref, b_i_ref, lam_ref,
                         gn_w_ref, w_out_ref, w_router_ref, b_router_ref,
                         mask_ref, wstate_ref, cross_ref, cdecay_ref,
                         x1_ref, pk_ref, route_ref, conv_out_ref, lru_out_ref, ret_out_ref, cnt_out_ref,
                         conv_scr, h_scr, s_scr, cnt_scr):
    t = pl.program_id(1)
    tt = x_ref.shape[1]

    @pl.when((pl.program_id(0) == 0) & (t == 0))
    def _():
        cnt_scr[...] = jnp.zeros_like(cnt_scr)

    @pl.when(t == 0)
    def _():
        conv_scr[...] = jnp.zeros_like(conv_scr)
        h_scr[...] = jnp.zeros_like(h_scr)
        s_scr[...] = jnp.zeros_like(s_scr)

    x3 = x_ref[...]
    mod3 = mod_ref[...]
    z = _in_proj(x3, mod3, n_pre1_ref, w_in_ref)
    xr = z[:, 0:D_LRU]
    yg = z[:, D_LRU:2 * D_LRU]
    q = z[:, 1024:1536]
    k = z[:, 1536:2048]
    v = z[:, 2048:2560]
    g = z[:, 2560:3072]

    groups = tt // SUBLANES
    xr3 = xr.reshape(groups, SUBLANES, D_LRU)
    tpos = lax.broadcasted_iota(jnp.int32, xr3.shape, 1)
    tail = conv_scr[...]
    xc3 = jnp.broadcast_to(conv_b_ref[...].reshape(1, 1, D_LRU), xr3.shape)
    for j in range(CONV_W):
        back = CONV_W - 1 - j
        w_j = conv_w_ref[j:j + 1, :].reshape(1, 1, D_LRU)
        if back == 0:
            term = xr3
        else:
            cur = pltpu.roll(xr3, back, axis=1)
            first = pltpu.roll(tail, back, axis=0).reshape(1, SUBLANES, D_LRU)
            prev = jnp.concatenate([first, cur[:groups - 1]], axis=0)
            term = jnp.where(tpos >= back, cur, prev)
        xc3 = xc3 + term * w_j
    xc = xc3.reshape(tt, D_LRU)
    conv_scr[...] = xr[tt - SUBLANES:, :]

    a, b = _lru_coeffs(xc, wg_ref, b_r_ref, b_i_ref, lam_ref)
    hseq = _scan_rows(a, b, h_scr[0:1, :])
    h_scr[...] = jnp.broadcast_to(hseq[tt - 1:tt, :], h_scr.shape)
    out_a = hseq * jax.nn.gelu(yg, approximate=True)

    cos2 = cos_ref[...]
    sin2 = sin_ref[...]
    scale = DK ** -0.5
    o_heads = []
    for h in range(N_HEADS):
        hs = slice(h * DK, (h + 1) * DK)
        qh = (_rope(q[:, hs], cos2, sin2, 1) * scale).astype(BF16)
        kh = _rope(k[:, hs], cos2, sin2, 1)
        vh = v[:, hs].astype(BF16)
        o_chunks = []
        for c in range(tt // RET_CHUNK):
            cs = slice(c * RET_CHUNK, (c + 1) * RET_CHUNK)
            qc = qh[cs]
            kc = kh[cs]
            vc = vh[cs]
            s_prev = s_scr[h]
            scores = lax.dot_general(qc, kc.astype(BF16), (((1,), (1,)), ((), ())),
                                     preferred_element_type=F32) * mask_ref[h]
            inner = jnp.dot(scores.astype(BF16), vc, preferred_element_type=F32)
            cross = jnp.dot(qc, s_prev.astype(BF16), preferred_element_type=F32) * cross_ref[:, hs]
            kw = (kc * wstate_ref[:, hs]).astype(BF16)
            kv = lax.dot_general(kw, vc, (((0,), (0,)), ((), ())), preferred_element_type=F32)
            s_scr[h] = cdecay_ref[:, hs] * s_prev + kv
            o_chunks.append(inner + cross)
        o_heads.append(_group_norm(jnp.concatenate(o_chunks, axis=0)))
    o = jnp.concatenate(o_heads, axis=1)
    out_b = o * gn_w_ref[...] * _silu(g)

    _post_mixer(x3, mod3, out_a, out_b, w_out_ref, n_post1_ref, n_pre2_ref, w_router_ref, b_router_ref,
                x1_ref, pk_ref, route_ref, cnt_scr)
    cnt_out_ref[...] = cnt_scr[...]

    @pl.when(t == pl.num_programs(1) - 1)
    def _():
        conv_out_ref[0] = xr[tt - SUBLANES:, :]
        lru_out_ref[0] = hseq[tt - SUBLANES:, :]
        ret_out_ref[0] = s_scr[...]


def _const_spec(shape):
    nd = len(shape)
    return pl.BlockSpec(shape, lambda *_: (0,) * nd)


def _prompt_mixer_call(x, mod3, cos2, sin2, wts, tables):
    bsz, seq, _ = x.shape
    tt = PROMPT_TILE
    nt = seq // tt
    n_tok = bsz * seq
    (n_pre1, n_post1, n_pre2, w_in, conv_w, conv_b, wg, b_r, b_i, lam, gn_w, w_out, w_router, b_router) = wts
    mask, wstate, cross, cdecay = tables
    tok_spec = pl.BlockSpec((tt, D_MODEL), lambda b, t: (b * nt + t, 0))
    in_specs = [
        pl.BlockSpec((1, tt, D_MODEL), lambda b, t: (b, t, 0)),
        pl.BlockSpec((1, 1, 6 * D_MODEL), lambda b, t: (b, 0, 0)),
        pl.BlockSpec((tt, LANES), lambda b, t: (t, 0)),
        pl.BlockSpec((tt, LANES), lambda b, t: (t, 0)),
    ] + [_const_spec(w.shape) for w in wts] + [_const_spec(tb.shape) for tb in tables]
    out_specs = [
        tok_spec,
        pl.BlockSpec((tt * SUBLANES, LANES), lambda b, t: (b * nt + t, 0)),
        pl.BlockSpec((SUBLANES, tt), lambda b, t: (0, b * nt + t)),
        pl.BlockSpec((1, SUBLANES, D_LRU), lambda b, t: (b, 0, 0)),
        pl.BlockSpec((1, SUBLANES, D_LRU), lambda b, t: (b, 0, 0)),
        pl.BlockSpec((1, N_HEADS, DK, DV), lambda b, t: (b, 0, 0, 0)),
        pl.BlockSpec((SUBLANES, LANES), lambda b, t: (0, 0)),
    ]
    out_shape = [
        jax.ShapeDtypeStruct((n_tok, D_MODEL), F32),
        jax.ShapeDtypeStruct((n_tok * SUBLANES, LANES), F32),
        jax.ShapeDtypeStruct((SUBLANES, n_tok), F32),
        jax.ShapeDtypeStruct((bsz, SUBLANES, D_LRU), F32),
        jax.ShapeDtypeStruct((bsz, SUBLANES, D_LRU), F32),
        jax.ShapeDtypeStruct((bsz, N_HEADS, DK, DV), F32),
        jax.ShapeDtypeStruct((SUBLANES, LANES), F32),
    ]
    return pl.pallas_call(
        _prompt_mixer_kernel,
        grid=(bsz, nt),
        in_specs=in_specs,
        out_specs=out_specs,
        out_shape=out_shape,
        scratch_shapes=[
            pltpu.VMEM((SUBLANES, D_LRU), F32),
            pltpu.VMEM((SUBLANES, D_LRU), F32),
            pltpu.VMEM((N_HEADS, DK, DV), F32),
            pltpu.VMEM((SUBLANES, LANES), F32),
        ],
        compiler_params=pltpu.CompilerParams(
            dimension_semantics=("arbitrary", "arbitrary"), vmem_limit_bytes=VMEM_LIMIT),
        name="prompt_mixer",
    )(x, mod3, cos2, sin2, *wts, *tables)


def _sample_mixer_kernel(x_ref, mod_ref, cos_ref, sin_ref, buf_ref, h0_ref, s0_ref, cnt_in_ref,
                         n_pre1_ref, n_post1_ref, n_pre2_ref,
                         w_in_ref, conv_w_ref, conv_b_ref, wg_ref, b_r_ref, b_i_ref, lam_ref,
                         gn_w_ref, w_out_ref, w_router_ref, b_router_ref,
                         smask_ref, wstate_ref, cross_ref, cdecay_ref,
                         x1_ref, pk_ref, route_ref, xr_out_ref, h_out_ref, ret_out_ref, cnt_out_ref,
                         cnt_scr):
    bb, ts, _ = x_ref.shape
    m = bb * ts

    @pl.when(pl.program_id(0) == 0)
    def _():
        cnt_scr[...] = cnt_in_ref[...]

    x3 = x_ref[...]
    mod3 = mod_ref[...]
    z = _in_proj(x3, mod3, n_pre1_ref, w_in_ref)
    xr = z[:, 0:D_LRU]
    yg = z[:, D_LRU:2 * D_LRU]
    q = z[:, 1024:1536]
    k = z[:, 1536:2048]
    v = z[:, 2048:2560]
    g = z[:, 2560:3072]
    xr_out_ref[...] = xr

    xr3 = xr.reshape(bb, ts, D_LRU)
    buf3 = buf_ref[...]
    tpos = lax.broadcasted_iota(jnp.int32, (bb, ts, D_LRU), 1)
    xc3 = jnp.broadcast_to(conv_b_ref[...].reshape(1, 1, D_LRU), (bb, ts, D_LRU))
    for j in range(CONV_W):
        back = CONV_W - 1 - j
        w_j = conv_w_ref[j:j + 1, :].reshape(1, 1, D_LRU)
        if back == 0:
            term = xr3
        else:
            cur = pltpu.roll(xr3, back, axis=1)
            up = CONV_W - 1 - back
            old = buf3 if up == 0 else pltpu.roll(buf3, ts - up, axis=1)
            term = jnp.where(tpos >= back, cur, old)
        xc3 = xc3 + term * w_j
    xc = xc3.reshape(m, D_LRU)

    a, b = _lru_coeffs(xc, wg_ref, b_r_ref, b_i_ref, lam_ref)
    a3 = a.reshape(bb, ts, D_LRU)
    b3 = b.reshape(bb, ts, D_LRU) + a3 * h0_ref[...]
    _, h3 = _group_scan(a3, b3)
    hseq = h3.reshape(m, D_LRU)
    h_out_ref[...] = hseq
    out_a = hseq * jax.nn.gelu(yg, approximate=True)

    cos2 = cos_ref[...].reshape(1, ts, LANES)
    sin2 = sin_ref[...].reshape(1, ts, LANES)
    scale = DK ** -0.5
    o_heads = []
    for h in range(N_HEADS):
        hs = slice(h * DK, (h + 1) * DK)
        q3 = (_rope(q[:, hs].reshape(bb, ts, DK), cos2, sin2, 2) * scale).astype(BF16)
        k3 = _rope(k[:, hs].reshape(bb, ts, DK), cos2, sin2, 2)
        v3 = v[:, hs].reshape(bb, ts, DV).astype(BF16)
        q2 = q3.reshape(m, DK)
        k2 = k3.reshape(m, DK).astype(BF16)
        v2 = v3.reshape(m, DV)
        scores = lax.dot_general(q2, k2, (((1,), (1,)), ((), ())),
                                 preferred_element_type=F32) * smask_ref[h]
        inner = jnp.dot(scores.astype(BF16), v2, preferred_element_type=F32)
        s0h = s0_ref[:, h]
        cross = jnp.einsum('bid,bde->bie', q3, s0h.astype(BF16), preferred_element_type=F32)
        cross = cross * cross_ref[:, hs].reshape(1, ts, DV)
        kw3 = (k3 * wstate_ref[:, hs].reshape(1, ts, DK)).astype(BF16)
        kv = jnp.einsum('bjd,bje->bde', kw3, v3, preferred_element_type=F32)
        ret_out_ref[:, h] = cdecay_ref[:, hs].reshape(1, 1, DV) * s0h + kv
        o_heads.append(_group_norm(inner + cross.reshape(m, DV)))
    o = jnp.concatenate(o_heads, axis=1)
    out_b = o * gn_w_ref[...] * _silu(g)

    _post_mixer(x3, mod3, out_a, out_b, w_out_ref, n_post1_ref, n_pre2_ref, w_router_ref, b_router_ref,
                x1_ref, pk_ref, route_ref, cnt_scr)
    cnt_out_ref[...] = cnt_scr[...]


def _sample_mixer_call(x, mod3, cos2, sin2, buf8, h0p, s0, cnt_in, wts, tables):
    bsz, ts, _ = x.shape
    bb = SAMPLE_SEQS
    m = bb * ts
    n_tok = bsz * ts
    seq_spec = lambda w: pl.BlockSpec((bb, ts, w), lambda i: (i, 0, 0))
    tok_spec = lambda w: pl.BlockSpec((m, w), lambda i: (i, 0))
    in_specs = [
        seq_spec(D_MODEL),
        pl.BlockSpec((bb, 1, 6 * D_MODEL), lambda i: (i, 0, 0)),
        _const_spec(cos2.shape),
        _const_spec(sin2.shape),
        seq_spec(D_LRU),
        seq_spec(D_LRU),
        pl.BlockSpec((bb, N_HEADS, DK, DV), lambda i: (i, 0, 0, 0)),
        _const_spec(cnt_in.shape),
    ] + [_const_spec(w.shape) for w in wts] + [_const_spec(tb.shape) for tb in tables]
    out_specs = [
        tok_spec(D_MODEL),
        pl.BlockSpec((m * SUBLANES, LANES), lambda i: (i, 0)),
        pl.BlockSpec((SUBLANES, m), lambda i: (0, i)),
        tok_spec(D_LRU),
        tok_spec(D_LRU),
        pl.BlockSpec((bb, N_HEADS, DK, DV), lambda i: (i, 0, 0, 0)),
        _const_spec(cnt_in.shape),
    ]
    out_shape = [
        jax.ShapeDtypeStruct((n_tok, D_MODEL), F32),
        jax.ShapeDtypeStruct((n_tok * SUBLANES, LANES), F32),
        jax.ShapeDtypeStruct((SUBLANES, n_tok), F32),
        jax.ShapeDtypeStruct((n_tok, D_LRU), F32),
        jax.ShapeDtypeStruct((n_tok, D_LRU), F32),
        jax.ShapeDtypeStruct((bsz, N_HEADS, DK, DV), F32),
        jax.ShapeDtypeStruct(cnt_in.shape, F32),
    ]
    return pl.pallas_call(
        _sample_mixer_kernel,
        grid=(bsz // bb,),
        in_specs=in_specs,
        out_specs=out_specs,
        out_shape=out_shape,
        scratch_shapes=[pltpu.VMEM((SUBLANES, LANES), F32)],
        compiler_params=pltpu.CompilerParams(
            dimension_semantics=("arbitrary",), vmem_limit_bytes=VMEM_LIMIT),
        name="sample_mixer",
    )(x, mod3, cos2, sin2, buf8, h0p, s0, cnt_in, *wts, *tables)


def _slab(ref, r):
    return ref.at[pl.ds(pl.multiple_of(r * SUBLANES, SUBLANES), SUBLANES)]


def _dispatch_kernel(cls_ref, rank_ref, cnt_ref, padcnt_ref, start_ref, srcp_ref, srcs_ref, out_ref, ring, sem,
                     *, p_tiles):
    i = pl.program_id(0)
    n = pl.num_programs(0)
    tile_rows = ring.shape[1]
    td = tile_rows // SUBLANES
    moe_rows = MOE_TILE * SUBLANES

    def issue(slot):
        def body(g, carry):
            first = i * td + g * DMA_GROUP
            slots = [start_ref[cls_ref[first + j]] + rank_ref[first + j] for j in range(DMA_GROUP)]
            first = g * DMA_GROUP
            for j in range(DMA_GROUP):
                pltpu.make_async_copy(
                    _slab(ring.at[slot], first + j), _slab(out_ref, slots[j]), sem.at[slot]
                ).start(priority=j % 2)
            return carry
        lax.fori_loop(0, td // DMA_GROUP, body, 0)

    def wait_tile(slot):
        pltpu.make_async_copy(ring.at[slot], out_ref.at[pl.ds(0, tile_rows)], sem.at[slot]).wait()

    def zero_fill(slot):
        ring[slot] = jnp.zeros((tile_rows, LANES), F32)
        zero_src = ring.at[slot]

        def per_class(c, carry):
            lo = start_ref[c] + cnt_ref[c]
            hi = start_ref[c] + padcnt_ref[c]

            def fill(r, carry2):
                pltpu.make_async_copy(_slab(zero_src, 0), _slab(out_ref, r), sem.at[slot]).start()
                return carry2
            lax.fori_loop(lo, hi, fill, 0)

            def done(r, carry2):
                pltpu.make_async_copy(_slab(zero_src, 0), _slab(out_ref, 0), sem.at[slot]).wait()
                return carry2
            lax.fori_loop(lo, hi, done, 0)
            return carry
        lax.fori_loop(0, N_CLASSES, per_class, 0)

        used_tiles = (start_ref[N_CLASSES - 1] + padcnt_ref[N_CLASSES - 1]) // MOE_TILE
        all_tiles = out_ref.shape[0] // moe_rows

        def tile_copy(t):
            return pltpu.make_async_copy(
                zero_src.at[pl.ds(0, moe_rows)],
                out_ref.at[pl.ds(pl.multiple_of(t * moe_rows, moe_rows), moe_rows)], sem.at[slot])

        def fill_tile(t, carry):
            tile_copy(t).start()
            return carry
        lax.fori_loop(used_tiles, all_tiles, fill_tile, 0)

        def done_tile(t, carry):
            tile_copy(t).wait()
            return carry
        lax.fori_loop(used_tiles, all_tiles, done_tile, 0)

    for s in range(RING):
        @pl.when(i % RING == s)
        def _():
            @pl.when(i < p_tiles)
            def _():
                ring[s] = srcp_ref[...]

            @pl.when(i >= p_tiles)
            def _():
                ring[s] = srcs_ref[...]

            issue(s)

            @pl.when(i >= RING - 1)
            def _():
                wait_tile((s + 1) % RING)

            @pl.when(i == n - 1)
            def _():
                for back in range(RING - 2, -1, -1):
                    wait_tile((s - back) % RING)
                zero_fill(s)


def _dispatch_call(cls, rank, cnt, padcnt, start, pk_p, pk_s, n_rows):
    td = DISPATCH_TILE
    tile_rows = td * SUBLANES
    p_tiles = pk_p.shape[0] // tile_rows
    s_tiles = pk_s.shape[0] // tile_rows
    assert p_tiles + s_tiles >= RING and td >= MOE_TILE
    return pl.pallas_call(
        functools.partial(_dispatch_kernel, p_tiles=p_tiles),
        grid_spec=pltpu.PrefetchScalarGridSpec(
            num_scalar_prefetch=5,
            grid=(p_tiles + s_tiles,),
            in_specs=[
                pl.BlockSpec((tile_rows, LANES), lambda i, *_: (jnp.minimum(i, p_tiles - 1), 0)),
                pl.BlockSpec((tile_rows, LANES), lambda i, *_: (jnp.maximum(i - p_tiles, 0), 0)),
            ],
            out_specs=pl.BlockSpec(memory_space=pl.ANY),
            scratch_shapes=[pltpu.VMEM((RING, tile_rows, LANES), F32),
                            pltpu.SemaphoreType.DMA((RING,))],
        ),
        out_shape=jax.ShapeDtypeStruct((n_rows * SUBLANES, LANES), F32),
        compiler_params=pltpu.CompilerParams(
            dimension_semantics=("arbitrary",), has_side_effects=True, vmem_limit_bytes=VMEM_LIMIT),
        name="moe_dispatch",
    )(cls, rank, cnt, padcnt, start, pk_p, pk_s)


def _moe_kernel(ea_ref, eb_ref, valid_ref, xs_ref, w_router_ref, b_router_ref,
                wga_ref, wua_ref, wda_ref, wgb_ref, wub_ref, wdb_ref, f_ref):
    t = pl.program_id(0)

    @pl.when(valid_ref[t] == 1)
    def _():
        tm = xs_ref.shape[0] // SUBLANES
        x = jnp.concatenate([xs_ref[pl.ds(j, tm, stride=SUBLANES), :] for j in range(ROW_GROUPS)],
                            axis=1).astype(BF16)
        e_a = ea_ref[t]
        e_b = eb_ref[t]
        e_lo = (e_a // PER_GROUP) * PER_GROUP
        logits = jnp.dot(x, w_router_ref[...], preferred_element_type=F32) + b_router_ref[...]
        lane = lax.broadcasted_iota(jnp.int32, (tm, LANES), 1)
        pick = lambda p, idx: jnp.sum(jnp.where(lane == idx, p, 0.0), axis=-1, keepdims=True)
        p_group = _masked_softmax(logits, (lane >= N_EXPERTS) & (lane < N_EXPERTS + N_GROUPS))
        p_g = pick(p_group, N_EXPERTS + e_a // PER_GROUP)
        p_e = _masked_softmax(logits, (lane >= e_lo) & (lane < e_lo + PER_GROUP))
        w_a = pick(p_e, e_a)
        w_b = pick(p_e, e_b)
        wsum = w_a + w_b

        def expert(wg_ref, wu_ref, gate):
            hg = jnp.dot(x, wg_ref[0].astype(BF16), preferred_element_type=F32)
            hu = jnp.dot(x, wu_ref[0].astype(BF16), preferred_element_type=F32)
            return (_silu(hg) * hu * gate).astype(BF16)

        ha = expert(wga_ref, wua_ref, p_g * (w_a / wsum))
        hb = expert(wgb_ref, wub_ref, p_g * (w_b / wsum))
        for c in range(D_MODEL // GATE_HALF):
            cols = slice(c * GATE_HALF, (c + 1) * GATE_HALF)
            f = (jnp.dot(ha, wda_ref[0, :, cols].astype(BF16), preferred_element_type=F32)
                 + jnp.dot(hb, wdb_ref[0, :, cols].astype(BF16), preferred_element_type=F32))
            for jj in range(GATE_HALF // LANES):
                j = c * (GATE_HALF // LANES) + jj
                f_ref[pl.ds(j, tm, stride=SUBLANES), :] = f[:, jj * LANES:(jj + 1) * LANES]

    @pl.when(valid_ref[t] == 0)
    def _():
        f_ref[...] = jnp.zeros_like(f_ref)


def _moe_call(tile_ea, tile_eb, tile_valid, rows, w_router, b_router, wg, wu, wd):
    n_rows = rows.shape[0] // SUBLANES
    tm = MOE_TILE
    const = lambda a: pl.BlockSpec(a.shape, lambda t, ea, eb, v: (0,) * a.ndim)
    up = lambda sel: pl.BlockSpec((1, D_MODEL, D_EXPERT), lambda t, ea, eb, v: (sel(ea, eb)[t], 0, 0))
    down = lambda sel: pl.BlockSpec((1, D_EXPERT, D_MODEL), lambda t, ea, eb, v: (sel(ea, eb)[t], 0, 0))
    first = lambda ea, eb: ea
    second = lambda ea, eb: eb
    return pl.pallas_call(
        _moe_kernel,
        grid_spec=pltpu.PrefetchScalarGridSpec(
            num_scalar_prefetch=3,
            grid=(n_rows // tm,),
            in_specs=[
                pl.BlockSpec((tm * SUBLANES, LANES), lambda t, ea, eb, v: (t, 0)),
                const(w_router), const(b_router),
                up(first), up(first), down(first), up(second), up(second), down(second),
            ],
            out_specs=pl.BlockSpec((tm * SUBLANES, LANES), lambda t, ea, eb, v: (t, 0)),
        ),
        out_shape=jax.ShapeDtypeStruct((n_rows * SUBLANES, LANES), F32),
        compiler_params=pltpu.CompilerParams(
            dimension_semantics=("arbitrary",), vmem_limit_bytes=VMEM_LIMIT),
        name="moe_experts",
    )(tile_ea, tile_eb, tile_valid, rows, w_router, b_router, wg, wu, wd, wg, wu, wd)


def _combine_kernel(cls_ref, rank_ref, start_ref, x1_ref, mod_ref, n_post2_ref, f_hbm, o_ref, fbuf, sem):
    i = pl.program_id(0)
    n = pl.num_programs(0)
    tf = x1_ref.shape[0]

    def issue(tile, slot):
        def body(g, carry):
            first = tile * tf + g * DMA_GROUP
            slots = [start_ref[cls_ref[first + j]] + rank_ref[first + j] for j in range(DMA_GROUP)]
            first = g * DMA_GROUP
            for j in range(DMA_GROUP):
                pltpu.make_async_copy(
                    _slab(f_hbm, slots[j]), _slab(fbuf.at[slot], first + j), sem.at[slot]
                ).start(priority=j % 2)
            return carry
        lax.fori_loop(0, tf // DMA_GROUP, body, 0)

    def finish(slot):
        pltpu.make_async_copy(f_hbm.at[pl.ds(0, tf * SUBLANES)], fbuf.at[slot], sem.at[slot]).wait()
        f = jnp.concatenate([fbuf[slot, pl.ds(j, tf, stride=SUBLANES), :] for j in range(D_MODEL // LANES)],
                            axis=1)
        mod3 = mod_ref[...]
        bb = mod3.shape[0]
        f3 = f.reshape(bb, tf // bb, D_MODEL)
        g2 = mod3[:, :, 5 * D_MODEL:6 * D_MODEL]
        x13 = x1_ref[...].reshape(f3.shape)
        out = x13 + (f3 * _rms_scale(f3)) * (g2 * n_post2_ref[...].reshape(1, 1, D_MODEL))
        o_ref[...] = out.reshape(o_ref.shape)

    @pl.when(i == 0)
    def _():
        issue(0, 0)

    for parity in range(2):
        @pl.when(i % 2 == parity)
        def _():
            @pl.when(i + 1 < n)
            def _():
                issue(i + 1, 1 - parity)
            finish(parity)


def _combine_call(cls, rank, start, x1, mod3, n_post2, f_sorted, seqs_per_tile):
    n_tok = x1.shape[0]
    tf = FINAL_TILE
    tok_per_seq = n_tok // mod3.shape[0]
    if seqs_per_tile > 1:
        mod_map = lambda i, *_: (i, 0, 0)
    else:
        mod_map = lambda i, *_: ((i * tf) // tok_per_seq, 0, 0)
    return pl.pallas_call(
        _combine_kernel,
        grid_spec=pltpu.PrefetchScalarGridSpec(
            num_scalar_prefetch=3,
            grid=(n_tok // tf,),
            in_specs=[
                pl.BlockSpec((tf, D_MODEL), lambda i, *_: (i, 0)),
                pl.BlockSpec((seqs_per_tile, 1, 6 * D_MODEL), mod_map),
                pl.BlockSpec((1, D_MODEL), lambda i, *_: (0, 0)),
                pl.BlockSpec(memory_space=pl.ANY),
            ],
            out_specs=pl.BlockSpec((tf, D_MODEL), lambda i, *_: (i, 0)),
            scratch_shapes=[pltpu.VMEM((2, tf * SUBLANES, LANES), F32), pltpu.SemaphoreType.DMA((2,))],
        ),
        out_shape=jax.ShapeDtypeStruct((n_tok, D_MODEL), F32),
        compiler_params=pltpu.CompilerParams(
            dimension_semantics=("arbitrary",), vmem_limit_bytes=VMEM_LIMIT),
        name="moe_combine",
    )(cls, rank, start, x1, mod3, n_post2, f_sorted)


def _block_diag_gate(w_r, w_i):
    per_half = GATE_HALF // LRU_BLOCK
    n_half = D_LRU // GATE_HALF
    eye = np.eye(per_half, dtype=np.float32)

    def block_diag(w):
        w4 = w.reshape(n_half, per_half, LRU_BLOCK, LRU_BLOCK)
        return jnp.einsum('hnij,nm->hnimj', w4, eye).reshape(n_half, GATE_HALF, GATE_HALF)

    return jnp.concatenate([block_diag(w_r), block_diag(w_i)], axis=-1).astype(BF16)


def _rope_tables(pos):
    half = DK // 2
    inv = np.float64(ROPE_BASE) ** (-np.arange(half, dtype=np.float64) / half)
    ang = np.asarray(pos, np.float64)[:, None] * inv[None, :]
    cos = np.cos(ang)
    sin = np.sin(ang)
    return (np.concatenate([cos, cos], axis=-1).astype(np.float32),
            np.concatenate([-sin, sin], axis=-1).astype(np.float32))


def _decay_tables(c):
    log_g = np.log1p(-np.exp2(-5.0 - np.arange(N_HEADS, dtype=np.float64)))
    idx = np.arange(c, dtype=np.float64)
    diff = idx[:, None] - idx[None, :]
    mask = np.where(diff[None] >= 0, np.exp(np.maximum(diff, 0.0)[None] * log_g[:, None, None]), 0.0)
    w_state = np.exp((c - 1.0 - idx)[None, :] * log_g[:, None])
    cross_decay = np.exp((idx + 1.0)[:, None] * log_g[None, :])
    chunk_decay = np.exp(c * log_g)
    wstate_full = np.repeat(w_state.T, DK, axis=1)
    cross_full = np.repeat(cross_decay, DV, axis=1)
    cdecay_full = np.repeat(chunk_decay, DV)[None, :]
    return tuple(t.astype(np.float32) for t in (mask, wstate_full, cross_full, cdecay_full))


def kernel(x_prompt, x_sample, state_conv, state_lru, state_ret, c_prompt, c_sample, w_mod, b_mod, norm_pre_mix, norm_post_mix, norm_pre_ffn, norm_post_ffn, w_in, conv_w, conv_b, w_rgate, b_rgate, w_igate, b_igate, lru_lambda, ret_gn_w, w_out, w_router_group, b_router_group, w_router_expert, b_router_expert, w_exp_gate, w_exp_up, w_exp_down):
    bp, tp, _ = x_prompt.shape
    bs, ts, _ = x_sample.shape
    l = 0

    mod = _mod_call(jnp.concatenate([c_prompt, c_sample], axis=0), w_mod[l], b_mod[l][None, :])
    mod_p = mod[:bp][:, None, :]
    mod_s = mod[bp:][:, None, :]

    unused = LANES - N_EXPERTS - N_GROUPS
    w_router = jnp.concatenate(
        [w_router_expert[l], w_router_group[l], jnp.zeros((D_MODEL, unused), F32)], axis=1).astype(BF16)
    b_router = jnp.concatenate(
        [b_router_expert[l], b_router_group[l], jnp.zeros((unused,), F32)])[None, :]

    row = lambda vec: vec.reshape(1, -1)
    wts = (row(norm_pre_mix[l]), row(norm_post_mix[l]), row(norm_pre_ffn[l]),
           w_in[l].astype(BF16), conv_w[l], row(conv_b[l]),
           _block_diag_gate(w_rgate[l], w_igate[l]),
           row(b_rgate[l]), row(b_igate[l]), row(lru_lambda[l]), row(ret_gn_w[l]),
           w_out[l].astype(BF16), w_router, b_router)

    cos_p, sin_p = _rope_tables(np.arange(tp))
    x1_p, pk_p, route_p, conv_p8, lru_p8, ret_p, cnt_p = _prompt_mixer_call(
        x_prompt, mod_p, cos_p, sin_p, wts, _decay_tables(math.gcd(tp, RET_CHUNK)))

    cos_s, sin_s = _rope_tables(PAST_LEN + np.arange(ts))
    mask8, wstate_s, cross_s, cdecay_s = _decay_tables(math.gcd(ts, RET_CHUNK))
    eye = np.eye(SAMPLE_SEQS, dtype=np.float32)
    smask = np.stack([np.kron(eye, mask8[h]) for h in range(N_HEADS)])
    buf8 = jnp.pad(state_conv[l], ((0, 0), (0, ts - (CONV_W - 1)), (0, 0)))
    h0p = jnp.pad(state_lru[l][:, None, :], ((0, 0), (0, ts - 1), (0, 0)))
    x1_s, pk_s, route_s, xr_s, h_s, ret_s, cnt_all = _sample_mixer_call(
        x_sample, mod_s, cos_s, sin_s, buf8, h0p, state_ret[l], cnt_p, wts,
        (smask, wstate_s, cross_s, cdecay_s))

    n_p = bp * tp
    n_tok = n_p + bs * ts
    tm = MOE_TILE
    max_tiles = n_tok // tm + N_CLASSES
    cls_p, rank_p = route_p[0].astype(jnp.int32), route_p[1].astype(jnp.int32)
    cls_s, rank_s = route_s[0].astype(jnp.int32), route_s[1].astype(jnp.int32)
    cls = jnp.concatenate([cls_p, cls_s])
    rank = jnp.concatenate([rank_p, rank_s])
    cnt = cnt_all[0, :N_CLASSES].astype(jnp.int32)
    ntile = (cnt + (tm - 1)) // tm
    padcnt = ntile * tm
    before = np.tril(np.ones((N_CLASSES, N_CLASSES), np.int32), -1)
    upto = np.tril(np.ones((N_CLASSES, N_CLASSES), np.int32))
    start = jnp.sum(before * padcnt[None, :], axis=1)
    tile_end = jnp.sum(upto * ntile[None, :], axis=1)
    n_used = jnp.sum(ntile)
    tile_ids = np.arange(max_tiles, dtype=np.int32)
    tile_valid = (tile_ids < n_used).astype(jnp.int32)
    last_used = jnp.minimum(tile_ids, n_used - 1)
    tile_cls = jnp.sum((last_used[:, None] >= tile_end[None, :]).astype(jnp.int32), axis=1)
    pair = tile_cls % N_PAIRS
    first_expert = (pair >= 3).astype(jnp.int32) + (pair >= 5).astype(jnp.int32)
    second_expert = jnp.where(pair < 3, pair + 1, jnp.where(pair < 5, pair - 1, 3))
    tile_ea = (tile_cls // N_PAIRS) * PER_GROUP + first_expert
    tile_eb = (tile_cls // N_PAIRS) * PER_GROUP + second_expert

    n_post2 = row(norm_post_ffn[l])
    wg = w_exp_gate[l]
    wu = w_exp_up[l]
    wd = w_exp_down[l]
    rows = _dispatch_call(cls, rank, cnt, padcnt, start, pk_p, pk_s, max_tiles * tm)
    f_sorted = _moe_call(tile_ea, tile_eb, tile_valid, rows, w_router, b_router, wg, wu, wd)
    y_p = _combine_call(cls_p, rank_p, start, x1_p, mod_p, n_post2, f_sorted, 1)
    y_s = _combine_call(cls_s, rank_s, start, x1_s, mod_s, n_post2, f_sorted, FINAL_TILE // ts)

    conv_p = conv_p8[:, SUBLANES - (CONV_W - 1):, :]
    lru_p = lru_p8[:, SUBLANES - 1, :]
    xr_s3 = xr_s.reshape(bs, ts, D_LRU)
    conv_s = xr_s3[:, ts - (CONV_W - 1):, :]
    lru_s = h_s.reshape(bs, ts, D_LRU)[:, ts - 1, :]
    return (y_p.reshape(bp, tp, D_MODEL), y_s.reshape(bs, ts, D_MODEL),
            conv_p[None], lru_p[None], ret_p[None],
            conv_s[None], lru_s[None], ret_s[None])
```

```python
import functools
import math

import jax
import jax.numpy as jnp
import numpy as np
from jax import lax
from jax.experimental import pallas as pl
from jax.experimental.pallas import tpu as pltpu

F32 = jnp.float32
BF16 = jnp.bfloat16

D_MODEL = 1024
D_LRU = 512
D_RET = 512
N_LRU_BLOCKS = 8
LRU_BLOCK = D_LRU // N_LRU_BLOCKS
CONV_W = 4
LRU_C = 8.0
N_HEADS = 4
DK = 128
DV = 128
RET_CHUNK = 128
ROPE_BASE = 10000.0
D_IN_PROJ = 3072
N_GROUPS = 4
PER_GROUP = 4
N_EXPERTS = 16
D_EXPERT = 256
EXPM1_DIRECT_BELOW = -0.5
NORM_EPS = 1e-6
GN_EPS = 1e-5
PAST_LEN = 16384

SUBLANES = 8
LANES = 128
GATE_HALF = 256
VMEM_LIMIT = 56 * 1024 * 1024

PROMPT_TILE = 512
SAMPLE_SEQS = 16
MOE_TILE = 256
FINAL_TILE = 512
DISPATCH_TILE = 512
RING = 3
DMA_GROUP = 8

N_PAIRS = 6
N_CLASSES = N_GROUPS * N_PAIRS
ROW_GROUPS = D_MODEL // LANES


def _silu(x):
    return x * jax.nn.sigmoid(x)


def _rms_scale(x):
    return lax.rsqrt(jnp.mean(x * x, axis=-1, keepdims=True) + NORM_EPS)


def _masked_softmax(logits, mask):
    top = jnp.max(jnp.where(mask, logits, -jnp.inf), axis=-1, keepdims=True)
    e = jnp.where(mask, jnp.exp(logits - top), 0.0)
    return e / jnp.sum(e, axis=-1, keepdims=True)


def _mod_kernel(c_ref, w_ref, b_ref, o_ref):
    s = _silu(c_ref[...]).astype(BF16)
    o_ref[...] = jnp.dot(s, w_ref[...].astype(BF16), preferred_element_type=F32) + b_ref[...]


def _mod_call(c_all, w_mod, b_mod):
    rows = c_all.shape[0]
    ncol = w_mod.shape[1]
    blk = D_MODEL
    return pl.pallas_call(
        _mod_kernel,
        grid=(ncol // blk,),
        in_specs=[
            pl.BlockSpec((rows, D_MODEL), lambda j: (0, 0)),
            pl.BlockSpec((D_MODEL, blk), lambda j: (0, j)),
            pl.BlockSpec((1, blk), lambda j: (0, j)),
        ],
        out_specs=pl.BlockSpec((rows, blk), lambda j: (0, j)),
        out_shape=jax.ShapeDtypeStruct((rows, ncol), F32),
        compiler_params=pltpu.CompilerParams(
            dimension_semantics=("arbitrary",), vmem_limit_bytes=VMEM_LIMIT),
        name="mod",
    )(c_all, w_mod, b_mod)


def _in_proj(x3, mod3, n_pre1_ref, w_in_ref):
    bb, tt, _ = x3.shape
    sh1 = mod3[:, :, 0:D_MODEL]
    sc1 = mod3[:, :, D_MODEL:2 * D_MODEL]
    coef = n_pre1_ref[...].reshape(1, 1, D_MODEL) * (1.0 + sc1)
    u = (x3 * _rms_scale(x3)) * coef + sh1
    u2d = u.reshape(bb * tt, D_MODEL).astype(BF16)
    return jnp.dot(u2d, w_in_ref[...], preferred_element_type=F32)


def _lru_coeffs(xc, wg_ref, b_r_ref, b_i_ref, lam_ref):
    xcb = xc.astype(BF16)
    g0 = jnp.dot(xcb[:, :GATE_HALF], wg_ref[0], preferred_element_type=F32)
    g1 = jnp.dot(xcb[:, GATE_HALF:], wg_ref[1], preferred_element_type=F32)
    r = jax.nn.sigmoid(jnp.concatenate([g0[:, :GATE_HALF], g1[:, :GATE_HALF]], axis=1) + b_r_ref[...])
    i = jax.nn.sigmoid(jnp.concatenate([g0[:, GATE_HALF:], g1[:, GATE_HALF:]], axis=1) + b_i_ref[...])
    lam = lam_ref[...]
    sp = jnp.maximum(-lam, 0.0) + jnp.log1p(jnp.exp(-jnp.abs(lam)))
    log_a = -LRU_C * r * sp
    a = jnp.exp(log_a)
    y = 2.0 * log_a
    a2 = a * a
    d = a2 - 1.0
    small = d * y / jnp.log(a2)
    em1 = jnp.where(y < EXPM1_DIRECT_BELOW, d, jnp.where(d == 0.0, y, small))
    gain = jnp.sqrt(-em1)
    return a, gain * (i * xc)


def _rope(xh, cos2, sin2, lane_axis):
    return xh * cos2 + pltpu.roll(xh, DK // 2, axis=lane_axis) * sin2


def _group_norm(o):
    mu = jnp.mean(o, axis=-1, keepdims=True)
    d = o - mu
    var = jnp.mean(d * d, axis=-1, keepdims=True)
    return d * lax.rsqrt(var + GN_EPS)


def _post_mixer(x3, mod3, out_a, out_b, w_out_ref, n_post1_ref, n_pre2_ref, w_router_ref, b_router_ref,
                x1_ref, pk_ref, route_ref, cnt_scr):
    bb, tt, _ = x3.shape
    m = bb * tt
    y = (jnp.dot(out_a.astype(BF16), w_out_ref[0:D_LRU, :], preferred_element_type=F32)
         + jnp.dot(out_b.astype(BF16), w_out_ref[D_LRU:, :], preferred_element_type=F32))
    g1 = mod3[:, :, 2 * D_MODEL:3 * D_MODEL]
    sh2 = mod3[:, :, 3 * D_MODEL:4 * D_MODEL]
    sc2 = mod3[:, :, 4 * D_MODEL:5 * D_MODEL]
    y3 = y.reshape(bb, tt, D_MODEL)
    x1 = x3 + (y3 * _rms_scale(y3)) * (g1 * n_post1_ref[...].reshape(1, 1, D_MODEL))
    u2 = (x1 * _rms_scale(x1)) * (n_pre2_ref[...].reshape(1, 1, D_MODEL) * (1.0 + sc2)) + sh2
    x1_ref[...] = x1.reshape(m, D_MODEL)
    u2f = u2.reshape(m, D_MODEL)
    for j in range(ROW_GROUPS):
        pk_ref[pl.ds(j, m, stride=SUBLANES), :] = u2f[:, j * LANES:(j + 1) * LANES]
    u2b = u2f.astype(BF16)

    logits = jnp.dot(u2b, w_router_ref[...], preferred_element_type=F32) + b_router_ref[...]
    lane = lax.broadcasted_iota(jnp.int32, (m, LANES), 1)
    is_g = (lane >= N_EXPERTS) & (lane < N_EXPERTS + N_GROUPS)
    p_group = _masked_softmax(logits, is_g)
    p_g = jnp.max(p_group, axis=-1, keepdims=True)
    g_lane = jnp.min(jnp.where(is_g & (p_group == p_g), lane, LANES), axis=-1, keepdims=True)
    e_lo = (g_lane - N_EXPERTS) * PER_GROUP
    in_g = (lane >= e_lo) & (lane < e_lo + PER_GROUP)
    p_e = _masked_softmax(logits, in_g)
    pm = jnp.where(in_g, p_e, -1.0)
    w1 = jnp.max(pm, axis=-1, keepdims=True)
    i1 = jnp.min(jnp.where(pm == w1, lane, LANES), axis=-1, keepdims=True)
    pm2 = jnp.where(lane == i1, -1.0, pm)
    w2 = jnp.max(pm2, axis=-1, keepdims=True)
    i2 = jnp.min(jnp.where(pm2 == w2, lane, LANES), axis=-1, keepdims=True)
    a = jnp.minimum(i1, i2) - e_lo
    b = jnp.maximum(i1, i2) - e_lo
    pair = jnp.where(a == 0, jnp.where(b == 3, 4, b - 1), jnp.where(a == 1, b, 5))
    cls = (g_lane - N_EXPERTS) * N_PAIRS + pair
    onehot = lane == cls
    r_i = lax.broadcasted_iota(jnp.int32, (m, m), 0)
    c_i = lax.broadcasted_iota(jnp.int32, (m, m), 1)
    earlier = jnp.where(r_i > c_i, 1.0, 0.0).astype(BF16)
    prefix = jnp.dot(earlier, jnp.where(onehot, 1.0, 0.0).astype(BF16), preferred_element_type=F32)
    run = cnt_scr[0:1, :]
    rank = jnp.sum(jnp.where(onehot, prefix + run, 0.0), axis=-1, keepdims=True)
    cnt_scr[...] = jnp.broadcast_to(
        run + jnp.sum(jnp.where(onehot, 1.0, 0.0), axis=0, keepdims=True), cnt_scr.shape)
    route = jnp.where(lane == 0, cls.astype(F32), jnp.where(lane == 1, rank, 0.0))
    route_ref[...] = jnp.transpose(route)[0:SUBLANES, :]


def _group_scan(a3, b3):
    tpos = lax.broadcasted_iota(jnp.int32, a3.shape, 1)
    s = 1
    while s < a3.shape[1]:
        keep = tpos >= s
        a_sh = jnp.where(keep, pltpu.roll(a3, s, axis=1), 1.0)
        b_sh = jnp.where(keep, pltpu.roll(b3, s, axis=1), 0.0)
        b3 = a3 * b_sh + b3
        a3 = a3 * a_sh
        s *= 2
    return a3, b3


def _scan_rows(a, b, h0):
    n, c = a.shape
    groups = n // SUBLANES
    a3, b3 = _group_scan(a.reshape(groups, SUBLANES, c), b.reshape(groups, SUBLANES, c))
    carry = h0
    out = []
    for g in range(groups):
        hg = b3[g] + a3[g] * carry
        out.append(hg)
        carry = hg[SUBLANES - 1:SUBLANES, :]
    return jnp.concatenate(out, axis=0)


def _prompt_mixer_kernel(x_ref, mod_ref, cos_ref, sin_ref,
                         n_pre1_ref, n_post1_ref, n_pre2_ref,
                         w_in_ref, conv_w_ref, conv_b_ref, wg_ref, b_r_ref, b_i_ref, lam_ref,
                         gn_w_ref, w_out_ref, w_router_ref, b_router_ref,
                         mask_ref, wstate_ref, cross_ref, cdecay_ref,
                         x1_ref, pk_ref, route_ref, conv_out_ref, lru_out_ref, ret_out_ref, cnt_out_ref,
                         conv_scr, h_scr, s_scr, cnt_scr):
    t = pl.program_id(1)
    tt = x_ref.shape[1]

    @pl.when((pl.program_id(0) == 0) & (t == 0))
    def _():
        cnt_scr[...] = jnp.zeros_like(cnt_scr)

    @pl.when(t == 0)
    def _():
        conv_scr[...] = jnp.zeros_like(conv_scr)
        h_scr[...] = jnp.zeros_like(h_scr)
        s_scr[...] = jnp.zeros_like(s_scr)

    x3 = x_ref[...]
    mod3 = mod_ref[...]
    z = _in_proj(x3, mod3, n_pre1_ref, w_in_ref)
    xr, yg, q, k, v, g = (z[:, c * D_LRU:(c + 1) * D_LRU] for c in range(D_IN_PROJ // D_LRU))

    groups = tt // SUBLANES
    xr3 = xr.reshape(groups, SUBLANES, D_LRU)
    tpos = lax.broadcasted_iota(jnp.int32, xr3.shape, 1)
    tail = conv_scr[...]
    xc3 = jnp.broadcast_to(conv_b_ref[...].reshape(1, 1, D_LRU), xr3.shape)
    for j in range(CONV_W):
        back = CONV_W - 1 - j
        w_j = conv_w_ref[j:j + 1, :].reshape(1, 1, D_LRU)
        if back == 0:
            term = xr3
        else:
            cur = pltpu.roll(xr3, back, axis=1)
            first = pltpu.roll(tail, back, axis=0).reshape(1, SUBLANES, D_LRU)
            prev = jnp.concatenate([first, cur[:groups - 1]], axis=0)
            term = jnp.where(tpos >= back, cur, prev)
        xc3 = xc3 + term * w_j
    xc = xc3.reshape(tt, D_LRU)
    conv_scr[...] = xr[tt - SUBLANES:, :]

    a, b = _lru_coeffs(xc, wg_ref, b_r_ref, b_i_ref, lam_ref)
    hseq = _scan_rows(a, b, h_scr[0:1, :])
    h_scr[...] = jnp.broadcast_to(hseq[tt - 1:tt, :], h_scr.shape)
    out_a = hseq * jax.nn.gelu(yg, approximate=True)

    cos2 = cos_ref[...]
    sin2 = sin_ref[...]
    scale = DK ** -0.5
    o_heads = []
    for h in range(N_HEADS):
        hs = slice(h * DK, (h + 1) * DK)
        qh = (_rope(q[:, hs], cos2, sin2, 1) * scale).astype(BF16)
        kh = _rope(k[:, hs], cos2, sin2, 1)
        vh = v[:, hs].astype(BF16)
        o_chunks = []
        for c in range(tt // RET_CHUNK):
            cs = slice(c * RET_CHUNK, (c + 1) * RET_CHUNK)
            qc = qh[cs]
            kc = kh[cs]
            vc = vh[cs]
            s_prev = s_scr[h]
            scores = lax.dot_general(qc, kc.astype(BF16), (((1,), (1,)), ((), ())),
                                     preferred_element_type=F32) * mask_ref[h]
            inner = jnp.dot(scores.astype(BF16), vc, preferred_element_type=F32)
            cross = jnp.dot(qc, s_prev.astype(BF16), preferred_element_type=F32) * cross_ref[:, hs]
            kw = (kc * wstate_ref[:, hs]).astype(BF16)
            kv = lax.dot_general(kw, vc, (((0,), (0,)), ((), ())), preferred_element_type=F32)
            s_scr[h] = cdecay_ref[:, hs] * s_prev + kv
            o_chunks.append(inner + cross)
        o_heads.append(_group_norm(jnp.concatenate(o_chunks, axis=0)))
    o = jnp.concatenate(o_heads, axis=1)
    out_b = o * gn_w_ref[...] * _silu(g)

    _post_mixer(x3, mod3, out_a, out_b, w_out_ref, n_post1_ref, n_pre2_ref, w_router_ref, b_router_ref,
                x1_ref, pk_ref, route_ref, cnt_scr)
    cnt_out_ref[...] = cnt_scr[...]

    @pl.when(t == pl.num_programs(1) - 1)
    def _():
        conv_out_ref[0] = xr[tt - SUBLANES:, :]
        lru_out_ref[0] = hseq[tt - SUBLANES:, :]
        ret_out_ref[0] = s_scr[...]


def _const_spec(shape):
    nd = len(shape)
    return pl.BlockSpec(shape, lambda *_: (0,) * nd)


def _prompt_mixer_call(x, mod3, cos2, sin2, wts, tables):
    bsz, seq, _ = x.shape
    tt = PROMPT_TILE
    nt = seq // tt
    n_tok = bsz * seq
    (n_pre1, n_post1, n_pre2, w_in, conv_w, conv_b, wg, b_r, b_i, lam, gn_w, w_out, w_router, b_router) = wts
    mask, wstate, cross, cdecay = tables
    tok_spec = pl.BlockSpec((tt, D_MODEL), lambda b, t: (b * nt + t, 0))
    in_specs = [
        pl.BlockSpec((1, tt, D_MODEL), lambda b, t: (b, t, 0)),
        pl.BlockSpec((1, 1, 6 * D_MODEL), lambda b, t: (b, 0, 0)),
        pl.BlockSpec((tt, LANES), lambda b, t: (t, 0)),
        pl.BlockSpec((tt, LANES), lambda b, t: (t, 0)),
    ] + [_const_spec(w.shape) for w in wts] + [_const_spec(tb.shape) for tb in tables]
    out_specs = [
        tok_spec,
        pl.BlockSpec((tt * SUBLANES, LANES), lambda b, t: (b * nt + t, 0)),
        pl.BlockSpec((SUBLANES, tt), lambda b, t: (0, b * nt + t)),
        pl.BlockSpec((1, SUBLANES, D_LRU), lambda b, t: (b, 0, 0)),
        pl.BlockSpec((1, SUBLANES, D_LRU), lambda b, t: (b, 0, 0)),
        pl.BlockSpec((1, N_HEADS, DK, DV), lambda b, t: (b, 0, 0, 0)),
        pl.BlockSpec((SUBLANES, LANES), lambda b, t: (0, 0)),
    ]
    out_shape = [
        jax.ShapeDtypeStruct((n_tok, D_MODEL), F32),
        jax.ShapeDtypeStruct((n_tok * SUBLANES, LANES), F32),
        jax.ShapeDtypeStruct((SUBLANES, n_tok), F32),
        jax.ShapeDtypeStruct((bsz, SUBLANES, D_LRU), F32),
        jax.ShapeDtypeStruct((bsz, SUBLANES, D_LRU), F32),
        jax.ShapeDtypeStruct((bsz, N_HEADS, DK, DV), F32),
        jax.ShapeDtypeStruct((SUBLANES, LANES), F32),
    ]
    return pl.pallas_call(
        _prompt_mixer_kernel,
        grid=(bsz, nt),
        in_specs=in_specs,
        out_specs=out_specs,
        out_shape=out_shape,
        scratch_shapes=[
            pltpu.VMEM((SUBLANES, D_LRU), F32),
            pltpu.VMEM((SUBLANES, D_LRU), F32),
            pltpu.VMEM((N_HEADS, DK, DV), F32),
            pltpu.VMEM((SUBLANES, LANES), F32),
        ],
        compiler_params=pltpu.CompilerParams(
            dimension_semantics=("arbitrary", "arbitrary"), vmem_limit_bytes=VMEM_LIMIT),
        name="prompt_mixer",
    )(x, mod3, cos2, sin2, *wts, *tables)


def _sample_mixer_kernel(x_ref, mod_ref, cos_ref, sin_ref, buf_ref, h0_ref, s0_ref, cnt_in_ref,
                         n_pre1_ref, n_post1_ref, n_pre2_ref,
                         w_in_ref, conv_w_ref, conv_b_ref, wg_ref, b_r_ref, b_i_ref, lam_ref,
                         gn_w_ref, w_out_ref, w_router_ref, b_router_ref,
                         smask_ref, wstate_ref, cross_ref, cdecay_ref,
                         x1_ref, pk_ref, route_ref, xr_out_ref, h_out_ref, ret_out_ref, cnt_out_ref,
                         cnt_scr):
    bb, ts, _ = x_ref.shape
    m = bb * ts

    @pl.when(pl.program_id(0) == 0)
    def _():
        cnt_scr[...] = cnt_in_ref[...]

    x3 = x_ref[...]
    mod3 = mod_ref[...]
    z = _in_proj(x3, mod3, n_pre1_ref, w_in_ref)
    xr, yg, q, k, v, g = (z[:, c * D_LRU:(c + 1) * D_LRU] for c in range(D_IN_PROJ // D_LRU))
    xr_out_ref[...] = xr

    xr3 = xr.reshape(bb, ts, D_LRU)
    buf3 = buf_ref[...]
    tpos = lax.broadcasted_iota(jnp.int32, (bb, ts, D_LRU), 1)
    xc3 = jnp.broadcast_to(conv_b_ref[...].reshape(1, 1, D_LRU), (bb, ts, D_LRU))
    for j in range(CONV_W):
        back = CONV_W - 1 - j
        w_j = conv_w_ref[j:j + 1, :].reshape(1, 1, D_LRU)
        if back == 0:
            term = xr3
        else:
            cur = pltpu.roll(xr3, back, axis=1)
            up = CONV_W - 1 - back
            old = buf3 if up == 0 else pltpu.roll(buf3, ts - up, axis=1)
            term = jnp.where(tpos >= back, cur, old)
        xc3 = xc3 + term * w_j
    xc = xc3.reshape(m, D_LRU)

    a, b = _lru_coeffs(xc, wg_ref, b_r_ref, b_i_ref, lam_ref)
    a3 = a.reshape(bb, ts, D_LRU)
    b3 = b.reshape(bb, ts, D_LRU) + a3 * h0_ref[...]
    _, h3 = _group_scan(a3, b3)
    hseq = h3.reshape(m, D_LRU)
    h_out_ref[...] = hseq
    out_a = hseq * jax.nn.gelu(yg, approximate=True)

    cos2 = cos_ref[...].reshape(1, ts, LANES)
    sin2 = sin_ref[...].reshape(1, ts, LANES)
    scale = DK ** -0.5
    o_heads = []
    for h in range(N_HEADS):
        hs = slice(h * DK, (h + 1) * DK)
        q3 = (_rope(q[:, hs].reshape(bb, ts, DK), cos2, sin2, 2) * scale).astype(BF16)
        k3 = _rope(k[:, hs].reshape(bb, ts, DK), cos2, sin2, 2)
        v3 = v[:, hs].reshape(bb, ts, DV).astype(BF16)
        q2 = q3.reshape(m, DK)
        k2 = k3.reshape(m, DK).astype(BF16)
        v2 = v3.reshape(m, DV)
        scores = lax.dot_general(q2, k2, (((1,), (1,)), ((), ())),
                                 preferred_element_type=F32) * smask_ref[h]
        inner = jnp.dot(scores.astype(BF16), v2, preferred_element_type=F32)
        s0h = s0_ref[:, h]
        cross = jnp.einsum('bid,bde->bie', q3, s0h.astype(BF16), preferred_element_type=F32)
        cross = cross * cross_ref[:, hs].reshape(1, ts, DV)
        kw3 = (k3 * wstate_ref[:, hs].reshape(1, ts, DK)).astype(BF16)
        kv = jnp.einsum('bjd,bje->bde', kw3, v3, preferred_element_type=F32)
        ret_out_ref[:, h] = cdecay_ref[:, hs].reshape(1, 1, DV) * s0h + kv
        o_heads.append(_group_norm(inner + cross.reshape(m, DV)))
    o = jnp.concatenate(o_heads, axis=1)
    out_b = o * gn_w_ref[...] * _silu(g)

    _post_mixer(x3, mod3, out_a, out_b, w_out_ref, n_post1_ref, n_pre2_ref, w_router_ref, b_router_ref,
                x1_ref, pk_ref, route_ref, cnt_scr)
    cnt_out_ref[...] = cnt_scr[...]


def _sample_mixer_call(x, mod3, cos2, sin2, buf8, h0p, s0, cnt_in, wts, tables):
    bsz, ts, _ = x.shape
    bb = SAMPLE_SEQS
    m = bb * ts
    n_tok = bsz * ts
    seq_spec = lambda w: pl.BlockSpec((bb, ts, w), lambda i: (i, 0, 0))
    tok_spec = lambda w: pl.BlockSpec((m, w), lambda i: (i, 0))
    in_specs = [
        seq_spec(D_MODEL),
        pl.BlockSpec((bb, 1, 6 * D_MODEL), lambda i: (i, 0, 0)),
        _const_spec(cos2.shape),
        _const_spec(sin2.shape),
        seq_spec(D_LRU),
        seq_spec(D_LRU),
        pl.BlockSpec((bb, N_HEADS, DK, DV), lambda i: (i, 0, 0, 0)),
        _const_spec(cnt_in.shape),
    ] + [_const_spec(w.shape) for w in wts] + [_const_spec(tb.shape) for tb in tables]
    out_specs = [
        tok_spec(D_MODEL),
        pl.BlockSpec((m * SUBLANES, LANES), lambda i: (i, 0)),
        pl.BlockSpec((SUBLANES, m), lambda i: (0, i)),
        tok_spec(D_LRU),
        tok_spec(D_LRU),
        pl.BlockSpec((bb, N_HEADS, DK, DV), lambda i: (i, 0, 0, 0)),
        _const_spec(cnt_in.shape),
    ]
    out_shape = [
        jax.ShapeDtypeStruct((n_tok, D_MODEL), F32),
        jax.ShapeDtypeStruct((n_tok * SUBLANES, LANES), F32),
        jax.ShapeDtypeStruct((SUBLANES, n_tok), F32),
        jax.ShapeDtypeStruct((n_tok, D_LRU), F32),
        jax.ShapeDtypeStruct((n_tok, D_LRU), F32),
        jax.ShapeDtypeStruct((bsz, N_HEADS, DK, DV), F32),
        jax.ShapeDtypeStruct(cnt_in.shape, F32),
    ]
    return pl.pallas_call(
        _sample_mixer_kernel,
        grid=(bsz // bb,),
        in_specs=in_specs,
        out_specs=out_specs,
        out_shape=out_shape,
        scratch_shapes=[pltpu.VMEM((SUBLANES, LANES), F32)],
        compiler_params=pltpu.CompilerParams(
            dimension_semantics=("arbitrary",), vmem_limit_bytes=VMEM_LIMIT),
        name="sample_mixer",
    )(x, mod3, cos2, sin2, buf8, h0p, s0, cnt_in, *wts, *tables)


def _slab(ref, r):
    return ref.at[pl.ds(pl.multiple_of(r * SUBLANES, SUBLANES), SUBLANES)]


def _dispatch_kernel(cls_ref, rank_ref, cnt_ref, padcnt_ref, start_ref, srcp_ref, srcs_ref, out_ref, ring, sem,
                     *, p_tiles):
    i = pl.program_id(0)
    n = pl.num_programs(0)
    tile_rows = ring.shape[1]
    td = tile_rows // SUBLANES
    moe_rows = MOE_TILE * SUBLANES

    def issue(slot):
        def body(g, carry):
            first = i * td + g * DMA_GROUP
            slots = [start_ref[cls_ref[first + j]] + rank_ref[first + j] for j in range(DMA_GROUP)]
            first = g * DMA_GROUP
            for j in range(DMA_GROUP):
                pltpu.make_async_copy(
                    _slab(ring.at[slot], first + j), _slab(out_ref, slots[j]), sem.at[slot]
                ).start(priority=j % 2)
            return carry
        lax.fori_loop(0, td // DMA_GROUP, body, 0)

    def wait_tile(slot):
        pltpu.make_async_copy(ring.at[slot], out_ref.at[pl.ds(0, tile_rows)], sem.at[slot]).wait()

    def zero_fill(slot):
        ring[slot] = jnp.zeros((tile_rows, LANES), F32)
        zero_src = ring.at[slot]

        def per_class(c, carry):
            lo = start_ref[c] + cnt_ref[c]
            hi = start_ref[c] + padcnt_ref[c]

            def fill(r, carry2):
                pltpu.make_async_copy(_slab(zero_src, 0), _slab(out_ref, r), sem.at[slot]).start()
                return carry2
            lax.fori_loop(lo, hi, fill, 0)

            def done(r, carry2):
                pltpu.make_async_copy(_slab(zero_src, 0), _slab(out_ref, 0), sem.at[slot]).wait()
                return carry2
            lax.fori_loop(lo, hi, done, 0)
            return carry
        lax.fori_loop(0, N_CLASSES, per_class, 0)

        used_tiles = (start_ref[N_CLASSES - 1] + padcnt_ref[N_CLASSES - 1]) // MOE_TILE
        all_tiles = out_ref.shape[0] // moe_rows

        def tile_copy(t):
            return pltpu.make_async_copy(
                zero_src.at[pl.ds(0, moe_rows)],
                out_ref.at[pl.ds(pl.multiple_of(t * moe_rows, moe_rows), moe_rows)], sem.at[slot])

        def fill_tile(t, carry):
            tile_copy(t).start()
            return carry
        lax.fori_loop(used_tiles, all_tiles, fill_tile, 0)

        def done_tile(t, carry):
            tile_copy(t).wait()
            return carry
        lax.fori_loop(used_tiles, all_tiles, done_tile, 0)

    for s in range(RING):
        @pl.when(i % RING == s)
        def _():
            @pl.when(i < p_tiles)
            def _():
                ring[s] = srcp_ref[...]

            @pl.when(i >= p_tiles)
            def _():
                ring[s] = srcs_ref[...]

            issue(s)

            @pl.when(i >= RING - 1)
            def _():
                wait_tile((s + 1) % RING)

            @pl.when(i == n - 1)
            def _():
                for back in range(RING - 2, -1, -1):
                    wait_tile((s - back) % RING)
                zero_fill(s)


def _dispatch_call(cls, rank, cnt, padcnt, start, pk_p, pk_s, n_rows):
    td = DISPATCH_TILE
    tile_rows = td * SUBLANES
    p_tiles = pk_p.shape[0] // tile_rows
    s_tiles = pk_s.shape[0] // tile_rows
    assert p_tiles + s_tiles >= RING and td >= MOE_TILE
    return pl.pallas_call(
        functools.partial(_dispatch_kernel, p_tiles=p_tiles),
        grid_spec=pltpu.PrefetchScalarGridSpec(
            num_scalar_prefetch=5,
            grid=(p_tiles + s_tiles,),
            in_specs=[
                pl.BlockSpec((tile_rows, LANES), lambda i, *_: (jnp.minimum(i, p_tiles - 1), 0)),
                pl.BlockSpec((tile_rows, LANES), lambda i, *_: (jnp.maximum(i - p_tiles, 0), 0)),
            ],
            out_specs=pl.BlockSpec(memory_space=pl.ANY),
            scratch_shapes=[pltpu.VMEM((RING, tile_rows, LANES), F32),
                            pltpu.SemaphoreType.DMA((RING,))],
        ),
        out_shape=jax.ShapeDtypeStruct((n_rows * SUBLANES, LANES), F32),
        compiler_params=pltpu.CompilerParams(
            dimension_semantics=("arbitrary",), has_side_effects=True, vmem_limit_bytes=VMEM_LIMIT),
        name="moe_dispatch",
    )(cls, rank, cnt, padcnt, start, pk_p, pk_s)


def _moe_kernel(ea_ref, eb_ref, valid_ref, xs_ref, w_router_ref, b_router_ref,
                wga_ref, wua_ref, wda_ref, wgb_ref, wub_ref, wdb_ref, f_ref):
    t = pl.program_id(0)

    @pl.when(valid_ref[t] == 1)
    def _():
        tm = xs_ref.shape[0] // SUBLANES
        x = jnp.concatenate([xs_ref[pl.ds(j, tm, stride=SUBLANES), :] for j in range(ROW_GROUPS)],
                            axis=1).astype(BF16)
        e_a = ea_ref[t]
        e_b = eb_ref[t]
        e_lo = (e_a // PER_GROUP) * PER_GROUP
        logits = jnp.dot(x, w_router_ref[...], preferred_element_type=F32) + b_router_ref[...]
        lane = lax.broadcasted_iota(jnp.int32, (tm, LANES), 1)
        pick = lambda p, idx: jnp.sum(jnp.where(lane == idx, p, 0.0), axis=-1, keepdims=True)
        p_group = _masked_softmax(logits, (lane >= N_EXPERTS) & (lane < N_EXPERTS + N_GROUPS))
        p_g = pick(p_group, N_EXPERTS + e_a // PER_GROUP)
        p_e = _masked_softmax(logits, (lane >= e_lo) & (lane < e_lo + PER_GROUP))
        w_a = pick(p_e, e_a)
        w_b = pick(p_e, e_b)
        wsum = w_a + w_b

        def expert(wg_ref, wu_ref, gate):
            hg = jnp.dot(x, wg_ref[0].astype(BF16), preferred_element_type=F32)
            hu = jnp.dot(x, wu_ref[0].astype(BF16), preferred_element_type=F32)
            return (_silu(hg) * hu * gate).astype(BF16)

        ha = expert(wga_ref, wua_ref, p_g * (w_a / wsum))
        hb = expert(wgb_ref, wub_ref, p_g * (w_b / wsum))
        for c in range(D_MODEL // GATE_HALF):
            cols = slice(c * GATE_HALF, (c + 1) * GATE_HALF)
            f = (jnp.dot(ha, wda_ref[0, :, cols].astype(BF16), preferred_element_type=F32)
                 + jnp.dot(hb, wdb_ref[0, :, cols].astype(BF16), preferred_element_type=F32))
            for jj in range(GATE_HALF // LANES):
                j = c * (GATE_HALF // LANES) + jj
                f_ref[pl.ds(j, tm, stride=SUBLANES), :] = f[:, jj * LANES:(jj + 1) * LANES]

    @pl.when(valid_ref[t] == 0)
    def _():
        f_ref[...] = jnp.zeros_like(f_ref)


def _moe_call(tile_ea, tile_eb, tile_valid, rows, w_router, b_router, wg, wu, wd):
    n_rows = rows.shape[0] // SUBLANES
    tm = MOE_TILE
    const = lambda a: pl.BlockSpec(a.shape, lambda t, ea, eb, v: (0,) * a.ndim)
    up = lambda sel: pl.BlockSpec((1, D_MODEL, D_EXPERT), lambda t, ea, eb, v: (sel(ea, eb)[t], 0, 0))
    down = lambda sel: pl.BlockSpec((1, D_EXPERT, D_MODEL), lambda t, ea, eb, v: (sel(ea, eb)[t], 0, 0))
    first = lambda ea, eb: ea
    second = lambda ea, eb: eb
    return pl.pallas_call(
        _moe_kernel,
        grid_spec=pltpu.PrefetchScalarGridSpec(
            num_scalar_prefetch=3,
            grid=(n_rows // tm,),
            in_specs=[
                pl.BlockSpec((tm * SUBLANES, LANES), lambda t, ea, eb, v: (t, 0)),
                const(w_router), const(b_router),
                up(first), up(first), down(first), up(second), up(second), down(second),
            ],
            out_specs=pl.BlockSpec((tm * SUBLANES, LANES), lambda t, ea, eb, v: (t, 0)),
        ),
        out_shape=jax.ShapeDtypeStruct((n_rows * SUBLANES, LANES), F32),
        compiler_params=pltpu.CompilerParams(
            dimension_semantics=("arbitrary",), vmem_limit_bytes=VMEM_LIMIT),
        name="moe_experts",
    )(tile_ea, tile_eb, tile_valid, rows, w_router, b_router, wg, wu, wd, wg, wu, wd)


def _combine_kernel(cls_ref, rank_ref, start_ref, x1_ref, mod_ref, n_post2_ref, f_hbm, o_ref, fbuf, sem):
    i = pl.program_id(0)
    n = pl.num_programs(0)
    tf = x1_ref.shape[0]

    def issue(tile, slot):
        def body(g, carry):
            first = tile * tf + g * DMA_GROUP
            slots = [start_ref[cls_ref[first + j]] + rank_ref[first + j] for j in range(DMA_GROUP)]
            first = g * DMA_GROUP
            for j in range(DMA_GROUP):
                pltpu.make_async_copy(
                    _slab(f_hbm, slots[j]), _slab(fbuf.at[slot], first + j), sem.at[slot]
                ).start(priority=j % 2)
            return carry
        lax.fori_loop(0, tf // DMA_GROUP, body, 0)

    def finish(slot):
        pltpu.make_async_copy(f_hbm.at[pl.ds(0, tf * SUBLANES)], fbuf.at[slot], sem.at[slot]).wait()
        f = jnp.concatenate([fbuf[slot, pl.ds(j, tf, stride=SUBLANES), :] for j in range(D_MODEL // LANES)],
                            axis=1)
        mod3 = mod_ref[...]
        bb = mod3.shape[0]
        f3 = f.reshape(bb, tf // bb, D_MODEL)
        g2 = mod3[:, :, 5 * D_MODEL:6 * D_MODEL]
        x13 = x1_ref[...].reshape(f3.shape)
        out = x13 + (f3 * _rms_scale(f3)) * (g2 * n_post2_ref[...].reshape(1, 1, D_MODEL))
        o_ref[...] = out.reshape(o_ref.shape)

    @pl.when(i == 0)
    def _():
        issue(0, 0)

    for parity in range(2):
        @pl.when(i % 2 == parity)
        def _():
            @pl.when(i + 1 < n)
            def _():
                issue(i + 1, 1 - parity)
            finish(parity)


def _combine_call(cls, rank, start, x1, mod3, n_post2, f_sorted, seqs_per_tile):
    n_tok = x1.shape[0]
    tf = FINAL_TILE
    tok_per_seq = n_tok // mod3.shape[0]
    if seqs_per_tile > 1:
        mod_map = lambda i, *_: (i, 0, 0)
    else:
        mod_map = lambda i, *_: ((i * tf) // tok_per_seq, 0, 0)
    return pl.pallas_call(
        _combine_kernel,
        grid_spec=pltpu.PrefetchScalarGridSpec(
            num_scalar_prefetch=3,
            grid=(n_tok // tf,),
            in_specs=[
                pl.BlockSpec((tf, D_MODEL), lambda i, *_: (i, 0)),
                pl.BlockSpec((seqs_per_tile, 1, 6 * D_MODEL), mod_map),
                pl.BlockSpec((1, D_MODEL), lambda i, *_: (0, 0)),
                pl.BlockSpec(memory_space=pl.ANY),
            ],
            out_specs=pl.BlockSpec((tf, D_MODEL), lambda i, *_: (i, 0)),
            scratch_shapes=[pltpu.VMEM((2, tf * SUBLANES, LANES), F32), pltpu.SemaphoreType.DMA((2,))],
        ),
        out_shape=jax.ShapeDtypeStruct((n_tok, D_MODEL), F32),
        compiler_params=pltpu.CompilerParams(
            dimension_semantics=("arbitrary",), vmem_limit_bytes=VMEM_LIMIT),
        name="moe_combine",
    )(cls, rank, start, x1, mod3, n_post2, f_sorted)


def _block_diag_gate(w_r, w_i):
    per_half = GATE_HALF // LRU_BLOCK
    n_half = D_LRU // GATE_HALF
    eye = np.eye(per_half, dtype=np.float32)

    def block_diag(w):
        w4 = w.reshape(n_half, per_half, LRU_BLOCK, LRU_BLOCK)
        return jnp.einsum('hnij,nm->hnimj', w4, eye).reshape(n_half, GATE_HALF, GATE_HALF)

    return jnp.concatenate([block_diag(w_r), block_diag(w_i)], axis=-1).astype(BF16)


def _rope_tables(pos):
    half = DK // 2
    inv = np.float64(ROPE_BASE) ** (-np.arange(half, dtype=np.float64) / half)
    ang = np.asarray(pos, np.float64)[:, None] * inv[None, :]
    cos = np.cos(ang)
    sin = np.sin(ang)
    return (np.concatenate([cos, cos], axis=-1).astype(np.float32),
            np.concatenate([-sin, sin], axis=-1).astype(np.float32))


def _decay_tables(c):
    log_g = np.log1p(-np.exp2(-5.0 - np.arange(N_HEADS, dtype=np.float64)))
    idx = np.arange(c, dtype=np.float64)
    diff = idx[:, None] - idx[None, :]
    mask = np.where(diff[None] >= 0, np.exp(np.maximum(diff, 0.0)[None] * log_g[:, None, None]), 0.0)
    w_state = np.exp((c - 1.0 - idx)[None, :] * log_g[:, None])
    cross_decay = np.exp((idx + 1.0)[:, None] * log_g[None, :])
    chunk_decay = np.exp(c * log_g)
    wstate_full = np.repeat(w_state.T, DK, axis=1)
    cross_full = np.repeat(cross_decay, DV, axis=1)
    cdecay_full = np.repeat(chunk_decay, DV)[None, :]
    return tuple(t.astype(np.float32) for t in (mask, wstate_full, cross_full, cdecay_full))


def kernel(x_prompt, x_sample, state_conv, state_lru, state_ret, c_prompt, c_sample, w_mod, b_mod, norm_pre_mix, norm_post_mix, norm_pre_ffn, norm_post_ffn, w_in, conv_w, conv_b, w_rgate, b_rgate, w_igate, b_igate, lru_lambda, ret_gn_w, w_out, w_router_group, b_router_group, w_router_expert, b_router_expert, w_exp_gate, w_exp_up, w_exp_down):
    bp, tp, _ = x_prompt.shape
    bs, ts, _ = x_sample.shape
    l = 0

    mod = _mod_call(jnp.concatenate([c_prompt, c_sample], axis=0), w_mod[l], b_mod[l][None, :])
    mod_p = mod[:bp][:, None, :]
    mod_s = mod[bp:][:, None, :]

    unused = LANES - N_EXPERTS - N_GROUPS
    w_router = jnp.concatenate(
        [w_router_expert[l], w_router_group[l], jnp.zeros((D_MODEL, unused), F32)], axis=1).astype(BF16)
    b_router = jnp.concatenate(
        [b_router_expert[l], b_router_group[l], jnp.zeros((unused,), F32)])[None, :]

    row = lambda vec: vec.reshape(1, -1)
    wts = (row(norm_pre_mix[l]), row(norm_post_mix[l]), row(norm_pre_ffn[l]),
           w_in[l].astype(BF16), conv_w[l], row(conv_b[l]),
           _block_diag_gate(w_rgate[l], w_igate[l]),
           row(b_rgate[l]), row(b_igate[l]), row(lru_lambda[l]), row(ret_gn_w[l]),
           w_out[l].astype(BF16), w_router, b_router)

    cos_p, sin_p = _rope_tables(np.arange(tp))
    x1_p, pk_p, route_p, conv_p8, lru_p8, ret_p, cnt_p = _prompt_mixer_call(
        x_prompt, mod_p, cos_p, sin_p, wts, _decay_tables(math.gcd(tp, RET_CHUNK)))

    cos_s, sin_s = _rope_tables(PAST_LEN + np.arange(ts))
    mask8, wstate_s, cross_s, cdecay_s = _decay_tables(math.gcd(ts, RET_CHUNK))
    eye = np.eye(SAMPLE_SEQS, dtype=np.float32)
    smask = np.stack([np.kron(eye, mask8[h]) for h in range(N_HEADS)])
    buf8 = jnp.pad(state_conv[l], ((0, 0), (0, ts - (CONV_W - 1)), (0, 0)))
    h0p = jnp.pad(state_lru[l][:, None, :], ((0, 0), (0, ts - 1), (0, 0)))
    x1_s, pk_s, route_s, xr_s, h_s, ret_s, cnt_all = _sample_mixer_call(
        x_sample, mod_s, cos_s, sin_s, buf8, h0p, state_ret[l], cnt_p, wts,
        (smask, wstate_s, cross_s, cdecay_s))

    n_p = bp * tp
    n_tok = n_p + bs * ts
    tm = MOE_TILE
    max_tiles = n_tok // tm + N_CLASSES
    cls_p, rank_p = route_p[0].astype(jnp.int32), route_p[1].astype(jnp.int32)
    cls_s, rank_s = route_s[0].astype(jnp.int32), route_s[1].astype(jnp.int32)
    cls = jnp.concatenate([cls_p, cls_s])
    rank = jnp.concatenate([rank_p, rank_s])
    cnt = cnt_all[0, :N_CLASSES].astype(jnp.int32)
    ntile = (cnt + (tm - 1)) // tm
    padcnt = ntile * tm
    before = np.tril(np.ones((N_CLASSES, N_CLASSES), np.int32), -1)
    upto = np.tril(np.ones((N_CLASSES, N_CLASSES), np.int32))
    start = jnp.sum(before * padcnt[None, :], axis=1)
    tile_end = jnp.sum(upto * ntile[None, :], axis=1)
    n_used = jnp.sum(ntile)
    tile_ids = np.arange(max_tiles, dtype=np.int32)
    tile_valid = (tile_ids < n_used).astype(jnp.int32)
    last_used = jnp.minimum(tile_ids, n_used - 1)
    tile_cls = jnp.sum((last_used[:, None] >= tile_end[None, :]).astype(jnp.int32), axis=1)
    pair = tile_cls % N_PAIRS
    first_expert = jnp.where(pair == 5, 2, jnp.where((pair == 2) | (pair == 3), 1, 0))
    second_expert = jnp.where(pair == 0, 1, jnp.where(pair <= 2, 2, 3))
    tile_ea = (tile_cls // N_PAIRS) * PER_GROUP + first_expert
    tile_eb = (tile_cls // N_PAIRS) * PER_GROUP + second_expert

    n_post2 = row(norm_post_ffn[l])
    wg = w_exp_gate[l]
    wu = w_exp_up[l]
    wd = w_exp_down[l]
    rows = _dispatch_call(cls, rank, cnt, padcnt, start, pk_p, pk_s, max_tiles * tm)
    f_sorted = _moe_call(tile_ea, tile_eb, tile_valid, rows, w_router, b_router, wg, wu, wd)
    y_p = _combine_call(cls_p, rank_p, start, x1_p, mod_p, n_post2, f_sorted, 1)
    y_s = _combine_call(cls_s, rank_s, start, x1_s, mod_s, n_post2, f_sorted, FINAL_TILE // ts)

    conv_p = conv_p8[:, SUBLANES - (CONV_W - 1):, :]
    lru_p = lru_p8[:, SUBLANES - 1, :]
    xr_s3 = xr_s.reshape(bs, ts, D_LRU)
    conv_s = xr_s3[:, ts - (CONV_W - 1):, :]
    lru_s = h_s.reshape(bs, ts, D_LRU)[:, ts - 1, :]
    return (y_p.reshape(bp, tp, D_MODEL), y_s.reshape(bs, ts, D_MODEL),
            conv_p[None], lru_p[None], ret_p[None],
            conv_s[None], lru_s[None], ret_s[None])
```

```python
import functools
import math

import jax
import jax.numpy as jnp
import numpy as np
from jax import lax
from jax.experimental import pallas as pl
from jax.experimental.pallas import tpu as pltpu

F32 = jnp.float32
BF16 = jnp.bfloat16

D_MODEL = 1024
D_LRU = 512
D_RET = 512
N_LRU_BLOCKS = 8
LRU_BLOCK = D_LRU // N_LRU_BLOCKS
CONV_W = 4
LRU_C = 8.0
N_HEADS = 4
DK = 128
DV = 128
RET_CHUNK = 128
ROPE_BASE = 10000.0
D_IN_PROJ = 3072
N_GROUPS = 4
PER_GROUP = 4
N_EXPERTS = 16
D_EXPERT = 256
EXPM1_DIRECT_BELOW = -0.5
NORM_EPS = 1e-6
GN_EPS = 1e-5
PAST_LEN = 16384

SUBLANES = 8
LANES = 128
GATE_HALF = 256
VMEM_LIMIT = 56 * 1024 * 1024

PROMPT_TILE = 512
SAMPLE_SEQS = 16
MOE_TILE = 256
FINAL_TILE = 512
DISPATCH_TILE = 512
RING = 3
DMA_GROUP = 8

N_PAIRS = 6
N_CLASSES = N_GROUPS * N_PAIRS
ROW_GROUPS = D_MODEL // LANES


def _silu(x):
    return x * jax.nn.sigmoid(x)


def _rms_scale(x):
    return lax.rsqrt(jnp.mean(x * x, axis=-1, keepdims=True) + NORM_EPS)


def _masked_softmax(logits, mask):
    top = jnp.max(jnp.where(mask, logits, -jnp.inf), axis=-1, keepdims=True)
    e = jnp.where(mask, jnp.exp(logits - top), 0.0)
    return e / jnp.sum(e, axis=-1, keepdims=True)


def _mod_kernel(c_ref, w_ref, b_ref, o_ref):
    s = _silu(c_ref[...]).astype(BF16)
    o_ref[...] = jnp.dot(s, w_ref[...].astype(BF16), preferred_element_type=F32) + b_ref[...]


def _mod_call(c_all, w_mod, b_mod):
    rows = c_all.shape[0]
    ncol = w_mod.shape[1]
    blk = D_MODEL
    return pl.pallas_call(
        _mod_kernel,
        grid=(ncol // blk,),
        in_specs=[
            pl.BlockSpec((rows, D_MODEL), lambda j: (0, 0)),
            pl.BlockSpec((D_MODEL, blk), lambda j: (0, j)),
            pl.BlockSpec((1, blk), lambda j: (0, j)),
        ],
        out_specs=pl.BlockSpec((rows, blk), lambda j: (0, j)),
        out_shape=jax.ShapeDtypeStruct((rows, ncol), F32),
        compiler_params=pltpu.CompilerParams(
            dimension_semantics=("arbitrary",), vmem_limit_bytes=VMEM_LIMIT),
        name="mod",
    )(c_all, w_mod, b_mod)


def _in_proj(x3, mod3, n_pre1_ref, w_in_ref):
    bb, tt, _ = x3.shape
    sh1 = mod3[:, :, 0:D_MODEL]
    sc1 = mod3[:, :, D_MODEL:2 * D_MODEL]
    coef = n_pre1_ref[...].reshape(1, 1, D_MODEL) * (1.0 + sc1)
    u = (x3 * _rms_scale(x3)) * coef + sh1
    u2d = u.reshape(bb * tt, D_MODEL).astype(BF16)
    return jnp.dot(u2d, w_in_ref[...], preferred_element_type=F32)


def _lru_coeffs(xc, wg_ref, b_r_ref, b_i_ref, lam_ref):
    xcb = xc.astype(BF16)
    g0 = jnp.dot(xcb[:, :GATE_HALF], wg_ref[0], preferred_element_type=F32)
    g1 = jnp.dot(xcb[:, GATE_HALF:], wg_ref[1], preferred_element_type=F32)
    r = jax.nn.sigmoid(jnp.concatenate([g0[:, :GATE_HALF], g1[:, :GATE_HALF]], axis=1) + b_r_ref[...])
    i = jax.nn.sigmoid(jnp.concatenate([g0[:, GATE_HALF:], g1[:, GATE_HALF:]], axis=1) + b_i_ref[...])
    lam = lam_ref[...]
    sp = jnp.maximum(-lam, 0.0) + jnp.log1p(jnp.exp(-jnp.abs(lam)))
    log_a = -LRU_C * r * sp
    a = jnp.exp(log_a)
    y = 2.0 * log_a
    a2 = a * a
    d = a2 - 1.0
    small = d * y / jnp.log(a2)
    em1 = jnp.where(y < EXPM1_DIRECT_BELOW, d, jnp.where(d == 0.0, y, small))
    gain = jnp.sqrt(-em1)
    return a, gain * (i * xc)


def _rope(xh, cos2, sin2, lane_axis):
    return xh * cos2 + pltpu.roll(xh, DK // 2, axis=lane_axis) * sin2


def _group_norm(o):
    mu = jnp.mean(o, axis=-1, keepdims=True)
    d = o - mu
    var = jnp.mean(d * d, axis=-1, keepdims=True)
    return d * lax.rsqrt(var + GN_EPS)


def _post_mixer(x3, mod3, out_a, out_b, w_out_ref, n_post1_ref, n_pre2_ref, w_router_ref, b_router_ref,
                earlier_ref, x1_ref, pk_ref, route_ref, cnt_scr):
    bb, tt, _ = x3.shape
    m = bb * tt
    y = (jnp.dot(out_a.astype(BF16), w_out_ref[0:D_LRU, :], preferred_element_type=F32)
         + jnp.dot(out_b.astype(BF16), w_out_ref[D_LRU:, :], preferred_element_type=F32))
    g1 = mod3[:, :, 2 * D_MODEL:3 * D_MODEL]
    sh2 = mod3[:, :, 3 * D_MODEL:4 * D_MODEL]
    sc2 = mod3[:, :, 4 * D_MODEL:5 * D_MODEL]
    y3 = y.reshape(bb, tt, D_MODEL)
    x1 = x3 + (y3 * _rms_scale(y3)) * (g1 * n_post1_ref[...].reshape(1, 1, D_MODEL))
    u2 = (x1 * _rms_scale(x1)) * (n_pre2_ref[...].reshape(1, 1, D_MODEL) * (1.0 + sc2)) + sh2
    x1_ref[...] = x1.reshape(m, D_MODEL)
    u2f = u2.reshape(m, D_MODEL)
    for j in range(ROW_GROUPS):
        pk_ref[pl.ds(j, m, stride=SUBLANES), :] = u2f[:, j * LANES:(j + 1) * LANES]
    u2b = u2f.astype(BF16)

    logits = jnp.dot(u2b, w_router_ref[...], preferred_element_type=F32) + b_router_ref[...]
    lane = lax.broadcasted_iota(jnp.int32, (m, LANES), 1)
    lane_f = lane.astype(F32)
    is_g = (lane >= N_EXPERTS) & (lane < N_EXPERTS + N_GROUPS)
    p_group = _masked_softmax(logits, is_g)
    p_g = jnp.max(p_group, axis=-1, keepdims=True)
    g_lane = jnp.min(jnp.where(is_g & (p_group == p_g), lane_f, float(LANES)), axis=-1, keepdims=True)
    e_lo = (g_lane - N_EXPERTS) * PER_GROUP
    in_g = (lane_f >= e_lo) & (lane_f < e_lo + PER_GROUP)
    p_e = _masked_softmax(logits, in_g)
    pm = jnp.where(in_g, p_e, -1.0)
    w1 = jnp.max(pm, axis=-1, keepdims=True)
    i1 = jnp.min(jnp.where(pm == w1, lane_f, float(LANES)), axis=-1, keepdims=True)
    pm2 = jnp.where(lane_f == i1, -1.0, pm)
    w2 = jnp.max(pm2, axis=-1, keepdims=True)
    i2 = jnp.min(jnp.where(pm2 == w2, lane_f, float(LANES)), axis=-1, keepdims=True)
    a = jnp.minimum(i1, i2) - e_lo
    b = jnp.maximum(i1, i2) - e_lo
    pair = jnp.where(a == 0.0, jnp.where(b == 3.0, 4.0, b - 1.0), jnp.where(a == 1.0, b, 5.0))
    cls = (g_lane - N_EXPERTS) * N_PAIRS + pair
    onehot = lane_f == cls
    prefix = jnp.dot(earlier_ref[...], jnp.where(onehot, 1.0, 0.0).astype(BF16), preferred_element_type=F32)
    run = cnt_scr[0:1, :]
    rank = jnp.sum(jnp.where(onehot, prefix + run, 0.0), axis=-1, keepdims=True)
    cnt_scr[...] = jnp.broadcast_to(
        run + jnp.sum(jnp.where(onehot, 1.0, 0.0), axis=0, keepdims=True), cnt_scr.shape)
    route = jnp.where(lane == 0, cls, jnp.where(lane == 1, rank, 0.0))
    route_ref[...] = jnp.transpose(route)[0:SUBLANES, :]


def _group_scan(a3, b3):
    tpos = lax.broadcasted_iota(jnp.int32, a3.shape, 1)
    s = 1
    while s < a3.shape[1]:
        keep = tpos >= s
        a_sh = jnp.where(keep, pltpu.roll(a3, s, axis=1), 1.0)
        b_sh = jnp.where(keep, pltpu.roll(b3, s, axis=1), 0.0)
        b3 = a3 * b_sh + b3
        a3 = a3 * a_sh
        s *= 2
    return a3, b3


def _scan_rows(a, b, h0):
    n, c = a.shape
    groups = n // SUBLANES
    a3, b3 = _group_scan(a.reshape(groups, SUBLANES, c), b.reshape(groups, SUBLANES, c))
    carry = h0
    out = []
    for g in range(groups):
        hg = b3[g] + a3[g] * carry
        out.append(hg)
        carry = hg[SUBLANES - 1:SUBLANES, :]
    return jnp.concatenate(out, axis=0)


def _prompt_mixer_kernel(x_ref, mod_ref, cos_ref, sin_ref,
                         n_pre1_ref, n_post1_ref, n_pre2_ref,
                         w_in_ref, conv_w_ref, conv_b_ref, wg_ref, b_r_ref, b_i_ref, lam_ref,
                         gn_w_ref, w_out_ref, w_router_ref, b_router_ref,
                         mask_ref, wstate_ref, cross_ref, cdecay_ref, earlier_ref,
                         x1_ref, pk_ref, route_ref, conv_out_ref, lru_out_ref, ret_out_ref, cnt_out_ref,
                         conv_scr, h_scr, s_scr, cnt_scr):
    t = pl.program_id(1)
    tt = x_ref.shape[1]

    @pl.when((pl.program_id(0) == 0) & (t == 0))
    def _():
        cnt_scr[...] = jnp.zeros_like(cnt_scr)

    @pl.when(t == 0)
    def _():
        conv_scr[...] = jnp.zeros_like(conv_scr)
        h_scr[...] = jnp.zeros_like(h_scr)
        s_scr[...] = jnp.zeros_like(s_scr)

    x3 = x_ref[...]
    mod3 = mod_ref[...]
    z = _in_proj(x3, mod3, n_pre1_ref, w_in_ref)
    xr, yg, q, k, v, g = (z[:, c * D_LRU:(c + 1) * D_LRU] for c in range(D_IN_PROJ // D_LRU))

    groups = tt // SUBLANES
    xr3 = xr.reshape(groups, SUBLANES, D_LRU)
    tpos = lax.broadcasted_iota(jnp.int32, xr3.shape, 1)
    tail = conv_scr[...]
    xc3 = jnp.broadcast_to(conv_b_ref[...].reshape(1, 1, D_LRU), xr3.shape)
    for j in range(CONV_W):
        back = CONV_W - 1 - j
        w_j = conv_w_ref[j:j + 1, :].reshape(1, 1, D_LRU)
        if back == 0:
            term = xr3
        else:
            cur = pltpu.roll(xr3, back, axis=1)
            first = pltpu.roll(tail, back, axis=0).reshape(1, SUBLANES, D_LRU)
            prev = jnp.concatenate([first, cur[:groups - 1]], axis=0)
            term = jnp.where(tpos >= back, cur, prev)
        xc3 = xc3 + term * w_j
    xc = xc3.reshape(tt, D_LRU)
    conv_scr[...] = xr[tt - SUBLANES:, :]

    a, b = _lru_coeffs(xc, wg_ref, b_r_ref, b_i_ref, lam_ref)
    hseq = _scan_rows(a, b, h_scr[0:1, :])
    h_scr[...] = jnp.broadcast_to(hseq[tt - 1:tt, :], h_scr.shape)
    out_a = hseq * jax.nn.gelu(yg, approximate=True)

    cos2 = cos_ref[...]
    sin2 = sin_ref[...]
    scale = DK ** -0.5
    o_heads = []
    for h in range(N_HEADS):
        hs = slice(h * DK, (h + 1) * DK)
        qh = (_rope(q[:, hs], cos2, sin2, 1) * scale).astype(BF16)
        kh = _rope(k[:, hs], cos2, sin2, 1)
        vh = v[:, hs].astype(BF16)
        o_chunks = []
        for c in range(tt // RET_CHUNK):
            cs = slice(c * RET_CHUNK, (c + 1) * RET_CHUNK)
            qc = qh[cs]
            kc = kh[cs]
            vc = vh[cs]
            s_prev = s_scr[h]
            scores = lax.dot_general(qc, kc.astype(BF16), (((1,), (1,)), ((), ())),
                                     preferred_element_type=F32) * mask_ref[h]
            inner = jnp.dot(scores.astype(BF16), vc, preferred_element_type=F32)
            cross = jnp.dot(qc, s_prev.astype(BF16), preferred_element_type=F32) * cross_ref[:, hs]
            kw = (kc * wstate_ref[:, hs]).astype(BF16)
            kv = lax.dot_general(kw, vc, (((0,), (0,)), ((), ())), preferred_element_type=F32)
            s_scr[h] = cdecay_ref[:, hs] * s_prev + kv
            o_chunks.append(inner + cross)
        o_heads.append(_group_norm(jnp.concatenate(o_chunks, axis=0)))
    o = jnp.concatenate(o_heads, axis=1)
    out_b = o * gn_w_ref[...] * _silu(g)

    _post_mixer(x3, mod3, out_a, out_b, w_out_ref, n_post1_ref, n_pre2_ref, w_router_ref, b_router_ref,
                earlier_ref, x1_ref, pk_ref, route_ref, cnt_scr)
    cnt_out_ref[...] = cnt_scr[...]

    @pl.when(t == pl.num_programs(1) - 1)
    def _():
        conv_out_ref[0] = xr[tt - SUBLANES:, :]
        lru_out_ref[0] = hseq[tt - SUBLANES:, :]
        ret_out_ref[0] = s_scr[...]


def _const_spec(shape):
    nd = len(shape)
    return pl.BlockSpec(shape, lambda *_: (0,) * nd)


def _prompt_mixer_call(x, mod3, cos2, sin2, wts, tables):
    bsz, seq, _ = x.shape
    tt = PROMPT_TILE
    nt = seq // tt
    n_tok = bsz * seq
    tok_spec = pl.BlockSpec((tt, D_MODEL), lambda b, t: (b * nt + t, 0))
    in_specs = [
        pl.BlockSpec((1, tt, D_MODEL), lambda b, t: (b, t, 0)),
        pl.BlockSpec((1, 1, 6 * D_MODEL), lambda b, t: (b, 0, 0)),
        pl.BlockSpec((tt, LANES), lambda b, t: (t, 0)),
        pl.BlockSpec((tt, LANES), lambda b, t: (t, 0)),
    ] + [_const_spec(w.shape) for w in wts] + [_const_spec(tb.shape) for tb in tables]
    out_specs = [
        tok_spec,
        pl.BlockSpec((tt * SUBLANES, LANES), lambda b, t: (b * nt + t, 0)),
        pl.BlockSpec((SUBLANES, tt), lambda b, t: (0, b * nt + t)),
        pl.BlockSpec((1, SUBLANES, D_LRU), lambda b, t: (b, 0, 0)),
        pl.BlockSpec((1, SUBLANES, D_LRU), lambda b, t: (b, 0, 0)),
        pl.BlockSpec((1, N_HEADS, DK, DV), lambda b, t: (b, 0, 0, 0)),
        pl.BlockSpec((SUBLANES, LANES), lambda b, t: (0, 0)),
    ]
    out_shape = [
        jax.ShapeDtypeStruct((n_tok, D_MODEL), F32),
        jax.ShapeDtypeStruct((n_tok * SUBLANES, LANES), F32),
        jax.ShapeDtypeStruct((SUBLANES, n_tok), F32),
        jax.ShapeDtypeStruct((bsz, SUBLANES, D_LRU), F32),
        jax.ShapeDtypeStruct((bsz, SUBLANES, D_LRU), F32),
        jax.ShapeDtypeStruct((bsz, N_HEADS, DK, DV), F32),
        jax.ShapeDtypeStruct((SUBLANES, LANES), F32),
    ]
    return pl.pallas_call(
        _prompt_mixer_kernel,
        grid=(bsz, nt),
        in_specs=in_specs,
        out_specs=out_specs,
        out_shape=out_shape,
        scratch_shapes=[
            pltpu.VMEM((SUBLANES, D_LRU), F32),
            pltpu.VMEM((SUBLANES, D_LRU), F32),
            pltpu.VMEM((N_HEADS, DK, DV), F32),
            pltpu.VMEM((SUBLANES, LANES), F32),
        ],
        compiler_params=pltpu.CompilerParams(
            dimension_semantics=("arbitrary", "arbitrary"), vmem_limit_bytes=VMEM_LIMIT),
        name="prompt_mixer",
    )(x, mod3, cos2, sin2, *wts, *tables)


def _sample_mixer_kernel(x_ref, mod_ref, cos_ref, sin_ref, buf_ref, h0_ref, s0_ref, cnt_in_ref,
                         n_pre1_ref, n_post1_ref, n_pre2_ref,
                         w_in_ref, conv_w_ref, conv_b_ref, wg_ref, b_r_ref, b_i_ref, lam_ref,
                         gn_w_ref, w_out_ref, w_router_ref, b_router_ref,
                         smask_ref, wstate_ref, cross_ref, cdecay_ref, earlier_ref,
                         x1_ref, pk_ref, route_ref, xr_out_ref, h_out_ref, ret_out_ref, cnt_out_ref,
                         cnt_scr):
    bb, ts, _ = x_ref.shape
    m = bb * ts

    @pl.when(pl.program_id(0) == 0)
    def _():
        cnt_scr[...] = cnt_in_ref[...]

    x3 = x_ref[...]
    mod3 = mod_ref[...]
    z = _in_proj(x3, mod3, n_pre1_ref, w_in_ref)
    xr, yg, q, k, v, g = (z[:, c * D_LRU:(c + 1) * D_LRU] for c in range(D_IN_PROJ // D_LRU))
    xr_out_ref[...] = xr

    xr3 = xr.reshape(bb, ts, D_LRU)
    buf3 = buf_ref[...]
    tpos = lax.broadcasted_iota(jnp.int32, (bb, ts, D_LRU), 1)
    xc3 = jnp.broadcast_to(conv_b_ref[...].reshape(1, 1, D_LRU), (bb, ts, D_LRU))
    for j in range(CONV_W):
        back = CONV_W - 1 - j
        w_j = conv_w_ref[j:j + 1, :].reshape(1, 1, D_LRU)
        if back == 0:
            term = xr3
        else:
            cur = pltpu.roll(xr3, back, axis=1)
            up = CONV_W - 1 - back
            old = buf3 if up == 0 else pltpu.roll(buf3, ts - up, axis=1)
            term = jnp.where(tpos >= back, cur, old)
        xc3 = xc3 + term * w_j
    xc = xc3.reshape(m, D_LRU)

    a, b = _lru_coeffs(xc, wg_ref, b_r_ref, b_i_ref, lam_ref)
    a3 = a.reshape(bb, ts, D_LRU)
    b3 = b.reshape(bb, ts, D_LRU) + a3 * h0_ref[...]
    _, h3 = _group_scan(a3, b3)
    hseq = h3.reshape(m, D_LRU)
    h_out_ref[...] = hseq
    out_a = hseq * jax.nn.gelu(yg, approximate=True)

    cos2 = cos_ref[...].reshape(1, ts, LANES)
    sin2 = sin_ref[...].reshape(1, ts, LANES)
    scale = DK ** -0.5
    o_heads = []
    for h in range(N_HEADS):
        hs = slice(h * DK, (h + 1) * DK)
        q3 = (_rope(q[:, hs].reshape(bb, ts, DK), cos2, sin2, 2) * scale).astype(BF16)
        k3 = _rope(k[:, hs].reshape(bb, ts, DK), cos2, sin2, 2)
        v3 = v[:, hs].reshape(bb, ts, DV).astype(BF16)
        q2 = q3.reshape(m, DK)
        k2 = k3.reshape(m, DK).astype(BF16)
        v2 = v3.reshape(m, DV)
        scores = lax.dot_general(q2, k2, (((1,), (1,)), ((), ())),
                                 preferred_element_type=F32) * smask_ref[h]
        inner = jnp.dot(scores.astype(BF16), v2, preferred_element_type=F32)
        s0h = s0_ref[:, h]
        cross = jnp.einsum('bid,bde->bie', q3, s0h.astype(BF16), preferred_element_type=F32)
        cross = cross * cross_ref[:, hs].reshape(1, ts, DV)
        kw3 = (k3 * wstate_ref[:, hs].reshape(1, ts, DK)).astype(BF16)
        kv = jnp.einsum('bjd,bje->bde', kw3, v3, preferred_element_type=F32)
        ret_out_ref[:, h] = cdecay_ref[:, hs].reshape(1, 1, DV) * s0h + kv
        o_heads.append(_group_norm(inner + cross.reshape(m, DV)))
    o = jnp.concatenate(o_heads, axis=1)
    out_b = o * gn_w_ref[...] * _silu(g)

    _post_mixer(x3, mod3, out_a, out_b, w_out_ref, n_post1_ref, n_pre2_ref, w_router_ref, b_router_ref,
                earlier_ref, x1_ref, pk_ref, route_ref, cnt_scr)
    cnt_out_ref[...] = cnt_scr[...]


def _sample_mixer_call(x, mod3, cos2, sin2, buf8, h0p, s0, cnt_in, wts, tables):
    bsz, ts, _ = x.shape
    bb = SAMPLE_SEQS
    m = bb * ts
    n_tok = bsz * ts
    seq_spec = lambda w: pl.BlockSpec((bb, ts, w), lambda i: (i, 0, 0))
    tok_spec = lambda w: pl.BlockSpec((m, w), lambda i: (i, 0))
    in_specs = [
        seq_spec(D_MODEL),
        pl.BlockSpec((bb, 1, 6 * D_MODEL), lambda i: (i, 0, 0)),
        _const_spec(cos2.shape),
        _const_spec(sin2.shape),
        seq_spec(D_LRU),
        seq_spec(D_LRU),
        pl.BlockSpec((bb, N_HEADS, DK, DV), lambda i: (i, 0, 0, 0)),
        _const_spec(cnt_in.shape),
    ] + [_const_spec(w.shape) for w in wts] + [_const_spec(tb.shape) for tb in tables]
    out_specs = [
        tok_spec(D_MODEL),
        pl.BlockSpec((m * SUBLANES, LANES), lambda i: (i, 0)),
        pl.BlockSpec((SUBLANES, m), lambda i: (0, i)),
        tok_spec(D_LRU),
        tok_spec(D_LRU),
        pl.BlockSpec((bb, N_HEADS, DK, DV), lambda i: (i, 0, 0, 0)),
        _const_spec(cnt_in.shape),
    ]
    out_shape = [
        jax.ShapeDtypeStruct((n_tok, D_MODEL), F32),
        jax.ShapeDtypeStruct((n_tok * SUBLANES, LANES), F32),
        jax.ShapeDtypeStruct((SUBLANES, n_tok), F32),
        jax.ShapeDtypeStruct((n_tok, D_LRU), F32),
        jax.ShapeDtypeStruct((n_tok, D_LRU), F32),
        jax.ShapeDtypeStruct((bsz, N_HEADS, DK, DV), F32),
        jax.ShapeDtypeStruct(cnt_in.shape, F32),
    ]
    return pl.pallas_call(
        _sample_mixer_kernel,
        grid=(bsz // bb,),
        in_specs=in_specs,
        out_specs=out_specs,
        out_shape=out_shape,
        scratch_shapes=[pltpu.VMEM((SUBLANES, LANES), F32)],
        compiler_params=pltpu.CompilerParams(
            dimension_semantics=("arbitrary",), vmem_limit_bytes=VMEM_LIMIT),
        name="sample_mixer",
    )(x, mod3, cos2, sin2, buf8, h0p, s0, cnt_in, *wts, *tables)


def _slab(ref, r):
    return ref.at[pl.ds(pl.multiple_of(r * SUBLANES, SUBLANES), SUBLANES)]


def _dispatch_kernel(cls_ref, rank_ref, cnt_ref, padcnt_ref, start_ref, srcp_ref, srcs_ref, out_ref, ring, sem,
                     *, p_tiles):
    i = pl.program_id(0)
    n = pl.num_programs(0)
    tile_rows = ring.shape[1]
    td = tile_rows // SUBLANES
    moe_rows = MOE_TILE * SUBLANES

    def issue(slot):
        def body(g, carry):
            first = i * td + g * DMA_GROUP
            slots = [start_ref[cls_ref[first + j]] + rank_ref[first + j] for j in range(DMA_GROUP)]
            first = g * DMA_GROUP
            for j in range(DMA_GROUP):
                pltpu.make_async_copy(
                    _slab(ring.at[slot], first + j), _slab(out_ref, slots[j]), sem.at[slot]
                ).start(priority=j % 2)
            return carry
        lax.fori_loop(0, td // DMA_GROUP, body, 0)

    def wait_tile(slot):
        pltpu.make_async_copy(ring.at[slot], out_ref.at[pl.ds(0, tile_rows)], sem.at[slot]).wait()

    def zero_fill(slot):
        ring[slot] = jnp.zeros((tile_rows, LANES), F32)
        zero_src = ring.at[slot]

        def per_class(c, carry):
            lo = start_ref[c] + cnt_ref[c]
            hi = start_ref[c] + padcnt_ref[c]

            def fill(r, carry2):
                pltpu.make_async_copy(_slab(zero_src, 0), _slab(out_ref, r), sem.at[slot]).start()
                return carry2
            lax.fori_loop(lo, hi, fill, 0)

            def done(r, carry2):
                pltpu.make_async_copy(_slab(zero_src, 0), _slab(out_ref, 0), sem.at[slot]).wait()
                return carry2
            lax.fori_loop(lo, hi, done, 0)
            return carry
        lax.fori_loop(0, N_CLASSES, per_class, 0)

        used_tiles = (start_ref[N_CLASSES - 1] + padcnt_ref[N_CLASSES - 1]) // MOE_TILE
        all_tiles = out_ref.shape[0] // moe_rows

        def tile_copy(t):
            return pltpu.make_async_copy(
                zero_src.at[pl.ds(0, moe_rows)],
                out_ref.at[pl.ds(pl.multiple_of(t * moe_rows, moe_rows), moe_rows)], sem.at[slot])

        def fill_tile(t, carry):
            tile_copy(t).start()
            return carry
        lax.fori_loop(used_tiles, all_tiles, fill_tile, 0)

        def done_tile(t, carry):
            tile_copy(t).wait()
            return carry
        lax.fori_loop(used_tiles, all_tiles, done_tile, 0)

    for s in range(RING):
        @pl.when(i % RING == s)
        def _():
            @pl.when(i < p_tiles)
            def _():
                ring[s] = srcp_ref[...]

            @pl.when(i >= p_tiles)
            def _():
                ring[s] = srcs_ref[...]

            issue(s)

            @pl.when(i >= RING - 1)
            def _():
                wait_tile((s + 1) % RING)

            @pl.when(i == n - 1)
            def _():
                for back in range(RING - 2, -1, -1):
                    wait_tile((s - back) % RING)
                zero_fill(s)


def _dispatch_call(cls, rank, cnt, padcnt, start, pk_p, pk_s, n_rows):
    td = DISPATCH_TILE
    tile_rows = td * SUBLANES
    p_tiles = pk_p.shape[0] // tile_rows
    s_tiles = pk_s.shape[0] // tile_rows
    assert p_tiles + s_tiles >= RING and td >= MOE_TILE
    return pl.pallas_call(
        functools.partial(_dispatch_kernel, p_tiles=p_tiles),
        grid_spec=pltpu.PrefetchScalarGridSpec(
            num_scalar_prefetch=5,
            grid=(p_tiles + s_tiles,),
            in_specs=[
                pl.BlockSpec((tile_rows, LANES), lambda i, *_: (jnp.minimum(i, p_tiles - 1), 0)),
                pl.BlockSpec((tile_rows, LANES), lambda i, *_: (jnp.maximum(i - p_tiles, 0), 0)),
            ],
            out_specs=pl.BlockSpec(memory_space=pl.ANY),
            scratch_shapes=[pltpu.VMEM((RING, tile_rows, LANES), F32),
                            pltpu.SemaphoreType.DMA((RING,))],
        ),
        out_shape=jax.ShapeDtypeStruct((n_rows * SUBLANES, LANES), F32),
        compiler_params=pltpu.CompilerParams(
            dimension_semantics=("arbitrary",), has_side_effects=True, vmem_limit_bytes=VMEM_LIMIT),
        name="moe_dispatch",
    )(cls, rank, cnt, padcnt, start, pk_p, pk_s)


def _moe_kernel(ea_ref, eb_ref, valid_ref, xs_ref, w_router_ref, b_router_ref,
                wga_ref, wua_ref, wda_ref, wgb_ref, wub_ref, wdb_ref, f_ref):
    t = pl.program_id(0)

    @pl.when(valid_ref[t] == 1)
    def _():
        tm = xs_ref.shape[0] // SUBLANES
        x = jnp.concatenate([xs_ref[pl.ds(j, tm, stride=SUBLANES), :] for j in range(ROW_GROUPS)],
                            axis=1).astype(BF16)
        e_a = ea_ref[t]
        e_b = eb_ref[t]
        e_lo = (e_a // PER_GROUP) * PER_GROUP
        logits = jnp.dot(x, w_router_ref[...], preferred_element_type=F32) + b_router_ref[...]
        lane = lax.broadcasted_iota(jnp.int32, (tm, LANES), 1)
        pick = lambda p, idx: jnp.sum(jnp.where(lane == idx, p, 0.0), axis=-1, keepdims=True)
        p_group = _masked_softmax(logits, (lane >= N_EXPERTS) & (lane < N_EXPERTS + N_GROUPS))
        p_g = pick(p_group, N_EXPERTS + e_a // PER_GROUP)
        p_e = _masked_softmax(logits, (lane >= e_lo) & (lane < e_lo + PER_GROUP))
        w_a = pick(p_e, e_a)
        w_b = pick(p_e, e_b)
        wsum = w_a + w_b

        def expert(wg_ref, wu_ref, gate):
            hg = jnp.dot(x, wg_ref[0].astype(BF16), preferred_element_type=F32)
            hu = jnp.dot(x, wu_ref[0].astype(BF16), preferred_element_type=F32)
            return (_silu(hg) * hu * gate).astype(BF16)

        ha = expert(wga_ref, wua_ref, p_g * (w_a / wsum))
        hb = expert(wgb_ref, wub_ref, p_g * (w_b / wsum))
        for c in range(D_MODEL // GATE_HALF):
            cols = slice(c * GATE_HALF, (c + 1) * GATE_HALF)
            f = (jnp.dot(ha, wda_ref[0, :, cols].astype(BF16), preferred_element_type=F32)
                 + jnp.dot(hb, wdb_ref[0, :, cols].astype(BF16), preferred_element_type=F32))
            for jj in range(GATE_HALF // LANES):
                j = c * (GATE_HALF // LANES) + jj
                f_ref[pl.ds(j, tm, stride=SUBLANES), :] = f[:, jj * LANES:(jj + 1) * LANES]

    @pl.when(valid_ref[t] == 0)
    def _():
        f_ref[...] = jnp.zeros_like(f_ref)


def _moe_call(tile_ea, tile_eb, tile_valid, rows, w_router, b_router, wg, wu, wd):
    n_rows = rows.shape[0] // SUBLANES
    tm = MOE_TILE
    const = lambda a: pl.BlockSpec(a.shape, lambda t, ea, eb, v: (0,) * a.ndim)
    up = lambda sel: pl.BlockSpec((1, D_MODEL, D_EXPERT), lambda t, ea, eb, v: (sel(ea, eb)[t], 0, 0))
    down = lambda sel: pl.BlockSpec((1, D_EXPERT, D_MODEL), lambda t, ea, eb, v: (sel(ea, eb)[t], 0, 0))
    first = lambda ea, eb: ea
    second = lambda ea, eb: eb
    return pl.pallas_call(
        _moe_kernel,
        grid_spec=pltpu.PrefetchScalarGridSpec(
            num_scalar_prefetch=3,
            grid=(n_rows // tm,),
            in_specs=[
                pl.BlockSpec((tm * SUBLANES, LANES), lambda t, ea, eb, v: (t, 0)),
                const(w_router), const(b_router),
                up(first), up(first), down(first), up(second), up(second), down(second),
            ],
            out_specs=pl.BlockSpec((tm * SUBLANES, LANES), lambda t, ea, eb, v: (t, 0)),
        ),
        out_shape=jax.ShapeDtypeStruct((n_rows * SUBLANES, LANES), F32),
        compiler_params=pltpu.CompilerParams(
            dimension_semantics=("arbitrary",), vmem_limit_bytes=VMEM_LIMIT),
        name="moe_experts",
    )(tile_ea, tile_eb, tile_valid, rows, w_router, b_router, wg, wu, wd, wg, wu, wd)


def _combine_kernel(cls_ref, rank_ref, start_ref, x1_ref, mod_ref, n_post2_ref, f_hbm, o_ref, fbuf, sem):
    i = pl.program_id(0)
    n = pl.num_programs(0)
    tf = x1_ref.shape[0]

    def issue(tile, slot):
        def body(g, carry):
            first = tile * tf + g * DMA_GROUP
            slots = [start_ref[cls_ref[first + j]] + rank_ref[first + j] for j in range(DMA_GROUP)]
            first = g * DMA_GROUP
            for j in range(DMA_GROUP):
                pltpu.make_async_copy(
                    _slab(f_hbm, slots[j]), _slab(fbuf.at[slot], first + j), sem.at[slot]
                ).start(priority=j % 2)
            return carry
        lax.fori_loop(0, tf // DMA_GROUP, body, 0)

    def finish(slot):
        pltpu.make_async_copy(f_hbm.at[pl.ds(0, tf * SUBLANES)], fbuf.at[slot], sem.at[slot]).wait()
        f = jnp.concatenate([fbuf[slot, pl.ds(j, tf, stride=SUBLANES), :] for j in range(D_MODEL // LANES)],
                            axis=1)
        mod3 = mod_ref[...]
        bb = mod3.shape[0]
        f3 = f.reshape(bb, tf // bb, D_MODEL)
        g2 = mod3[:, :, 5 * D_MODEL:6 * D_MODEL]
        x13 = x1_ref[...].reshape(f3.shape)
        out = x13 + (f3 * _rms_scale(f3)) * (g2 * n_post2_ref[...].reshape(1, 1, D_MODEL))
        o_ref[...] = out.reshape(o_ref.shape)

    @pl.when(i == 0)
    def _():
        issue(0, 0)

    for parity in range(2):
        @pl.when(i % 2 == parity)
        def _():
            @pl.when(i + 1 < n)
            def _():
                issue(i + 1, 1 - parity)
            finish(parity)


def _combine_call(cls, rank, start, x1, mod3, n_post2, f_sorted, seqs_per_tile):
    n_tok = x1.shape[0]
    tf = FINAL_TILE
    tok_per_seq = n_tok // mod3.shape[0]
    if seqs_per_tile > 1:
        mod_map = lambda i, *_: (i, 0, 0)
    else:
        mod_map = lambda i, *_: ((i * tf) // tok_per_seq, 0, 0)
    return pl.pallas_call(
        _combine_kernel,
        grid_spec=pltpu.PrefetchScalarGridSpec(
            num_scalar_prefetch=3,
            grid=(n_tok // tf,),
            in_specs=[
                pl.BlockSpec((tf, D_MODEL), lambda i, *_: (i, 0)),
                pl.BlockSpec((seqs_per_tile, 1, 6 * D_MODEL), mod_map),
                pl.BlockSpec((1, D_MODEL), lambda i, *_: (0, 0)),
                pl.BlockSpec(memory_space=pl.ANY),
            ],
            out_specs=pl.BlockSpec((tf, D_MODEL), lambda i, *_: (i, 0)),
            scratch_shapes=[pltpu.VMEM((2, tf * SUBLANES, LANES), F32), pltpu.SemaphoreType.DMA((2,))],
        ),
        out_shape=jax.ShapeDtypeStruct((n_tok, D_MODEL), F32),
        compiler_params=pltpu.CompilerParams(
            dimension_semantics=("arbitrary",), vmem_limit_bytes=VMEM_LIMIT),
        name="moe_combine",
    )(cls, rank, start, x1, mod3, n_post2, f_sorted)


def _block_diag_gate(w_r, w_i):
    per_half = GATE_HALF // LRU_BLOCK
    n_half = D_LRU // GATE_HALF
    eye = np.eye(per_half, dtype=np.float32)

    def block_diag(w):
        w4 = w.reshape(n_half, per_half, LRU_BLOCK, LRU_BLOCK)
        return jnp.einsum('hnij,nm->hnimj', w4, eye).reshape(n_half, GATE_HALF, GATE_HALF)

    return jnp.concatenate([block_diag(w_r), block_diag(w_i)], axis=-1).astype(BF16)


def _rope_tables(pos):
    half = DK // 2
    inv = np.float64(ROPE_BASE) ** (-np.arange(half, dtype=np.float64) / half)
    ang = np.asarray(pos, np.float64)[:, None] * inv[None, :]
    cos = np.cos(ang)
    sin = np.sin(ang)
    return (np.concatenate([cos, cos], axis=-1).astype(np.float32),
            np.concatenate([-sin, sin], axis=-1).astype(np.float32))


def _earlier_table(m):
    return jnp.asarray(np.tril(np.ones((m, m), np.float32), -1), BF16)


def _decay_tables(c):
    log_g = np.log1p(-np.exp2(-5.0 - np.arange(N_HEADS, dtype=np.float64)))
    idx = np.arange(c, dtype=np.float64)
    diff = idx[:, None] - idx[None, :]
    mask = np.where(diff[None] >= 0, np.exp(np.maximum(diff, 0.0)[None] * log_g[:, None, None]), 0.0)
    w_state = np.exp((c - 1.0 - idx)[None, :] * log_g[:, None])
    cross_decay = np.exp((idx + 1.0)[:, None] * log_g[None, :])
    chunk_decay = np.exp(c * log_g)
    wstate_full = np.repeat(w_state.T, DK, axis=1)
    cross_full = np.repeat(cross_decay, DV, axis=1)
    cdecay_full = np.repeat(chunk_decay, DV)[None, :]
    return tuple(t.astype(np.float32) for t in (mask, wstate_full, cross_full, cdecay_full))


def kernel(x_prompt, x_sample, state_conv, state_lru, state_ret, c_prompt, c_sample, w_mod, b_mod, norm_pre_mix, norm_post_mix, norm_pre_ffn, norm_post_ffn, w_in, conv_w, conv_b, w_rgate, b_rgate, w_igate, b_igate, lru_lambda, ret_gn_w, w_out, w_router_group, b_router_group, w_router_expert, b_router_expert, w_exp_gate, w_exp_up, w_exp_down):
    bp, tp, _ = x_prompt.shape
    bs, ts, _ = x_sample.shape
    l = 0

    mod = _mod_call(jnp.concatenate([c_prompt, c_sample], axis=0), w_mod[l], b_mod[l][None, :])
    mod_p = mod[:bp][:, None, :]
    mod_s = mod[bp:][:, None, :]

    unused = LANES - N_EXPERTS - N_GROUPS
    w_router = jnp.concatenate(
        [w_router_expert[l], w_router_group[l], jnp.zeros((D_MODEL, unused), F32)], axis=1).astype(BF16)
    b_router = jnp.concatenate(
        [b_router_expert[l], b_router_group[l], jnp.zeros((unused,), F32)])[None, :]

    row = lambda vec: vec.reshape(1, -1)
    wts = (row(norm_pre_mix[l]), row(norm_post_mix[l]), row(norm_pre_ffn[l]),
           w_in[l].astype(BF16), conv_w[l], row(conv_b[l]),
           _block_diag_gate(w_rgate[l], w_igate[l]),
           row(b_rgate[l]), row(b_igate[l]), row(lru_lambda[l]), row(ret_gn_w[l]),
           w_out[l].astype(BF16), w_router, b_router)

    cos_p, sin_p = _rope_tables(np.arange(tp))
    x1_p, pk_p, route_p, conv_p8, lru_p8, ret_p, cnt_p = _prompt_mixer_call(
        x_prompt, mod_p, cos_p, sin_p, wts,
        _decay_tables(math.gcd(tp, RET_CHUNK)) + (_earlier_table(PROMPT_TILE),))

    cos_s, sin_s = _rope_tables(PAST_LEN + np.arange(ts))
    mask8, wstate_s, cross_s, cdecay_s = _decay_tables(math.gcd(ts, RET_CHUNK))
    eye = np.eye(SAMPLE_SEQS, dtype=np.float32)
    smask = np.stack([np.kron(eye, mask8[h]) for h in range(N_HEADS)])
    buf8 = jnp.pad(state_conv[l], ((0, 0), (0, ts - (CONV_W - 1)), (0, 0)))
    h0p = jnp.pad(state_lru[l][:, None, :], ((0, 0), (0, ts - 1), (0, 0)))
    x1_s, pk_s, route_s, xr_s, h_s, ret_s, cnt_all = _sample_mixer_call(
        x_sample, mod_s, cos_s, sin_s, buf8, h0p, state_ret[l], cnt_p, wts,
        (smask, wstate_s, cross_s, cdecay_s, _earlier_table(SAMPLE_SEQS * ts)))

    n_p = bp * tp
    n_tok = n_p + bs * ts
    tm = MOE_TILE
    max_tiles = n_tok // tm + N_CLASSES
    cls_p, rank_p = route_p[0].astype(jnp.int32), route_p[1].astype(jnp.int32)
    cls_s, rank_s = route_s[0].astype(jnp.int32), route_s[1].astype(jnp.int32)
    cls = jnp.concatenate([cls_p, cls_s])
    rank = jnp.concatenate([rank_p, rank_s])
    cnt = cnt_all[0, :N_CLASSES].astype(jnp.int32)
    ntile = (cnt + (tm - 1)) // tm
    padcnt = ntile * tm
    before = np.tril(np.ones((N_CLASSES, N_CLASSES), np.int32), -1)
    upto = np.tril(np.ones((N_CLASSES, N_CLASSES), np.int32))
    start = jnp.sum(before * padcnt[None, :], axis=1)
    tile_end = jnp.sum(upto * ntile[None, :], axis=1)
    n_used = jnp.sum(ntile)
    tile_ids = np.arange(max_tiles, dtype=np.int32)
    tile_valid = (tile_ids < n_used).astype(jnp.int32)
    last_used = jnp.minimum(tile_ids, n_used - 1)
    tile_cls = jnp.sum((last_used[:, None] >= tile_end[None, :]).astype(jnp.int32), axis=1)
    pair = tile_cls % N_PAIRS
    first_expert = jnp.where(pair == 5, 2, jnp.where((pair == 2) | (pair == 3), 1, 0))
    second_expert = jnp.where(pair == 0, 1, jnp.where(pair <= 2, 2, 3))
    tile_ea = (tile_cls // N_PAIRS) * PER_GROUP + first_expert
    tile_eb = (tile_cls // N_PAIRS) * PER_GROUP + second_expert

    n_post2 = row(norm_post_ffn[l])
    wg = w_exp_gate[l]
    wu = w_exp_up[l]
    wd = w_exp_down[l]
    rows = _dispatch_call(cls, rank, cnt, padcnt, start, pk_p, pk_s, max_tiles * tm)
    f_sorted = _moe_call(tile_ea, tile_eb, tile_valid, rows, w_router, b_router, wg, wu, wd)
    y_p = _combine_call(cls_p, rank_p, start, x1_p, mod_p, n_post2, f_sorted, 1)
    y_s = _combine_call(cls_s, rank_s, start, x1_s, mod_s, n_post2, f_sorted, FINAL_TILE // ts)

    conv_p = conv_p8[:, SUBLANES - (CONV_W - 1):, :]
    lru_p = lru_p8[:, SUBLANES - 1, :]
    xr_s3 = xr_s.reshape(bs, ts, D_LRU)
    conv_s = xr_s3[:, ts - (CONV_W - 1):, :]
    lru_s = h_s.reshape(bs, ts, D_LRU)[:, ts - 1, :]
    return (y_p.reshape(bp, tp, D_MODEL), y_s.reshape(bs, ts, D_MODEL),
            conv_p[None], lru_p[None], ret_p[None],
            conv_s[None], lru_s[None], ret_s[None])
```

```python
import functools
import math

import jax
import jax.numpy as jnp
import numpy as np
from jax import lax
from jax.experimental import pallas as pl
from jax.experimental.pallas import tpu as pltpu

F32 = jnp.float32
BF16 = jnp.bfloat16

D_MODEL = 1024
D_LRU = 512
D_RET = 512
N_LRU_BLOCKS = 8
LRU_BLOCK = D_LRU // N_LRU_BLOCKS
CONV_W = 4
LRU_C = 8.0
N_HEADS = 4
DK = 128
DV = 128
RET_CHUNK = 128
ROPE_BASE = 10000.0
D_IN_PROJ = 3072
N_GROUPS = 4
PER_GROUP = 4
N_EXPERTS = 16
D_EXPERT = 256
EXPM1_DIRECT_BELOW = -0.5
NORM_EPS = 1e-6
GN_EPS = 1e-5
PAST_LEN = 16384

SUBLANES = 8
LANES = 128
GATE_HALF = 256
VMEM_LIMIT = 56 * 1024 * 1024

PROMPT_TILE = 512
SAMPLE_SEQS = 16
MOE_TILE = 256
MOE_TILES_PER_STEP = 2
FINAL_TILE = 512
DISPATCH_TILE = 512
RING = 3
DMA_GROUP = 8

N_PAIRS = 6
N_CLASSES = N_GROUPS * N_PAIRS
ROW_GROUPS = D_MODEL // LANES


def _silu(x):
    return x * jax.nn.sigmoid(x)


def _rms_scale(x):
    return lax.rsqrt(jnp.mean(x * x, axis=-1, keepdims=True) + NORM_EPS)


def _masked_softmax(logits, mask):
    top = jnp.max(jnp.where(mask, logits, -jnp.inf), axis=-1, keepdims=True)
    e = jnp.where(mask, jnp.exp(logits - top), 0.0)
    return e / jnp.sum(e, axis=-1, keepdims=True)


def _mod_kernel(c_ref, w_ref, b_ref, o_ref):
    s = _silu(c_ref[...]).astype(BF16)
    o_ref[...] = jnp.dot(s, w_ref[...].astype(BF16), preferred_element_type=F32) + b_ref[...]


def _mod_call(c_all, w_mod, b_mod):
    rows = c_all.shape[0]
    ncol = w_mod.shape[1]
    blk = D_MODEL
    return pl.pallas_call(
        _mod_kernel,
        grid=(ncol // blk,),
        in_specs=[
            pl.BlockSpec((rows, D_MODEL), lambda j: (0, 0)),
            pl.BlockSpec((D_MODEL, blk), lambda j: (0, j)),
            pl.BlockSpec((1, blk), lambda j: (0, j)),
        ],
        out_specs=pl.BlockSpec((rows, blk), lambda j: (0, j)),
        out_shape=jax.ShapeDtypeStruct((rows, ncol), F32),
        compiler_params=pltpu.CompilerParams(
            dimension_semantics=("arbitrary",), vmem_limit_bytes=VMEM_LIMIT),
        name="mod",
    )(c_all, w_mod, b_mod)


def _in_proj(x3, mod3, n_pre1_ref, w_in_ref):
    bb, tt, _ = x3.shape
    sh1 = mod3[:, :, 0:D_MODEL]
    sc1 = mod3[:, :, D_MODEL:2 * D_MODEL]
    coef = n_pre1_ref[...].reshape(1, 1, D_MODEL) * (1.0 + sc1)
    u = (x3 * _rms_scale(x3)) * coef + sh1
    u2d = u.reshape(bb * tt, D_MODEL).astype(BF16)
    return jnp.dot(u2d, w_in_ref[...], preferred_element_type=F32)


def _lru_coeffs(xc, wg_ref, b_r_ref, b_i_ref, lam_ref):
    xcb = xc.astype(BF16)
    g0 = jnp.dot(xcb[:, :GATE_HALF], wg_ref[0], preferred_element_type=F32)
    g1 = jnp.dot(xcb[:, GATE_HALF:], wg_ref[1], preferred_element_type=F32)
    r = jax.nn.sigmoid(jnp.concatenate([g0[:, :GATE_HALF], g1[:, :GATE_HALF]], axis=1) + b_r_ref[...])
    i = jax.nn.sigmoid(jnp.concatenate([g0[:, GATE_HALF:], g1[:, GATE_HALF:]], axis=1) + b_i_ref[...])
    lam = lam_ref[...]
    sp = jnp.maximum(-lam, 0.0) + jnp.log1p(jnp.exp(-jnp.abs(lam)))
    log_a = -LRU_C * r * sp
    a = jnp.exp(log_a)
    y = 2.0 * log_a
    a2 = a * a
    d = a2 - 1.0
    small = d * y / jnp.log(a2)
    em1 = jnp.where(y < EXPM1_DIRECT_BELOW, d, jnp.where(d == 0.0, y, small))
    gain = jnp.sqrt(-em1)
    return a, gain * (i * xc)


def _rope(xh, cos2, sin2, lane_axis):
    return xh * cos2 + pltpu.roll(xh, DK // 2, axis=lane_axis) * sin2


def _group_norm(o):
    mu = jnp.mean(o, axis=-1, keepdims=True)
    d = o - mu
    var = jnp.mean(d * d, axis=-1, keepdims=True)
    return d * lax.rsqrt(var + GN_EPS)


def _post_mixer(x3, mod3, out_a, out_b, w_out_ref, n_post1_ref, n_pre2_ref, w_router_ref, b_router_ref,
                earlier_ref, x1_ref, pk_ref, route_ref, cnt_scr):
    bb, tt, _ = x3.shape
    m = bb * tt
    y = (jnp.dot(out_a.astype(BF16), w_out_ref[0:D_LRU, :], preferred_element_type=F32)
         + jnp.dot(out_b.astype(BF16), w_out_ref[D_LRU:, :], preferred_element_type=F32))
    g1 = mod3[:, :, 2 * D_MODEL:3 * D_MODEL]
    sh2 = mod3[:, :, 3 * D_MODEL:4 * D_MODEL]
    sc2 = mod3[:, :, 4 * D_MODEL:5 * D_MODEL]
    y3 = y.reshape(bb, tt, D_MODEL)
    x1 = x3 + (y3 * _rms_scale(y3)) * (g1 * n_post1_ref[...].reshape(1, 1, D_MODEL))
    u2 = (x1 * _rms_scale(x1)) * (n_pre2_ref[...].reshape(1, 1, D_MODEL) * (1.0 + sc2)) + sh2
    x1_ref[...] = x1.reshape(m, D_MODEL)
    u2f = u2.reshape(m, D_MODEL)
    for j in range(ROW_GROUPS):
        pk_ref[pl.ds(j, m, stride=SUBLANES), :] = u2f[:, j * LANES:(j + 1) * LANES]
    u2b = u2f.astype(BF16)

    logits = jnp.dot(u2b, w_router_ref[...], preferred_element_type=F32) + b_router_ref[...]
    lane = lax.broadcasted_iota(jnp.int32, (m, LANES), 1)
    lane_f = lane.astype(F32)
    is_g = (lane >= N_EXPERTS) & (lane < N_EXPERTS + N_GROUPS)
    p_group = _masked_softmax(logits, is_g)
    p_g = jnp.max(p_group, axis=-1, keepdims=True)
    g_lane = jnp.min(jnp.where(is_g & (p_group == p_g), lane_f, float(LANES)), axis=-1, keepdims=True)
    e_lo = (g_lane - N_EXPERTS) * PER_GROUP
    in_g = (lane_f >= e_lo) & (lane_f < e_lo + PER_GROUP)
    p_e = _masked_softmax(logits, in_g)
    pm = jnp.where(in_g, p_e, -1.0)
    w1 = jnp.max(pm, axis=-1, keepdims=True)
    i1 = jnp.min(jnp.where(pm == w1, lane_f, float(LANES)), axis=-1, keepdims=True)
    pm2 = jnp.where(lane_f == i1, -1.0, pm)
    w2 = jnp.max(pm2, axis=-1, keepdims=True)
    i2 = jnp.min(jnp.where(pm2 == w2, lane_f, float(LANES)), axis=-1, keepdims=True)
    a = jnp.minimum(i1, i2) - e_lo
    b = jnp.maximum(i1, i2) - e_lo
    pair = jnp.where(a == 0.0, jnp.where(b == 3.0, 4.0, b - 1.0), jnp.where(a == 1.0, b, 5.0))
    cls = (g_lane - N_EXPERTS) * N_PAIRS + pair
    onehot = lane_f == cls
    prefix = jnp.dot(earlier_ref[...], jnp.where(onehot, 1.0, 0.0).astype(BF16), preferred_element_type=F32)
    run = cnt_scr[0:1, :]
    rank = jnp.sum(jnp.where(onehot, prefix + run, 0.0), axis=-1, keepdims=True)
    cnt_scr[...] = jnp.broadcast_to(
        run + jnp.sum(jnp.where(onehot, 1.0, 0.0), axis=0, keepdims=True), cnt_scr.shape)
    route = jnp.where(lane == 0, cls, jnp.where(lane == 1, rank, 0.0))
    route_ref[...] = jnp.transpose(route)[0:SUBLANES, :]


def _group_scan(a3, b3):
    tpos = lax.broadcasted_iota(jnp.int32, a3.shape, 1)
    s = 1
    while s < a3.shape[1]:
        keep = tpos >= s
        a_sh = jnp.where(keep, pltpu.roll(a3, s, axis=1), 1.0)
        b_sh = jnp.where(keep, pltpu.roll(b3, s, axis=1), 0.0)
        b3 = a3 * b_sh + b3
        a3 = a3 * a_sh
        s *= 2
    return a3, b3


def _scan_rows(a, b, h0):
    n, c = a.shape
    groups = n // SUBLANES
    a3, b3 = _group_scan(a.reshape(groups, SUBLANES, c), b.reshape(groups, SUBLANES, c))
    carry = h0
    out = []
    for g in range(groups):
        hg = b3[g] + a3[g] * carry
        out.append(hg)
        carry = hg[SUBLANES - 1:SUBLANES, :]
    return jnp.concatenate(out, axis=0)


def _prompt_mixer_kernel(x_ref, mod_ref, cos_ref, sin_ref,
                         n_pre1_ref, n_post1_ref, n_pre2_ref,
                         w_in_ref, conv_w_ref, conv_b_ref, wg_ref, b_r_ref, b_i_ref, lam_ref,
                         gn_w_ref, w_out_ref, w_router_ref, b_router_ref,
                         mask_ref, wstate_ref, cross_ref, cdecay_ref, earlier_ref,
                         x1_ref, pk_ref, route_ref, conv_out_ref, lru_out_ref, ret_out_ref, cnt_out_ref,
                         conv_scr, h_scr, s_scr, cnt_scr):
    t = pl.program_id(1)
    tt = x_ref.shape[1]

    @pl.when((pl.program_id(0) == 0) & (t == 0))
    def _():
        cnt_scr[...] = jnp.zeros_like(cnt_scr)

    @pl.when(t == 0)
    def _():
        conv_scr[...] = jnp.zeros_like(conv_scr)
        h_scr[...] = jnp.zeros_like(h_scr)
        s_scr[...] = jnp.zeros_like(s_scr)

    x3 = x_ref[...]
    mod3 = mod_ref[...]
    z = _in_proj(x3, mod3, n_pre1_ref, w_in_ref)
    xr, yg, q, k, v, g = (z[:, c * D_LRU:(c + 1) * D_LRU] for c in range(D_IN_PROJ // D_LRU))

    groups = tt // SUBLANES
    xr3 = xr.reshape(groups, SUBLANES, D_LRU)
    tpos = lax.broadcasted_iota(jnp.int32, xr3.shape, 1)
    tail = conv_scr[...]
    xc3 = jnp.broadcast_to(conv_b_ref[...].reshape(1, 1, D_LRU), xr3.shape)
    for j in range(CONV_W):
        back = CONV_W - 1 - j
        w_j = conv_w_ref[j:j + 1, :].reshape(1, 1, D_LRU)
        if back == 0:
            term = xr3
        else:
            cur = pltpu.roll(xr3, back, axis=1)
            first = pltpu.roll(tail, back, axis=0).reshape(1, SUBLANES, D_LRU)
            prev = jnp.concatenate([first, cur[:groups - 1]], axis=0)
            term = jnp.where(tpos >= back, cur, prev)
        xc3 = xc3 + term * w_j
    xc = xc3.reshape(tt, D_LRU)
    conv_scr[...] = xr[tt - SUBLANES:, :]

    a, b = _lru_coeffs(xc, wg_ref, b_r_ref, b_i_ref, lam_ref)
    hseq = _scan_rows(a, b, h_scr[0:1, :])
    h_scr[...] = jnp.broadcast_to(hseq[tt - 1:tt, :], h_scr.shape)
    out_a = hseq * jax.nn.gelu(yg, approximate=True)

    cos2 = cos_ref[...]
    sin2 = sin_ref[...]
    scale = DK ** -0.5
    o_heads = []
    for h in range(N_HEADS):
        hs = slice(h * DK, (h + 1) * DK)
        qh = (_rope(q[:, hs], cos2, sin2, 1) * scale).astype(BF16)
        kh = _rope(k[:, hs], cos2, sin2, 1)
        vh = v[:, hs].astype(BF16)
        o_chunks = []
        for c in range(tt // RET_CHUNK):
            cs = slice(c * RET_CHUNK, (c + 1) * RET_CHUNK)
            qc = qh[cs]
            kc = kh[cs]
            vc = vh[cs]
            s_prev = s_scr[h]
            scores = lax.dot_general(qc, kc.astype(BF16), (((1,), (1,)), ((), ())),
                                     preferred_element_type=F32) * mask_ref[h]
            inner = jnp.dot(scores.astype(BF16), vc, preferred_element_type=F32)
            cross = jnp.dot(qc, s_prev.astype(BF16), preferred_element_type=F32) * cross_ref[:, hs]
            kw = (kc * wstate_ref[:, hs]).astype(BF16)
            kv = lax.dot_general(kw, vc, (((0,), (0,)), ((), ())), preferred_element_type=F32)
            s_scr[h] = cdecay_ref[:, hs] * s_prev + kv
            o_chunks.append(inner + cross)
        o_heads.append(_group_norm(jnp.concatenate(o_chunks, axis=0)))
    o = jnp.concatenate(o_heads, axis=1)
    out_b = o * gn_w_ref[...] * _silu(g)

    _post_mixer(x3, mod3, out_a, out_b, w_out_ref, n_post1_ref, n_pre2_ref, w_router_ref, b_router_ref,
                earlier_ref, x1_ref, pk_ref, route_ref, cnt_scr)
    cnt_out_ref[...] = cnt_scr[...]

    @pl.when(t == pl.num_programs(1) - 1)
    def _():
        conv_out_ref[0] = xr[tt - SUBLANES:, :]
        lru_out_ref[0] = hseq[tt - SUBLANES:, :]
        ret_out_ref[0] = s_scr[...]


def _const_spec(shape):
    nd = len(shape)
    return pl.BlockSpec(shape, lambda *_: (0,) * nd)


def _prompt_mixer_call(x, mod3, cos2, sin2, wts, tables):
    bsz, seq, _ = x.shape
    tt = PROMPT_TILE
    nt = seq // tt
    n_tok = bsz * seq
    tok_spec = pl.BlockSpec((tt, D_MODEL), lambda b, t: (b * nt + t, 0))
    in_specs = [
        pl.BlockSpec((1, tt, D_MODEL), lambda b, t: (b, t, 0)),
        pl.BlockSpec((1, 1, 6 * D_MODEL), lambda b, t: (b, 0, 0)),
        pl.BlockSpec((tt, LANES), lambda b, t: (t, 0)),
        pl.BlockSpec((tt, LANES), lambda b, t: (t, 0)),
    ] + [_const_spec(w.shape) for w in wts] + [_const_spec(tb.shape) for tb in tables]
    out_specs = [
        tok_spec,
        pl.BlockSpec((tt * SUBLANES, LANES), lambda b, t: (b * nt + t, 0)),
        pl.BlockSpec((SUBLANES, tt), lambda b, t: (0, b * nt + t)),
        pl.BlockSpec((1, SUBLANES, D_LRU), lambda b, t: (b, 0, 0)),
        pl.BlockSpec((1, SUBLANES, D_LRU), lambda b, t: (b, 0, 0)),
        pl.BlockSpec((1, N_HEADS, DK, DV), lambda b, t: (b, 0, 0, 0)),
        pl.BlockSpec((SUBLANES, LANES), lambda b, t: (0, 0)),
    ]
    out_shape = [
        jax.ShapeDtypeStruct((n_tok, D_MODEL), F32),
        jax.ShapeDtypeStruct((n_tok * SUBLANES, LANES), F32),
        jax.ShapeDtypeStruct((SUBLANES, n_tok), F32),
        jax.ShapeDtypeStruct((bsz, SUBLANES, D_LRU), F32),
        jax.ShapeDtypeStruct((bsz, SUBLANES, D_LRU), F32),
        jax.ShapeDtypeStruct((bsz, N_HEADS, DK, DV), F32),
        jax.ShapeDtypeStruct((SUBLANES, LANES), F32),
    ]
    return pl.pallas_call(
        _prompt_mixer_kernel,
        grid=(bsz, nt),
        in_specs=in_specs,
        out_specs=out_specs,
        out_shape=out_shape,
        scratch_shapes=[
            pltpu.VMEM((SUBLANES, D_LRU), F32),
            pltpu.VMEM((SUBLANES, D_LRU), F32),
            pltpu.VMEM((N_HEADS, DK, DV), F32),
            pltpu.VMEM((SUBLANES, LANES), F32),
        ],
        compiler_params=pltpu.CompilerParams(
            dimension_semantics=("arbitrary", "arbitrary"), vmem_limit_bytes=VMEM_LIMIT),
        name="prompt_mixer",
    )(x, mod3, cos2, sin2, *wts, *tables)


def _sample_mixer_kernel(x_ref, mod_ref, cos_ref, sin_ref, buf_ref, h0_ref, s0_ref, cnt_in_ref,
                         n_pre1_ref, n_post1_ref, n_pre2_ref,
                         w_in_ref, conv_w_ref, conv_b_ref, wg_ref, b_r_ref, b_i_ref, lam_ref,
                         gn_w_ref, w_out_ref, w_router_ref, b_router_ref,
                         smask_ref, wstate_ref, cross_ref, cdecay_ref, earlier_ref,
                         x1_ref, pk_ref, route_ref, xr_out_ref, h_out_ref, ret_out_ref, cnt_out_ref,
                         cnt_scr):
    bb, ts, _ = x_ref.shape
    m = bb * ts

    @pl.when(pl.program_id(0) == 0)
    def _():
        cnt_scr[...] = cnt_in_ref[...]

    x3 = x_ref[...]
    mod3 = mod_ref[...]
    z = _in_proj(x3, mod3, n_pre1_ref, w_in_ref)
    xr, yg, q, k, v, g = (z[:, c * D_LRU:(c + 1) * D_LRU] for c in range(D_IN_PROJ // D_LRU))
    xr_out_ref[...] = xr

    xr3 = xr.reshape(bb, ts, D_LRU)
    buf3 = buf_ref[...]
    tpos = lax.broadcasted_iota(jnp.int32, (bb, ts, D_LRU), 1)
    xc3 = jnp.broadcast_to(conv_b_ref[...].reshape(1, 1, D_LRU), (bb, ts, D_LRU))
    for j in range(CONV_W):
        back = CONV_W - 1 - j
        w_j = conv_w_ref[j:j + 1, :].reshape(1, 1, D_LRU)
        if back == 0:
            term = xr3
        else:
            cur = pltpu.roll(xr3, back, axis=1)
            up = CONV_W - 1 - back
            old = buf3 if up == 0 else pltpu.roll(buf3, ts - up, axis=1)
            term = jnp.where(tpos >= back, cur, old)
        xc3 = xc3 + term * w_j
    xc = xc3.reshape(m, D_LRU)

    a, b = _lru_coeffs(xc, wg_ref, b_r_ref, b_i_ref, lam_ref)
    a3 = a.reshape(bb, ts, D_LRU)
    b3 = b.reshape(bb, ts, D_LRU) + a3 * h0_ref[...]
    _, h3 = _group_scan(a3, b3)
    hseq = h3.reshape(m, D_LRU)
    h_out_ref[...] = hseq
    out_a = hseq * jax.nn.gelu(yg, approximate=True)

    cos2 = cos_ref[...].reshape(1, ts, LANES)
    sin2 = sin_ref[...].reshape(1, ts, LANES)
    scale = DK ** -0.5
    o_heads = []
    for h in range(N_HEADS):
        hs = slice(h * DK, (h + 1) * DK)
        q3 = (_rope(q[:, hs].reshape(bb, ts, DK), cos2, sin2, 2) * scale).astype(BF16)
        k3 = _rope(k[:, hs].reshape(bb, ts, DK), cos2, sin2, 2)
        v3 = v[:, hs].reshape(bb, ts, DV).astype(BF16)
        q2 = q3.reshape(m, DK)
        k2 = k3.reshape(m, DK).astype(BF16)
        v2 = v3.reshape(m, DV)
        scores = lax.dot_general(q2, k2, (((1,), (1,)), ((), ())),
                                 preferred_element_type=F32) * smask_ref[h]
        inner = jnp.dot(scores.astype(BF16), v2, preferred_element_type=F32)
        s0h = s0_ref[:, h]
        cross = jnp.einsum('bid,bde->bie', q3, s0h.astype(BF16), preferred_element_type=F32)
        cross = cross * cross_ref[:, hs].reshape(1, ts, DV)
        kw3 = (k3 * wstate_ref[:, hs].reshape(1, ts, DK)).astype(BF16)
        kv = jnp.einsum('bjd,bje->bde', kw3, v3, preferred_element_type=F32)
        ret_out_ref[:, h] = cdecay_ref[:, hs].reshape(1, 1, DV) * s0h + kv
        o_heads.append(_group_norm(inner + cross.reshape(m, DV)))
    o = jnp.concatenate(o_heads, axis=1)
    out_b = o * gn_w_ref[...] * _silu(g)

    _post_mixer(x3, mod3, out_a, out_b, w_out_ref, n_post1_ref, n_pre2_ref, w_router_ref, b_router_ref,
                earlier_ref, x1_ref, pk_ref, route_ref, cnt_scr)
    cnt_out_ref[...] = cnt_scr[...]


def _sample_mixer_call(x, mod3, cos2, sin2, buf8, h0p, s0, cnt_in, wts, tables):
    bsz, ts, _ = x.shape
    bb = SAMPLE_SEQS
    m = bb * ts
    n_tok = bsz * ts
    seq_spec = lambda w: pl.BlockSpec((bb, ts, w), lambda i: (i, 0, 0))
    tok_spec = lambda w: pl.BlockSpec((m, w), lambda i: (i, 0))
    in_specs = [
        seq_spec(D_MODEL),
        pl.BlockSpec((bb, 1, 6 * D_MODEL), lambda i: (i, 0, 0)),
        _const_spec(cos2.shape),
        _const_spec(sin2.shape),
        seq_spec(D_LRU),
        seq_spec(D_LRU),
        pl.BlockSpec((bb, N_HEADS, DK, DV), lambda i: (i, 0, 0, 0)),
        _const_spec(cnt_in.shape),
    ] + [_const_spec(w.shape) for w in wts] + [_const_spec(tb.shape) for tb in tables]
    out_specs = [
        tok_spec(D_MODEL),
        pl.BlockSpec((m * SUBLANES, LANES), lambda i: (i, 0)),
        pl.BlockSpec((SUBLANES, m), lambda i: (0, i)),
        tok_spec(D_LRU),
        tok_spec(D_LRU),
        pl.BlockSpec((bb, N_HEADS, DK, DV), lambda i: (i, 0, 0, 0)),
        _const_spec(cnt_in.shape),
    ]
    out_shape = [
        jax.ShapeDtypeStruct((n_tok, D_MODEL), F32),
        jax.ShapeDtypeStruct((n_tok * SUBLANES, LANES), F32),
        jax.ShapeDtypeStruct((SUBLANES, n_tok), F32),
        jax.ShapeDtypeStruct((n_tok, D_LRU), F32),
        jax.ShapeDtypeStruct((n_tok, D_LRU), F32),
        jax.ShapeDtypeStruct((bsz, N_HEADS, DK, DV), F32),
        jax.ShapeDtypeStruct(cnt_in.shape, F32),
    ]
    return pl.pallas_call(
        _sample_mixer_kernel,
        grid=(bsz // bb,),
        in_specs=in_specs,
        out_specs=out_specs,
        out_shape=out_shape,
        scratch_shapes=[pltpu.VMEM((SUBLANES, LANES), F32)],
        compiler_params=pltpu.CompilerParams(
            dimension_semantics=("arbitrary",), vmem_limit_bytes=VMEM_LIMIT),
        name="sample_mixer",
    )(x, mod3, cos2, sin2, buf8, h0p, s0, cnt_in, *wts, *tables)


def _slab(ref, r):
    return ref.at[pl.ds(pl.multiple_of(r * SUBLANES, SUBLANES), SUBLANES)]


def _dispatch_kernel(cls_ref, rank_ref, cnt_ref, padcnt_ref, start_ref, srcp_ref, srcs_ref, out_ref, ring, sem,
                     *, p_tiles):
    i = pl.program_id(0)
    n = pl.num_programs(0)
    tile_rows = ring.shape[1]
    td = tile_rows // SUBLANES
    moe_rows = MOE_TILE * SUBLANES

    def issue(slot):
        def body(g, carry):
            first = i * td + g * DMA_GROUP
            slots = [start_ref[cls_ref[first + j]] + rank_ref[first + j] for j in range(DMA_GROUP)]
            first = g * DMA_GROUP
            for j in range(DMA_GROUP):
                pltpu.make_async_copy(
                    _slab(ring.at[slot], first + j), _slab(out_ref, slots[j]), sem.at[slot]
                ).start(priority=j % 2)
            return carry
        lax.fori_loop(0, td // DMA_GROUP, body, 0)

    def wait_tile(slot):
        pltpu.make_async_copy(ring.at[slot], out_ref.at[pl.ds(0, tile_rows)], sem.at[slot]).wait()

    def zero_fill(slot):
        ring[slot] = jnp.zeros((tile_rows, LANES), F32)
        zero_src = ring.at[slot]

        def per_class(c, carry):
            lo = start_ref[c] + cnt_ref[c]
            hi = start_ref[c] + padcnt_ref[c]

            def fill(r, carry2):
                pltpu.make_async_copy(_slab(zero_src, 0), _slab(out_ref, r), sem.at[slot]).start()
                return carry2
            lax.fori_loop(lo, hi, fill, 0)

            def done(r, carry2):
                pltpu.make_async_copy(_slab(zero_src, 0), _slab(out_ref, 0), sem.at[slot]).wait()
                return carry2
            lax.fori_loop(lo, hi, done, 0)
            return carry
        lax.fori_loop(0, N_CLASSES, per_class, 0)

        used_tiles = (start_ref[N_CLASSES - 1] + padcnt_ref[N_CLASSES - 1]) // MOE_TILE
        all_tiles = out_ref.shape[0] // moe_rows

        def tile_copy(t):
            return pltpu.make_async_copy(
                zero_src.at[pl.ds(0, moe_rows)],
                out_ref.at[pl.ds(pl.multiple_of(t * moe_rows, moe_rows), moe_rows)], sem.at[slot])

        def fill_tile(t, carry):
            tile_copy(t).start()
            return carry
        lax.fori_loop(used_tiles, all_tiles, fill_tile, 0)

        def done_tile(t, carry):
            tile_copy(t).wait()
            return carry
        lax.fori_loop(used_tiles, all_tiles, done_tile, 0)

    for s in range(RING):
        @pl.when(i % RING == s)
        def _():
            @pl.when(i < p_tiles)
            def _():
                ring[s] = srcp_ref[...]

            @pl.when(i >= p_tiles)
            def _():
                ring[s] = srcs_ref[...]

            issue(s)

            @pl.when(i >= RING - 1)
            def _():
                wait_tile((s + 1) % RING)

            @pl.when(i == n - 1)
            def _():
                for back in range(RING - 2, -1, -1):
                    wait_tile((s - back) % RING)
                zero_fill(s)


def _dispatch_call(cls, rank, cnt, padcnt, start, pk_p, pk_s, n_rows):
    td = DISPATCH_TILE
    tile_rows = td * SUBLANES
    p_tiles = pk_p.shape[0] // tile_rows
    s_tiles = pk_s.shape[0] // tile_rows
    assert p_tiles + s_tiles >= RING and td >= MOE_TILE
    return pl.pallas_call(
        functools.partial(_dispatch_kernel, p_tiles=p_tiles),
        grid_spec=pltpu.PrefetchScalarGridSpec(
            num_scalar_prefetch=5,
            grid=(p_tiles + s_tiles,),
            in_specs=[
                pl.BlockSpec((tile_rows, LANES), lambda i, *_: (jnp.minimum(i, p_tiles - 1), 0)),
                pl.BlockSpec((tile_rows, LANES), lambda i, *_: (jnp.maximum(i - p_tiles, 0), 0)),
            ],
            out_specs=pl.BlockSpec(memory_space=pl.ANY),
            scratch_shapes=[pltpu.VMEM((RING, tile_rows, LANES), F32),
                            pltpu.SemaphoreType.DMA((RING,))],
        ),
        out_shape=jax.ShapeDtypeStruct((n_rows * SUBLANES, LANES), F32),
        compiler_params=pltpu.CompilerParams(
            dimension_semantics=("arbitrary",), has_side_effects=True, vmem_limit_bytes=VMEM_LIMIT),
        name="moe_dispatch",
    )(cls, rank, cnt, padcnt, start, pk_p, pk_s)


def _moe_kernel(ea_ref, eb_ref, valid_ref, xs_ref, w_router_ref, b_router_ref, *refs):
    f_ref = refs[-1]
    for s in range(MOE_TILES_PER_STEP):
        _moe_tile(pl.program_id(0) * MOE_TILES_PER_STEP + s, s * MOE_TILE * SUBLANES, ea_ref, eb_ref,
                  valid_ref, xs_ref, w_router_ref, b_router_ref, *refs[6 * s:6 * s + 6], f_ref)


def _moe_tile(t, row0, ea_ref, eb_ref, valid_ref, xs_ref, w_router_ref, b_router_ref,
              wga_ref, wua_ref, wda_ref, wgb_ref, wub_ref, wdb_ref, f_ref):
    tm = MOE_TILE

    @pl.when(valid_ref[t] == 1)
    def _():
        x = jnp.concatenate([xs_ref[pl.ds(row0 + j, tm, stride=SUBLANES), :] for j in range(ROW_GROUPS)],
                            axis=1).astype(BF16)
        e_a = ea_ref[t]
        e_b = eb_ref[t]
        e_lo = (e_a // PER_GROUP) * PER_GROUP
        logits = jnp.dot(x, w_router_ref[...], preferred_element_type=F32) + b_router_ref[...]
        lane = lax.broadcasted_iota(jnp.int32, (tm, LANES), 1)
        pick = lambda p, idx: jnp.sum(jnp.where(lane == idx, p, 0.0), axis=-1, keepdims=True)
        p_group = _masked_softmax(logits, (lane >= N_EXPERTS) & (lane < N_EXPERTS + N_GROUPS))
        p_g = pick(p_group, N_EXPERTS + e_a // PER_GROUP)
        p_e = _masked_softmax(logits, (lane >= e_lo) & (lane < e_lo + PER_GROUP))
        w_a = pick(p_e, e_a)
        w_b = pick(p_e, e_b)
        wsum = w_a + w_b

        def expert(wg_ref, wu_ref, gate):
            hg = jnp.dot(x, wg_ref[0].astype(BF16), preferred_element_type=F32)
            hu = jnp.dot(x, wu_ref[0].astype(BF16), preferred_element_type=F32)
            return (_silu(hg) * hu * gate).astype(BF16)

        ha = expert(wga_ref, wua_ref, p_g * (w_a / wsum))
        hb = expert(wgb_ref, wub_ref, p_g * (w_b / wsum))
        for c in range(D_MODEL // GATE_HALF):
            cols = slice(c * GATE_HALF, (c + 1) * GATE_HALF)
            f = (jnp.dot(ha, wda_ref[0, :, cols].astype(BF16), preferred_element_type=F32)
                 + jnp.dot(hb, wdb_ref[0, :, cols].astype(BF16), preferred_element_type=F32))
            for jj in range(GATE_HALF // LANES):
                j = c * (GATE_HALF // LANES) + jj
                f_ref[pl.ds(row0 + j, tm, stride=SUBLANES), :] = f[:, jj * LANES:(jj + 1) * LANES]

    @pl.when(valid_ref[t] == 0)
    def _():
        f_ref[pl.ds(row0, tm * SUBLANES), :] = jnp.zeros((tm * SUBLANES, LANES), F32)


def _moe_call(tile_ea, tile_eb, tile_valid, rows, w_router, b_router, wg, wu, wd):
    n_rows = rows.shape[0] // SUBLANES
    per_step = MOE_TILES_PER_STEP
    step_rows = MOE_TILE * per_step * SUBLANES
    assert n_rows % (MOE_TILE * per_step) == 0
    const = lambda a: pl.BlockSpec(a.shape, lambda t, ea, eb, v: (0,) * a.ndim)

    def expert_block(shape, table, s):
        return pl.BlockSpec(shape, lambda t, ea, eb, v: ((ea, eb)[table][t * per_step + s], 0, 0))

    weight_specs, weight_args = [], []
    for s in range(per_step):
        for table in range(2):
            weight_specs += [expert_block((1, D_MODEL, D_EXPERT), table, s),
                             expert_block((1, D_MODEL, D_EXPERT), table, s),
                             expert_block((1, D_EXPERT, D_MODEL), table, s)]
            weight_args += [wg, wu, wd]
    return pl.pallas_call(
        _moe_kernel,
        grid_spec=pltpu.PrefetchScalarGridSpec(
            num_scalar_prefetch=3,
            grid=(n_rows // (MOE_TILE * per_step),),
            in_specs=[
                pl.BlockSpec((step_rows, LANES), lambda t, ea, eb, v: (t, 0)),
                const(w_router), const(b_router),
            ] + weight_specs,
            out_specs=pl.BlockSpec((step_rows, LANES), lambda t, ea, eb, v: (t, 0)),
        ),
        out_shape=jax.ShapeDtypeStruct((n_rows * SUBLANES, LANES), F32),
        compiler_params=pltpu.CompilerParams(
            dimension_semantics=("arbitrary",), vmem_limit_bytes=VMEM_LIMIT),
        name="moe_experts",
    )(tile_ea, tile_eb, tile_valid, rows, w_router, b_router, *weight_args)


def _combine_kernel(cls_ref, rank_ref, start_ref, x1_ref, mod_ref, n_post2_ref, f_hbm, o_ref, fbuf, sem):
    i = pl.program_id(0)
    n = pl.num_programs(0)
    tf = x1_ref.shape[0]

    def issue(tile, slot):
        def body(g, carry):
            first = tile * tf + g * DMA_GROUP
            slots = [start_ref[cls_ref[first + j]] + rank_ref[first + j] for j in range(DMA_GROUP)]
            first = g * DMA_GROUP
            for j in range(DMA_GROUP):
                pltpu.make_async_copy(
                    _slab(f_hbm, slots[j]), _slab(fbuf.at[slot], first + j), sem.at[slot]
                ).start(priority=j % 2)
            return carry
        lax.fori_loop(0, tf // DMA_GROUP, body, 0)

    def finish(slot):
        pltpu.make_async_copy(f_hbm.at[pl.ds(0, tf * SUBLANES)], fbuf.at[slot], sem.at[slot]).wait()
        f = jnp.concatenate([fbuf[slot, pl.ds(j, tf, stride=SUBLANES), :] for j in range(D_MODEL // LANES)],
                            axis=1)
        mod3 = mod_ref[...]
        bb = mod3.shape[0]
        f3 = f.reshape(bb, tf // bb, D_MODEL)
        g2 = mod3[:, :, 5 * D_MODEL:6 * D_MODEL]
        x13 = x1_ref[...].reshape(f3.shape)
        out = x13 + (f3 * _rms_scale(f3)) * (g2 * n_post2_ref[...].reshape(1, 1, D_MODEL))
        o_ref[...] = out.reshape(o_ref.shape)

    @pl.when(i == 0)
    def _():
        issue(0, 0)

    for parity in range(2):
        @pl.when(i % 2 == parity)
        def _():
            @pl.when(i + 1 < n)
            def _():
                issue(i + 1, 1 - parity)
            finish(parity)


def _combine_call(cls, rank, start, x1, mod3, n_post2, f_sorted, seqs_per_tile):
    n_tok = x1.shape[0]
    tf = FINAL_TILE
    tok_per_seq = n_tok // mod3.shape[0]
    if seqs_per_tile > 1:
        mod_map = lambda i, *_: (i, 0, 0)
    else:
        mod_map = lambda i, *_: ((i * tf) // tok_per_seq, 0, 0)
    return pl.pallas_call(
        _combine_kernel,
        grid_spec=pltpu.PrefetchScalarGridSpec(
            num_scalar_prefetch=3,
            grid=(n_tok // tf,),
            in_specs=[
                pl.BlockSpec((tf, D_MODEL), lambda i, *_: (i, 0)),
                pl.BlockSpec((seqs_per_tile, 1, 6 * D_MODEL), mod_map),
                pl.BlockSpec((1, D_MODEL), lambda i, *_: (0, 0)),
                pl.BlockSpec(memory_space=pl.ANY),
            ],
            out_specs=pl.BlockSpec((tf, D_MODEL), lambda i, *_: (i, 0)),
            scratch_shapes=[pltpu.VMEM((2, tf * SUBLANES, LANES), F32), pltpu.SemaphoreType.DMA((2,))],
        ),
        out_shape=jax.ShapeDtypeStruct((n_tok, D_MODEL), F32),
        compiler_params=pltpu.CompilerParams(
            dimension_semantics=("arbitrary",), vmem_limit_bytes=VMEM_LIMIT),
        name="moe_combine",
    )(cls, rank, start, x1, mod3, n_post2, f_sorted)


def _block_diag_gate(w_r, w_i):
    per_half = GATE_HALF // LRU_BLOCK
    n_half = D_LRU // GATE_HALF
    eye = np.eye(per_half, dtype=np.float32)

    def block_diag(w):
        w4 = w.reshape(n_half, per_half, LRU_BLOCK, LRU_BLOCK)
        return jnp.einsum('hnij,nm->hnimj', w4, eye).reshape(n_half, GATE_HALF, GATE_HALF)

    return jnp.concatenate([block_diag(w_r), block_diag(w_i)], axis=-1).astype(BF16)


def _rope_tables(pos):
    half = DK // 2
    inv = np.float64(ROPE_BASE) ** (-np.arange(half, dtype=np.float64) / half)
    ang = np.asarray(pos, np.float64)[:, None] * inv[None, :]
    cos = np.cos(ang)
    sin = np.sin(ang)
    return (np.concatenate([cos, cos], axis=-1).astype(np.float32),
            np.concatenate([-sin, sin], axis=-1).astype(np.float32))


def _earlier_table(m):
    return jnp.asarray(np.tril(np.ones((m, m), np.float32), -1), BF16)


def _decay_tables(c):
    log_g = np.log1p(-np.exp2(-5.0 - np.arange(N_HEADS, dtype=np.float64)))
    idx = np.arange(c, dtype=np.float64)
    diff = idx[:, None] - idx[None, :]
    mask = np.where(diff[None] >= 0, np.exp(np.maximum(diff, 0.0)[None] * log_g[:, None, None]), 0.0)
    w_state = np.exp((c - 1.0 - idx)[None, :] * log_g[:, None])
    cross_decay = np.exp((idx + 1.0)[:, None] * log_g[None, :])
    chunk_decay = np.exp(c * log_g)
    wstate_full = np.repeat(w_state.T, DK, axis=1)
    cross_full = np.repeat(cross_decay, DV, axis=1)
    cdecay_full = np.repeat(chunk_decay, DV)[None, :]
    return tuple(t.astype(np.float32) for t in (mask, wstate_full, cross_full, cdecay_full))


def kernel(x_prompt, x_sample, state_conv, state_lru, state_ret, c_prompt, c_sample, w_mod, b_mod, norm_pre_mix, norm_post_mix, norm_pre_ffn, norm_post_ffn, w_in, conv_w, conv_b, w_rgate, b_rgate, w_igate, b_igate, lru_lambda, ret_gn_w, w_out, w_router_group, b_router_group, w_router_expert, b_router_expert, w_exp_gate, w_exp_up, w_exp_down):
    bp, tp, _ = x_prompt.shape
    bs, ts, _ = x_sample.shape
    l = 0

    mod = _mod_call(jnp.concatenate([c_prompt, c_sample], axis=0), w_mod[l], b_mod[l][None, :])
    mod_p = mod[:bp][:, None, :]
    mod_s = mod[bp:][:, None, :]

    unused = LANES - N_EXPERTS - N_GROUPS
    w_router = jnp.concatenate(
        [w_router_expert[l], w_router_group[l], jnp.zeros((D_MODEL, unused), F32)], axis=1).astype(BF16)
    b_router = jnp.concatenate(
        [b_router_expert[l], b_router_group[l], jnp.zeros((unused,), F32)])[None, :]

    row = lambda vec: vec.reshape(1, -1)
    wts = (row(norm_pre_mix[l]), row(norm_post_mix[l]), row(norm_pre_ffn[l]),
           w_in[l].astype(BF16), conv_w[l], row(conv_b[l]),
           _block_diag_gate(w_rgate[l], w_igate[l]),
           row(b_rgate[l]), row(b_igate[l]), row(lru_lambda[l]), row(ret_gn_w[l]),
           w_out[l].astype(BF16), w_router, b_router)

    cos_p, sin_p = _rope_tables(np.arange(tp))
    x1_p, pk_p, route_p, conv_p8, lru_p8, ret_p, cnt_p = _prompt_mixer_call(
        x_prompt, mod_p, cos_p, sin_p, wts,
        _decay_tables(math.gcd(tp, RET_CHUNK)) + (_earlier_table(PROMPT_TILE),))

    cos_s, sin_s = _rope_tables(PAST_LEN + np.arange(ts))
    mask8, wstate_s, cross_s, cdecay_s = _decay_tables(math.gcd(ts, RET_CHUNK))
    eye = np.eye(SAMPLE_SEQS, dtype=np.float32)
    smask = np.stack([np.kron(eye, mask8[h]) for h in range(N_HEADS)])
    buf8 = jnp.pad(state_conv[l], ((0, 0), (0, ts - (CONV_W - 1)), (0, 0)))
    h0p = jnp.pad(state_lru[l][:, None, :], ((0, 0), (0, ts - 1), (0, 0)))
    x1_s, pk_s, route_s, xr_s, h_s, ret_s, cnt_all = _sample_mixer_call(
        x_sample, mod_s, cos_s, sin_s, buf8, h0p, state_ret[l], cnt_p, wts,
        (smask, wstate_s, cross_s, cdecay_s, _earlier_table(SAMPLE_SEQS * ts)))

    n_p = bp * tp
    n_tok = n_p + bs * ts
    tm = MOE_TILE
    max_tiles = n_tok // tm + N_CLASSES
    cls_p, rank_p = route_p[0].astype(jnp.int32), route_p[1].astype(jnp.int32)
    cls_s, rank_s = route_s[0].astype(jnp.int32), route_s[1].astype(jnp.int32)
    cls = jnp.concatenate([cls_p, cls_s])
    rank = jnp.concatenate([rank_p, rank_s])
    cnt = cnt_all[0, :N_CLASSES].astype(jnp.int32)
    ntile = (cnt + (tm - 1)) // tm
    padcnt = ntile * tm
    before = np.tril(np.ones((N_CLASSES, N_CLASSES), np.int32), -1)
    upto = np.tril(np.ones((N_CLASSES, N_CLASSES), np.int32))
    start = jnp.sum(before * padcnt[None, :], axis=1)
    tile_end = jnp.sum(upto * ntile[None, :], axis=1)
    n_used = jnp.sum(ntile)
    tile_ids = np.arange(max_tiles, dtype=np.int32)
    tile_valid = (tile_ids < n_used).astype(jnp.int32)
    last_used = jnp.minimum(tile_ids, n_used - 1)
    tile_cls = jnp.sum((last_used[:, None] >= tile_end[None, :]).astype(jnp.int32), axis=1)
    pair = tile_cls % N_PAIRS
    first_expert = jnp.where(pair == 5, 2, jnp.where((pair == 2) | (pair == 3), 1, 0))
    second_expert = jnp.where(pair == 0, 1, jnp.where(pair <= 2, 2, 3))
    tile_ea = (tile_cls // N_PAIRS) * PER_GROUP + first_expert
    tile_eb = (tile_cls // N_PAIRS) * PER_GROUP + second_expert

    n_post2 = row(norm_post_ffn[l])
    wg = w_exp_gate[l]
    wu = w_exp_up[l]
    wd = w_exp_down[l]
    rows = _dispatch_call(cls, rank, cnt, padcnt, start, pk_p, pk_s, max_tiles * tm)
    f_sorted = _moe_call(tile_ea, tile_eb, tile_valid, rows, w_router, b_router, wg, wu, wd)
    y_p = _combine_call(cls_p, rank_p, start, x1_p, mod_p, n_post2, f_sorted, 1)
    y_s = _combine_call(cls_s, rank_s, start, x1_s, mod_s, n_post2, f_sorted, FINAL_TILE // ts)

    conv_p = conv_p8[:, SUBLANES - (CONV_W - 1):, :]
    lru_p = lru_p8[:, SUBLANES - 1, :]
    xr_s3 = xr_s.reshape(bs, ts, D_LRU)
    conv_s = xr_s3[:, ts - (CONV_W - 1):, :]
    lru_s = h_s.reshape(bs, ts, D_LRU)[:, ts - 1, :]
    return (y_p.reshape(bp, tp, D_MODEL), y_s.reshape(bs, ts, D_MODEL),
            conv_p[None], lru_p[None], ret_p[None],
            conv_s[None], lru_s[None], ret_s[None])
```

```python
import functools
import math

import jax
import jax.numpy as jnp
import numpy as np
from jax import lax
from jax.experimental import pallas as pl
from jax.experimental.pallas import tpu as pltpu

F32 = jnp.float32
BF16 = jnp.bfloat16

D_MODEL = 1024
D_LRU = 512
D_RET = 512
N_LRU_BLOCKS = 8
LRU_BLOCK = D_LRU // N_LRU_BLOCKS
CONV_W = 4
LRU_C = 8.0
N_HEADS = 4
DK = 128
DV = 128
RET_CHUNK = 128
ROPE_BASE = 10000.0
D_IN_PROJ = 3072
N_GROUPS = 4
PER_GROUP = 4
N_EXPERTS = 16
D_EXPERT = 256
EXPM1_DIRECT_BELOW = -0.5
NORM_EPS = 1e-6
GN_EPS = 1e-5
PAST_LEN = 16384

SUBLANES = 8
LANES = 128
GATE_HALF = 256
VMEM_LIMIT = 56 * 1024 * 1024

PROMPT_TILE = 512
SAMPLE_SEQS = 16
MOE_TILE = 256
MOE_TILES_PER_STEP = 2
FINAL_TILE = 1024
DISPATCH_TILE = 1024
RING = 3
DMA_GROUP = 8

N_PAIRS = 6
N_CLASSES = N_GROUPS * N_PAIRS
ROW_GROUPS = D_MODEL // LANES


def _silu(x):
    return x * jax.nn.sigmoid(x)


def _rms_scale(x):
    return lax.rsqrt(jnp.mean(x * x, axis=-1, keepdims=True) + NORM_EPS)


def _masked_softmax(logits, mask):
    top = jnp.max(jnp.where(mask, logits, -jnp.inf), axis=-1, keepdims=True)
    e = jnp.where(mask, jnp.exp(logits - top), 0.0)
    return e / jnp.sum(e, axis=-1, keepdims=True)


def _mod_kernel(c_ref, w_ref, b_ref, o_ref):
    s = _silu(c_ref[...]).astype(BF16)
    o_ref[...] = jnp.dot(s, w_ref[...].astype(BF16), preferred_element_type=F32) + b_ref[...]


def _mod_call(c_all, w_mod, b_mod):
    rows = c_all.shape[0]
    ncol = w_mod.shape[1]
    blk = D_MODEL
    return pl.pallas_call(
        _mod_kernel,
        grid=(ncol // blk,),
        in_specs=[
            pl.BlockSpec((rows, D_MODEL), lambda j: (0, 0)),
            pl.BlockSpec((D_MODEL, blk), lambda j: (0, j)),
            pl.BlockSpec((1, blk), lambda j: (0, j)),
        ],
        out_specs=pl.BlockSpec((rows, blk), lambda j: (0, j)),
        out_shape=jax.ShapeDtypeStruct((rows, ncol), F32),
        compiler_params=pltpu.CompilerParams(
            dimension_semantics=("arbitrary",), vmem_limit_bytes=VMEM_LIMIT),
        name="mod",
    )(c_all, w_mod, b_mod)


def _in_proj(x3, mod3, n_pre1_ref, w_in_ref):
    bb, tt, _ = x3.shape
    sh1 = mod3[:, :, 0:D_MODEL]
    sc1 = mod3[:, :, D_MODEL:2 * D_MODEL]
    coef = n_pre1_ref[...].reshape(1, 1, D_MODEL) * (1.0 + sc1)
    u = (x3 * _rms_scale(x3)) * coef + sh1
    u2d = u.reshape(bb * tt, D_MODEL).astype(BF16)
    return jnp.dot(u2d, w_in_ref[...], preferred_element_type=F32)


def _lru_coeffs(xc, wg_ref, b_r_ref, b_i_ref, lam_ref):
    xcb = xc.astype(BF16)
    g0 = jnp.dot(xcb[:, :GATE_HALF], wg_ref[0], preferred_element_type=F32)
    g1 = jnp.dot(xcb[:, GATE_HALF:], wg_ref[1], preferred_element_type=F32)
    r = jax.nn.sigmoid(jnp.concatenate([g0[:, :GATE_HALF], g1[:, :GATE_HALF]], axis=1) + b_r_ref[...])
    i = jax.nn.sigmoid(jnp.concatenate([g0[:, GATE_HALF:], g1[:, GATE_HALF:]], axis=1) + b_i_ref[...])
    lam = lam_ref[...]
    sp = jnp.maximum(-lam, 0.0) + jnp.log1p(jnp.exp(-jnp.abs(lam)))
    log_a = -LRU_C * r * sp
    a = jnp.exp(log_a)
    y = 2.0 * log_a
    a2 = a * a
    d = a2 - 1.0
    small = d * y / jnp.log(a2)
    em1 = jnp.where(y < EXPM1_DIRECT_BELOW, d, jnp.where(d == 0.0, y, small))
    gain = jnp.sqrt(-em1)
    return a, gain * (i * xc)


def _rope(xh, cos2, sin2, lane_axis):
    return xh * cos2 + pltpu.roll(xh, DK // 2, axis=lane_axis) * sin2


def _group_norm(o):
    mu = jnp.mean(o, axis=-1, keepdims=True)
    d = o - mu
    var = jnp.mean(d * d, axis=-1, keepdims=True)
    return d * lax.rsqrt(var + GN_EPS)


def _post_mixer(x3, mod3, out_a, out_b, w_out_ref, n_post1_ref, n_pre2_ref, w_router_ref, b_router_ref,
                earlier_ref, x1_ref, pk_ref, route_ref, cnt_scr):
    bb, tt, _ = x3.shape
    m = bb * tt
    y = (jnp.dot(out_a.astype(BF16), w_out_ref[0:D_LRU, :], preferred_element_type=F32)
         + jnp.dot(out_b.astype(BF16), w_out_ref[D_LRU:, :], preferred_element_type=F32))
    g1 = mod3[:, :, 2 * D_MODEL:3 * D_MODEL]
    sh2 = mod3[:, :, 3 * D_MODEL:4 * D_MODEL]
    sc2 = mod3[:, :, 4 * D_MODEL:5 * D_MODEL]
    y3 = y.reshape(bb, tt, D_MODEL)
    x1 = x3 + (y3 * _rms_scale(y3)) * (g1 * n_post1_ref[...].reshape(1, 1, D_MODEL))
    u2 = (x1 * _rms_scale(x1)) * (n_pre2_ref[...].reshape(1, 1, D_MODEL) * (1.0 + sc2)) + sh2
    x1_ref[...] = x1.reshape(m, D_MODEL)
    u2f = u2.reshape(m, D_MODEL)
    for j in range(ROW_GROUPS):
        pk_ref[pl.ds(j, m, stride=SUBLANES), :] = u2f[:, j * LANES:(j + 1) * LANES]
    u2b = u2f.astype(BF16)

    logits = jnp.dot(u2b, w_router_ref[...], preferred_element_type=F32) + b_router_ref[...]
    lane = lax.broadcasted_iota(jnp.int32, (m, LANES), 1)
    lane_f = lane.astype(F32)
    is_g = (lane >= N_EXPERTS) & (lane < N_EXPERTS + N_GROUPS)
    p_group = _masked_softmax(logits, is_g)
    p_g = jnp.max(p_group, axis=-1, keepdims=True)
    g_lane = jnp.min(jnp.where(is_g & (p_group == p_g), lane_f, float(LANES)), axis=-1, keepdims=True)
    e_lo = (g_lane - N_EXPERTS) * PER_GROUP
    in_g = (lane_f >= e_lo) & (lane_f < e_lo + PER_GROUP)
    p_e = _masked_softmax(logits, in_g)
    pm = jnp.where(in_g, p_e, -1.0)
    w1 = jnp.max(pm, axis=-1, keepdims=True)
    i1 = jnp.min(jnp.where(pm == w1, lane_f, float(LANES)), axis=-1, keepdims=True)
    pm2 = jnp.where(lane_f == i1, -1.0, pm)
    w2 = jnp.max(pm2, axis=-1, keepdims=True)
    i2 = jnp.min(jnp.where(pm2 == w2, lane_f, float(LANES)), axis=-1, keepdims=True)
    a = jnp.minimum(i1, i2) - e_lo
    b = jnp.maximum(i1, i2) - e_lo
    pair = jnp.where(a == 0.0, jnp.where(b == 3.0, 4.0, b - 1.0), jnp.where(a == 1.0, b, 5.0))
    cls = (g_lane - N_EXPERTS) * N_PAIRS + pair
    onehot = lane_f == cls
    prefix = jnp.dot(earlier_ref[...], jnp.where(onehot, 1.0, 0.0).astype(BF16), preferred_element_type=F32)
    run = cnt_scr[0:1, :]
    rank = jnp.sum(jnp.where(onehot, prefix + run, 0.0), axis=-1, keepdims=True)
    cnt_scr[...] = jnp.broadcast_to(
        run + jnp.sum(jnp.where(onehot, 1.0, 0.0), axis=0, keepdims=True), cnt_scr.shape)
    route = jnp.where(lane == 0, cls, jnp.where(lane == 1, rank, 0.0))
    route_ref[...] = jnp.transpose(route)[0:SUBLANES, :]


def _group_scan(a3, b3):
    tpos = lax.broadcasted_iota(jnp.int32, a3.shape, 1)
    s = 1
    while s < a3.shape[1]:
        keep = tpos >= s
        a_sh = jnp.where(keep, pltpu.roll(a3, s, axis=1), 1.0)
        b_sh = jnp.where(keep, pltpu.roll(b3, s, axis=1), 0.0)
        b3 = a3 * b_sh + b3
        a3 = a3 * a_sh
        s *= 2
    return a3, b3


def _scan_rows(a, b, h0):
    n, c = a.shape
    groups = n // SUBLANES
    a3, b3 = _group_scan(a.reshape(groups, SUBLANES, c), b.reshape(groups, SUBLANES, c))
    carry = h0
    out = []
    for g in range(groups):
        hg = b3[g] + a3[g] * carry
        out.append(hg)
        carry = hg[SUBLANES - 1:SUBLANES, :]
    return jnp.concatenate(out, axis=0)


def _prompt_mixer_kernel(x_ref, mod_ref, cos_ref, sin_ref,
                         n_pre1_ref, n_post1_ref, n_pre2_ref,
                         w_in_ref, conv_w_ref, conv_b_ref, wg_ref, b_r_ref, b_i_ref, lam_ref,
                         gn_w_ref, w_out_ref, w_router_ref, b_router_ref,
                         mask_ref, wstate_ref, cross_ref, cdecay_ref, earlier_ref,
                         x1_ref, pk_ref, route_ref, conv_out_ref, lru_out_ref, ret_out_ref, cnt_out_ref,
                         conv_scr, h_scr, s_scr, cnt_scr):
    t = pl.program_id(1)
    tt = x_ref.shape[1]

    @pl.when((pl.program_id(0) == 0) & (t == 0))
    def _():
        cnt_scr[...] = jnp.zeros_like(cnt_scr)

    @pl.when(t == 0)
    def _():
        conv_scr[...] = jnp.zeros_like(conv_scr)
        h_scr[...] = jnp.zeros_like(h_scr)
        s_scr[...] = jnp.zeros_like(s_scr)

    x3 = x_ref[...]
    mod3 = mod_ref[...]
    z = _in_proj(x3, mod3, n_pre1_ref, w_in_ref)
    xr, yg, q, k, v, g = (z[:, c * D_LRU:(c + 1) * D_LRU] for c in range(D_IN_PROJ // D_LRU))

    groups = tt // SUBLANES
    xr3 = xr.reshape(groups, SUBLANES, D_LRU)
    tpos = lax.broadcasted_iota(jnp.int32, xr3.shape, 1)
    tail = conv_scr[...]
    xc3 = jnp.broadcast_to(conv_b_ref[...].reshape(1, 1, D_LRU), xr3.shape)
    for j in range(CONV_W):
        back = CONV_W - 1 - j
        w_j = conv_w_ref[j:j + 1, :].reshape(1, 1, D_LRU)
        if back == 0:
            term = xr3
        else:
            cur = pltpu.roll(xr3, back, axis=1)
            first = pltpu.roll(tail, back, axis=0).reshape(1, SUBLANES, D_LRU)
            prev = jnp.concatenate([first, cur[:groups - 1]], axis=0)
            term = jnp.where(tpos >= back, cur, prev)
        xc3 = xc3 + term * w_j
    xc = xc3.reshape(tt, D_LRU)
    conv_scr[...] = xr[tt - SUBLANES:, :]

    a, b = _lru_coeffs(xc, wg_ref, b_r_ref, b_i_ref, lam_ref)
    hseq = _scan_rows(a, b, h_scr[0:1, :])
    h_scr[...] = jnp.broadcast_to(hseq[tt - 1:tt, :], h_scr.shape)
    out_a = hseq * jax.nn.gelu(yg, approximate=True)

    cos2 = cos_ref[...]
    sin2 = sin_ref[...]
    scale = DK ** -0.5
    o_heads = []
    for h in range(N_HEADS):
        hs = slice(h * DK, (h + 1) * DK)
        qh = (_rope(q[:, hs], cos2, sin2, 1) * scale).astype(BF16)
        kh = _rope(k[:, hs], cos2, sin2, 1)
        vh = v[:, hs].astype(BF16)
        o_chunks = []
        for c in range(tt // RET_CHUNK):
            cs = slice(c * RET_CHUNK, (c + 1) * RET_CHUNK)
            qc = qh[cs]
            kc = kh[cs]
            vc = vh[cs]
            s_prev = s_scr[h]
            scores = lax.dot_general(qc, kc.astype(BF16), (((1,), (1,)), ((), ())),
                                     preferred_element_type=F32) * mask_ref[h]
            inner = jnp.dot(scores.astype(BF16), vc, preferred_element_type=F32)
            cross = jnp.dot(qc, s_prev.astype(BF16), preferred_element_type=F32) * cross_ref[:, hs]
            kw = (kc * wstate_ref[:, hs]).astype(BF16)
            kv = lax.dot_general(kw, vc, (((0,), (0,)), ((), ())), preferred_element_type=F32)
            s_scr[h] = cdecay_ref[:, hs] * s_prev + kv
            o_chunks.append(inner + cross)
        o_heads.append(_group_norm(jnp.concatenate(o_chunks, axis=0)))
    o = jnp.concatenate(o_heads, axis=1)
    out_b = o * gn_w_ref[...] * _silu(g)

    _post_mixer(x3, mod3, out_a, out_b, w_out_ref, n_post1_ref, n_pre2_ref, w_router_ref, b_router_ref,
                earlier_ref, x1_ref, pk_ref, route_ref, cnt_scr)
    cnt_out_ref[...] = cnt_scr[...]

    @pl.when(t == pl.num_programs(1) - 1)
    def _():
        conv_out_ref[0] = xr[tt - SUBLANES:, :]
        lru_out_ref[0] = hseq[tt - SUBLANES:, :]
        ret_out_ref[0] = s_scr[...]


def _const_spec(shape):
    nd = len(shape)
    return pl.BlockSpec(shape, lambda *_: (0,) * nd)


def _prompt_mixer_call(x, mod3, cos2, sin2, wts, tables):
    bsz, seq, _ = x.shape
    tt = PROMPT_TILE
    nt = seq // tt
    n_tok = bsz * seq
    tok_spec = pl.BlockSpec((tt, D_MODEL), lambda b, t: (b * nt + t, 0))
    in_specs = [
        pl.BlockSpec((1, tt, D_MODEL), lambda b, t: (b, t, 0)),
        pl.BlockSpec((1, 1, 6 * D_MODEL), lambda b, t: (b, 0, 0)),
        pl.BlockSpec((tt, LANES), lambda b, t: (t, 0)),
        pl.BlockSpec((tt, LANES), lambda b, t: (t, 0)),
    ] + [_const_spec(w.shape) for w in wts] + [_const_spec(tb.shape) for tb in tables]
    out_specs = [
        tok_spec,
        pl.BlockSpec((tt * SUBLANES, LANES), lambda b, t: (b * nt + t, 0)),
        pl.BlockSpec((SUBLANES, tt), lambda b, t: (0, b * nt + t)),
        pl.BlockSpec((1, SUBLANES, D_LRU), lambda b, t: (b, 0, 0)),
        pl.BlockSpec((1, SUBLANES, D_LRU), lambda b, t: (b, 0, 0)),
        pl.BlockSpec((1, N_HEADS, DK, DV), lambda b, t: (b, 0, 0, 0)),
        pl.BlockSpec((SUBLANES, LANES), lambda b, t: (0, 0)),
    ]
    out_shape = [
        jax.ShapeDtypeStruct((n_tok, D_MODEL), F32),
        jax.ShapeDtypeStruct((n_tok * SUBLANES, LANES), F32),
        jax.ShapeDtypeStruct((SUBLANES, n_tok), F32),
        jax.ShapeDtypeStruct((bsz, SUBLANES, D_LRU), F32),
        jax.ShapeDtypeStruct((bsz, SUBLANES, D_LRU), F32),
        jax.ShapeDtypeStruct((bsz, N_HEADS, DK, DV), F32),
        jax.ShapeDtypeStruct((SUBLANES, LANES), F32),
    ]
    return pl.pallas_call(
        _prompt_mixer_kernel,
        grid=(bsz, nt),
        in_specs=in_specs,
        out_specs=out_specs,
        out_shape=out_shape,
        scratch_shapes=[
            pltpu.VMEM((SUBLANES, D_LRU), F32),
            pltpu.VMEM((SUBLANES, D_LRU), F32),
            pltpu.VMEM((N_HEADS, DK, DV), F32),
            pltpu.VMEM((SUBLANES, LANES), F32),
        ],
        compiler_params=pltpu.CompilerParams(
            dimension_semantics=("arbitrary", "arbitrary"), vmem_limit_bytes=VMEM_LIMIT),
        name="prompt_mixer",
    )(x, mod3, cos2, sin2, *wts, *tables)


def _sample_mixer_kernel(x_ref, mod_ref, cos_ref, sin_ref, buf_ref, h0_ref, s0_ref, cnt_in_ref,
                         n_pre1_ref, n_post1_ref, n_pre2_ref,
                         w_in_ref, conv_w_ref, conv_b_ref, wg_ref, b_r_ref, b_i_ref, lam_ref,
                         gn_w_ref, w_out_ref, w_router_ref, b_router_ref,
                         smask_ref, wstate_ref, cross_ref, cdecay_ref, earlier_ref,
                         x1_ref, pk_ref, route_ref, xr_out_ref, h_out_ref, ret_out_ref, cnt_out_ref,
                         cnt_scr):
    bb, ts, _ = x_ref.shape
    m = bb * ts

    @pl.when(pl.program_id(0) == 0)
    def _():
        cnt_scr[...] = cnt_in_ref[...]

    x3 = x_ref[...]
    mod3 = mod_ref[...]
    z = _in_proj(x3, mod3, n_pre1_ref, w_in_ref)
    xr, yg, q, k, v, g = (z[:, c * D_LRU:(c + 1) * D_LRU] for c in range(D_IN_PROJ // D_LRU))
    xr_out_ref[...] = xr

    xr3 = xr.reshape(bb, ts, D_LRU)
    buf3 = buf_ref[...]
    tpos = lax.broadcasted_iota(jnp.int32, (bb, ts, D_LRU), 1)
    xc3 = jnp.broadcast_to(conv_b_ref[...].reshape(1, 1, D_LRU), (bb, ts, D_LRU))
    for j in range(CONV_W):
        back = CONV_W - 1 - j
        w_j = conv_w_ref[j:j + 1, :].reshape(1, 1, D_LRU)
        if back == 0:
            term = xr3
        else:
            cur = pltpu.roll(xr3, back, axis=1)
            up = CONV_W - 1 - back
            old = buf3 if up == 0 else pltpu.roll(buf3, ts - up, axis=1)
            term = jnp.where(tpos >= back, cur, old)
        xc3 = xc3 + term * w_j
    xc = xc3.reshape(m, D_LRU)

    a, b = _lru_coeffs(xc, wg_ref, b_r_ref, b_i_ref, lam_ref)
    a3 = a.reshape(bb, ts, D_LRU)
    b3 = b.reshape(bb, ts, D_LRU) + a3 * h0_ref[...]
    _, h3 = _group_scan(a3, b3)
    hseq = h3.reshape(m, D_LRU)
    h_out_ref[...] = hseq
    out_a = hseq * jax.nn.gelu(yg, approximate=True)

    cos2 = cos_ref[...].reshape(1, ts, LANES)
    sin2 = sin_ref[...].reshape(1, ts, LANES)
    scale = DK ** -0.5
    o_heads = []
    for h in range(N_HEADS):
        hs = slice(h * DK, (h + 1) * DK)
        q3 = (_rope(q[:, hs].reshape(bb, ts, DK), cos2, sin2, 2) * scale).astype(BF16)
        k3 = _rope(k[:, hs].reshape(bb, ts, DK), cos2, sin2, 2)
        v3 = v[:, hs].reshape(bb, ts, DV).astype(BF16)
        q2 = q3.reshape(m, DK)
        k2 = k3.reshape(m, DK).astype(BF16)
        v2 = v3.reshape(m, DV)
        scores = lax.dot_general(q2, k2, (((1,), (1,)), ((), ())),
                                 preferred_element_type=F32) * smask_ref[h]
        inner = jnp.dot(scores.astype(BF16), v2, preferred_element_type=F32)
        s0h = s0_ref[:, h]
        cross = jnp.einsum('bid,bde->bie', q3, s0h.astype(BF16), preferred_element_type=F32)
        cross = cross * cross_ref[:, hs].reshape(1, ts, DV)
        kw3 = (k3 * wstate_ref[:, hs].reshape(1, ts, DK)).astype(BF16)
        kv = jnp.einsum('bjd,bje->bde', kw3, v3, preferred_element_type=F32)
        ret_out_ref[:, h] = cdecay_ref[:, hs].reshape(1, 1, DV) * s0h + kv
        o_heads.append(_group_norm(inner + cross.reshape(m, DV)))
    o = jnp.concatenate(o_heads, axis=1)
    out_b = o * gn_w_ref[...] * _silu(g)

    _post_mixer(x3, mod3, out_a, out_b, w_out_ref, n_post1_ref, n_pre2_ref, w_router_ref, b_router_ref,
                earlier_ref, x1_ref, pk_ref, route_ref, cnt_scr)
    cnt_out_ref[...] = cnt_scr[...]


def _sample_mixer_call(x, mod3, cos2, sin2, buf8, h0p, s0, cnt_in, wts, tables):
    bsz, ts, _ = x.shape
    bb = SAMPLE_SEQS
    m = bb * ts
    n_tok = bsz * ts
    seq_spec = lambda w: pl.BlockSpec((bb, ts, w), lambda i: (i, 0, 0))
    tok_spec = lambda w: pl.BlockSpec((m, w), lambda i: (i, 0))
    in_specs = [
        seq_spec(D_MODEL),
        pl.BlockSpec((bb, 1, 6 * D_MODEL), lambda i: (i, 0, 0)),
        _const_spec(cos2.shape),
        _const_spec(sin2.shape),
        seq_spec(D_LRU),
        seq_spec(D_LRU),
        pl.BlockSpec((bb, N_HEADS, DK, DV), lambda i: (i, 0, 0, 0)),
        _const_spec(cnt_in.shape),
    ] + [_const_spec(w.shape) for w in wts] + [_const_spec(tb.shape) for tb in tables]
    out_specs = [
        tok_spec(D_MODEL),
        pl.BlockSpec((m * SUBLANES, LANES), lambda i: (i, 0)),
        pl.BlockSpec((SUBLANES, m), lambda i: (0, i)),
        tok_spec(D_LRU),
        tok_spec(D_LRU),
        pl.BlockSpec((bb, N_HEADS, DK, DV), lambda i: (i, 0, 0, 0)),
        _const_spec(cnt_in.shape),
    ]
    out_shape = [
        jax.ShapeDtypeStruct((n_tok, D_MODEL), F32),
        jax.ShapeDtypeStruct((n_tok * SUBLANES, LANES), F32),
        jax.ShapeDtypeStruct((SUBLANES, n_tok), F32),
        jax.ShapeDtypeStruct((n_tok, D_LRU), F32),
        jax.ShapeDtypeStruct((n_tok, D_LRU), F32),
        jax.ShapeDtypeStruct((bsz, N_HEADS, DK, DV), F32),
        jax.ShapeDtypeStruct(cnt_in.shape, F32),
    ]
    return pl.pallas_call(
        _sample_mixer_kernel,
        grid=(bsz // bb,),
        in_specs=in_specs,
        out_specs=out_specs,
        out_shape=out_shape,
        scratch_shapes=[pltpu.VMEM((SUBLANES, LANES), F32)],
        compiler_params=pltpu.CompilerParams(
            dimension_semantics=("arbitrary",), vmem_limit_bytes=VMEM_LIMIT),
        name="sample_mixer",
    )(x, mod3, cos2, sin2, buf8, h0p, s0, cnt_in, *wts, *tables)


def _slab(ref, r):
    return ref.at[pl.ds(pl.multiple_of(r * SUBLANES, SUBLANES), SUBLANES)]


def _dispatch_kernel(cls_ref, rank_ref, cnt_ref, padcnt_ref, start_ref, srcp_ref, srcs_ref, out_ref, ring, sem,
                     *, p_tiles):
    i = pl.program_id(0)
    n = pl.num_programs(0)
    tile_rows = ring.shape[1]
    td = tile_rows // SUBLANES
    moe_rows = MOE_TILE * SUBLANES

    def issue(slot):
        def body(g, carry):
            first = i * td + g * DMA_GROUP
            slots = [start_ref[cls_ref[first + j]] + rank_ref[first + j] for j in range(DMA_GROUP)]
            first = g * DMA_GROUP
            for j in range(DMA_GROUP):
                pltpu.make_async_copy(
                    _slab(ring.at[slot], first + j), _slab(out_ref, slots[j]), sem.at[slot]
                ).start(priority=j % 2)
            return carry
        lax.fori_loop(0, td // DMA_GROUP, body, 0)

    def wait_tile(slot):
        pltpu.make_async_copy(ring.at[slot], out_ref.at[pl.ds(0, tile_rows)], sem.at[slot]).wait()

    def zero_fill(slot):
        ring[slot] = jnp.zeros((tile_rows, LANES), F32)
        zero_src = ring.at[slot]

        def per_class(c, carry):
            lo = start_ref[c] + cnt_ref[c]
            hi = start_ref[c] + padcnt_ref[c]

            def fill(r, carry2):
                pltpu.make_async_copy(_slab(zero_src, 0), _slab(out_ref, r), sem.at[slot]).start()
                return carry2
            lax.fori_loop(lo, hi, fill, 0)

            def done(r, carry2):
                pltpu.make_async_copy(_slab(zero_src, 0), _slab(out_ref, 0), sem.at[slot]).wait()
                return carry2
            lax.fori_loop(lo, hi, done, 0)
            return carry
        lax.fori_loop(0, N_CLASSES, per_class, 0)

        used_tiles = (start_ref[N_CLASSES - 1] + padcnt_ref[N_CLASSES - 1]) // MOE_TILE
        all_tiles = out_ref.shape[0] // moe_rows

        def tile_copy(t):
            return pltpu.make_async_copy(
                zero_src.at[pl.ds(0, moe_rows)],
                out_ref.at[pl.ds(pl.multiple_of(t * moe_rows, moe_rows), moe_rows)], sem.at[slot])

        def fill_tile(t, carry):
            tile_copy(t).start()
            return carry
        lax.fori_loop(used_tiles, all_tiles, fill_tile, 0)

        def done_tile(t, carry):
            tile_copy(t).wait()
            return carry
        lax.fori_loop(used_tiles, all_tiles, done_tile, 0)

    for s in range(RING):
        @pl.when(i % RING == s)
        def _():
            @pl.when(i < p_tiles)
            def _():
                ring[s] = srcp_ref[...]

            @pl.when(i >= p_tiles)
            def _():
                ring[s] = srcs_ref[...]

            issue(s)

            @pl.when(i >= RING - 1)
            def _():
                wait_tile((s + 1) % RING)

            @pl.when(i == n - 1)
            def _():
                for back in range(RING - 2, -1, -1):
                    wait_tile((s - back) % RING)
                zero_fill(s)


def _dispatch_call(cls, rank, cnt, padcnt, start, pk_p, pk_s, n_rows):
    td = DISPATCH_TILE
    tile_rows = td * SUBLANES
    p_tiles = pk_p.shape[0] // tile_rows
    s_tiles = pk_s.shape[0] // tile_rows
    assert p_tiles + s_tiles >= RING and td >= MOE_TILE
    return pl.pallas_call(
        functools.partial(_dispatch_kernel, p_tiles=p_tiles),
        grid_spec=pltpu.PrefetchScalarGridSpec(
            num_scalar_prefetch=5,
            grid=(p_tiles + s_tiles,),
            in_specs=[
                pl.BlockSpec((tile_rows, LANES), lambda i, *_: (jnp.minimum(i, p_tiles - 1), 0)),
                pl.BlockSpec((tile_rows, LANES), lambda i, *_: (jnp.maximum(i - p_tiles, 0), 0)),
            ],
            out_specs=pl.BlockSpec(memory_space=pl.ANY),
            scratch_shapes=[pltpu.VMEM((RING, tile_rows, LANES), F32),
                            pltpu.SemaphoreType.DMA((RING,))],
        ),
        out_shape=jax.ShapeDtypeStruct((n_rows * SUBLANES, LANES), F32),
        compiler_params=pltpu.CompilerParams(
            dimension_semantics=("arbitrary",), has_side_effects=True, vmem_limit_bytes=VMEM_LIMIT),
        name="moe_dispatch",
    )(cls, rank, cnt, padcnt, start, pk_p, pk_s)


def _moe_kernel(ea_ref, eb_ref, valid_ref, xs_ref, w_router_ref, b_router_ref, *refs):
    f_ref = refs[-1]
    for s in range(MOE_TILES_PER_STEP):
        _moe_tile(pl.program_id(0) * MOE_TILES_PER_STEP + s, s * MOE_TILE * SUBLANES, ea_ref, eb_ref,
                  valid_ref, xs_ref, w_router_ref, b_router_ref, *refs[6 * s:6 * s + 6], f_ref)


def _moe_tile(t, row0, ea_ref, eb_ref, valid_ref, xs_ref, w_router_ref, b_router_ref,
              wga_ref, wua_ref, wda_ref, wgb_ref, wub_ref, wdb_ref, f_ref):
    tm = MOE_TILE

    @pl.when(valid_ref[t] == 1)
    def _():
        x = jnp.concatenate([xs_ref[pl.ds(row0 + j, tm, stride=SUBLANES), :] for j in range(ROW_GROUPS)],
                            axis=1).astype(BF16)
        e_a = ea_ref[t]
        e_b = eb_ref[t]
        e_lo = (e_a // PER_GROUP) * PER_GROUP
        logits = jnp.dot(x, w_router_ref[...], preferred_element_type=F32) + b_router_ref[...]
        lane = lax.broadcasted_iota(jnp.int32, (tm, LANES), 1)
        pick = lambda p, idx: jnp.sum(jnp.where(lane == idx, p, 0.0), axis=-1, keepdims=True)
        p_group = _masked_softmax(logits, (lane >= N_EXPERTS) & (lane < N_EXPERTS + N_GROUPS))
        p_g = pick(p_group, N_EXPERTS + e_a // PER_GROUP)
        p_e = _masked_softmax(logits, (lane >= e_lo) & (lane < e_lo + PER_GROUP))
        w_a = pick(p_e, e_a)
        w_b = pick(p_e, e_b)
        wsum = w_a + w_b

        def expert(wg_ref, wu_ref, gate):
            hg = jnp.dot(x, wg_ref[0].astype(BF16), preferred_element_type=F32)
            hu = jnp.dot(x, wu_ref[0].astype(BF16), preferred_element_type=F32)
            return (_silu(hg) * hu * gate).astype(BF16)

        ha = expert(wga_ref, wua_ref, p_g * (w_a / wsum))
        hb = expert(wgb_ref, wub_ref, p_g * (w_b / wsum))
        for c in range(D_MODEL // GATE_HALF):
            cols = slice(c * GATE_HALF, (c + 1) * GATE_HALF)
            f = (jnp.dot(ha, wda_ref[0, :, cols].astype(BF16), preferred_element_type=F32)
                 + jnp.dot(hb, wdb_ref[0, :, cols].astype(BF16), preferred_element_type=F32))
            for jj in range(GATE_HALF // LANES):
                j = c * (GATE_HALF // LANES) + jj
                f_ref[pl.ds(row0 + j, tm, stride=SUBLANES), :] = f[:, jj * LANES:(jj + 1) * LANES]

    @pl.when(valid_ref[t] == 0)
    def _():
        f_ref[pl.ds(row0, tm * SUBLANES), :] = jnp.zeros((tm * SUBLANES, LANES), F32)


def _moe_call(tile_ea, tile_eb, tile_valid, rows, w_router, b_router, wg, wu, wd):
    n_rows = rows.shape[0] // SUBLANES
    per_step = MOE_TILES_PER_STEP
    step_rows = MOE_TILE * per_step * SUBLANES
    assert n_rows % (MOE_TILE * per_step) == 0
    const = lambda a: pl.BlockSpec(a.shape, lambda t, ea, eb, v: (0,) * a.ndim)

    def expert_block(shape, table, s):
        return pl.BlockSpec(shape, lambda t, ea, eb, v: ((ea, eb)[table][t * per_step + s], 0, 0))

    weight_specs, weight_args = [], []
    for s in range(per_step):
        for table in range(2):
            weight_specs += [expert_block((1, D_MODEL, D_EXPERT), table, s),
                             expert_block((1, D_MODEL, D_EXPERT), table, s),
                             expert_block((1, D_EXPERT, D_MODEL), table, s)]
            weight_args += [wg, wu, wd]
    return pl.pallas_call(
        _moe_kernel,
        grid_spec=pltpu.PrefetchScalarGridSpec(
            num_scalar_prefetch=3,
            grid=(n_rows // (MOE_TILE * per_step),),
            in_specs=[
                pl.BlockSpec((step_rows, LANES), lambda t, ea, eb, v: (t, 0)),
                const(w_router), const(b_router),
            ] + weight_specs,
            out_specs=pl.BlockSpec((step_rows, LANES), lambda t, ea, eb, v: (t, 0)),
        ),
        out_shape=jax.ShapeDtypeStruct((n_rows * SUBLANES, LANES), F32),
        compiler_params=pltpu.CompilerParams(
            dimension_semantics=("arbitrary",), vmem_limit_bytes=VMEM_LIMIT),
        name="moe_experts",
    )(tile_ea, tile_eb, tile_valid, rows, w_router, b_router, *weight_args)


def _combine_kernel(cls_ref, rank_ref, start_ref, x1_ref, mod_ref, n_post2_ref, f_hbm, o_ref, fbuf, sem):
    i = pl.program_id(0)
    n = pl.num_programs(0)
    tf = x1_ref.shape[0]

    def issue(tile, slot):
        def body(g, carry):
            first = tile * tf + g * DMA_GROUP
            slots = [start_ref[cls_ref[first + j]] + rank_ref[first + j] for j in range(DMA_GROUP)]
            first = g * DMA_GROUP
            for j in range(DMA_GROUP):
                pltpu.make_async_copy(
                    _slab(f_hbm, slots[j]), _slab(fbuf.at[slot], first + j), sem.at[slot]
                ).start(priority=j % 2)
            return carry
        lax.fori_loop(0, tf // DMA_GROUP, body, 0)

    def finish(slot):
        pltpu.make_async_copy(f_hbm.at[pl.ds(0, tf * SUBLANES)], fbuf.at[slot], sem.at[slot]).wait()
        f = jnp.concatenate([fbuf[slot, pl.ds(j, tf, stride=SUBLANES), :] for j in range(D_MODEL // LANES)],
                            axis=1)
        mod3 = mod_ref[...]
        bb = mod3.shape[0]
        f3 = f.reshape(bb, tf // bb, D_MODEL)
        g2 = mod3[:, :, 5 * D_MODEL:6 * D_MODEL]
        x13 = x1_ref[...].reshape(f3.shape)
        out = x13 + (f3 * _rms_scale(f3)) * (g2 * n_post2_ref[...].reshape(1, 1, D_MODEL))
        o_ref[...] = out.reshape(o_ref.shape)

    @pl.when(i == 0)
    def _():
        issue(0, 0)

    for parity in range(2):
        @pl.when(i % 2 == parity)
        def _():
            @pl.when(i + 1 < n)
            def _():
                issue(i + 1, 1 - parity)
            finish(parity)


def _combine_call(cls, rank, start, x1, mod3, n_post2, f_sorted, seqs_per_tile):
    n_tok = x1.shape[0]
    tf = FINAL_TILE
    tok_per_seq = n_tok // mod3.shape[0]
    if seqs_per_tile > 1:
        mod_map = lambda i, *_: (i, 0, 0)
    else:
        mod_map = lambda i, *_: ((i * tf) // tok_per_seq, 0, 0)
    return pl.pallas_call(
        _combine_kernel,
        grid_spec=pltpu.PrefetchScalarGridSpec(
            num_scalar_prefetch=3,
            grid=(n_tok // tf,),
            in_specs=[
                pl.BlockSpec((tf, D_MODEL), lambda i, *_: (i, 0)),
                pl.BlockSpec((seqs_per_tile, 1, 6 * D_MODEL), mod_map),
                pl.BlockSpec((1, D_MODEL), lambda i, *_: (0, 0)),
                pl.BlockSpec(memory_space=pl.ANY),
            ],
            out_specs=pl.BlockSpec((tf, D_MODEL), lambda i, *_: (i, 0)),
            scratch_shapes=[pltpu.VMEM((2, tf * SUBLANES, LANES), F32), pltpu.SemaphoreType.DMA((2,))],
        ),
        out_shape=jax.ShapeDtypeStruct((n_tok, D_MODEL), F32),
        compiler_params=pltpu.CompilerParams(
            dimension_semantics=("arbitrary",), vmem_limit_bytes=VMEM_LIMIT),
        name="moe_combine",
    )(cls, rank, start, x1, mod3, n_post2, f_sorted)


def _block_diag_gate(w_r, w_i):
    per_half = GATE_HALF // LRU_BLOCK
    n_half = D_LRU // GATE_HALF
    eye = np.eye(per_half, dtype=np.float32)

    def block_diag(w):
        w4 = w.reshape(n_half, per_half, LRU_BLOCK, LRU_BLOCK)
        return jnp.einsum('hnij,nm->hnimj', w4, eye).reshape(n_half, GATE_HALF, GATE_HALF)

    return jnp.concatenate([block_diag(w_r), block_diag(w_i)], axis=-1).astype(BF16)


def _rope_tables(pos):
    half = DK // 2
    inv = np.float64(ROPE_BASE) ** (-np.arange(half, dtype=np.float64) / half)
    ang = np.asarray(pos, np.float64)[:, None] * inv[None, :]
    cos = np.cos(ang)
    sin = np.sin(ang)
    return (np.concatenate([cos, cos], axis=-1).astype(np.float32),
            np.concatenate([-sin, sin], axis=-1).astype(np.float32))


def _earlier_table(m):
    return jnp.asarray(np.tril(np.ones((m, m), np.float32), -1), BF16)


def _decay_tables(c):
    log_g = np.log1p(-np.exp2(-5.0 - np.arange(N_HEADS, dtype=np.float64)))
    idx = np.arange(c, dtype=np.float64)
    diff = idx[:, None] - idx[None, :]
    mask = np.where(diff[None] >= 0, np.exp(np.maximum(diff, 0.0)[None] * log_g[:, None, None]), 0.0)
    w_state = np.exp((c - 1.0 - idx)[None, :] * log_g[:, None])
    cross_decay = np.exp((idx + 1.0)[:, None] * log_g[None, :])
    chunk_decay = np.exp(c * log_g)
    wstate_full = np.repeat(w_state.T, DK, axis=1)
    cross_full = np.repeat(cross_decay, DV, axis=1)
    cdecay_full = np.repeat(chunk_decay, DV)[None, :]
    return tuple(t.astype(np.float32) for t in (mask, wstate_full, cross_full, cdecay_full))


def kernel(x_prompt, x_sample, state_conv, state_lru, state_ret, c_prompt, c_sample, w_mod, b_mod, norm_pre_mix, norm_post_mix, norm_pre_ffn, norm_post_ffn, w_in, conv_w, conv_b, w_rgate, b_rgate, w_igate, b_igate, lru_lambda, ret_gn_w, w_out, w_router_group, b_router_group, w_router_expert, b_router_expert, w_exp_gate, w_exp_up, w_exp_down):
    bp, tp, _ = x_prompt.shape
    bs, ts, _ = x_sample.shape
    l = 0

    mod = _mod_call(jnp.concatenate([c_prompt, c_sample], axis=0), w_mod[l], b_mod[l][None, :])
    mod_p = mod[:bp][:, None, :]
    mod_s = mod[bp:][:, None, :]

    unused = LANES - N_EXPERTS - N_GROUPS
    w_router = jnp.concatenate(
        [w_router_expert[l], w_router_group[l], jnp.zeros((D_MODEL, unused), F32)], axis=1).astype(BF16)
    b_router = jnp.concatenate(
        [b_router_expert[l], b_router_group[l], jnp.zeros((unused,), F32)])[None, :]

    row = lambda vec: vec.reshape(1, -1)
    wts = (row(norm_pre_mix[l]), row(norm_post_mix[l]), row(norm_pre_ffn[l]),
           w_in[l].astype(BF16), conv_w[l], row(conv_b[l]),
           _block_diag_gate(w_rgate[l], w_igate[l]),
           row(b_rgate[l]), row(b_igate[l]), row(lru_lambda[l]), row(ret_gn_w[l]),
           w_out[l].astype(BF16), w_router, b_router)

    cos_p, sin_p = _rope_tables(np.arange(tp))
    x1_p, pk_p, route_p, conv_p8, lru_p8, ret_p, cnt_p = _prompt_mixer_call(
        x_prompt, mod_p, cos_p, sin_p, wts,
        _decay_tables(math.gcd(tp, RET_CHUNK)) + (_earlier_table(PROMPT_TILE),))

    cos_s, sin_s = _rope_tables(PAST_LEN + np.arange(ts))
    mask8, wstate_s, cross_s, cdecay_s = _decay_tables(math.gcd(ts, RET_CHUNK))
    eye = np.eye(SAMPLE_SEQS, dtype=np.float32)
    smask = np.stack([np.kron(eye, mask8[h]) for h in range(N_HEADS)])
    buf8 = jnp.pad(state_conv[l], ((0, 0), (0, ts - (CONV_W - 1)), (0, 0)))
    h0p = jnp.pad(state_lru[l][:, None, :], ((0, 0), (0, ts - 1), (0, 0)))
    x1_s, pk_s, route_s, xr_s, h_s, ret_s, cnt_all = _sample_mixer_call(
        x_sample, mod_s, cos_s, sin_s, buf8, h0p, state_ret[l], cnt_p, wts,
        (smask, wstate_s, cross_s, cdecay_s, _earlier_table(SAMPLE_SEQS * ts)))

    n_p = bp * tp
    n_tok = n_p + bs * ts
    tm = MOE_TILE
    max_tiles = n_tok // tm + N_CLASSES
    cls_p, rank_p = route_p[0].astype(jnp.int32), route_p[1].astype(jnp.int32)
    cls_s, rank_s = route_s[0].astype(jnp.int32), route_s[1].astype(jnp.int32)
    cls = jnp.concatenate([cls_p, cls_s])
    rank = jnp.concatenate([rank_p, rank_s])
    cnt = cnt_all[0, :N_CLASSES].astype(jnp.int32)
    ntile = (cnt + (tm - 1)) // tm
    padcnt = ntile * tm
    before = np.tril(np.ones((N_CLASSES, N_CLASSES), np.int32), -1)
    upto = np.tril(np.ones((N_CLASSES, N_CLASSES), np.int32))
    start = jnp.sum(before * padcnt[None, :], axis=1)
    tile_end = jnp.sum(upto * ntile[None, :], axis=1)
    n_used = jnp.sum(ntile)
    tile_ids = np.arange(max_tiles, dtype=np.int32)
    tile_valid = (tile_ids < n_used).astype(jnp.int32)
    last_used = jnp.minimum(tile_ids, n_used - 1)
    tile_cls = jnp.sum((last_used[:, None] >= tile_end[None, :]).astype(jnp.int32), axis=1)
    pair = tile_cls % N_PAIRS
    first_expert = jnp.where(pair == 5, 2, jnp.where((pair == 2) | (pair == 3), 1, 0))
    second_expert = jnp.where(pair == 0, 1, jnp.where(pair <= 2, 2, 3))
    tile_ea = (tile_cls // N_PAIRS) * PER_GROUP + first_expert
    tile_eb = (tile_cls // N_PAIRS) * PER_GROUP + second_expert

    n_post2 = row(norm_post_ffn[l])
    wg = w_exp_gate[l]
    wu = w_exp_up[l]
    wd = w_exp_down[l]
    rows = _dispatch_call(cls, rank, cnt, padcnt, start, pk_p, pk_s, max_tiles * tm)
    f_sorted = _moe_call(tile_ea, tile_eb, tile_valid, rows, w_router, b_router, wg, wu, wd)
    y_p = _combine_call(cls_p, rank_p, start, x1_p, mod_p, n_post2, f_sorted, 1)
    y_s = _combine_call(cls_s, rank_s, start, x1_s, mod_s, n_post2, f_sorted, FINAL_TILE // ts)

    conv_p = conv_p8[:, SUBLANES - (CONV_W - 1):, :]
    lru_p = lru_p8[:, SUBLANES - 1, :]
    xr_s3 = xr_s.reshape(bs, ts, D_LRU)
    conv_s = xr_s3[:, ts - (CONV_W - 1):, :]
    lru_s = h_s.reshape(bs, ts, D_LRU)[:, ts - 1, :]
    return (y_p.reshape(bp, tp, D_MODEL), y_s.reshape(bs, ts, D_MODEL),
            conv_p[None], lru_p[None], ret_p[None],
            conv_s[None], lru_s[None], ret_s[None])
```

```python
import functools
import math

import jax
import jax.numpy as jnp
import numpy as np
from jax import lax
from jax.experimental import pallas as pl
from jax.experimental.pallas import tpu as pltpu

F32 = jnp.float32
BF16 = jnp.bfloat16

D_MODEL = 1024
D_LRU = 512
D_RET = 512
N_LRU_BLOCKS = 8
LRU_BLOCK = D_LRU // N_LRU_BLOCKS
CONV_W = 4
LRU_C = 8.0
N_HEADS = 4
DK = 128
DV = 128
RET_CHUNK = 128
ROPE_BASE = 10000.0
D_IN_PROJ = 3072
N_GROUPS = 4
PER_GROUP = 4
N_EXPERTS = 16
D_EXPERT = 256
EXPM1_DIRECT_BELOW = -0.5
NORM_EPS = 1e-6
GN_EPS = 1e-5
PAST_LEN = 16384

SUBLANES = 8
LANES = 128
GATE_HALF = 256
VMEM_LIMIT = 56 * 1024 * 1024

PROMPT_TILE = 512
SAMPLE_SEQS = 16
MOE_TILE = 256
MOE_TILES_PER_STEP = 2
FINAL_TILE = 512
DISPATCH_TILE = 512
RING = 3
DMA_GROUP = 8

N_PAIRS = 6
N_CLASSES = N_GROUPS * N_PAIRS
ROW_GROUPS = D_MODEL // LANES


def _silu(x):
    return x * jax.nn.sigmoid(x)


def _rms_scale(x):
    return lax.rsqrt(jnp.mean(x * x, axis=-1, keepdims=True) + NORM_EPS)


def _masked_softmax(logits, mask):
    top = jnp.max(jnp.where(mask, logits, -jnp.inf), axis=-1, keepdims=True)
    e = jnp.where(mask, jnp.exp(logits - top), 0.0)
    return e / jnp.sum(e, axis=-1, keepdims=True)


def _mod_kernel(c_ref, w_ref, b_ref, o_ref):
    s = _silu(c_ref[...]).astype(BF16)
    o_ref[...] = jnp.dot(s, w_ref[...].astype(BF16), preferred_element_type=F32) + b_ref[...]


def _mod_call(c_all, w_mod, b_mod):
    rows = c_all.shape[0]
    ncol = w_mod.shape[1]
    blk = D_MODEL
    return pl.pallas_call(
        _mod_kernel,
        grid=(ncol // blk,),
        in_specs=[
            pl.BlockSpec((rows, D_MODEL), lambda j: (0, 0)),
            pl.BlockSpec((D_MODEL, blk), lambda j: (0, j)),
            pl.BlockSpec((1, blk), lambda j: (0, j)),
        ],
        out_specs=pl.BlockSpec((rows, blk), lambda j: (0, j)),
        out_shape=jax.ShapeDtypeStruct((rows, ncol), F32),
        compiler_params=pltpu.CompilerParams(
            dimension_semantics=("arbitrary",), vmem_limit_bytes=VMEM_LIMIT),
        name="mod",
    )(c_all, w_mod, b_mod)


def _in_proj(x3, mod3, n_pre1_ref, w_in_ref):
    bb, tt, _ = x3.shape
    sh1 = mod3[:, :, 0:D_MODEL]
    sc1 = mod3[:, :, D_MODEL:2 * D_MODEL]
    coef = n_pre1_ref[...].reshape(1, 1, D_MODEL) * (1.0 + sc1)
    u = (x3 * _rms_scale(x3)) * coef + sh1
    u2d = u.reshape(bb * tt, D_MODEL).astype(BF16)
    return jnp.dot(u2d, w_in_ref[...], preferred_element_type=F32)


def _lru_coeffs(xc, wg_ref, b_r_ref, b_i_ref, lam_ref):
    xcb = xc.astype(BF16)
    g0 = jnp.dot(xcb[:, :GATE_HALF], wg_ref[0], preferred_element_type=F32)
    g1 = jnp.dot(xcb[:, GATE_HALF:], wg_ref[1], preferred_element_type=F32)
    r = jax.nn.sigmoid(jnp.concatenate([g0[:, :GATE_HALF], g1[:, :GATE_HALF]], axis=1) + b_r_ref[...])
    i = jax.nn.sigmoid(jnp.concatenate([g0[:, GATE_HALF:], g1[:, GATE_HALF:]], axis=1) + b_i_ref[...])
    lam = lam_ref[...]
    sp = jnp.maximum(-lam, 0.0) + jnp.log1p(jnp.exp(-jnp.abs(lam)))
    log_a = -LRU_C * r * sp
    a = jnp.exp(log_a)
    y = 2.0 * log_a
    a2 = a * a
    d = a2 - 1.0
    small = d * y / jnp.log(a2)
    em1 = jnp.where(y < EXPM1_DIRECT_BELOW, d, jnp.where(d == 0.0, y, small))
    gain = jnp.sqrt(-em1)
    return a, gain * (i * xc)


def _rope(xh, cos2, sin2, lane_axis):
    return xh * cos2 + pltpu.roll(xh, DK // 2, axis=lane_axis) * sin2


def _group_norm(o):
    mu = jnp.mean(o, axis=-1, keepdims=True)
    d = o - mu
    var = jnp.mean(d * d, axis=-1, keepdims=True)
    return d * lax.rsqrt(var + GN_EPS)


def _post_mixer(x3, mod3, out_a, out_b, w_out_ref, n_post1_ref, n_pre2_ref, w_router_ref, b_router_ref,
                earlier_ref, x1_ref, pk_ref, cls_ref, rank_ref, cnt_scr):
    bb, tt, _ = x3.shape
    m = bb * tt
    y = (jnp.dot(out_a.astype(BF16), w_out_ref[0:D_LRU, :], preferred_element_type=F32)
         + jnp.dot(out_b.astype(BF16), w_out_ref[D_LRU:, :], preferred_element_type=F32))
    g1 = mod3[:, :, 2 * D_MODEL:3 * D_MODEL]
    sh2 = mod3[:, :, 3 * D_MODEL:4 * D_MODEL]
    sc2 = mod3[:, :, 4 * D_MODEL:5 * D_MODEL]
    y3 = y.reshape(bb, tt, D_MODEL)
    x1 = x3 + (y3 * _rms_scale(y3)) * (g1 * n_post1_ref[...].reshape(1, 1, D_MODEL))
    u2 = (x1 * _rms_scale(x1)) * (n_pre2_ref[...].reshape(1, 1, D_MODEL) * (1.0 + sc2)) + sh2
    x1_ref[...] = x1.reshape(m, D_MODEL)
    u2f = u2.reshape(m, D_MODEL)
    for j in range(ROW_GROUPS):
        pk_ref[pl.ds(j, m, stride=SUBLANES), :] = u2f[:, j * LANES:(j + 1) * LANES]
    u2b = u2f.astype(BF16)

    logits = jnp.dot(u2b, w_router_ref[...], preferred_element_type=F32) + b_router_ref[...]
    lane = lax.broadcasted_iota(jnp.int32, (m, LANES), 1)
    lane_f = lane.astype(F32)
    is_g = (lane >= N_EXPERTS) & (lane < N_EXPERTS + N_GROUPS)
    p_group = _masked_softmax(logits, is_g)
    p_g = jnp.max(p_group, axis=-1, keepdims=True)
    g_lane = jnp.min(jnp.where(is_g & (p_group == p_g), lane_f, float(LANES)), axis=-1, keepdims=True)
    e_lo = (g_lane - N_EXPERTS) * PER_GROUP
    in_g = (lane_f >= e_lo) & (lane_f < e_lo + PER_GROUP)
    p_e = _masked_softmax(logits, in_g)
    pm = jnp.where(in_g, p_e, -1.0)
    w1 = jnp.max(pm, axis=-1, keepdims=True)
    i1 = jnp.min(jnp.where(pm == w1, lane_f, float(LANES)), axis=-1, keepdims=True)
    pm2 = jnp.where(lane_f == i1, -1.0, pm)
    w2 = jnp.max(pm2, axis=-1, keepdims=True)
    i2 = jnp.min(jnp.where(pm2 == w2, lane_f, float(LANES)), axis=-1, keepdims=True)
    a = jnp.minimum(i1, i2) - e_lo
    b = jnp.maximum(i1, i2) - e_lo
    pair = jnp.where(a == 0.0, jnp.where(b == 3.0, 4.0, b - 1.0), jnp.where(a == 1.0, b, 5.0))
    cls = (g_lane - N_EXPERTS) * N_PAIRS + pair
    onehot = lane_f == cls
    prefix = jnp.dot(earlier_ref[...], jnp.where(onehot, 1.0, 0.0).astype(BF16), preferred_element_type=F32)
    run = cnt_scr[0:1, :]
    rank = jnp.sum(jnp.where(onehot, prefix + run, 0.0), axis=-1, keepdims=True)
    cnt_scr[...] = jnp.broadcast_to(
        run + jnp.sum(jnp.where(onehot, 1.0, 0.0), axis=0, keepdims=True), cnt_scr.shape)
    route = jnp.where(lane == 0, cls, jnp.where(lane == 1, rank, 0.0))
    route_t = jnp.transpose(route)
    cls_ref[...] = route_t[0:1, :].astype(jnp.int32)
    rank_ref[...] = route_t[1:2, :].astype(jnp.int32)


def _group_scan(a3, b3):
    tpos = lax.broadcasted_iota(jnp.int32, a3.shape, 1)
    s = 1
    while s < a3.shape[1]:
        keep = tpos >= s
        a_sh = jnp.where(keep, pltpu.roll(a3, s, axis=1), 1.0)
        b_sh = jnp.where(keep, pltpu.roll(b3, s, axis=1), 0.0)
        b3 = a3 * b_sh + b3
        a3 = a3 * a_sh
        s *= 2
    return a3, b3


def _scan_rows(a, b, h0):
    n, c = a.shape
    groups = n // SUBLANES
    a3, b3 = _group_scan(a.reshape(groups, SUBLANES, c), b.reshape(groups, SUBLANES, c))
    carry = h0
    out = []
    for g in range(groups):
        hg = b3[g] + a3[g] * carry
        out.append(hg)
        carry = hg[SUBLANES - 1:SUBLANES, :]
    return jnp.concatenate(out, axis=0)


def _prompt_mixer_kernel(x_ref, mod_ref, cos_ref, sin_ref,
                         n_pre1_ref, n_post1_ref, n_pre2_ref,
                         w_in_ref, conv_w_ref, conv_b_ref, wg_ref, b_r_ref, b_i_ref, lam_ref,
                         gn_w_ref, w_out_ref, w_router_ref, b_router_ref,
                         mask_ref, wstate_ref, cross_ref, cdecay_ref, earlier_ref,
                         x1_ref, pk_ref, cls_ref, rank_ref, conv_out_ref, lru_out_ref, ret_out_ref,
                         cnt_out_ref,
                         conv_scr, h_scr, s_scr, cnt_scr):
    t = pl.program_id(1)
    tt = x_ref.shape[1]

    @pl.when((pl.program_id(0) == 0) & (t == 0))
    def _():
        cnt_scr[...] = jnp.zeros_like(cnt_scr)

    @pl.when(t == 0)
    def _():
        conv_scr[...] = jnp.zeros_like(conv_scr)
        h_scr[...] = jnp.zeros_like(h_scr)
        s_scr[...] = jnp.zeros_like(s_scr)

    x3 = x_ref[...]
    mod3 = mod_ref[...]
    z = _in_proj(x3, mod3, n_pre1_ref, w_in_ref)
    xr, yg, q, k, v, g = (z[:, c * D_LRU:(c + 1) * D_LRU] for c in range(D_IN_PROJ // D_LRU))

    groups = tt // SUBLANES
    xr3 = xr.reshape(groups, SUBLANES, D_LRU)
    tpos = lax.broadcasted_iota(jnp.int32, xr3.shape, 1)
    tail = conv_scr[...]
    xc3 = jnp.broadcast_to(conv_b_ref[...].reshape(1, 1, D_LRU), xr3.shape)
    for j in range(CONV_W):
        back = CONV_W - 1 - j
        w_j = conv_w_ref[j:j + 1, :].reshape(1, 1, D_LRU)
        if back == 0:
            term = xr3
        else:
            cur = pltpu.roll(xr3, back, axis=1)
            first = pltpu.roll(tail, back, axis=0).reshape(1, SUBLANES, D_LRU)
            prev = jnp.concatenate([first, cur[:groups - 1]], axis=0)
            term = jnp.where(tpos >= back, cur, prev)
        xc3 = xc3 + term * w_j
    xc = xc3.reshape(tt, D_LRU)
    conv_scr[...] = xr[tt - SUBLANES:, :]

    a, b = _lru_coeffs(xc, wg_ref, b_r_ref, b_i_ref, lam_ref)
    hseq = _scan_rows(a, b, h_scr[0:1, :])
    h_scr[...] = jnp.broadcast_to(hseq[tt - 1:tt, :], h_scr.shape)
    out_a = hseq * jax.nn.gelu(yg, approximate=True)

    cos2 = cos_ref[...]
    sin2 = sin_ref[...]
    scale = DK ** -0.5
    o_heads = []
    for h in range(N_HEADS):
        hs = slice(h * DK, (h + 1) * DK)
        qh = (_rope(q[:, hs], cos2, sin2, 1) * scale).astype(BF16)
        kh = _rope(k[:, hs], cos2, sin2, 1)
        vh = v[:, hs].astype(BF16)
        o_chunks = []
        for c in range(tt // RET_CHUNK):
            cs = slice(c * RET_CHUNK, (c + 1) * RET_CHUNK)
            qc = qh[cs]
            kc = kh[cs]
            vc = vh[cs]
            s_prev = s_scr[h]
            scores = lax.dot_general(qc, kc.astype(BF16), (((1,), (1,)), ((), ())),
                                     preferred_element_type=F32) * mask_ref[h]
            inner = jnp.dot(scores.astype(BF16), vc, preferred_element_type=F32)
            cross = jnp.dot(qc, s_prev.astype(BF16), preferred_element_type=F32) * cross_ref[:, hs]
            kw = (kc * wstate_ref[:, hs]).astype(BF16)
            kv = lax.dot_general(kw, vc, (((0,), (0,)), ((), ())), preferred_element_type=F32)
            s_scr[h] = cdecay_ref[:, hs] * s_prev + kv
            o_chunks.append(inner + cross)
        o_heads.append(_group_norm(jnp.concatenate(o_chunks, axis=0)))
    o = jnp.concatenate(o_heads, axis=1)
    out_b = o * gn_w_ref[...] * _silu(g)

    _post_mixer(x3, mod3, out_a, out_b, w_out_ref, n_post1_ref, n_pre2_ref, w_router_ref, b_router_ref,
                earlier_ref, x1_ref, pk_ref, cls_ref, rank_ref, cnt_scr)
    cnt_out_ref[...] = cnt_scr[...]

    @pl.when(t == pl.num_programs(1) - 1)
    def _():
        conv_out_ref[0] = xr[tt - SUBLANES:, :]
        lru_out_ref[0] = hseq[tt - SUBLANES:, :]
        ret_out_ref[0] = s_scr[...]


def _const_spec(shape):
    nd = len(shape)
    return pl.BlockSpec(shape, lambda *_: (0,) * nd)


def _prompt_mixer_call(x, mod3, cos2, sin2, wts, tables):
    bsz, seq, _ = x.shape
    tt = PROMPT_TILE
    nt = seq // tt
    n_tok = bsz * seq
    tok_spec = pl.BlockSpec((tt, D_MODEL), lambda b, t: (b * nt + t, 0))
    in_specs = [
        pl.BlockSpec((1, tt, D_MODEL), lambda b, t: (b, t, 0)),
        pl.BlockSpec((1, 1, 6 * D_MODEL), lambda b, t: (b, 0, 0)),
        pl.BlockSpec((tt, LANES), lambda b, t: (t, 0)),
        pl.BlockSpec((tt, LANES), lambda b, t: (t, 0)),
    ] + [_const_spec(w.shape) for w in wts] + [_const_spec(tb.shape) for tb in tables]
    out_specs = [
        tok_spec,
        pl.BlockSpec((tt * SUBLANES, LANES), lambda b, t: (b * nt + t, 0)),
        pl.BlockSpec((1, tt), lambda b, t: (0, b * nt + t)),
        pl.BlockSpec((1, tt), lambda b, t: (0, b * nt + t)),
        pl.BlockSpec((1, SUBLANES, D_LRU), lambda b, t: (b, 0, 0)),
        pl.BlockSpec((1, SUBLANES, D_LRU), lambda b, t: (b, 0, 0)),
        pl.BlockSpec((1, N_HEADS, DK, DV), lambda b, t: (b, 0, 0, 0)),
        pl.BlockSpec((SUBLANES, LANES), lambda b, t: (0, 0)),
    ]
    out_shape = [
        jax.ShapeDtypeStruct((n_tok, D_MODEL), F32),
        jax.ShapeDtypeStruct((n_tok * SUBLANES, LANES), F32),
        jax.ShapeDtypeStruct((1, n_tok), jnp.int32),
        jax.ShapeDtypeStruct((1, n_tok), jnp.int32),
        jax.ShapeDtypeStruct((bsz, SUBLANES, D_LRU), F32),
        jax.ShapeDtypeStruct((bsz, SUBLANES, D_LRU), F32),
        jax.ShapeDtypeStruct((bsz, N_HEADS, DK, DV), F32),
        jax.ShapeDtypeStruct((SUBLANES, LANES), F32),
    ]
    return pl.pallas_call(
        _prompt_mixer_kernel,
        grid=(bsz, nt),
        in_specs=in_specs,
        out_specs=out_specs,
        out_shape=out_shape,
        scratch_shapes=[
            pltpu.VMEM((SUBLANES, D_LRU), F32),
            pltpu.VMEM((SUBLANES, D_LRU), F32),
            pltpu.VMEM((N_HEADS, DK, DV), F32),
            pltpu.VMEM((SUBLANES, LANES), F32),
        ],
        compiler_params=pltpu.CompilerParams(
            dimension_semantics=("arbitrary", "arbitrary"), vmem_limit_bytes=VMEM_LIMIT),
        name="prompt_mixer",
    )(x, mod3, cos2, sin2, *wts, *tables)


def _sample_mixer_kernel(x_ref, mod_ref, cos_ref, sin_ref, buf_ref, h0_ref, s0_ref, cnt_in_ref,
                         n_pre1_ref, n_post1_ref, n_pre2_ref,
                         w_in_ref, conv_w_ref, conv_b_ref, wg_ref, b_r_ref, b_i_ref, lam_ref,
                         gn_w_ref, w_out_ref, w_router_ref, b_router_ref,
                         smask_ref, wstate_ref, cross_ref, cdecay_ref, earlier_ref,
                         x1_ref, pk_ref, cls_ref, rank_ref, xr_out_ref, h_out_ref, ret_out_ref,
                         cnt_out_ref,
                         cnt_scr):
    bb, ts, _ = x_ref.shape
    m = bb * ts

    @pl.when(pl.program_id(0) == 0)
    def _():
        cnt_scr[...] = cnt_in_ref[...]

    x3 = x_ref[...]
    mod3 = mod_ref[...]
    z = _in_proj(x3, mod3, n_pre1_ref, w_in_ref)
    xr, yg, q, k, v, g = (z[:, c * D_LRU:(c + 1) * D_LRU] for c in range(D_IN_PROJ // D_LRU))
    xr_out_ref[...] = xr

    xr3 = xr.reshape(bb, ts, D_LRU)
    buf3 = buf_ref[...]
    tpos = lax.broadcasted_iota(jnp.int32, (bb, ts, D_LRU), 1)
    xc3 = jnp.broadcast_to(conv_b_ref[...].reshape(1, 1, D_LRU), (bb, ts, D_LRU))
    for j in range(CONV_W):
        back = CONV_W - 1 - j
        w_j = conv_w_ref[j:j + 1, :].reshape(1, 1, D_LRU)
        if back == 0:
            term = xr3
        else:
            cur = pltpu.roll(xr3, back, axis=1)
            up = CONV_W - 1 - back
            old = buf3 if up == 0 else pltpu.roll(buf3, ts - up, axis=1)
            term = jnp.where(tpos >= back, cur, old)
        xc3 = xc3 + term * w_j
    xc = xc3.reshape(m, D_LRU)

    a, b = _lru_coeffs(xc, wg_ref, b_r_ref, b_i_ref, lam_ref)
    a3 = a.reshape(bb, ts, D_LRU)
    b3 = b.reshape(bb, ts, D_LRU) + a3 * h0_ref[...]
    _, h3 = _group_scan(a3, b3)
    hseq = h3.reshape(m, D_LRU)
    h_out_ref[...] = hseq
    out_a = hseq * jax.nn.gelu(yg, approximate=True)

    cos2 = cos_ref[...].reshape(1, ts, LANES)
    sin2 = sin_ref[...].reshape(1, ts, LANES)
    scale = DK ** -0.5
    o_heads = []
    for h in range(N_HEADS):
        hs = slice(h * DK, (h + 1) * DK)
        q3 = (_rope(q[:, hs].reshape(bb, ts, DK), cos2, sin2, 2) * scale).astype(BF16)
        k3 = _rope(k[:, hs].reshape(bb, ts, DK), cos2, sin2, 2)
        v3 = v[:, hs].reshape(bb, ts, DV).astype(BF16)
        q2 = q3.reshape(m, DK)
        k2 = k3.reshape(m, DK).astype(BF16)
        v2 = v3.reshape(m, DV)
        scores = lax.dot_general(q2, k2, (((1,), (1,)), ((), ())),
                                 preferred_element_type=F32) * smask_ref[h]
        inner = jnp.dot(scores.astype(BF16), v2, preferred_element_type=F32)
        s0h = s0_ref[:, h]
        cross = jnp.einsum('bid,bde->bie', q3, s0h.astype(BF16), preferred_element_type=F32)
        cross = cross * cross_ref[:, hs].reshape(1, ts, DV)
        kw3 = (k3 * wstate_ref[:, hs].reshape(1, ts, DK)).astype(BF16)
        kv = jnp.einsum('bjd,bje->bde', kw3, v3, preferred_element_type=F32)
        ret_out_ref[:, h] = cdecay_ref[:, hs].reshape(1, 1, DV) * s0h + kv
        o_heads.append(_group_norm(inner + cross.reshape(m, DV)))
    o = jnp.concatenate(o_heads, axis=1)
    out_b = o * gn_w_ref[...] * _silu(g)

    _post_mixer(x3, mod3, out_a, out_b, w_out_ref, n_post1_ref, n_pre2_ref, w_router_ref, b_router_ref,
                earlier_ref, x1_ref, pk_ref, cls_ref, rank_ref, cnt_scr)
    cnt_out_ref[...] = cnt_scr[...]


def _sample_mixer_call(x, mod3, cos2, sin2, buf8, h0p, s0, cnt_in, wts, tables):
    bsz, ts, _ = x.shape
    bb = SAMPLE_SEQS
    m = bb * ts
    n_tok = bsz * ts
    seq_spec = lambda w: pl.BlockSpec((bb, ts, w), lambda i: (i, 0, 0))
    tok_spec = lambda w: pl.BlockSpec((m, w), lambda i: (i, 0))
    in_specs = [
        seq_spec(D_MODEL),
        pl.BlockSpec((bb, 1, 6 * D_MODEL), lambda i: (i, 0, 0)),
        _const_spec(cos2.shape),
        _const_spec(sin2.shape),
        seq_spec(D_LRU),
        seq_spec(D_LRU),
        pl.BlockSpec((bb, N_HEADS, DK, DV), lambda i: (i, 0, 0, 0)),
        _const_spec(cnt_in.shape),
    ] + [_const_spec(w.shape) for w in wts] + [_const_spec(tb.shape) for tb in tables]
    out_specs = [
        tok_spec(D_MODEL),
        pl.BlockSpec((m * SUBLANES, LANES), lambda i: (i, 0)),
        pl.BlockSpec((1, m), lambda i: (0, i)),
        pl.BlockSpec((1, m), lambda i: (0, i)),
        tok_spec(D_LRU),
        tok_spec(D_LRU),
        pl.BlockSpec((bb, N_HEADS, DK, DV), lambda i: (i, 0, 0, 0)),
        _const_spec(cnt_in.shape),
    ]
    out_shape = [
        jax.ShapeDtypeStruct((n_tok, D_MODEL), F32),
        jax.ShapeDtypeStruct((n_tok * SUBLANES, LANES), F32),
        jax.ShapeDtypeStruct((1, n_tok), jnp.int32),
        jax.ShapeDtypeStruct((1, n_tok), jnp.int32),
        jax.ShapeDtypeStruct((n_tok, D_LRU), F32),
        jax.ShapeDtypeStruct((n_tok, D_LRU), F32),
        jax.ShapeDtypeStruct((bsz, N_HEADS, DK, DV), F32),
        jax.ShapeDtypeStruct(cnt_in.shape, F32),
    ]
    return pl.pallas_call(
        _sample_mixer_kernel,
        grid=(bsz // bb,),
        in_specs=in_specs,
        out_specs=out_specs,
        out_shape=out_shape,
        scratch_shapes=[pltpu.VMEM((SUBLANES, LANES), F32)],
        compiler_params=pltpu.CompilerParams(
            dimension_semantics=("arbitrary",), vmem_limit_bytes=VMEM_LIMIT),
        name="sample_mixer",
    )(x, mod3, cos2, sin2, buf8, h0p, s0, cnt_in, *wts, *tables)


def _slab(ref, r):
    return ref.at[pl.ds(pl.multiple_of(r * SUBLANES, SUBLANES), SUBLANES)]


def _dispatch_kernel(cls_ref, rank_ref, cnt_ref, padcnt_ref, start_ref, srcp_ref, srcs_ref, out_ref, ring, sem,
                     *, p_tiles):
    i = pl.program_id(0)
    n = pl.num_programs(0)
    tile_rows = ring.shape[1]
    td = tile_rows // SUBLANES
    moe_rows = MOE_TILE * SUBLANES

    def issue(slot):
        def body(g, carry):
            first = i * td + g * DMA_GROUP
            slots = [start_ref[cls_ref[first + j]] + rank_ref[first + j] for j in range(DMA_GROUP)]
            first = g * DMA_GROUP
            for j in range(DMA_GROUP):
                pltpu.make_async_copy(
                    _slab(ring.at[slot], first + j), _slab(out_ref, slots[j]), sem.at[slot]
                ).start(priority=j % 2)
            return carry
        lax.fori_loop(0, td // DMA_GROUP, body, 0)

    def wait_tile(slot):
        pltpu.make_async_copy(ring.at[slot], out_ref.at[pl.ds(0, tile_rows)], sem.at[slot]).wait()

    def zero_fill(slot):
        ring[slot] = jnp.zeros((tile_rows, LANES), F32)
        zero_src = ring.at[slot]

        def per_class(c, carry):
            lo = start_ref[c] + cnt_ref[c]
            hi = start_ref[c] + padcnt_ref[c]

            def fill(r, carry2):
                pltpu.make_async_copy(_slab(zero_src, 0), _slab(out_ref, r), sem.at[slot]).start()
                return carry2
            lax.fori_loop(lo, hi, fill, 0)

            def done(r, carry2):
                pltpu.make_async_copy(_slab(zero_src, 0), _slab(out_ref, 0), sem.at[slot]).wait()
                return carry2
            lax.fori_loop(lo, hi, done, 0)
            return carry
        lax.fori_loop(0, N_CLASSES, per_class, 0)

        used_tiles = (start_ref[N_CLASSES - 1] + padcnt_ref[N_CLASSES - 1]) // MOE_TILE
        all_tiles = out_ref.shape[0] // moe_rows

        def tile_copy(t):
            return pltpu.make_async_copy(
                zero_src.at[pl.ds(0, moe_rows)],
                out_ref.at[pl.ds(pl.multiple_of(t * moe_rows, moe_rows), moe_rows)], sem.at[slot])

        def fill_tile(t, carry):
            tile_copy(t).start()
            return carry
        lax.fori_loop(used_tiles, all_tiles, fill_tile, 0)

        def done_tile(t, carry):
            tile_copy(t).wait()
            return carry
        lax.fori_loop(used_tiles, all_tiles, done_tile, 0)

    for s in range(RING):
        @pl.when(i % RING == s)
        def _():
            @pl.when(i < p_tiles)
            def _():
                ring[s] = srcp_ref[...]

            @pl.when(i >= p_tiles)
            def _():
                ring[s] = srcs_ref[...]

            issue(s)

            @pl.when(i >= RING - 1)
            def _():
                wait_tile((s + 1) % RING)

            @pl.when(i == n - 1)
            def _():
                for back in range(RING - 2, -1, -1):
                    wait_tile((s - back) % RING)
                zero_fill(s)


def _dispatch_call(cls, rank, cnt, padcnt, start, pk_p, pk_s, n_rows):
    td = DISPATCH_TILE
    tile_rows = td * SUBLANES
    p_tiles = pk_p.shape[0] // tile_rows
    s_tiles = pk_s.shape[0] // tile_rows
    assert p_tiles + s_tiles >= RING and td >= MOE_TILE
    return pl.pallas_call(
        functools.partial(_dispatch_kernel, p_tiles=p_tiles),
        grid_spec=pltpu.PrefetchScalarGridSpec(
            num_scalar_prefetch=5,
            grid=(p_tiles + s_tiles,),
            in_specs=[
                pl.BlockSpec((tile_rows, LANES), lambda i, *_: (jnp.minimum(i, p_tiles - 1), 0)),
                pl.BlockSpec((tile_rows, LANES), lambda i, *_: (jnp.maximum(i - p_tiles, 0), 0)),
            ],
            out_specs=pl.BlockSpec(memory_space=pl.ANY),
            scratch_shapes=[pltpu.VMEM((RING, tile_rows, LANES), F32),
                            pltpu.SemaphoreType.DMA((RING,))],
        ),
        out_shape=jax.ShapeDtypeStruct((n_rows * SUBLANES, LANES), F32),
        compiler_params=pltpu.CompilerParams(
            dimension_semantics=("arbitrary",), has_side_effects=True, vmem_limit_bytes=VMEM_LIMIT),
        name="moe_dispatch",
    )(cls, rank, cnt, padcnt, start, pk_p, pk_s)


def _moe_kernel(ea_ref, eb_ref, valid_ref, xs_ref, w_router_ref, b_router_ref, *refs):
    f_ref = refs[-1]
    for s in range(MOE_TILES_PER_STEP):
        _moe_tile(pl.program_id(0) * MOE_TILES_PER_STEP + s, s * MOE_TILE * SUBLANES, ea_ref, eb_ref,
                  valid_ref, xs_ref, w_router_ref, b_router_ref, *refs[6 * s:6 * s + 6], f_ref)


def _moe_tile(t, row0, ea_ref, eb_ref, valid_ref, xs_ref, w_router_ref, b_router_ref,
              wga_ref, wua_ref, wda_ref, wgb_ref, wub_ref, wdb_ref, f_ref):
    tm = MOE_TILE

    @pl.when(valid_ref[t] == 1)
    def _():
        x = jnp.concatenate([xs_ref[pl.ds(row0 + j, tm, stride=SUBLANES), :] for j in range(ROW_GROUPS)],
                            axis=1).astype(BF16)
        e_a = ea_ref[t]
        e_b = eb_ref[t]
        e_lo = (e_a // PER_GROUP) * PER_GROUP
        logits = jnp.dot(x, w_router_ref[...], preferred_element_type=F32) + b_router_ref[...]
        lane = lax.broadcasted_iota(jnp.int32, (tm, LANES), 1)
        pick = lambda p, idx: jnp.sum(jnp.where(lane == idx, p, 0.0), axis=-1, keepdims=True)
        p_group = _masked_softmax(logits, (lane >= N_EXPERTS) & (lane < N_EXPERTS + N_GROUPS))
        p_g = pick(p_group, N_EXPERTS + e_a // PER_GROUP)
        p_e = _masked_softmax(logits, (lane >= e_lo) & (lane < e_lo + PER_GROUP))
        w_a = pick(p_e, e_a)
        w_b = pick(p_e, e_b)
        wsum = w_a + w_b

        def expert(wg_ref, wu_ref, gate):
            hg = jnp.dot(x, wg_ref[0].astype(BF16), preferred_element_type=F32)
            hu = jnp.dot(x, wu_ref[0].astype(BF16), preferred_element_type=F32)
            return (_silu(hg) * hu * gate).astype(BF16)

        ha = expert(wga_ref, wua_ref, p_g * (w_a / wsum))
        hb = expert(wgb_ref, wub_ref, p_g * (w_b / wsum))
        for c in range(D_MODEL // GATE_HALF):
            cols = slice(c * GATE_HALF, (c + 1) * GATE_HALF)
            f = (jnp.dot(ha, wda_ref[0, :, cols].astype(BF16), preferred_element_type=F32)
                 + jnp.dot(hb, wdb_ref[0, :, cols].astype(BF16), preferred_element_type=F32))
            for jj in range(GATE_HALF // LANES):
                j = c * (GATE_HALF // LANES) + jj
                f_ref[pl.ds(row0 + j, tm, stride=SUBLANES), :] = f[:, jj * LANES:(jj + 1) * LANES]

    @pl.when(valid_ref[t] == 0)
    def _():
        f_ref[pl.ds(row0, tm * SUBLANES), :] = jnp.zeros((tm * SUBLANES, LANES), F32)


def _moe_call(tile_ea, tile_eb, tile_valid, rows, w_router, b_router, wg, wu, wd):
    n_rows = rows.shape[0] // SUBLANES
    per_step = MOE_TILES_PER_STEP
    step_rows = MOE_TILE * per_step * SUBLANES
    assert n_rows % (MOE_TILE * per_step) == 0
    const = lambda a: pl.BlockSpec(a.shape, lambda t, ea, eb, v: (0,) * a.ndim)

    def expert_block(shape, table, s):
        return pl.BlockSpec(shape, lambda t, ea, eb, v: ((ea, eb)[table][t * per_step + s], 0, 0))

    weight_specs, weight_args = [], []
    for s in range(per_step):
        for table in range(2):
            weight_specs += [expert_block((1, D_MODEL, D_EXPERT), table, s),
                             expert_block((1, D_MODEL, D_EXPERT), table, s),
                             expert_block((1, D_EXPERT, D_MODEL), table, s)]
            weight_args += [wg, wu, wd]
    return pl.pallas_call(
        _moe_kernel,
        grid_spec=pltpu.PrefetchScalarGridSpec(
            num_scalar_prefetch=3,
            grid=(n_rows // (MOE_TILE * per_step),),
            in_specs=[
                pl.BlockSpec((step_rows, LANES), lambda t, ea, eb, v: (t, 0)),
                const(w_router), const(b_router),
            ] + weight_specs,
            out_specs=pl.BlockSpec((step_rows, LANES), lambda t, ea, eb, v: (t, 0)),
        ),
        out_shape=jax.ShapeDtypeStruct((n_rows * SUBLANES, LANES), F32),
        compiler_params=pltpu.CompilerParams(
            dimension_semantics=("arbitrary",), vmem_limit_bytes=VMEM_LIMIT),
        name="moe_experts",
    )(tile_ea, tile_eb, tile_valid, rows, w_router, b_router, *weight_args)


def _combine_kernel(cls_ref, rank_ref, start_ref, x1_ref, mod_ref, n_post2_ref, f_hbm, o_ref, fbuf, sem):
    i = pl.program_id(0)
    n = pl.num_programs(0)
    tf = x1_ref.shape[0]

    def issue(tile, slot):
        def body(g, carry):
            first = tile * tf + g * DMA_GROUP
            slots = [start_ref[cls_ref[first + j]] + rank_ref[first + j] for j in range(DMA_GROUP)]
            first = g * DMA_GROUP
            for j in range(DMA_GROUP):
                pltpu.make_async_copy(
                    _slab(f_hbm, slots[j]), _slab(fbuf.at[slot], first + j), sem.at[slot]
                ).start(priority=j % 2)
            return carry
        lax.fori_loop(0, tf // DMA_GROUP, body, 0)

    def finish(slot):
        pltpu.make_async_copy(f_hbm.at[pl.ds(0, tf * SUBLANES)], fbuf.at[slot], sem.at[slot]).wait()
        f = jnp.concatenate([fbuf[slot, pl.ds(j, tf, stride=SUBLANES), :] for j in range(D_MODEL // LANES)],
                            axis=1)
        mod3 = mod_ref[...]
        bb = mod3.shape[0]
        f3 = f.reshape(bb, tf // bb, D_MODEL)
        g2 = mod3[:, :, 5 * D_MODEL:6 * D_MODEL]
        x13 = x1_ref[...].reshape(f3.shape)
        out = x13 + (f3 * _rms_scale(f3)) * (g2 * n_post2_ref[...].reshape(1, 1, D_MODEL))
        o_ref[...] = out.reshape(o_ref.shape)

    @pl.when(i == 0)
    def _():
        issue(0, 0)

    for parity in range(2):
        @pl.when(i % 2 == parity)
        def _():
            @pl.when(i + 1 < n)
            def _():
                issue(i + 1, 1 - parity)
            finish(parity)


def _combine_call(cls, rank, start, x1, mod3, n_post2, f_sorted, seqs_per_tile):
    n_tok = x1.shape[0]
    tf = FINAL_TILE
    tok_per_seq = n_tok // mod3.shape[0]
    if seqs_per_tile > 1:
        mod_map = lambda i, *_: (i, 0, 0)
    else:
        mod_map = lambda i, *_: ((i * tf) // tok_per_seq, 0, 0)
    return pl.pallas_call(
        _combine_kernel,
        grid_spec=pltpu.PrefetchScalarGridSpec(
            num_scalar_prefetch=3,
            grid=(n_tok // tf,),
            in_specs=[
                pl.BlockSpec((tf, D_MODEL), lambda i, *_: (i, 0)),
                pl.BlockSpec((seqs_per_tile, 1, 6 * D_MODEL), mod_map),
                pl.BlockSpec((1, D_MODEL), lambda i, *_: (0, 0)),
                pl.BlockSpec(memory_space=pl.ANY),
            ],
            out_specs=pl.BlockSpec((tf, D_MODEL), lambda i, *_: (i, 0)),
            scratch_shapes=[pltpu.VMEM((2, tf * SUBLANES, LANES), F32), pltpu.SemaphoreType.DMA((2,))],
        ),
        out_shape=jax.ShapeDtypeStruct((n_tok, D_MODEL), F32),
        compiler_params=pltpu.CompilerParams(
            dimension_semantics=("arbitrary",), vmem_limit_bytes=VMEM_LIMIT),
        name="moe_combine",
    )(cls, rank, start, x1, mod3, n_post2, f_sorted)


def _block_diag_gate(w_r, w_i):
    per_half = GATE_HALF // LRU_BLOCK
    n_half = D_LRU // GATE_HALF
    eye = np.eye(per_half, dtype=np.float32)

    def block_diag(w):
        w4 = w.reshape(n_half, per_half, LRU_BLOCK, LRU_BLOCK)
        return jnp.einsum('hnij,nm->hnimj', w4, eye).reshape(n_half, GATE_HALF, GATE_HALF)

    return jnp.concatenate([block_diag(w_r), block_diag(w_i)], axis=-1).astype(BF16)


def _rope_tables(pos):
    half = DK // 2
    inv = np.float64(ROPE_BASE) ** (-np.arange(half, dtype=np.float64) / half)
    ang = np.asarray(pos, np.float64)[:, None] * inv[None, :]
    cos = np.cos(ang)
    sin = np.sin(ang)
    return (np.concatenate([cos, cos], axis=-1).astype(np.float32),
            np.concatenate([-sin, sin], axis=-1).astype(np.float32))


def _earlier_table(m):
    return jnp.asarray(np.tril(np.ones((m, m), np.float32), -1), BF16)


def _decay_tables(c):
    log_g = np.log1p(-np.exp2(-5.0 - np.arange(N_HEADS, dtype=np.float64)))
    idx = np.arange(c, dtype=np.float64)
    diff = idx[:, None] - idx[None, :]
    mask = np.where(diff[None] >= 0, np.exp(np.maximum(diff, 0.0)[None] * log_g[:, None, None]), 0.0)
    w_state = np.exp((c - 1.0 - idx)[None, :] * log_g[:, None])
    cross_decay = np.exp((idx + 1.0)[:, None] * log_g[None, :])
    chunk_decay = np.exp(c * log_g)
    wstate_full = np.repeat(w_state.T, DK, axis=1)
    cross_full = np.repeat(cross_decay, DV, axis=1)
    cdecay_full = np.repeat(chunk_decay, DV)[None, :]
    return tuple(t.astype(np.float32) for t in (mask, wstate_full, cross_full, cdecay_full))


def kernel(x_prompt, x_sample, state_conv, state_lru, state_ret, c_prompt, c_sample, w_mod, b_mod, norm_pre_mix, norm_post_mix, norm_pre_ffn, norm_post_ffn, w_in, conv_w, conv_b, w_rgate, b_rgate, w_igate, b_igate, lru_lambda, ret_gn_w, w_out, w_router_group, b_router_group, w_router_expert, b_router_expert, w_exp_gate, w_exp_up, w_exp_down):
    bp, tp, _ = x_prompt.shape
    bs, ts, _ = x_sample.shape
    l = 0

    mod = _mod_call(jnp.concatenate([c_prompt, c_sample], axis=0), w_mod[l], b_mod[l][None, :])
    mod_p = mod[:bp][:, None, :]
    mod_s = mod[bp:][:, None, :]

    unused = LANES - N_EXPERTS - N_GROUPS
    w_router = jnp.concatenate(
        [w_router_expert[l], w_router_group[l], jnp.zeros((D_MODEL, unused), F32)], axis=1).astype(BF16)
    b_router = jnp.concatenate(
        [b_router_expert[l], b_router_group[l], jnp.zeros((unused,), F32)])[None, :]

    row = lambda vec: vec.reshape(1, -1)
    wts = (row(norm_pre_mix[l]), row(norm_post_mix[l]), row(norm_pre_ffn[l]),
           w_in[l].astype(BF16), conv_w[l], row(conv_b[l]),
           _block_diag_gate(w_rgate[l], w_igate[l]),
           row(b_rgate[l]), row(b_igate[l]), row(lru_lambda[l]), row(ret_gn_w[l]),
           w_out[l].astype(BF16), w_router, b_router)

    cos_p, sin_p = _rope_tables(np.arange(tp))
    x1_p, pk_p, cls_p, rank_p, conv_p8, lru_p8, ret_p, cnt_p = _prompt_mixer_call(
        x_prompt, mod_p, cos_p, sin_p, wts,
        _decay_tables(math.gcd(tp, RET_CHUNK)) + (_earlier_table(PROMPT_TILE),))

    cos_s, sin_s = _rope_tables(PAST_LEN + np.arange(ts))
    mask8, wstate_s, cross_s, cdecay_s = _decay_tables(math.gcd(ts, RET_CHUNK))
    eye = np.eye(SAMPLE_SEQS, dtype=np.float32)
    smask = np.stack([np.kron(eye, mask8[h]) for h in range(N_HEADS)])
    buf8 = jnp.pad(state_conv[l], ((0, 0), (0, ts - (CONV_W - 1)), (0, 0)))
    h0p = jnp.pad(state_lru[l][:, None, :], ((0, 0), (0, ts - 1), (0, 0)))
    x1_s, pk_s, cls_s, rank_s, xr_s, h_s, ret_s, cnt_all = _sample_mixer_call(
        x_sample, mod_s, cos_s, sin_s, buf8, h0p, state_ret[l], cnt_p, wts,
        (smask, wstate_s, cross_s, cdecay_s, _earlier_table(SAMPLE_SEQS * ts)))

    n_p = bp * tp
    n_tok = n_p + bs * ts
    tm = MOE_TILE
    max_tiles = n_tok // tm + N_CLASSES
    cls_p, rank_p, cls_s, rank_s = (a.reshape(-1) for a in (cls_p, rank_p, cls_s, rank_s))
    cls = jnp.concatenate([cls_p, cls_s])
    rank = jnp.concatenate([rank_p, rank_s])
    cnt = cnt_all[0, :N_CLASSES].astype(jnp.int32)
    ntile = (cnt + (tm - 1)) // tm
    padcnt = ntile * tm
    before = np.tril(np.ones((N_CLASSES, N_CLASSES), np.int32), -1)
    upto = np.tril(np.ones((N_CLASSES, N_CLASSES), np.int32))
    start = jnp.sum(before * padcnt[None, :], axis=1)
    tile_end = jnp.sum(upto * ntile[None, :], axis=1)
    n_used = jnp.sum(ntile)
    tile_ids = np.arange(max_tiles, dtype=np.int32)
    tile_valid = (tile_ids < n_used).astype(jnp.int32)
    last_used = jnp.minimum(tile_ids, n_used - 1)
    tile_cls = jnp.sum((last_used[:, None] >= tile_end[None, :]).astype(jnp.int32), axis=1)
    pair = tile_cls % N_PAIRS
    first_expert = jnp.where(pair == 5, 2, jnp.where((pair == 2) | (pair == 3), 1, 0))
    second_expert = jnp.where(pair == 0, 1, jnp.where(pair <= 2, 2, 3))
    tile_ea = (tile_cls // N_PAIRS) * PER_GROUP + first_expert
    tile_eb = (tile_cls // N_PAIRS) * PER_GROUP + second_expert

    n_post2 = row(norm_post_ffn[l])
    wg = w_exp_gate[l]
    wu = w_exp_up[l]
    wd = w_exp_down[l]
    rows = _dispatch_call(cls, rank, cnt, padcnt, start, pk_p, pk_s, max_tiles * tm)
    f_sorted = _moe_call(tile_ea, tile_eb, tile_valid, rows, w_router, b_router, wg, wu, wd)
    y_p = _combine_call(cls_p, rank_p, start, x1_p, mod_p, n_post2, f_sorted, 1)
    y_s = _combine_call(cls_s, rank_s, start, x1_s, mod_s, n_post2, f_sorted, FINAL_TILE // ts)

    conv_p = conv_p8[:, SUBLANES - (CONV_W - 1):, :]
    lru_p = lru_p8[:, SUBLANES - 1, :]
    xr_s3 = xr_s.reshape(bs, ts, D_LRU)
    conv_s = xr_s3[:, ts - (CONV_W - 1):, :]
    lru_s = h_s.reshape(bs, ts, D_LRU)[:, ts - 1, :]
    return (y_p.reshape(bp, tp, D_MODEL), y_s.reshape(bs, ts, D_MODEL),
            conv_p[None], lru_p[None], ret_p[None],
            conv_s[None], lru_s[None], ret_s[None])
```

```python
import functools
import math

import jax
import jax.numpy as jnp
import numpy as np
from jax import lax
from jax.experimental import pallas as pl
from jax.experimental.pallas import tpu as pltpu

F32 = jnp.float32
BF16 = jnp.bfloat16

D_MODEL = 1024
D_LRU = 512
D_RET = 512
N_LRU_BLOCKS = 8
LRU_BLOCK = D_LRU // N_LRU_BLOCKS
CONV_W = 4
LRU_C = 8.0
N_HEADS = 4
DK = 128
DV = 128
RET_CHUNK = 128
ROPE_BASE = 10000.0
D_IN_PROJ = 3072
N_GROUPS = 4
PER_GROUP = 4
N_EXPERTS = 16
D_EXPERT = 256
EXPM1_DIRECT_BELOW = -0.5
NORM_EPS = 1e-6
GN_EPS = 1e-5
PAST_LEN = 16384

SUBLANES = 8
LANES = 128
GATE_HALF = 256
VMEM_LIMIT = 56 * 1024 * 1024

PROMPT_TILE = 512
SAMPLE_SEQS = 16
MOE_TILE = 256
MOE_TILES_PER_STEP = 2
FINAL_TILE = 512
DISPATCH_TILE = 512
RING = 3
DMA_GROUP = 8

N_PAIRS = 6
N_CLASSES = N_GROUPS * N_PAIRS
ROW_GROUPS = D_MODEL // LANES


def _silu(x):
    return x * jax.nn.sigmoid(x)


def _rms_scale(x):
    return lax.rsqrt(jnp.mean(x * x, axis=-1, keepdims=True) + NORM_EPS)


def _masked_softmax(logits, mask):
    top = jnp.max(jnp.where(mask, logits, -jnp.inf), axis=-1, keepdims=True)
    e = jnp.where(mask, jnp.exp(logits - top), 0.0)
    return e / jnp.sum(e, axis=-1, keepdims=True)


def _mod_kernel(c_ref, w_ref, b_ref, o_ref):
    s = _silu(c_ref[...]).astype(BF16)
    o_ref[...] = jnp.dot(s, w_ref[...].astype(BF16), preferred_element_type=F32) + b_ref[...]


def _mod_call(c_all, w_mod, b_mod):
    rows = c_all.shape[0]
    ncol = w_mod.shape[1]
    blk = D_MODEL
    return pl.pallas_call(
        _mod_kernel,
        grid=(ncol // blk,),
        in_specs=[
            pl.BlockSpec((rows, D_MODEL), lambda j: (0, 0)),
            pl.BlockSpec((D_MODEL, blk), lambda j: (0, j)),
            pl.BlockSpec((1, blk), lambda j: (0, j)),
        ],
        out_specs=pl.BlockSpec((rows, blk), lambda j: (0, j)),
        out_shape=jax.ShapeDtypeStruct((rows, ncol), F32),
        compiler_params=pltpu.CompilerParams(
            dimension_semantics=("arbitrary",), vmem_limit_bytes=VMEM_LIMIT),
        name="mod",
    )(c_all, w_mod, b_mod)


def _in_proj(x3, mod3, n_pre1_ref, w_in_ref):
    bb, tt, _ = x3.shape
    sh1 = mod3[:, :, 0:D_MODEL]
    sc1 = mod3[:, :, D_MODEL:2 * D_MODEL]
    coef = n_pre1_ref[...].reshape(1, 1, D_MODEL) * (1.0 + sc1)
    u = (x3 * _rms_scale(x3)) * coef + sh1
    u2d = u.reshape(bb * tt, D_MODEL).astype(BF16)
    return jnp.dot(u2d, w_in_ref[...], preferred_element_type=F32)


def _lru_coeffs(xc, wg_ref, b_r_ref, b_i_ref, lam_ref):
    xcb = xc.astype(BF16)
    g0 = jnp.dot(xcb[:, :GATE_HALF], wg_ref[0], preferred_element_type=F32)
    g1 = jnp.dot(xcb[:, GATE_HALF:], wg_ref[1], preferred_element_type=F32)
    r = jax.nn.sigmoid(jnp.concatenate([g0[:, :GATE_HALF], g1[:, :GATE_HALF]], axis=1) + b_r_ref[...])
    i = jax.nn.sigmoid(jnp.concatenate([g0[:, GATE_HALF:], g1[:, GATE_HALF:]], axis=1) + b_i_ref[...])
    lam = lam_ref[...]
    sp = jnp.maximum(-lam, 0.0) + jnp.log1p(jnp.exp(-jnp.abs(lam)))
    log_a = -LRU_C * r * sp
    a = jnp.exp(log_a)
    y = 2.0 * log_a
    a2 = a * a
    d = a2 - 1.0
    small = d * y / jnp.log(a2)
    em1 = jnp.where(y < EXPM1_DIRECT_BELOW, d, jnp.where(d == 0.0, y, small))
    gain = jnp.sqrt(-em1)
    return a, gain * (i * xc)


def _rope(xh, cos2, sin2, lane_axis):
    return xh * cos2 + pltpu.roll(xh, DK // 2, axis=lane_axis) * sin2


def _group_norm(o):
    mu = jnp.mean(o, axis=-1, keepdims=True)
    d = o - mu
    var = jnp.mean(d * d, axis=-1, keepdims=True)
    return d * lax.rsqrt(var + GN_EPS)


def _post_mixer(x3, mod3, out_a, out_b, w_out_ref, n_post1_ref, n_pre2_ref, w_router_ref, b_router_ref,
                earlier_ref, x1_ref, pk_ref, cls_ref, rank_ref, cnt_scr):
    bb, tt, _ = x3.shape
    m = bb * tt
    y = (jnp.dot(out_a.astype(BF16), w_out_ref[0:D_LRU, :], preferred_element_type=F32)
         + jnp.dot(out_b.astype(BF16), w_out_ref[D_LRU:, :], preferred_element_type=F32))
    g1 = mod3[:, :, 2 * D_MODEL:3 * D_MODEL]
    sh2 = mod3[:, :, 3 * D_MODEL:4 * D_MODEL]
    sc2 = mod3[:, :, 4 * D_MODEL:5 * D_MODEL]
    y3 = y.reshape(bb, tt, D_MODEL)
    x1 = x3 + (y3 * _rms_scale(y3)) * (g1 * n_post1_ref[...].reshape(1, 1, D_MODEL))
    u2 = (x1 * _rms_scale(x1)) * (n_pre2_ref[...].reshape(1, 1, D_MODEL) * (1.0 + sc2)) + sh2
    x1_ref[...] = x1.reshape(m, D_MODEL)
    u2f = u2.reshape(m, D_MODEL)
    for j in range(ROW_GROUPS):
        pk_ref[pl.ds(j, m, stride=SUBLANES), :] = u2f[:, j * LANES:(j + 1) * LANES]
    u2b = u2f.astype(BF16)

    logits = jnp.dot(u2b, w_router_ref[...], preferred_element_type=F32) + b_router_ref[...]
    lane = lax.broadcasted_iota(jnp.int32, (m, LANES), 1)
    lane_f = lane.astype(F32)
    is_g = (lane >= N_EXPERTS) & (lane < N_EXPERTS + N_GROUPS)
    p_group = _masked_softmax(logits, is_g)
    p_g = jnp.max(p_group, axis=-1, keepdims=True)
    g_lane = jnp.min(jnp.where(is_g & (p_group == p_g), lane_f, float(LANES)), axis=-1, keepdims=True)
    e_lo = (g_lane - N_EXPERTS) * PER_GROUP
    in_g = (lane_f >= e_lo) & (lane_f < e_lo + PER_GROUP)
    p_e = _masked_softmax(logits, in_g)
    pm = jnp.where(in_g, p_e, -1.0)
    w1 = jnp.max(pm, axis=-1, keepdims=True)
    i1 = jnp.min(jnp.where(pm == w1, lane_f, float(LANES)), axis=-1, keepdims=True)
    pm2 = jnp.where(lane_f == i1, -1.0, pm)
    w2 = jnp.max(pm2, axis=-1, keepdims=True)
    i2 = jnp.min(jnp.where(pm2 == w2, lane_f, float(LANES)), axis=-1, keepdims=True)
    a = jnp.minimum(i1, i2) - e_lo
    b = jnp.maximum(i1, i2) - e_lo
    pair = jnp.where(a == 0.0, jnp.where(b == 3.0, 4.0, b - 1.0), jnp.where(a == 1.0, b, 5.0))
    cls = (g_lane - N_EXPERTS) * N_PAIRS + pair
    onehot = lane_f == cls
    prefix = jnp.dot(earlier_ref[...], jnp.where(onehot, 1.0, 0.0).astype(BF16), preferred_element_type=F32)
    run = cnt_scr[0:1, :]
    rank = jnp.sum(jnp.where(onehot, prefix + run, 0.0), axis=-1, keepdims=True)
    cnt_scr[...] = jnp.broadcast_to(
        run + jnp.sum(jnp.where(onehot, 1.0, 0.0), axis=0, keepdims=True), cnt_scr.shape)
    route = jnp.where(lane == 0, cls, jnp.where(lane == 1, rank, 0.0))
    route_t = jnp.transpose(route)
    cls_ref[...] = route_t[0:1, :].astype(jnp.int32)
    rank_ref[...] = route_t[1:2, :].astype(jnp.int32)


def _group_scan(a3, b3):
    tpos = lax.broadcasted_iota(jnp.int32, a3.shape, 1)
    s = 1
    while s < a3.shape[1]:
        keep = tpos >= s
        a_sh = jnp.where(keep, pltpu.roll(a3, s, axis=1), 1.0)
        b_sh = jnp.where(keep, pltpu.roll(b3, s, axis=1), 0.0)
        b3 = a3 * b_sh + b3
        a3 = a3 * a_sh
        s *= 2
    return a3, b3


def _scan_rows(a, b, h0):
    n, c = a.shape
    groups = n // SUBLANES
    a3, b3 = _group_scan(a.reshape(groups, SUBLANES, c), b.reshape(groups, SUBLANES, c))
    carry = h0
    out = []
    for g in range(groups):
        hg = b3[g] + a3[g] * carry
        out.append(hg)
        carry = hg[SUBLANES - 1:SUBLANES, :]
    return jnp.concatenate(out, axis=0)


def _prompt_mixer_kernel(x_ref, mod_ref, cos_ref, sin_ref,
                         n_pre1_ref, n_post1_ref, n_pre2_ref,
                         w_in_ref, conv_w_ref, conv_b_ref, wg_ref, b_r_ref, b_i_ref, lam_ref,
                         gn_w_ref, w_out_ref, w_router_ref, b_router_ref,
                         mask_ref, wstate_ref, cross_ref, cdecay_ref, earlier_ref,
                         x1_ref, pk_ref, cls_ref, rank_ref, conv_out_ref, lru_out_ref, ret_out_ref,
                         cnt_out_ref,
                         conv_scr, h_scr, s_scr, cnt_scr):
    t = pl.program_id(1)
    tt = x_ref.shape[1]

    @pl.when((pl.program_id(0) == 0) & (t == 0))
    def _():
        cnt_scr[...] = jnp.zeros_like(cnt_scr)

    @pl.when(t == 0)
    def _():
        conv_scr[...] = jnp.zeros_like(conv_scr)
        h_scr[...] = jnp.zeros_like(h_scr)
        s_scr[...] = jnp.zeros_like(s_scr)

    x3 = x_ref[...]
    mod3 = mod_ref[...]
    z = _in_proj(x3, mod3, n_pre1_ref, w_in_ref)
    xr, yg, q, k, v, g = (z[:, c * D_LRU:(c + 1) * D_LRU] for c in range(D_IN_PROJ // D_LRU))

    groups = tt // SUBLANES
    xr3 = xr.reshape(groups, SUBLANES, D_LRU)
    tpos = lax.broadcasted_iota(jnp.int32, xr3.shape, 1)
    tail = conv_scr[...]
    xc3 = jnp.broadcast_to(conv_b_ref[...].reshape(1, 1, D_LRU), xr3.shape)
    for j in range(CONV_W):
        back = CONV_W - 1 - j
        w_j = conv_w_ref[j:j + 1, :].reshape(1, 1, D_LRU)
        if back == 0:
            term = xr3
        else:
            cur = pltpu.roll(xr3, back, axis=1)
            first = pltpu.roll(tail, back, axis=0).reshape(1, SUBLANES, D_LRU)
            prev = jnp.concatenate([first, cur[:groups - 1]], axis=0)
            term = jnp.where(tpos >= back, cur, prev)
        xc3 = xc3 + term * w_j
    xc = xc3.reshape(tt, D_LRU)
    conv_scr[...] = xr[tt - SUBLANES:, :]

    a, b = _lru_coeffs(xc, wg_ref, b_r_ref, b_i_ref, lam_ref)
    hseq = _scan_rows(a, b, h_scr[0:1, :])
    h_scr[...] = jnp.broadcast_to(hseq[tt - 1:tt, :], h_scr.shape)
    out_a = hseq * jax.nn.gelu(yg, approximate=True)

    cos2 = cos_ref[...]
    sin2 = sin_ref[...]
    scale = DK ** -0.5
    o_heads = []
    for h in range(N_HEADS):
        hs = slice(h * DK, (h + 1) * DK)
        qh = (_rope(q[:, hs], cos2, sin2, 1) * scale).astype(BF16)
        kh = _rope(k[:, hs], cos2, sin2, 1)
        vh = v[:, hs].astype(BF16)
        o_chunks = []
        for c in range(tt // RET_CHUNK):
            cs = slice(c * RET_CHUNK, (c + 1) * RET_CHUNK)
            qc = qh[cs]
            kc = kh[cs]
            vc = vh[cs]
            s_prev = s_scr[h]
            scores = lax.dot_general(qc, kc.astype(BF16), (((1,), (1,)), ((), ())),
                                     preferred_element_type=F32) * mask_ref[h]
            inner = jnp.dot(scores.astype(BF16), vc, preferred_element_type=F32)
            cross = jnp.dot(qc, s_prev.astype(BF16), preferred_element_type=F32) * cross_ref[:, hs]
            kw = (kc * wstate_ref[:, hs]).astype(BF16)
            kv = lax.dot_general(kw, vc, (((0,), (0,)), ((), ())), preferred_element_type=F32)
            s_scr[h] = cdecay_ref[:, hs] * s_prev + kv
            o_chunks.append(inner + cross)
        o_heads.append(_group_norm(jnp.concatenate(o_chunks, axis=0)))
    o = jnp.concatenate(o_heads, axis=1)
    out_b = o * gn_w_ref[...] * _silu(g)

    _post_mixer(x3, mod3, out_a, out_b, w_out_ref, n_post1_ref, n_pre2_ref, w_router_ref, b_router_ref,
                earlier_ref, x1_ref, pk_ref, cls_ref, rank_ref, cnt_scr)
    cnt_out_ref[...] = cnt_scr[...]

    @pl.when(t == pl.num_programs(1) - 1)
    def _():
        conv_out_ref[0] = xr[tt - SUBLANES:, :]
        lru_out_ref[0] = hseq[tt - SUBLANES:, :]
        ret_out_ref[0] = s_scr[...]


def _const_spec(shape):
    nd = len(shape)
    return pl.BlockSpec(shape, lambda *_: (0,) * nd)


def _prompt_mixer_call(x, mod3, cos2, sin2, wts, tables):
    bsz, seq, _ = x.shape
    tt = PROMPT_TILE
    nt = seq // tt
    n_tok = bsz * seq
    tok_spec = pl.BlockSpec((tt, D_MODEL), lambda b, t: (b * nt + t, 0))
    in_specs = [
        pl.BlockSpec((1, tt, D_MODEL), lambda b, t: (b, t, 0)),
        pl.BlockSpec((1, 1, 6 * D_MODEL), lambda b, t: (b, 0, 0)),
        pl.BlockSpec((tt, LANES), lambda b, t: (t, 0)),
        pl.BlockSpec((tt, LANES), lambda b, t: (t, 0)),
    ] + [_const_spec(w.shape) for w in wts] + [_const_spec(tb.shape) for tb in tables]
    out_specs = [
        tok_spec,
        pl.BlockSpec((tt * SUBLANES, LANES), lambda b, t: (b * nt + t, 0)),
        pl.BlockSpec((1, tt), lambda b, t: (0, b * nt + t)),
        pl.BlockSpec((1, tt), lambda b, t: (0, b * nt + t)),
        pl.BlockSpec((1, SUBLANES, D_LRU), lambda b, t: (b, 0, 0)),
        pl.BlockSpec((1, SUBLANES, D_LRU), lambda b, t: (b, 0, 0)),
        pl.BlockSpec((1, N_HEADS, DK, DV), lambda b, t: (b, 0, 0, 0)),
        pl.BlockSpec((SUBLANES, LANES), lambda b, t: (0, 0)),
    ]
    out_shape = [
        jax.ShapeDtypeStruct((n_tok, D_MODEL), F32),
        jax.ShapeDtypeStruct((n_tok * SUBLANES, LANES), F32),
        jax.ShapeDtypeStruct((1, n_tok), jnp.int32),
        jax.ShapeDtypeStruct((1, n_tok), jnp.int32),
        jax.ShapeDtypeStruct((bsz, SUBLANES, D_LRU), F32),
        jax.ShapeDtypeStruct((bsz, SUBLANES, D_LRU), F32),
        jax.ShapeDtypeStruct((bsz, N_HEADS, DK, DV), F32),
        jax.ShapeDtypeStruct((SUBLANES, LANES), F32),
    ]
    return pl.pallas_call(
        _prompt_mixer_kernel,
        grid=(bsz, nt),
        in_specs=in_specs,
        out_specs=out_specs,
        out_shape=out_shape,
        scratch_shapes=[
            pltpu.VMEM((SUBLANES, D_LRU), F32),
            pltpu.VMEM((SUBLANES, D_LRU), F32),
            pltpu.VMEM((N_HEADS, DK, DV), F32),
            pltpu.VMEM((SUBLANES, LANES), F32),
        ],
        compiler_params=pltpu.CompilerParams(
            dimension_semantics=("arbitrary", "arbitrary"), vmem_limit_bytes=VMEM_LIMIT),
        name="prompt_mixer",
    )(x, mod3, cos2, sin2, *wts, *tables)


def _sample_mixer_kernel(x_ref, mod_ref, cos_ref, sin_ref, buf_ref, h0_ref, s0_ref, cnt_in_ref,
                         n_pre1_ref, n_post1_ref, n_pre2_ref,
                         w_in_ref, conv_w_ref, conv_b_ref, wg_ref, b_r_ref, b_i_ref, lam_ref,
                         gn_w_ref, w_out_ref, w_router_ref, b_router_ref,
                         smask_ref, wstate_ref, cross_ref, cdecay_ref, earlier_ref,
                         x1_ref, pk_ref, cls_ref, rank_ref, xr_out_ref, h_out_ref, ret_out_ref,
                         cnt_out_ref,
                         cnt_scr):
    bb, ts, _ = x_ref.shape
    m = bb * ts

    @pl.when(pl.program_id(0) == 0)
    def _():
        cnt_scr[...] = cnt_in_ref[...]

    x3 = x_ref[...]
    mod3 = mod_ref[...].reshape(bb, 1, 6 * D_MODEL)
    z = _in_proj(x3, mod3, n_pre1_ref, w_in_ref)
    xr, yg, q, k, v, g = (z[:, c * D_LRU:(c + 1) * D_LRU] for c in range(D_IN_PROJ // D_LRU))
    xr_out_ref[...] = xr

    xr3 = xr.reshape(bb, ts, D_LRU)
    buf3 = buf_ref[...]
    tpos = lax.broadcasted_iota(jnp.int32, (bb, ts, D_LRU), 1)
    xc3 = jnp.broadcast_to(conv_b_ref[...].reshape(1, 1, D_LRU), (bb, ts, D_LRU))
    for j in range(CONV_W):
        back = CONV_W - 1 - j
        w_j = conv_w_ref[j:j + 1, :].reshape(1, 1, D_LRU)
        if back == 0:
            term = xr3
        else:
            cur = pltpu.roll(xr3, back, axis=1)
            up = CONV_W - 1 - back
            old = buf3 if up == 0 else pltpu.roll(buf3, ts - up, axis=1)
            term = jnp.where(tpos >= back, cur, old)
        xc3 = xc3 + term * w_j
    xc = xc3.reshape(m, D_LRU)

    a, b = _lru_coeffs(xc, wg_ref, b_r_ref, b_i_ref, lam_ref)
    a3 = a.reshape(bb, ts, D_LRU)
    b3 = b.reshape(bb, ts, D_LRU) + a3 * h0_ref[...]
    _, h3 = _group_scan(a3, b3)
    hseq = h3.reshape(m, D_LRU)
    h_out_ref[...] = hseq
    out_a = hseq * jax.nn.gelu(yg, approximate=True)

    cos2 = cos_ref[...].reshape(1, ts, LANES)
    sin2 = sin_ref[...].reshape(1, ts, LANES)
    scale = DK ** -0.5
    o_heads = []
    for h in range(N_HEADS):
        hs = slice(h * DK, (h + 1) * DK)
        q3 = (_rope(q[:, hs].reshape(bb, ts, DK), cos2, sin2, 2) * scale).astype(BF16)
        k3 = _rope(k[:, hs].reshape(bb, ts, DK), cos2, sin2, 2)
        v3 = v[:, hs].reshape(bb, ts, DV).astype(BF16)
        q2 = q3.reshape(m, DK)
        k2 = k3.reshape(m, DK).astype(BF16)
        v2 = v3.reshape(m, DV)
        scores = lax.dot_general(q2, k2, (((1,), (1,)), ((), ())),
                                 preferred_element_type=F32) * smask_ref[h]
        inner = jnp.dot(scores.astype(BF16), v2, preferred_element_type=F32)
        s0h = s0_ref[:, h]
        cross = jnp.einsum('bid,bde->bie', q3, s0h.astype(BF16), preferred_element_type=F32)
        cross = cross * cross_ref[:, hs].reshape(1, ts, DV)
        kw3 = (k3 * wstate_ref[:, hs].reshape(1, ts, DK)).astype(BF16)
        kv = jnp.einsum('bjd,bje->bde', kw3, v3, preferred_element_type=F32)
        ret_out_ref[:, h] = cdecay_ref[:, hs].reshape(1, 1, DV) * s0h + kv
        o_heads.append(_group_norm(inner + cross.reshape(m, DV)))
    o = jnp.concatenate(o_heads, axis=1)
    out_b = o * gn_w_ref[...] * _silu(g)

    _post_mixer(x3, mod3, out_a, out_b, w_out_ref, n_post1_ref, n_pre2_ref, w_router_ref, b_router_ref,
                earlier_ref, x1_ref, pk_ref, cls_ref, rank_ref, cnt_scr)
    cnt_out_ref[...] = cnt_scr[...]


def _sample_mixer_call(x, mod3, cos2, sin2, buf8, h0p, s0, cnt_in, wts, tables):
    bsz, ts, _ = x.shape
    bb = SAMPLE_SEQS
    m = bb * ts
    n_tok = bsz * ts
    seq_spec = lambda w: pl.BlockSpec((bb, ts, w), lambda i: (i, 0, 0))
    tok_spec = lambda w: pl.BlockSpec((m, w), lambda i: (i, 0))
    in_specs = [
        seq_spec(D_MODEL),
        pl.BlockSpec((bb, 6 * D_MODEL), lambda i: (i, 0)),
        _const_spec(cos2.shape),
        _const_spec(sin2.shape),
        seq_spec(D_LRU),
        seq_spec(D_LRU),
        pl.BlockSpec((bb, N_HEADS, DK, DV), lambda i: (i, 0, 0, 0)),
        _const_spec(cnt_in.shape),
    ] + [_const_spec(w.shape) for w in wts] + [_const_spec(tb.shape) for tb in tables]
    out_specs = [
        tok_spec(D_MODEL),
        pl.BlockSpec((m * SUBLANES, LANES), lambda i: (i, 0)),
        pl.BlockSpec((1, m), lambda i: (0, i)),
        pl.BlockSpec((1, m), lambda i: (0, i)),
        tok_spec(D_LRU),
        tok_spec(D_LRU),
        pl.BlockSpec((bb, N_HEADS, DK, DV), lambda i: (i, 0, 0, 0)),
        _const_spec(cnt_in.shape),
    ]
    out_shape = [
        jax.ShapeDtypeStruct((n_tok, D_MODEL), F32),
        jax.ShapeDtypeStruct((n_tok * SUBLANES, LANES), F32),
        jax.ShapeDtypeStruct((1, n_tok), jnp.int32),
        jax.ShapeDtypeStruct((1, n_tok), jnp.int32),
        jax.ShapeDtypeStruct((n_tok, D_LRU), F32),
        jax.ShapeDtypeStruct((n_tok, D_LRU), F32),
        jax.ShapeDtypeStruct((bsz, N_HEADS, DK, DV), F32),
        jax.ShapeDtypeStruct(cnt_in.shape, F32),
    ]
    return pl.pallas_call(
        _sample_mixer_kernel,
        grid=(bsz // bb,),
        in_specs=in_specs,
        out_specs=out_specs,
        out_shape=out_shape,
        scratch_shapes=[pltpu.VMEM((SUBLANES, LANES), F32)],
        compiler_params=pltpu.CompilerParams(
            dimension_semantics=("arbitrary",), vmem_limit_bytes=VMEM_LIMIT),
        name="sample_mixer",
    )(x, mod3, cos2, sin2, buf8, h0p, s0, cnt_in, *wts, *tables)


def _slab(ref, r):
    return ref.at[pl.ds(pl.multiple_of(r * SUBLANES, SUBLANES), SUBLANES)]


def _dispatch_kernel(cls_ref, rank_ref, cnt_ref, padcnt_ref, start_ref, srcp_ref, srcs_ref, out_ref, ring, sem,
                     *, p_tiles):
    i = pl.program_id(0)
    n = pl.num_programs(0)
    tile_rows = ring.shape[1]
    td = tile_rows // SUBLANES
    moe_rows = MOE_TILE * SUBLANES

    def issue(slot):
        def body(g, carry):
            first = i * td + g * DMA_GROUP
            slots = [start_ref[cls_ref[first + j]] + rank_ref[first + j] for j in range(DMA_GROUP)]
            first = g * DMA_GROUP
            for j in range(DMA_GROUP):
                pltpu.make_async_copy(
                    _slab(ring.at[slot], first + j), _slab(out_ref, slots[j]), sem.at[slot]
                ).start(priority=j % 2)
            return carry
        lax.fori_loop(0, td // DMA_GROUP, body, 0)

    def wait_tile(slot):
        pltpu.make_async_copy(ring.at[slot], out_ref.at[pl.ds(0, tile_rows)], sem.at[slot]).wait()

    def zero_fill(slot):
        ring[slot] = jnp.zeros((tile_rows, LANES), F32)
        zero_src = ring.at[slot]

        def per_class(c, carry):
            lo = start_ref[c] + cnt_ref[c]
            hi = start_ref[c] + padcnt_ref[c]

            def fill(r, carry2):
                pltpu.make_async_copy(_slab(zero_src, 0), _slab(out_ref, r), sem.at[slot]).start()
                return carry2
            lax.fori_loop(lo, hi, fill, 0)

            def done(r, carry2):
                pltpu.make_async_copy(_slab(zero_src, 0), _slab(out_ref, 0), sem.at[slot]).wait()
                return carry2
            lax.fori_loop(lo, hi, done, 0)
            return carry
        lax.fori_loop(0, N_CLASSES, per_class, 0)

        used_tiles = (start_ref[N_CLASSES - 1] + padcnt_ref[N_CLASSES - 1]) // MOE_TILE
        all_tiles = out_ref.shape[0] // moe_rows

        def tile_copy(t):
            return pltpu.make_async_copy(
                zero_src.at[pl.ds(0, moe_rows)],
                out_ref.at[pl.ds(pl.multiple_of(t * moe_rows, moe_rows), moe_rows)], sem.at[slot])

        def fill_tile(t, carry):
            tile_copy(t).start()
            return carry
        lax.fori_loop(used_tiles, all_tiles, fill_tile, 0)

        def done_tile(t, carry):
            tile_copy(t).wait()
            return carry
        lax.fori_loop(used_tiles, all_tiles, done_tile, 0)

    for s in range(RING):
        @pl.when(i % RING == s)
        def _():
            @pl.when(i < p_tiles)
            def _():
                ring[s] = srcp_ref[...]

            @pl.when(i >= p_tiles)
            def _():
                ring[s] = srcs_ref[...]

            issue(s)

            @pl.when(i >= RING - 1)
            def _():
                wait_tile((s + 1) % RING)

            @pl.when(i == n - 1)
            def _():
                for back in range(RING - 2, -1, -1):
                    wait_tile((s - back) % RING)
                zero_fill(s)


def _dispatch_call(cls, rank, cnt, padcnt, start, pk_p, pk_s, n_rows):
    td = DISPATCH_TILE
    tile_rows = td * SUBLANES
    p_tiles = pk_p.shape[0] // tile_rows
    s_tiles = pk_s.shape[0] // tile_rows
    assert p_tiles + s_tiles >= RING and td >= MOE_TILE
    return pl.pallas_call(
        functools.partial(_dispatch_kernel, p_tiles=p_tiles),
        grid_spec=pltpu.PrefetchScalarGridSpec(
            num_scalar_prefetch=5,
            grid=(p_tiles + s_tiles,),
            in_specs=[
                pl.BlockSpec((tile_rows, LANES), lambda i, *_: (jnp.minimum(i, p_tiles - 1), 0)),
                pl.BlockSpec((tile_rows, LANES), lambda i, *_: (jnp.maximum(i - p_tiles, 0), 0)),
            ],
            out_specs=pl.BlockSpec(memory_space=pl.ANY),
            scratch_shapes=[pltpu.VMEM((RING, tile_rows, LANES), F32),
                            pltpu.SemaphoreType.DMA((RING,))],
        ),
        out_shape=jax.ShapeDtypeStruct((n_rows * SUBLANES, LANES), F32),
        compiler_params=pltpu.CompilerParams(
            dimension_semantics=("arbitrary",), has_side_effects=True, vmem_limit_bytes=VMEM_LIMIT),
        name="moe_dispatch",
    )(cls, rank, cnt, padcnt, start, pk_p, pk_s)


def _moe_kernel(ea_ref, eb_ref, valid_ref, xs_ref, w_router_ref, b_router_ref, *refs):
    f_ref = refs[-1]
    for s in range(MOE_TILES_PER_STEP):
        _moe_tile(pl.program_id(0) * MOE_TILES_PER_STEP + s, s * MOE_TILE * SUBLANES, ea_ref, eb_ref,
                  valid_ref, xs_ref, w_router_ref, b_router_ref, *refs[6 * s:6 * s + 6], f_ref)


def _moe_tile(t, row0, ea_ref, eb_ref, valid_ref, xs_ref, w_router_ref, b_router_ref,
              wga_ref, wua_ref, wda_ref, wgb_ref, wub_ref, wdb_ref, f_ref):
    tm = MOE_TILE

    @pl.when(valid_ref[t] == 1)
    def _():
        x = jnp.concatenate([xs_ref[pl.ds(row0 + j, tm, stride=SUBLANES), :] for j in range(ROW_GROUPS)],
                            axis=1).astype(BF16)
        e_a = ea_ref[t]
        e_b = eb_ref[t]
        e_lo = (e_a // PER_GROUP) * PER_GROUP
        logits = jnp.dot(x, w_router_ref[...], preferred_element_type=F32) + b_router_ref[...]
        lane = lax.broadcasted_iota(jnp.int32, (tm, LANES), 1)
        pick = lambda p, idx: jnp.sum(jnp.where(lane == idx, p, 0.0), axis=-1, keepdims=True)
        p_group = _masked_softmax(logits, (lane >= N_EXPERTS) & (lane < N_EXPERTS + N_GROUPS))
        p_g = pick(p_group, N_EXPERTS + e_a // PER_GROUP)
        p_e = _masked_softmax(logits, (lane >= e_lo) & (lane < e_lo + PER_GROUP))
        w_a = pick(p_e, e_a)
        w_b = pick(p_e, e_b)
        wsum = w_a + w_b

        def expert(wg_ref, wu_ref, gate):
            hg = jnp.dot(x, wg_ref[0].astype(BF16), preferred_element_type=F32)
            hu = jnp.dot(x, wu_ref[0].astype(BF16), preferred_element_type=F32)
            return (_silu(hg) * hu * gate).astype(BF16)

        ha = expert(wga_ref, wua_ref, p_g * (w_a / wsum))
        hb = expert(wgb_ref, wub_ref, p_g * (w_b / wsum))
        for c in range(D_MODEL // GATE_HALF):
            cols = slice(c * GATE_HALF, (c + 1) * GATE_HALF)
            f = (jnp.dot(ha, wda_ref[0, :, cols].astype(BF16), preferred_element_type=F32)
                 + jnp.dot(hb, wdb_ref[0, :, cols].astype(BF16), preferred_element_type=F32))
            for jj in range(GATE_HALF // LANES):
                j = c * (GATE_HALF // LANES) + jj
                f_ref[pl.ds(row0 + j, tm, stride=SUBLANES), :] = f[:, jj * LANES:(jj + 1) * LANES]

    @pl.when(valid_ref[t] == 0)
    def _():
        f_ref[pl.ds(row0, tm * SUBLANES), :] = jnp.zeros((tm * SUBLANES, LANES), F32)


def _moe_call(tile_ea, tile_eb, tile_valid, rows, w_router, b_router, wg, wu, wd):
    n_rows = rows.shape[0] // SUBLANES
    per_step = MOE_TILES_PER_STEP
    step_rows = MOE_TILE * per_step * SUBLANES
    assert n_rows % (MOE_TILE * per_step) == 0
    const = lambda a: pl.BlockSpec(a.shape, lambda t, ea, eb, v: (0,) * a.ndim)

    def expert_block(shape, table, s):
        return pl.BlockSpec(shape, lambda t, ea, eb, v: ((ea, eb)[table][t * per_step + s], 0, 0))

    weight_specs, weight_args = [], []
    for s in range(per_step):
        for table in range(2):
            weight_specs += [expert_block((1, D_MODEL, D_EXPERT), table, s),
                             expert_block((1, D_MODEL, D_EXPERT), table, s),
                             expert_block((1, D_EXPERT, D_MODEL), table, s)]
            weight_args += [wg, wu, wd]
    return pl.pallas_call(
        _moe_kernel,
        grid_spec=pltpu.PrefetchScalarGridSpec(
            num_scalar_prefetch=3,
            grid=(n_rows // (MOE_TILE * per_step),),
            in_specs=[
                pl.BlockSpec((step_rows, LANES), lambda t, ea, eb, v: (t, 0)),
                const(w_router), const(b_router),
            ] + weight_specs,
            out_specs=pl.BlockSpec((step_rows, LANES), lambda t, ea, eb, v: (t, 0)),
        ),
        out_shape=jax.ShapeDtypeStruct((n_rows * SUBLANES, LANES), F32),
        compiler_params=pltpu.CompilerParams(
            dimension_semantics=("arbitrary",), vmem_limit_bytes=VMEM_LIMIT),
        name="moe_experts",
    )(tile_ea, tile_eb, tile_valid, rows, w_router, b_router, *weight_args)


def _combine_kernel(cls_ref, rank_ref, start_ref, x1_ref, mod_ref, n_post2_ref, f_hbm, o_ref, fbuf, sem):
    i = pl.program_id(0)
    n = pl.num_programs(0)
    tf = x1_ref.shape[0]

    def issue(tile, slot):
        def body(g, carry):
            first = tile * tf + g * DMA_GROUP
            slots = [start_ref[cls_ref[first + j]] + rank_ref[first + j] for j in range(DMA_GROUP)]
            first = g * DMA_GROUP
            for j in range(DMA_GROUP):
                pltpu.make_async_copy(
                    _slab(f_hbm, slots[j]), _slab(fbuf.at[slot], first + j), sem.at[slot]
                ).start(priority=j % 2)
            return carry
        lax.fori_loop(0, tf // DMA_GROUP, body, 0)

    def finish(slot):
        pltpu.make_async_copy(f_hbm.at[pl.ds(0, tf * SUBLANES)], fbuf.at[slot], sem.at[slot]).wait()
        f = jnp.concatenate([fbuf[slot, pl.ds(j, tf, stride=SUBLANES), :] for j in range(D_MODEL // LANES)],
                            axis=1)
        bb = mod_ref.shape[0]
        mod3 = mod_ref[...].reshape(bb, 1, 6 * D_MODEL)
        f3 = f.reshape(bb, tf // bb, D_MODEL)
        g2 = mod3[:, :, 5 * D_MODEL:6 * D_MODEL]
        x13 = x1_ref[...].reshape(f3.shape)
        out = x13 + (f3 * _rms_scale(f3)) * (g2 * n_post2_ref[...].reshape(1, 1, D_MODEL))
        o_ref[...] = out.reshape(o_ref.shape)

    @pl.when(i == 0)
    def _():
        issue(0, 0)

    for parity in range(2):
        @pl.when(i % 2 == parity)
        def _():
            @pl.when(i + 1 < n)
            def _():
                issue(i + 1, 1 - parity)
            finish(parity)


def _combine_call(cls, rank, start, x1, mod3, n_post2, f_sorted, seqs_per_tile):
    n_tok = x1.shape[0]
    tf = FINAL_TILE
    tok_per_seq = n_tok // mod3.shape[0]
    if seqs_per_tile > 1:
        mod_spec = pl.BlockSpec((seqs_per_tile, 6 * D_MODEL), lambda i, *_: (i, 0))
    else:
        mod_spec = pl.BlockSpec((1, 1, 6 * D_MODEL), lambda i, *_: ((i * tf) // tok_per_seq, 0, 0))
    return pl.pallas_call(
        _combine_kernel,
        grid_spec=pltpu.PrefetchScalarGridSpec(
            num_scalar_prefetch=3,
            grid=(n_tok // tf,),
            in_specs=[
                pl.BlockSpec((tf, D_MODEL), lambda i, *_: (i, 0)),
                mod_spec,
                pl.BlockSpec((1, D_MODEL), lambda i, *_: (0, 0)),
                pl.BlockSpec(memory_space=pl.ANY),
            ],
            out_specs=pl.BlockSpec((tf, D_MODEL), lambda i, *_: (i, 0)),
            scratch_shapes=[pltpu.VMEM((2, tf * SUBLANES, LANES), F32), pltpu.SemaphoreType.DMA((2,))],
        ),
        out_shape=jax.ShapeDtypeStruct((n_tok, D_MODEL), F32),
        compiler_params=pltpu.CompilerParams(
            dimension_semantics=("arbitrary",), vmem_limit_bytes=VMEM_LIMIT),
        name="moe_combine",
    )(cls, rank, start, x1, mod3, n_post2, f_sorted)


def _block_diag_gate(w_r, w_i):
    per_half = GATE_HALF // LRU_BLOCK
    n_half = D_LRU // GATE_HALF
    eye = np.eye(per_half, dtype=np.float32)

    def block_diag(w):
        w4 = w.reshape(n_half, per_half, LRU_BLOCK, LRU_BLOCK)
        return jnp.einsum('hnij,nm->hnimj', w4, eye).reshape(n_half, GATE_HALF, GATE_HALF)

    return jnp.concatenate([block_diag(w_r), block_diag(w_i)], axis=-1).astype(BF16)


def _rope_tables(pos):
    half = DK // 2
    inv = np.float64(ROPE_BASE) ** (-np.arange(half, dtype=np.float64) / half)
    ang = np.asarray(pos, np.float64)[:, None] * inv[None, :]
    cos = np.cos(ang)
    sin = np.sin(ang)
    return (np.concatenate([cos, cos], axis=-1).astype(np.float32),
            np.concatenate([-sin, sin], axis=-1).astype(np.float32))


def _earlier_table(m):
    return jnp.asarray(np.tril(np.ones((m, m), np.float32), -1), BF16)


def _decay_tables(c):
    log_g = np.log1p(-np.exp2(-5.0 - np.arange(N_HEADS, dtype=np.float64)))
    idx = np.arange(c, dtype=np.float64)
    diff = idx[:, None] - idx[None, :]
    mask = np.where(diff[None] >= 0, np.exp(np.maximum(diff, 0.0)[None] * log_g[:, None, None]), 0.0)
    w_state = np.exp((c - 1.0 - idx)[None, :] * log_g[:, None])
    cross_decay = np.exp((idx + 1.0)[:, None] * log_g[None, :])
    chunk_decay = np.exp(c * log_g)
    wstate_full = np.repeat(w_state.T, DK, axis=1)
    cross_full = np.repeat(cross_decay, DV, axis=1)
    cdecay_full = np.repeat(chunk_decay, DV)[None, :]
    return tuple(t.astype(np.float32) for t in (mask, wstate_full, cross_full, cdecay_full))


def kernel(x_prompt, x_sample, state_conv, state_lru, state_ret, c_prompt, c_sample, w_mod, b_mod, norm_pre_mix, norm_post_mix, norm_pre_ffn, norm_post_ffn, w_in, conv_w, conv_b, w_rgate, b_rgate, w_igate, b_igate, lru_lambda, ret_gn_w, w_out, w_router_group, b_router_group, w_router_expert, b_router_expert, w_exp_gate, w_exp_up, w_exp_down):
    bp, tp, _ = x_prompt.shape
    bs, ts, _ = x_sample.shape
    l = 0

    mod = _mod_call(jnp.concatenate([c_prompt, c_sample], axis=0), w_mod[l], b_mod[l][None, :])
    mod_p = mod[:bp][:, None, :]
    mod_s = mod[bp:]

    unused = LANES - N_EXPERTS - N_GROUPS
    w_router = jnp.concatenate(
        [w_router_expert[l], w_router_group[l], jnp.zeros((D_MODEL, unused), F32)], axis=1).astype(BF16)
    b_router = jnp.concatenate(
        [b_router_expert[l], b_router_group[l], jnp.zeros((unused,), F32)])[None, :]

    row = lambda vec: vec.reshape(1, -1)
    wts = (row(norm_pre_mix[l]), row(norm_post_mix[l]), row(norm_pre_ffn[l]),
           w_in[l].astype(BF16), conv_w[l], row(conv_b[l]),
           _block_diag_gate(w_rgate[l], w_igate[l]),
           row(b_rgate[l]), row(b_igate[l]), row(lru_lambda[l]), row(ret_gn_w[l]),
           w_out[l].astype(BF16), w_router, b_router)

    cos_p, sin_p = _rope_tables(np.arange(tp))
    x1_p, pk_p, cls_p, rank_p, conv_p8, lru_p8, ret_p, cnt_p = _prompt_mixer_call(
        x_prompt, mod_p, cos_p, sin_p, wts,
        _decay_tables(math.gcd(tp, RET_CHUNK)) + (_earlier_table(PROMPT_TILE),))

    cos_s, sin_s = _rope_tables(PAST_LEN + np.arange(ts))
    mask8, wstate_s, cross_s, cdecay_s = _decay_tables(math.gcd(ts, RET_CHUNK))
    eye = np.eye(SAMPLE_SEQS, dtype=np.float32)
    smask = np.stack([np.kron(eye, mask8[h]) for h in range(N_HEADS)])
    buf8 = jnp.pad(state_conv[l], ((0, 0), (0, ts - (CONV_W - 1)), (0, 0)))
    h0p = jnp.pad(state_lru[l][:, None, :], ((0, 0), (0, ts - 1), (0, 0)))
    x1_s, pk_s, cls_s, rank_s, xr_s, h_s, ret_s, cnt_all = _sample_mixer_call(
        x_sample, mod_s, cos_s, sin_s, buf8, h0p, state_ret[l], cnt_p, wts,
        (smask, wstate_s, cross_s, cdecay_s, _earlier_table(SAMPLE_SEQS * ts)))

    n_p = bp * tp
    n_tok = n_p + bs * ts
    tm = MOE_TILE
    max_tiles = n_tok // tm + N_CLASSES
    cls_p, rank_p, cls_s, rank_s = (a.reshape(-1) for a in (cls_p, rank_p, cls_s, rank_s))
    cls = jnp.concatenate([cls_p, cls_s])
    rank = jnp.concatenate([rank_p, rank_s])
    cnt = cnt_all[0, :N_CLASSES].astype(jnp.int32)
    ntile = (cnt + (tm - 1)) // tm
    padcnt = ntile * tm
    before = np.tril(np.ones((N_CLASSES, N_CLASSES), np.int32), -1)
    upto = np.tril(np.ones((N_CLASSES, N_CLASSES), np.int32))
    start = jnp.sum(before * padcnt[None, :], axis=1)
    tile_end = jnp.sum(upto * ntile[None, :], axis=1)
    n_used = jnp.sum(ntile)
    tile_ids = np.arange(max_tiles, dtype=np.int32)
    tile_valid = (tile_ids < n_used).astype(jnp.int32)
    last_used = jnp.minimum(tile_ids, n_used - 1)
    tile_cls = jnp.sum((last_used[:, None] >= tile_end[None, :]).astype(jnp.int32), axis=1)
    pair = tile_cls % N_PAIRS
    first_expert = jnp.where(pair == 5, 2, jnp.where((pair == 2) | (pair == 3), 1, 0))
    second_expert = jnp.where(pair == 0, 1, jnp.where(pair <= 2, 2, 3))
    tile_ea = (tile_cls // N_PAIRS) * PER_GROUP + first_expert
    tile_eb = (tile_cls // N_PAIRS) * PER_GROUP + second_expert

    n_post2 = row(norm_post_ffn[l])
    wg = w_exp_gate[l]
    wu = w_exp_up[l]
    wd = w_exp_down[l]
    rows = _dispatch_call(cls, rank, cnt, padcnt, start, pk_p, pk_s, max_tiles * tm)
    f_sorted = _moe_call(tile_ea, tile_eb, tile_valid, rows, w_router, b_router, wg, wu, wd)
    y_p = _combine_call(cls_p, rank_p, start, x1_p, mod_p, n_post2, f_sorted, 1)
    y_s = _combine_call(cls_s, rank_s, start, x1_s, mod_s, n_post2, f_sorted, FINAL_TILE // ts)

    conv_p = conv_p8[:, SUBLANES - (CONV_W - 1):, :]
    lru_p = lru_p8[:, SUBLANES - 1, :]
    xr_s3 = xr_s.reshape(bs, ts, D_LRU)
    conv_s = xr_s3[:, ts - (CONV_W - 1):, :]
    lru_s = h_s.reshape(bs, ts, D_LRU)[:, ts - 1, :]
    return (y_p.reshape(bp, tp, D_MODEL), y_s.reshape(bs, ts, D_MODEL),
            conv_p[None], lru_p[None], ret_p[None],
            conv_s[None], lru_s[None], ret_s[None])
```

```python
import functools
import math

import jax
import jax.numpy as jnp
import numpy as np
from jax import lax
from jax.experimental import pallas as pl
from jax.experimental.pallas import tpu as pltpu

F32 = jnp.float32
BF16 = jnp.bfloat16

D_MODEL = 1024
D_LRU = 512
D_RET = 512
N_LRU_BLOCKS = 8
LRU_BLOCK = D_LRU // N_LRU_BLOCKS
CONV_W = 4
LRU_C = 8.0
N_HEADS = 4
DK = 128
DV = 128
RET_CHUNK = 128
ROPE_BASE = 10000.0
D_IN_PROJ = 3072
N_GROUPS = 4
PER_GROUP = 4
N_EXPERTS = 16
D_EXPERT = 256
EXPM1_DIRECT_BELOW = -0.5
NORM_EPS = 1e-6
GN_EPS = 1e-5
PAST_LEN = 16384

SUBLANES = 8
LANES = 128
GATE_HALF = 256
VMEM_LIMIT = 56 * 1024 * 1024

PROMPT_TILE = 512
SAMPLE_SEQS = 16
MOE_TILE = 256
MOE_TILES_PER_STEP = 2
FINAL_TILE = 512
DISPATCH_TILE = 512
RING = 3
DMA_GROUP = 8

N_PAIRS = 6
N_CLASSES = N_GROUPS * N_PAIRS
ROW_GROUPS = D_MODEL // LANES


def _silu(x):
    return x * jax.nn.sigmoid(x)


def _rms_scale(x):
    return lax.rsqrt(jnp.mean(x * x, axis=-1, keepdims=True) + NORM_EPS)


def _masked_softmax(logits, mask):
    top = jnp.max(jnp.where(mask, logits, -jnp.inf), axis=-1, keepdims=True)
    e = jnp.where(mask, jnp.exp(logits - top), 0.0)
    return e / jnp.sum(e, axis=-1, keepdims=True)


def _mod_kernel(c_ref, w_ref, b_ref, o_ref):
    s = _silu(c_ref[...]).astype(BF16)
    o_ref[...] = jnp.dot(s, w_ref[...].astype(BF16), preferred_element_type=F32) + b_ref[...]


def _mod_call(c_all, w_mod, b_mod):
    rows = c_all.shape[0]
    ncol = w_mod.shape[1]
    blk = D_MODEL
    return pl.pallas_call(
        _mod_kernel,
        grid=(ncol // blk,),
        in_specs=[
            pl.BlockSpec((rows, D_MODEL), lambda j: (0, 0)),
            pl.BlockSpec((D_MODEL, blk), lambda j: (0, j)),
            pl.BlockSpec((1, blk), lambda j: (0, j)),
        ],
        out_specs=pl.BlockSpec((rows, blk), lambda j: (0, j)),
        out_shape=jax.ShapeDtypeStruct((rows, ncol), F32),
        compiler_params=pltpu.CompilerParams(
            dimension_semantics=("arbitrary",), vmem_limit_bytes=VMEM_LIMIT),
        name="mod",
    )(c_all, w_mod, b_mod)


def _in_proj(x3, mod3, n_pre1_ref, w_in_ref):
    bb, tt, _ = x3.shape
    sh1 = mod3[:, :, 0:D_MODEL]
    sc1 = mod3[:, :, D_MODEL:2 * D_MODEL]
    coef = n_pre1_ref[...].reshape(1, 1, D_MODEL) * (1.0 + sc1)
    u = (x3 * _rms_scale(x3)) * coef + sh1
    u2d = u.reshape(bb * tt, D_MODEL).astype(BF16)
    return jnp.dot(u2d, w_in_ref[...], preferred_element_type=F32)


def _lru_coeffs(xc, wg_ref, b_r_ref, b_i_ref, lam_ref):
    xcb = xc.astype(BF16)
    g0 = jnp.dot(xcb[:, :GATE_HALF], wg_ref[0], preferred_element_type=F32)
    g1 = jnp.dot(xcb[:, GATE_HALF:], wg_ref[1], preferred_element_type=F32)
    r = jax.nn.sigmoid(jnp.concatenate([g0[:, :GATE_HALF], g1[:, :GATE_HALF]], axis=1) + b_r_ref[...])
    i = jax.nn.sigmoid(jnp.concatenate([g0[:, GATE_HALF:], g1[:, GATE_HALF:]], axis=1) + b_i_ref[...])
    lam = lam_ref[...]
    sp = jnp.maximum(-lam, 0.0) + jnp.log1p(jnp.exp(-jnp.abs(lam)))
    log_a = -LRU_C * r * sp
    a = jnp.exp(log_a)
    y = 2.0 * log_a
    a2 = a * a
    d = a2 - 1.0
    small = d * y / jnp.log(a2)
    em1 = jnp.where(y < EXPM1_DIRECT_BELOW, d, jnp.where(d == 0.0, y, small))
    gain = jnp.sqrt(-em1)
    return a, gain * (i * xc)


def _rope(xh, cos2, sin2, lane_axis):
    return xh * cos2 + pltpu.roll(xh, DK // 2, axis=lane_axis) * sin2


def _group_norm(o):
    mu = jnp.mean(o, axis=-1, keepdims=True)
    d = o - mu
    var = jnp.mean(d * d, axis=-1, keepdims=True)
    return d * lax.rsqrt(var + GN_EPS)


def _post_mixer(x3, mod3, out_a, out_b, w_out_ref, n_post1_ref, n_pre2_ref, w_router_ref, b_router_ref,
                earlier_ref, x1_ref, pk_ref, cls_ref, rank_ref, cnt_scr):
    bb, tt, _ = x3.shape
    m = bb * tt
    y = (jnp.dot(out_a.astype(BF16), w_out_ref[0:D_LRU, :], preferred_element_type=F32)
         + jnp.dot(out_b.astype(BF16), w_out_ref[D_LRU:, :], preferred_element_type=F32))
    g1 = mod3[:, :, 2 * D_MODEL:3 * D_MODEL]
    sh2 = mod3[:, :, 3 * D_MODEL:4 * D_MODEL]
    sc2 = mod3[:, :, 4 * D_MODEL:5 * D_MODEL]
    y3 = y.reshape(bb, tt, D_MODEL)
    x1 = x3 + (y3 * _rms_scale(y3)) * (g1 * n_post1_ref[...].reshape(1, 1, D_MODEL))
    u2 = (x1 * _rms_scale(x1)) * (n_pre2_ref[...].reshape(1, 1, D_MODEL) * (1.0 + sc2)) + sh2
    x1_ref[...] = x1.reshape(m, D_MODEL)
    u2f = u2.reshape(m, D_MODEL)
    for j in range(ROW_GROUPS):
        pk_ref[pl.ds(j, m, stride=SUBLANES), :] = u2f[:, j * LANES:(j + 1) * LANES]
    u2b = u2f.astype(BF16)

    logits = jnp.dot(u2b, w_router_ref[...], preferred_element_type=F32) + b_router_ref[...]
    lane = lax.broadcasted_iota(jnp.int32, (m, LANES), 1)
    lane_f = lane.astype(F32)
    is_g = (lane >= N_EXPERTS) & (lane < N_EXPERTS + N_GROUPS)
    p_group = _masked_softmax(logits, is_g)
    p_g = jnp.max(p_group, axis=-1, keepdims=True)
    g_lane = jnp.min(jnp.where(is_g & (p_group == p_g), lane_f, float(LANES)), axis=-1, keepdims=True)
    e_lo = (g_lane - N_EXPERTS) * PER_GROUP
    in_g = (lane_f >= e_lo) & (lane_f < e_lo + PER_GROUP)
    p_e = _masked_softmax(logits, in_g)
    pm = jnp.where(in_g, p_e, -1.0)
    w1 = jnp.max(pm, axis=-1, keepdims=True)
    i1 = jnp.min(jnp.where(pm == w1, lane_f, float(LANES)), axis=-1, keepdims=True)
    pm2 = jnp.where(lane_f == i1, -1.0, pm)
    w2 = jnp.max(pm2, axis=-1, keepdims=True)
    i2 = jnp.min(jnp.where(pm2 == w2, lane_f, float(LANES)), axis=-1, keepdims=True)
    a = jnp.minimum(i1, i2) - e_lo
    b = jnp.maximum(i1, i2) - e_lo
    pair = jnp.where(a == 0.0, jnp.where(b == 3.0, 4.0, b - 1.0), jnp.where(a == 1.0, b, 5.0))
    cls = (g_lane - N_EXPERTS) * N_PAIRS + pair
    onehot = lane_f == cls
    prefix = jnp.dot(earlier_ref[...], jnp.where(onehot, 1.0, 0.0).astype(BF16), preferred_element_type=F32)
    run = cnt_scr[0:1, :]
    rank = jnp.sum(jnp.where(onehot, prefix + run, 0.0), axis=-1, keepdims=True)
    cnt_scr[...] = jnp.broadcast_to(
        run + jnp.sum(jnp.where(onehot, 1.0, 0.0), axis=0, keepdims=True), cnt_scr.shape)
    route = jnp.where(lane == 0, cls, jnp.where(lane == 1, rank, 0.0))
    route_t = jnp.transpose(route)
    cls_ref[...] = route_t[0:1, :].astype(jnp.int32)
    rank_ref[...] = route_t[1:2, :].astype(jnp.int32)


def _group_scan(a3, b3):
    tpos = lax.broadcasted_iota(jnp.int32, a3.shape, 1)
    s = 1
    while s < a3.shape[1]:
        keep = tpos >= s
        a_sh = jnp.where(keep, pltpu.roll(a3, s, axis=1), 1.0)
        b_sh = jnp.where(keep, pltpu.roll(b3, s, axis=1), 0.0)
        b3 = a3 * b_sh + b3
        a3 = a3 * a_sh
        s *= 2
    return a3, b3


def _scan_rows(a, b, h0):
    n, c = a.shape
    groups = n // SUBLANES
    a3, b3 = _group_scan(a.reshape(groups, SUBLANES, c), b.reshape(groups, SUBLANES, c))
    carry = h0
    out = []
    for g in range(groups):
        hg = b3[g] + a3[g] * carry
        out.append(hg)
        carry = hg[SUBLANES - 1:SUBLANES, :]
    return jnp.concatenate(out, axis=0)


def _prompt_mixer_kernel(x_ref, mod_ref, cos_ref, sin_ref,
                         n_pre1_ref, n_post1_ref, n_pre2_ref,
                         w_in_ref, conv_w_ref, conv_b_ref, wg_ref, b_r_ref, b_i_ref, lam_ref,
                         gn_w_ref, w_out_ref, w_router_ref, b_router_ref,
                         mask_ref, wstate_ref, cross_ref, cdecay_ref, earlier_ref,
                         x1_ref, pk_ref, cls_ref, rank_ref, conv_out_ref, lru_out_ref, ret_out_ref,
                         cnt_out_ref,
                         conv_scr, h_scr, s_scr, cnt_scr):
    t = pl.program_id(1)
    tt = x_ref.shape[1]

    @pl.when((pl.program_id(0) == 0) & (t == 0))
    def _():
        cnt_scr[...] = jnp.zeros_like(cnt_scr)

    @pl.when(t == 0)
    def _():
        conv_scr[...] = jnp.zeros_like(conv_scr)
        h_scr[...] = jnp.zeros_like(h_scr)
        s_scr[...] = jnp.zeros_like(s_scr)

    x3 = x_ref[...]
    mod3 = mod_ref[...]
    z = _in_proj(x3, mod3, n_pre1_ref, w_in_ref)
    xr, yg, q, k, v, g = (z[:, c * D_LRU:(c + 1) * D_LRU] for c in range(D_IN_PROJ // D_LRU))

    groups = tt // SUBLANES
    xr3 = xr.reshape(groups, SUBLANES, D_LRU)
    tpos = lax.broadcasted_iota(jnp.int32, xr3.shape, 1)
    tail = conv_scr[...]
    xc3 = jnp.broadcast_to(conv_b_ref[...].reshape(1, 1, D_LRU), xr3.shape)
    for j in range(CONV_W):
        back = CONV_W - 1 - j
        w_j = conv_w_ref[j:j + 1, :].reshape(1, 1, D_LRU)
        if back == 0:
            term = xr3
        else:
            cur = pltpu.roll(xr3, back, axis=1)
            first = pltpu.roll(tail, back, axis=0).reshape(1, SUBLANES, D_LRU)
            prev = jnp.concatenate([first, cur[:groups - 1]], axis=0)
            term = jnp.where(tpos >= back, cur, prev)
        xc3 = xc3 + term * w_j
    xc = xc3.reshape(tt, D_LRU)
    conv_scr[...] = xr[tt - SUBLANES:, :]

    a, b = _lru_coeffs(xc, wg_ref, b_r_ref, b_i_ref, lam_ref)
    hseq = _scan_rows(a, b, h_scr[0:1, :])
    h_scr[...] = jnp.broadcast_to(hseq[tt - 1:tt, :], h_scr.shape)
    out_a = hseq * jax.nn.gelu(yg, approximate=True)

    cos2 = cos_ref[...]
    sin2 = sin_ref[...]
    scale = DK ** -0.5
    o_heads = []
    for h in range(N_HEADS):
        hs = slice(h * DK, (h + 1) * DK)
        qh = (_rope(q[:, hs], cos2, sin2, 1) * scale).astype(BF16)
        kh = _rope(k[:, hs], cos2, sin2, 1)
        vh = v[:, hs].astype(BF16)
        o_chunks = []
        for c in range(tt // RET_CHUNK):
            cs = slice(c * RET_CHUNK, (c + 1) * RET_CHUNK)
            qc = qh[cs]
            kc = kh[cs]
            vc = vh[cs]
            s_prev = s_scr[h]
            scores = lax.dot_general(qc, kc.astype(BF16), (((1,), (1,)), ((), ())),
                                     preferred_element_type=F32) * mask_ref[h]
            inner = jnp.dot(scores.astype(BF16), vc, preferred_element_type=F32)
            cross = jnp.dot(qc, s_prev.astype(BF16), preferred_element_type=F32) * cross_ref[:, hs]
            kw = (kc * wstate_ref[:, hs]).astype(BF16)
            kv = lax.dot_general(kw, vc, (((0,), (0,)), ((), ())), preferred_element_type=F32)
            s_scr[h] = cdecay_ref[:, hs] * s_prev + kv
            o_chunks.append(inner + cross)
        o_heads.append(_group_norm(jnp.concatenate(o_chunks, axis=0)))
    o = jnp.concatenate(o_heads, axis=1)
    out_b = o * gn_w_ref[...] * _silu(g)

    _post_mixer(x3, mod3, out_a, out_b, w_out_ref, n_post1_ref, n_pre2_ref, w_router_ref, b_router_ref,
                earlier_ref, x1_ref, pk_ref, cls_ref, rank_ref, cnt_scr)
    cnt_out_ref[...] = cnt_scr[...]

    @pl.when(t == pl.num_programs(1) - 1)
    def _():
        conv_out_ref[0] = xr[tt - SUBLANES:, :]
        lru_out_ref[0] = hseq[tt - SUBLANES:, :]
        ret_out_ref[0] = s_scr[...]


def _const_spec(shape):
    nd = len(shape)
    return pl.BlockSpec(shape, lambda *_: (0,) * nd)


def _prompt_mixer_call(x, mod3, cos2, sin2, wts, tables):
    bsz, seq, _ = x.shape
    tt = PROMPT_TILE
    nt = seq // tt
    n_tok = bsz * seq
    tok_spec = pl.BlockSpec((tt, D_MODEL), lambda b, t: (b * nt + t, 0))
    in_specs = [
        pl.BlockSpec((1, tt, D_MODEL), lambda b, t: (b, t, 0)),
        pl.BlockSpec((1, 1, 6 * D_MODEL), lambda b, t: (b, 0, 0)),
        pl.BlockSpec((tt, LANES), lambda b, t: (t, 0)),
        pl.BlockSpec((tt, LANES), lambda b, t: (t, 0)),
    ] + [_const_spec(w.shape) for w in wts] + [_const_spec(tb.shape) for tb in tables]
    out_specs = [
        tok_spec,
        pl.BlockSpec((tt * SUBLANES, LANES), lambda b, t: (b * nt + t, 0)),
        pl.BlockSpec((1, tt), lambda b, t: (0, b * nt + t)),
        pl.BlockSpec((1, tt), lambda b, t: (0, b * nt + t)),
        pl.BlockSpec((1, SUBLANES, D_LRU), lambda b, t: (b, 0, 0)),
        pl.BlockSpec((1, SUBLANES, D_LRU), lambda b, t: (b, 0, 0)),
        pl.BlockSpec((1, N_HEADS, DK, DV), lambda b, t: (b, 0, 0, 0)),
        pl.BlockSpec((SUBLANES, LANES), lambda b, t: (0, 0)),
    ]
    out_shape = [
        jax.ShapeDtypeStruct((n_tok, D_MODEL), F32),
        jax.ShapeDtypeStruct((n_tok * SUBLANES, LANES), F32),
        jax.ShapeDtypeStruct((1, n_tok), jnp.int32),
        jax.ShapeDtypeStruct((1, n_tok), jnp.int32),
        jax.ShapeDtypeStruct((bsz, SUBLANES, D_LRU), F32),
        jax.ShapeDtypeStruct((bsz, SUBLANES, D_LRU), F32),
        jax.ShapeDtypeStruct((bsz, N_HEADS, DK, DV), F32),
        jax.ShapeDtypeStruct((SUBLANES, LANES), F32),
    ]
    return pl.pallas_call(
        _prompt_mixer_kernel,
        grid=(bsz, nt),
        in_specs=in_specs,
        out_specs=out_specs,
        out_shape=out_shape,
        scratch_shapes=[
            pltpu.VMEM((SUBLANES, D_LRU), F32),
            pltpu.VMEM((SUBLANES, D_LRU), F32),
            pltpu.VMEM((N_HEADS, DK, DV), F32),
            pltpu.VMEM((SUBLANES, LANES), F32),
        ],
        compiler_params=pltpu.CompilerParams(
            dimension_semantics=("arbitrary", "arbitrary"), vmem_limit_bytes=VMEM_LIMIT),
        name="prompt_mixer",
    )(x, mod3, cos2, sin2, *wts, *tables)


def _sample_mixer_kernel(x_ref, mod_ref, cos_ref, sin_ref, buf_ref, h0_ref, s0_ref, cnt_in_ref,
                         n_pre1_ref, n_post1_ref, n_pre2_ref,
                         w_in_ref, conv_w_ref, conv_b_ref, wg_ref, b_r_ref, b_i_ref, lam_ref,
                         gn_w_ref, w_out_ref, w_router_ref, b_router_ref,
                         smask_ref, wstate_ref, cross_ref, cdecay_ref, earlier_ref,
                         x1_ref, pk_ref, cls_ref, rank_ref, xr_out_ref, h_out_ref, ret_out_ref,
                         cnt_out_ref,
                         cnt_scr):
    bb, ts, _ = x_ref.shape
    m = bb * ts

    @pl.when(pl.program_id(0) == 0)
    def _():
        cnt_scr[...] = cnt_in_ref[...]

    x3 = x_ref[...]
    mod3 = mod_ref[...].reshape(bb, 1, 6 * D_MODEL)
    z = _in_proj(x3, mod3, n_pre1_ref, w_in_ref)
    xr, yg, q, k, v, g = (z[:, c * D_LRU:(c + 1) * D_LRU] for c in range(D_IN_PROJ // D_LRU))
    xr_out_ref[...] = xr

    xr3 = xr.reshape(bb, ts, D_LRU)
    buf3 = buf_ref[...]
    tpos = lax.broadcasted_iota(jnp.int32, (bb, ts, D_LRU), 1)
    xc3 = jnp.broadcast_to(conv_b_ref[...].reshape(1, 1, D_LRU), (bb, ts, D_LRU))
    for j in range(CONV_W):
        back = CONV_W - 1 - j
        w_j = conv_w_ref[j:j + 1, :].reshape(1, 1, D_LRU)
        if back == 0:
            term = xr3
        else:
            cur = pltpu.roll(xr3, back, axis=1)
            up = CONV_W - 1 - back
            old = buf3 if up == 0 else pltpu.roll(buf3, ts - up, axis=1)
            term = jnp.where(tpos >= back, cur, old)
        xc3 = xc3 + term * w_j
    xc = xc3.reshape(m, D_LRU)

    a, b = _lru_coeffs(xc, wg_ref, b_r_ref, b_i_ref, lam_ref)
    a3 = a.reshape(bb, ts, D_LRU)
    b3 = b.reshape(bb, ts, D_LRU) + a3 * h0_ref[...]
    _, h3 = _group_scan(a3, b3)
    hseq = h3.reshape(m, D_LRU)
    h_out_ref[...] = hseq
    out_a = hseq * jax.nn.gelu(yg, approximate=True)

    cos2 = cos_ref[...].reshape(1, ts, LANES)
    sin2 = sin_ref[...].reshape(1, ts, LANES)
    scale = DK ** -0.5
    o_heads = []
    for h in range(N_HEADS):
        hs = slice(h * DK, (h + 1) * DK)
        q3 = (_rope(q[:, hs].reshape(bb, ts, DK), cos2, sin2, 2) * scale).astype(BF16)
        k3 = _rope(k[:, hs].reshape(bb, ts, DK), cos2, sin2, 2)
        v3 = v[:, hs].reshape(bb, ts, DV).astype(BF16)
        q2 = q3.reshape(m, DK)
        k2 = k3.reshape(m, DK).astype(BF16)
        v2 = v3.reshape(m, DV)
        scores = lax.dot_general(q2, k2, (((1,), (1,)), ((), ())),
                                 preferred_element_type=F32) * smask_ref[h]
        inner = jnp.dot(scores.astype(BF16), v2, preferred_element_type=F32)
        s0h = s0_ref[:, h]
        cross = jnp.einsum('bid,bde->bie', q3, s0h.astype(BF16), preferred_element_type=F32)
        cross = cross * cross_ref[:, hs].reshape(1, ts, DV)
        kw3 = (k3 * wstate_ref[:, hs].reshape(1, ts, DK)).astype(BF16)
        kv = jnp.einsum('bjd,bje->bde', kw3, v3, preferred_element_type=F32)
        ret_out_ref[:, h] = cdecay_ref[:, hs].reshape(1, 1, DV) * s0h + kv
        o_heads.append(_group_norm(inner + cross.reshape(m, DV)))
    o = jnp.concatenate(o_heads, axis=1)
    out_b = o * gn_w_ref[...] * _silu(g)

    _post_mixer(x3, mod3, out_a, out_b, w_out_ref, n_post1_ref, n_pre2_ref, w_router_ref, b_router_ref,
                earlier_ref, x1_ref, pk_ref, cls_ref, rank_ref, cnt_scr)
    cnt_out_ref[...] = cnt_scr[...]


def _sample_mixer_call(x, mod3, cos2, sin2, buf8, h0p, s0, cnt_in, wts, tables):
    bsz, ts, _ = x.shape
    bb = SAMPLE_SEQS
    m = bb * ts
    n_tok = bsz * ts
    seq_spec = lambda w: pl.BlockSpec((bb, ts, w), lambda i: (i, 0, 0))
    tok_spec = lambda w: pl.BlockSpec((m, w), lambda i: (i, 0))
    in_specs = [
        seq_spec(D_MODEL),
        pl.BlockSpec((bb, 6 * D_MODEL), lambda i: (i, 0)),
        _const_spec(cos2.shape),
        _const_spec(sin2.shape),
        seq_spec(D_LRU),
        seq_spec(D_LRU),
        pl.BlockSpec((bb, N_HEADS, DK, DV), lambda i: (i, 0, 0, 0)),
        _const_spec(cnt_in.shape),
    ] + [_const_spec(w.shape) for w in wts] + [_const_spec(tb.shape) for tb in tables]
    out_specs = [
        tok_spec(D_MODEL),
        pl.BlockSpec((m * SUBLANES, LANES), lambda i: (i, 0)),
        pl.BlockSpec((1, m), lambda i: (0, i)),
        pl.BlockSpec((1, m), lambda i: (0, i)),
        tok_spec(D_LRU),
        tok_spec(D_LRU),
        pl.BlockSpec((bb, N_HEADS, DK, DV), lambda i: (i, 0, 0, 0)),
        _const_spec(cnt_in.shape),
    ]
    out_shape = [
        jax.ShapeDtypeStruct((n_tok, D_MODEL), F32),
        jax.ShapeDtypeStruct((n_tok * SUBLANES, LANES), F32),
        jax.ShapeDtypeStruct((1, n_tok), jnp.int32),
        jax.ShapeDtypeStruct((1, n_tok), jnp.int32),
        jax.ShapeDtypeStruct((n_tok, D_LRU), F32),
        jax.ShapeDtypeStruct((n_tok, D_LRU), F32),
        jax.ShapeDtypeStruct((bsz, N_HEADS, DK, DV), F32),
        jax.ShapeDtypeStruct(cnt_in.shape, F32),
    ]
    return pl.pallas_call(
        _sample_mixer_kernel,
        grid=(bsz // bb,),
        in_specs=in_specs,
        out_specs=out_specs,
        out_shape=out_shape,
        scratch_shapes=[pltpu.VMEM((SUBLANES, LANES), F32)],
        compiler_params=pltpu.CompilerParams(
            dimension_semantics=("arbitrary",), vmem_limit_bytes=VMEM_LIMIT),
        name="sample_mixer",
    )(x, mod3, cos2, sin2, buf8, h0p, s0, cnt_in, *wts, *tables)


def _slab(ref, r):
    return ref.at[pl.ds(pl.multiple_of(r * SUBLANES, SUBLANES), SUBLANES)]


def _dispatch_kernel(cls_ref, rank_ref, cnt_ref, padcnt_ref, start_ref, srcp_ref, srcs_ref, out_ref, ring, sem,
                     *, p_tiles):
    i = pl.program_id(0)
    n = pl.num_programs(0)
    tile_rows = ring.shape[1]
    td = tile_rows // SUBLANES
    moe_rows = MOE_TILE * SUBLANES

    def issue(slot):
        def body(g, carry):
            first = i * td + g * DMA_GROUP
            slots = [start_ref[cls_ref[first + j]] + rank_ref[first + j] for j in range(DMA_GROUP)]
            first = g * DMA_GROUP
            for j in range(DMA_GROUP):
                pltpu.make_async_copy(
                    _slab(ring.at[slot], first + j), _slab(out_ref, slots[j]), sem.at[slot]
                ).start(priority=j % 2)
            return carry
        lax.fori_loop(0, td // DMA_GROUP, body, 0)

    def wait_tile(slot):
        pltpu.make_async_copy(ring.at[slot], out_ref.at[pl.ds(0, tile_rows)], sem.at[slot]).wait()

    def zero_fill(slot):
        ring[slot] = jnp.zeros((tile_rows, LANES), F32)
        zero_src = ring.at[slot]

        def per_class(c, carry):
            lo = start_ref[c] + cnt_ref[c]
            hi = start_ref[c] + padcnt_ref[c]

            def fill(r, carry2):
                pltpu.make_async_copy(_slab(zero_src, 0), _slab(out_ref, r), sem.at[slot]).start()
                return carry2
            lax.fori_loop(lo, hi, fill, 0)

            def done(r, carry2):
                pltpu.make_async_copy(_slab(zero_src, 0), _slab(out_ref, 0), sem.at[slot]).wait()
                return carry2
            lax.fori_loop(lo, hi, done, 0)
            return carry
        lax.fori_loop(0, N_CLASSES, per_class, 0)

        used_tiles = (start_ref[N_CLASSES - 1] + padcnt_ref[N_CLASSES - 1]) // MOE_TILE
        all_tiles = out_ref.shape[0] // moe_rows

        def tile_copy(t):
            return pltpu.make_async_copy(
                zero_src.at[pl.ds(0, moe_rows)],
                out_ref.at[pl.ds(pl.multiple_of(t * moe_rows, moe_rows), moe_rows)], sem.at[slot])

        def fill_tile(t, carry):
            tile_copy(t).start()
            return carry
        lax.fori_loop(used_tiles, all_tiles, fill_tile, 0)

        def done_tile(t, carry):
            tile_copy(t).wait()
            return carry
        lax.fori_loop(used_tiles, all_tiles, done_tile, 0)

    for s in range(RING):
        @pl.when(i % RING == s)
        def _():
            @pl.when(i < p_tiles)
            def _():
                ring[s] = srcp_ref[...]

            @pl.when(i >= p_tiles)
            def _():
                ring[s] = srcs_ref[...]

            issue(s)

            @pl.when(i >= RING - 1)
            def _():
                wait_tile((s + 1) % RING)

            @pl.when(i == n - 1)
            def _():
                for back in range(RING - 2, -1, -1):
                    wait_tile((s - back) % RING)
                zero_fill(s)


def _dispatch_call(cls, rank, cnt, padcnt, start, pk_p, pk_s, n_rows):
    td = DISPATCH_TILE
    tile_rows = td * SUBLANES
    p_tiles = pk_p.shape[0] // tile_rows
    s_tiles = pk_s.shape[0] // tile_rows
    assert p_tiles + s_tiles >= RING and td >= MOE_TILE
    return pl.pallas_call(
        functools.partial(_dispatch_kernel, p_tiles=p_tiles),
        grid_spec=pltpu.PrefetchScalarGridSpec(
            num_scalar_prefetch=5,
            grid=(p_tiles + s_tiles,),
            in_specs=[
                pl.BlockSpec((tile_rows, LANES), lambda i, *_: (jnp.minimum(i, p_tiles - 1), 0)),
                pl.BlockSpec((tile_rows, LANES), lambda i, *_: (jnp.maximum(i - p_tiles, 0), 0)),
            ],
            out_specs=pl.BlockSpec(memory_space=pl.ANY),
            scratch_shapes=[pltpu.VMEM((RING, tile_rows, LANES), F32),
                            pltpu.SemaphoreType.DMA((RING,))],
        ),
        out_shape=jax.ShapeDtypeStruct((n_rows * SUBLANES, LANES), F32),
        compiler_params=pltpu.CompilerParams(
            dimension_semantics=("arbitrary",), has_side_effects=True, vmem_limit_bytes=VMEM_LIMIT),
        name="moe_dispatch",
    )(cls, rank, cnt, padcnt, start, pk_p, pk_s)


def _moe_kernel(ea_ref, eb_ref, valid_ref, xs_ref, w_router_ref, b_router_ref, *refs):
    f_ref = refs[-1]
    for s in range(MOE_TILES_PER_STEP):
        _moe_tile(pl.program_id(0) * MOE_TILES_PER_STEP + s, s * MOE_TILE * SUBLANES, ea_ref, eb_ref,
                  valid_ref, xs_ref, w_router_ref, b_router_ref, *refs[6 * s:6 * s + 6], f_ref)


def _moe_tile(t, row0, ea_ref, eb_ref, valid_ref, xs_ref, w_router_ref, b_router_ref,
              wga_ref, wua_ref, wda_ref, wgb_ref, wub_ref, wdb_ref, f_ref):
    tm = MOE_TILE

    @pl.when(valid_ref[t] == 1)
    def _():
        x = jnp.concatenate([xs_ref[pl.ds(row0 + j, tm, stride=SUBLANES), :] for j in range(ROW_GROUPS)],
                            axis=1).astype(BF16)
        e_a = ea_ref[t]
        e_b = eb_ref[t]
        e_lo = (e_a // PER_GROUP) * PER_GROUP
        logits = jnp.dot(x, w_router_ref[...], preferred_element_type=F32) + b_router_ref[...]
        lane = lax.broadcasted_iota(jnp.int32, (tm, LANES), 1)
        pick = lambda p, idx: jnp.sum(jnp.where(lane == idx, p, 0.0), axis=-1, keepdims=True)
        p_group = _masked_softmax(logits, (lane >= N_EXPERTS) & (lane < N_EXPERTS + N_GROUPS))
        p_g = pick(p_group, N_EXPERTS + e_a // PER_GROUP)
        p_e = _masked_softmax(logits, (lane >= e_lo) & (lane < e_lo + PER_GROUP))
        w_a = pick(p_e, e_a)
        w_b = pick(p_e, e_b)
        wsum = w_a + w_b

        def expert(wg_ref, wu_ref, gate):
            hg = jnp.dot(x, wg_ref[0].astype(BF16), preferred_element_type=F32)
            hu = jnp.dot(x, wu_ref[0].astype(BF16), preferred_element_type=F32)
            return (_silu(hg) * hu * gate).astype(BF16)

        ha = expert(wga_ref, wua_ref, p_g * (w_a / wsum))
        hb = expert(wgb_ref, wub_ref, p_g * (w_b / wsum))
        for c in range(D_MODEL // GATE_HALF):
            cols = slice(c * GATE_HALF, (c + 1) * GATE_HALF)
            f = (jnp.dot(ha, wda_ref[0, :, cols].astype(BF16), preferred_element_type=F32)
                 + jnp.dot(hb, wdb_ref[0, :, cols].astype(BF16), preferred_element_type=F32))
            for jj in range(GATE_HALF // LANES):
                j = c * (GATE_HALF // LANES) + jj
                f_ref[pl.ds(row0 + j, tm, stride=SUBLANES), :] = f[:, jj * LANES:(jj + 1) * LANES]

    @pl.when(valid_ref[t] == 0)
    def _():
        f_ref[pl.ds(row0, tm * SUBLANES), :] = jnp.zeros((tm * SUBLANES, LANES), F32)


def _moe_call(tile_ea, tile_eb, tile_valid, rows, w_router, b_router, wg, wu, wd):
    n_rows = rows.shape[0] // SUBLANES
    per_step = MOE_TILES_PER_STEP
    step_rows = MOE_TILE * per_step * SUBLANES
    assert n_rows % (MOE_TILE * per_step) == 0
    const = lambda a: pl.BlockSpec(a.shape, lambda t, ea, eb, v: (0,) * a.ndim)

    def expert_block(shape, table, s):
        return pl.BlockSpec(shape, lambda t, ea, eb, v: ((ea, eb)[table][t * per_step + s], 0, 0))

    weight_specs, weight_args = [], []
    for s in range(per_step):
        for table in range(2):
            weight_specs += [expert_block((1, D_MODEL, D_EXPERT), table, s),
                             expert_block((1, D_MODEL, D_EXPERT), table, s),
                             expert_block((1, D_EXPERT, D_MODEL), table, s)]
            weight_args += [wg, wu, wd]
    return pl.pallas_call(
        _moe_kernel,
        grid_spec=pltpu.PrefetchScalarGridSpec(
            num_scalar_prefetch=3,
            grid=(n_rows // (MOE_TILE * per_step),),
            in_specs=[
                pl.BlockSpec((step_rows, LANES), lambda t, ea, eb, v: (t, 0)),
                const(w_router), const(b_router),
            ] + weight_specs,
            out_specs=pl.BlockSpec((step_rows, LANES), lambda t, ea, eb, v: (t, 0)),
        ),
        out_shape=jax.ShapeDtypeStruct((n_rows * SUBLANES, LANES), F32),
        compiler_params=pltpu.CompilerParams(
            dimension_semantics=("arbitrary",), vmem_limit_bytes=VMEM_LIMIT),
        name="moe_experts",
    )(tile_ea, tile_eb, tile_valid, rows, w_router, b_router, *weight_args)


def _combine_kernel(cls_ref, rank_ref, start_ref, x1_ref, mod_ref, n_post2_ref, f_hbm, o_ref, fbuf, sem):
    i = pl.program_id(0)
    n = pl.num_programs(0)
    tf = x1_ref.shape[0]

    def issue(tile, slot):
        def body(g, carry):
            first = tile * tf + g * DMA_GROUP
            slots = [start_ref[cls_ref[first + j]] + rank_ref[first + j] for j in range(DMA_GROUP)]
            first = g * DMA_GROUP
            for j in range(DMA_GROUP):
                pltpu.make_async_copy(
                    _slab(f_hbm, slots[j]), _slab(fbuf.at[slot], first + j), sem.at[slot]
                ).start(priority=j % 2)
            return carry
        lax.fori_loop(0, tf // DMA_GROUP, body, 0)

    def finish(slot):
        pltpu.make_async_copy(f_hbm.at[pl.ds(0, tf * SUBLANES)], fbuf.at[slot], sem.at[slot]).wait()
        f = jnp.concatenate([fbuf[slot, pl.ds(j, tf, stride=SUBLANES), :] for j in range(D_MODEL // LANES)],
                            axis=1)
        bb = mod_ref.shape[0]
        mod3 = mod_ref[...].reshape(bb, 1, 6 * D_MODEL)
        f3 = f.reshape(bb, tf // bb, D_MODEL)
        g2 = mod3[:, :, 5 * D_MODEL:6 * D_MODEL]
        x13 = x1_ref[...].reshape(f3.shape)
        out = x13 + (f3 * _rms_scale(f3)) * (g2 * n_post2_ref[...].reshape(1, 1, D_MODEL))
        o_ref[...] = out.reshape(o_ref.shape)

    @pl.when(i == 0)
    def _():
        issue(0, 0)

    for parity in range(2):
        @pl.when(i % 2 == parity)
        def _():
            @pl.when(i + 1 < n)
            def _():
                issue(i + 1, 1 - parity)
            finish(parity)


def _combine_call(cls, rank, start, x1, mod3, n_post2, f_sorted, seqs_per_tile):
    n_tok = x1.shape[0]
    tf = FINAL_TILE
    tok_per_seq = n_tok // mod3.shape[0]
    if seqs_per_tile > 1:
        mod_spec = pl.BlockSpec((seqs_per_tile, 6 * D_MODEL), lambda i, *_: (i, 0))
    else:
        mod_spec = pl.BlockSpec((1, 1, 6 * D_MODEL), lambda i, *_: ((i * tf) // tok_per_seq, 0, 0))
    return pl.pallas_call(
        _combine_kernel,
        grid_spec=pltpu.PrefetchScalarGridSpec(
            num_scalar_prefetch=3,
            grid=(n_tok // tf,),
            in_specs=[
                pl.BlockSpec((tf, D_MODEL), lambda i, *_: (i, 0)),
                mod_spec,
                pl.BlockSpec((1, D_MODEL), lambda i, *_: (0, 0)),
                pl.BlockSpec(memory_space=pl.ANY),
            ],
            out_specs=pl.BlockSpec((tf, D_MODEL), lambda i, *_: (i, 0)),
            scratch_shapes=[pltpu.VMEM((2, tf * SUBLANES, LANES), F32), pltpu.SemaphoreType.DMA((2,))],
        ),
        out_shape=jax.ShapeDtypeStruct((n_tok, D_MODEL), F32),
        compiler_params=pltpu.CompilerParams(
            dimension_semantics=("arbitrary",), vmem_limit_bytes=VMEM_LIMIT),
        name="moe_combine",
    )(cls, rank, start, x1, mod3, n_post2, f_sorted)


def _block_diag_gate(w_r, w_i):
    per_half = GATE_HALF // LRU_BLOCK
    n_half = D_LRU // GATE_HALF
    eye = np.eye(per_half, dtype=np.float32)

    def block_diag(w):
        w4 = w.reshape(n_half, per_half, LRU_BLOCK, LRU_BLOCK)
        return jnp.einsum('hnij,nm->hnimj', w4, eye).reshape(n_half, GATE_HALF, GATE_HALF)

    return jnp.concatenate([block_diag(w_r), block_diag(w_i)], axis=-1).astype(BF16)


def _rope_tables(pos):
    half = DK // 2
    inv = np.float64(ROPE_BASE) ** (-np.arange(half, dtype=np.float64) / half)
    ang = np.asarray(pos, np.float64)[:, None] * inv[None, :]
    cos = np.cos(ang)
    sin = np.sin(ang)
    return (np.concatenate([cos, cos], axis=-1).astype(np.float32),
            np.concatenate([-sin, sin], axis=-1).astype(np.float32))


def _earlier_table(m):
    return jnp.asarray(np.tril(np.ones((m, m), np.float32), -1), BF16)


def _decay_tables(c):
    log_g = np.log1p(-np.exp2(-5.0 - np.arange(N_HEADS, dtype=np.float64)))
    idx = np.arange(c, dtype=np.float64)
    diff = idx[:, None] - idx[None, :]
    mask = np.where(diff[None] >= 0, np.exp(np.maximum(diff, 0.0)[None] * log_g[:, None, None]), 0.0)
    w_state = np.exp((c - 1.0 - idx)[None, :] * log_g[:, None])
    cross_decay = np.exp((idx + 1.0)[:, None] * log_g[None, :])
    chunk_decay = np.exp(c * log_g)
    wstate_full = np.repeat(w_state.T, DK, axis=1)
    cross_full = np.repeat(cross_decay, DV, axis=1)
    cdecay_full = np.repeat(chunk_decay, DV)[None, :]
    return tuple(t.astype(np.float32) for t in (mask, wstate_full, cross_full, cdecay_full))


def kernel(x_prompt, x_sample, state_conv, state_lru, state_ret, c_prompt, c_sample, w_mod, b_mod, norm_pre_mix, norm_post_mix, norm_pre_ffn, norm_post_ffn, w_in, conv_w, conv_b, w_rgate, b_rgate, w_igate, b_igate, lru_lambda, ret_gn_w, w_out, w_router_group, b_router_group, w_router_expert, b_router_expert, w_exp_gate, w_exp_up, w_exp_down):
    bp, tp, _ = x_prompt.shape
    bs, ts, _ = x_sample.shape
    l = 0

    mod = _mod_call(jnp.concatenate([c_sample, c_prompt], axis=0), w_mod[l], b_mod[l][None, :])
    mod_p = mod[bs:][:, None, :]
    mod_s = mod

    unused = LANES - N_EXPERTS - N_GROUPS
    w_router = jnp.concatenate(
        [w_router_expert[l], w_router_group[l], jnp.zeros((D_MODEL, unused), F32)], axis=1).astype(BF16)
    b_router = jnp.concatenate(
        [b_router_expert[l], b_router_group[l], jnp.zeros((unused,), F32)])[None, :]

    row = lambda vec: vec.reshape(1, -1)
    wts = (row(norm_pre_mix[l]), row(norm_post_mix[l]), row(norm_pre_ffn[l]),
           w_in[l].astype(BF16), conv_w[l], row(conv_b[l]),
           _block_diag_gate(w_rgate[l], w_igate[l]),
           row(b_rgate[l]), row(b_igate[l]), row(lru_lambda[l]), row(ret_gn_w[l]),
           w_out[l].astype(BF16), w_router, b_router)

    cos_p, sin_p = _rope_tables(np.arange(tp))
    x1_p, pk_p, cls_p, rank_p, conv_p8, lru_p8, ret_p, cnt_p = _prompt_mixer_call(
        x_prompt, mod_p, cos_p, sin_p, wts,
        _decay_tables(math.gcd(tp, RET_CHUNK)) + (_earlier_table(PROMPT_TILE),))

    cos_s, sin_s = _rope_tables(PAST_LEN + np.arange(ts))
    mask8, wstate_s, cross_s, cdecay_s = _decay_tables(math.gcd(ts, RET_CHUNK))
    eye = np.eye(SAMPLE_SEQS, dtype=np.float32)
    smask = np.stack([np.kron(eye, mask8[h]) for h in range(N_HEADS)])
    buf8 = jnp.pad(state_conv[l], ((0, 0), (0, ts - (CONV_W - 1)), (0, 0)))
    h0p = jnp.pad(state_lru[l][:, None, :], ((0, 0), (0, ts - 1), (0, 0)))
    x1_s, pk_s, cls_s, rank_s, xr_s, h_s, ret_s, cnt_all = _sample_mixer_call(
        x_sample, mod_s, cos_s, sin_s, buf8, h0p, state_ret[l], cnt_p, wts,
        (smask, wstate_s, cross_s, cdecay_s, _earlier_table(SAMPLE_SEQS * ts)))

    n_p = bp * tp
    n_tok = n_p + bs * ts
    tm = MOE_TILE
    max_tiles = n_tok // tm + N_CLASSES
    cls_p, rank_p, cls_s, rank_s = (a.reshape(-1) for a in (cls_p, rank_p, cls_s, rank_s))
    cls = jnp.concatenate([cls_p, cls_s])
    rank = jnp.concatenate([rank_p, rank_s])
    cnt = cnt_all[0, :N_CLASSES].astype(jnp.int32)
    ntile = (cnt + (tm - 1)) // tm
    padcnt = ntile * tm
    before = np.tril(np.ones((N_CLASSES, N_CLASSES), np.int32), -1)
    upto = np.tril(np.ones((N_CLASSES, N_CLASSES), np.int32))
    start = jnp.sum(before * padcnt[None, :], axis=1)
    tile_end = jnp.sum(upto * ntile[None, :], axis=1)
    n_used = jnp.sum(ntile)
    tile_ids = np.arange(max_tiles, dtype=np.int32)
    tile_valid = (tile_ids < n_used).astype(jnp.int32)
    last_used = jnp.minimum(tile_ids, n_used - 1)
    tile_cls = jnp.sum((last_used[:, None] >= tile_end[None, :]).astype(jnp.int32), axis=1)
    pair = tile_cls % N_PAIRS
    first_expert = jnp.where(pair == 5, 2, jnp.where((pair == 2) | (pair == 3), 1, 0))
    second_expert = jnp.where(pair == 0, 1, jnp.where(pair <= 2, 2, 3))
    tile_ea = (tile_cls // N_PAIRS) * PER_GROUP + first_expert
    tile_eb = (tile_cls // N_PAIRS) * PER_GROUP + second_expert

    n_post2 = row(norm_post_ffn[l])
    wg = w_exp_gate[l]
    wu = w_exp_up[l]
    wd = w_exp_down[l]
    rows = _dispatch_call(cls, rank, cnt, padcnt, start, pk_p, pk_s, max_tiles * tm)
    f_sorted = _moe_call(tile_ea, tile_eb, tile_valid, rows, w_router, b_router, wg, wu, wd)
    y_p = _combine_call(cls_p, rank_p, start, x1_p, mod_p, n_post2, f_sorted, 1)
    y_s = _combine_call(cls_s, rank_s, start, x1_s, mod_s, n_post2, f_sorted, FINAL_TILE // ts)

    conv_p = conv_p8[:, SUBLANES - (CONV_W - 1):, :]
    lru_p = lru_p8[:, SUBLANES - 1, :]
    xr_s3 = xr_s.reshape(bs, ts, D_LRU)
    conv_s = xr_s3[:, ts - (CONV_W - 1):, :]
    lru_s = h_s.reshape(bs, ts, D_LRU)[:, ts - 1, :]
    return (y_p.reshape(bp, tp, D_MODEL), y_s.reshape(bs, ts, D_MODEL),
            conv_p[None], lru_p[None], ret_p[None],
            conv_s[None], lru_s[None], ret_s[None])
```

```python
import functools
import math

import jax
import jax.numpy as jnp
import numpy as np
from jax import lax
from jax.experimental import pallas as pl
from jax.experimental.pallas import tpu as pltpu

F32 = jnp.float32
BF16 = jnp.bfloat16

D_MODEL = 1024
D_LRU = 512
D_RET = 512
N_LRU_BLOCKS = 8
LRU_BLOCK = D_LRU // N_LRU_BLOCKS
CONV_W = 4
LRU_C = 8.0
N_HEADS = 4
DK = 128
DV = 128
RET_CHUNK = 128
ROPE_BASE = 10000.0
D_IN_PROJ = 3072
N_GROUPS = 4
PER_GROUP = 4
N_EXPERTS = 16
D_EXPERT = 256
EXPM1_DIRECT_BELOW = -0.5
NORM_EPS = 1e-6
GN_EPS = 1e-5
PAST_LEN = 16384

SUBLANES = 8
LANES = 128
GATE_HALF = 256
VMEM_LIMIT = 56 * 1024 * 1024

PROMPT_TILE = 512
SAMPLE_SEQS = 16
MOE_TILE = 256
MOE_TILES_PER_STEP = 2
FINAL_TILE = 512
DISPATCH_TILE = 512
RING = 3
DMA_GROUP = 8

N_PAIRS = 6
N_CLASSES = N_GROUPS * N_PAIRS
ROW_GROUPS = D_MODEL // LANES


def _silu(x):
    return x * jax.nn.sigmoid(x)


def _rms_scale(x):
    return lax.rsqrt(jnp.mean(x * x, axis=-1, keepdims=True) + NORM_EPS)


def _masked_softmax(logits, mask):
    top = jnp.max(jnp.where(mask, logits, -jnp.inf), axis=-1, keepdims=True)
    e = jnp.where(mask, jnp.exp(logits - top), 0.0)
    return e / jnp.sum(e, axis=-1, keepdims=True)


def _mod_kernel(c_ref, w_ref, b_ref, o_ref):
    s = _silu(c_ref[...]).astype(BF16)
    o_ref[...] = jnp.dot(s, w_ref[...].astype(BF16), preferred_element_type=F32) + b_ref[...]


def _mod_call(c_all, w_mod, b_mod):
    rows = c_all.shape[0]
    ncol = w_mod.shape[1]
    blk = D_MODEL
    return pl.pallas_call(
        _mod_kernel,
        grid=(ncol // blk,),
        in_specs=[
            pl.BlockSpec((rows, D_MODEL), lambda j: (0, 0)),
            pl.BlockSpec((D_MODEL, blk), lambda j: (0, j)),
            pl.BlockSpec((1, blk), lambda j: (0, j)),
        ],
        out_specs=pl.BlockSpec((rows, blk), lambda j: (0, j)),
        out_shape=jax.ShapeDtypeStruct((rows, ncol), F32),
        compiler_params=pltpu.CompilerParams(
            dimension_semantics=("arbitrary",), vmem_limit_bytes=VMEM_LIMIT),
        name="mod",
    )(c_all, w_mod, b_mod)


def _in_proj(x3, mod3, n_pre1_ref, w_in_ref):
    bb, tt, _ = x3.shape
    sh1 = mod3[:, :, 0:D_MODEL]
    sc1 = mod3[:, :, D_MODEL:2 * D_MODEL]
    coef = n_pre1_ref[...].reshape(1, 1, D_MODEL) * (1.0 + sc1)
    u = (x3 * _rms_scale(x3)) * coef + sh1
    u2d = u.reshape(bb * tt, D_MODEL).astype(BF16)
    return jnp.dot(u2d, w_in_ref[...], preferred_element_type=F32)


def _lru_coeffs(xc, wg_ref, b_r_ref, b_i_ref, lam_ref):
    xcb = xc.astype(BF16)
    g0 = jnp.dot(xcb[:, :GATE_HALF], wg_ref[0], preferred_element_type=F32)
    g1 = jnp.dot(xcb[:, GATE_HALF:], wg_ref[1], preferred_element_type=F32)
    r = jax.nn.sigmoid(jnp.concatenate([g0[:, :GATE_HALF], g1[:, :GATE_HALF]], axis=1) + b_r_ref[...])
    i = jax.nn.sigmoid(jnp.concatenate([g0[:, GATE_HALF:], g1[:, GATE_HALF:]], axis=1) + b_i_ref[...])
    lam = lam_ref[...]
    sp = jnp.maximum(-lam, 0.0) + jnp.log1p(jnp.exp(-jnp.abs(lam)))
    log_a = -LRU_C * r * sp
    a = jnp.exp(log_a)
    y = 2.0 * log_a
    a2 = a * a
    d = a2 - 1.0
    small = d * y / jnp.log(a2)
    em1 = jnp.where(y < EXPM1_DIRECT_BELOW, d, jnp.where(d == 0.0, y, small))
    gain = jnp.sqrt(-em1)
    return a, gain * (i * xc)


def _rope(xh, cos2, sin2, lane_axis):
    return xh * cos2 + pltpu.roll(xh, DK // 2, axis=lane_axis) * sin2


def _group_norm(o):
    mu = jnp.mean(o, axis=-1, keepdims=True)
    d = o - mu
    var = jnp.mean(d * d, axis=-1, keepdims=True)
    return d * lax.rsqrt(var + GN_EPS)


def _post_mixer(x3, mod3, out_a, out_b, w_out_ref, n_post1_ref, n_pre2_ref, w_router_ref, b_router_ref,
                earlier_ref, x1_ref, pk_ref, cls_ref, rank_ref, cnt_scr):
    bb, tt, _ = x3.shape
    m = bb * tt
    y = (jnp.dot(out_a.astype(BF16), w_out_ref[0:D_LRU, :], preferred_element_type=F32)
         + jnp.dot(out_b.astype(BF16), w_out_ref[D_LRU:, :], preferred_element_type=F32))
    g1 = mod3[:, :, 2 * D_MODEL:3 * D_MODEL]
    sh2 = mod3[:, :, 3 * D_MODEL:4 * D_MODEL]
    sc2 = mod3[:, :, 4 * D_MODEL:5 * D_MODEL]
    y3 = y.reshape(bb, tt, D_MODEL)
    x1 = x3 + (y3 * _rms_scale(y3)) * (g1 * n_post1_ref[...].reshape(1, 1, D_MODEL))
    u2 = (x1 * _rms_scale(x1)) * (n_pre2_ref[...].reshape(1, 1, D_MODEL) * (1.0 + sc2)) + sh2
    x1_ref[...] = x1.reshape(m, D_MODEL)
    u2f = u2.reshape(m, D_MODEL)
    for j in range(ROW_GROUPS):
        pk_ref[pl.ds(j, m, stride=SUBLANES), :] = u2f[:, j * LANES:(j + 1) * LANES]
    u2b = u2f.astype(BF16)

    logits = jnp.dot(u2b, w_router_ref[...], preferred_element_type=F32) + b_router_ref[...]
    lane = lax.broadcasted_iota(jnp.int32, (m, LANES), 1)
    lane_f = lane.astype(F32)
    is_g = (lane >= N_EXPERTS) & (lane < N_EXPERTS + N_GROUPS)
    p_group = _masked_softmax(logits, is_g)
    p_g = jnp.max(p_group, axis=-1, keepdims=True)
    g_lane = jnp.min(jnp.where(is_g & (p_group == p_g), lane_f, float(LANES)), axis=-1, keepdims=True)
    e_lo = (g_lane - N_EXPERTS) * PER_GROUP
    in_g = (lane_f >= e_lo) & (lane_f < e_lo + PER_GROUP)
    p_e = _masked_softmax(logits, in_g)
    pm = jnp.where(in_g, p_e, -1.0)
    w1 = jnp.max(pm, axis=-1, keepdims=True)
    i1 = jnp.min(jnp.where(pm == w1, lane_f, float(LANES)), axis=-1, keepdims=True)
    pm2 = jnp.where(lane_f == i1, -1.0, pm)
    w2 = jnp.max(pm2, axis=-1, keepdims=True)
    i2 = jnp.min(jnp.where(pm2 == w2, lane_f, float(LANES)), axis=-1, keepdims=True)
    a = jnp.minimum(i1, i2) - e_lo
    b = jnp.maximum(i1, i2) - e_lo
    pair = jnp.where(a == 0.0, jnp.where(b == 3.0, 4.0, b - 1.0), jnp.where(a == 1.0, b, 5.0))
    cls = (g_lane - N_EXPERTS) * N_PAIRS + pair
    onehot = lane_f == cls
    prefix = jnp.dot(earlier_ref[...], jnp.where(onehot, 1.0, 0.0).astype(BF16), preferred_element_type=F32)
    run = cnt_scr[0:1, :]
    rank = jnp.sum(jnp.where(onehot, prefix + run, 0.0), axis=-1, keepdims=True)
    cnt_scr[...] = jnp.broadcast_to(
        run + jnp.sum(jnp.where(onehot, 1.0, 0.0), axis=0, keepdims=True), cnt_scr.shape)
    route = jnp.where(lane == 0, cls, jnp.where(lane == 1, rank, 0.0))
    route_t = jnp.transpose(route)
    cls_ref[...] = route_t[0:1, :].astype(jnp.int32)
    rank_ref[...] = route_t[1:2, :].astype(jnp.int32)


def _group_scan(a3, b3):
    tpos = lax.broadcasted_iota(jnp.int32, a3.shape, 1)
    s = 1
    while s < a3.shape[1]:
        keep = tpos >= s
        a_sh = jnp.where(keep, pltpu.roll(a3, s, axis=1), 1.0)
        b_sh = jnp.where(keep, pltpu.roll(b3, s, axis=1), 0.0)
        b3 = a3 * b_sh + b3
        a3 = a3 * a_sh
        s *= 2
    return a3, b3


def _scan_rows(a, b, h0):
    n, c = a.shape
    groups = n // SUBLANES
    a3, b3 = _group_scan(a.reshape(groups, SUBLANES, c), b.reshape(groups, SUBLANES, c))
    carry = h0
    out = []
    for g in range(groups):
        hg = b3[g] + a3[g] * carry
        out.append(hg)
        carry = hg[SUBLANES - 1:SUBLANES, :]
    return jnp.concatenate(out, axis=0)


def _prompt_mixer_kernel(x_ref, mod_ref, cos_ref, sin_ref,
                         n_pre1_ref, n_post1_ref, n_pre2_ref,
                         w_in_ref, conv_w_ref, conv_b_ref, wg_ref, b_r_ref, b_i_ref, lam_ref,
                         gn_w_ref, w_out_ref, w_router_ref, b_router_ref,
                         mask_ref, wstate_ref, cross_ref, cdecay_ref, earlier_ref,
                         x1_ref, pk_ref, cls_ref, rank_ref, conv_out_ref, lru_out_ref, ret_out_ref,
                         cnt_out_ref,
                         conv_scr, h_scr, s_scr, cnt_scr, z_scr):
    t = pl.program_id(1)
    tt = x_ref.shape[1]

    @pl.when((pl.program_id(0) == 0) & (t == 0))
    def _():
        cnt_scr[...] = jnp.zeros_like(cnt_scr)

    @pl.when(t == 0)
    def _():
        conv_scr[...] = jnp.zeros_like(conv_scr)
        h_scr[...] = jnp.zeros_like(h_scr)
        s_scr[...] = jnp.zeros_like(s_scr)

    x3 = x_ref[...]
    mod3 = mod_ref[...]
    z_scr[...] = _in_proj(x3, mod3, n_pre1_ref, w_in_ref)
    group = lambda c: z_scr[:, c * D_LRU:(c + 1) * D_LRU]
    xr = group(0)

    groups = tt // SUBLANES
    xr3 = xr.reshape(groups, SUBLANES, D_LRU)
    tpos = lax.broadcasted_iota(jnp.int32, xr3.shape, 1)
    tail = conv_scr[...]
    xc3 = jnp.broadcast_to(conv_b_ref[...].reshape(1, 1, D_LRU), xr3.shape)
    for j in range(CONV_W):
        back = CONV_W - 1 - j
        w_j = conv_w_ref[j:j + 1, :].reshape(1, 1, D_LRU)
        if back == 0:
            term = xr3
        else:
            cur = pltpu.roll(xr3, back, axis=1)
            first = pltpu.roll(tail, back, axis=0).reshape(1, SUBLANES, D_LRU)
            prev = jnp.concatenate([first, cur[:groups - 1]], axis=0)
            term = jnp.where(tpos >= back, cur, prev)
        xc3 = xc3 + term * w_j
    xc = xc3.reshape(tt, D_LRU)
    conv_scr[...] = xr[tt - SUBLANES:, :]

    a, b = _lru_coeffs(xc, wg_ref, b_r_ref, b_i_ref, lam_ref)
    hseq = _scan_rows(a, b, h_scr[0:1, :])
    h_scr[...] = jnp.broadcast_to(hseq[tt - 1:tt, :], h_scr.shape)
    out_a = hseq * jax.nn.gelu(group(1), approximate=True)

    cos2 = cos_ref[...]
    sin2 = sin_ref[...]
    scale = DK ** -0.5
    o_heads = []
    for h in range(N_HEADS):
        hs = slice(h * DK, (h + 1) * DK)
        head = lambda c: z_scr[:, c * D_LRU + h * DK:c * D_LRU + (h + 1) * DK]
        qh = (_rope(head(2), cos2, sin2, 1) * scale).astype(BF16)
        kh = _rope(head(3), cos2, sin2, 1)
        vh = head(4).astype(BF16)
        o_chunks = []
        for c in range(tt // RET_CHUNK):
            cs = slice(c * RET_CHUNK, (c + 1) * RET_CHUNK)
            qc = qh[cs]
            kc = kh[cs]
            vc = vh[cs]
            s_prev = s_scr[h]
            scores = lax.dot_general(qc, kc.astype(BF16), (((1,), (1,)), ((), ())),
                                     preferred_element_type=F32) * mask_ref[h]
            inner = jnp.dot(scores.astype(BF16), vc, preferred_element_type=F32)
            cross = jnp.dot(qc, s_prev.astype(BF16), preferred_element_type=F32) * cross_ref[:, hs]
            kw = (kc * wstate_ref[:, hs]).astype(BF16)
            kv = lax.dot_general(kw, vc, (((0,), (0,)), ((), ())), preferred_element_type=F32)
            s_scr[h] = cdecay_ref[:, hs] * s_prev + kv
            o_chunks.append(inner + cross)
        o_heads.append(_group_norm(jnp.concatenate(o_chunks, axis=0)))
    o = jnp.concatenate(o_heads, axis=1)
    out_b = o * gn_w_ref[...] * _silu(group(5))

    _post_mixer(x3, mod3, out_a, out_b, w_out_ref, n_post1_ref, n_pre2_ref, w_router_ref, b_router_ref,
                earlier_ref, x1_ref, pk_ref, cls_ref, rank_ref, cnt_scr)
    cnt_out_ref[...] = cnt_scr[...]

    @pl.when(t == pl.num_programs(1) - 1)
    def _():
        conv_out_ref[0] = xr[tt - SUBLANES:, :]
        lru_out_ref[0] = hseq[tt - SUBLANES:, :]
        ret_out_ref[0] = s_scr[...]


def _const_spec(shape):
    nd = len(shape)
    return pl.BlockSpec(shape, lambda *_: (0,) * nd)


def _prompt_mixer_call(x, mod3, cos2, sin2, wts, tables):
    bsz, seq, _ = x.shape
    tt = PROMPT_TILE
    nt = seq // tt
    n_tok = bsz * seq
    tok_spec = pl.BlockSpec((tt, D_MODEL), lambda b, t: (b * nt + t, 0))
    in_specs = [
        pl.BlockSpec((1, tt, D_MODEL), lambda b, t: (b, t, 0)),
        pl.BlockSpec((1, 1, 6 * D_MODEL), lambda b, t: (b, 0, 0)),
        pl.BlockSpec((tt, LANES), lambda b, t: (t, 0)),
        pl.BlockSpec((tt, LANES), lambda b, t: (t, 0)),
    ] + [_const_spec(w.shape) for w in wts] + [_const_spec(tb.shape) for tb in tables]
    out_specs = [
        tok_spec,
        pl.BlockSpec((tt * SUBLANES, LANES), lambda b, t: (b * nt + t, 0)),
        pl.BlockSpec((1, tt), lambda b, t: (0, b * nt + t)),
        pl.BlockSpec((1, tt), lambda b, t: (0, b * nt + t)),
        pl.BlockSpec((1, SUBLANES, D_LRU), lambda b, t: (b, 0, 0)),
        pl.BlockSpec((1, SUBLANES, D_LRU), lambda b, t: (b, 0, 0)),
        pl.BlockSpec((1, N_HEADS, DK, DV), lambda b, t: (b, 0, 0, 0)),
        pl.BlockSpec((SUBLANES, LANES), lambda b, t: (0, 0)),
    ]
    out_shape = [
        jax.ShapeDtypeStruct((n_tok, D_MODEL), F32),
        jax.ShapeDtypeStruct((n_tok * SUBLANES, LANES), F32),
        jax.ShapeDtypeStruct((1, n_tok), jnp.int32),
        jax.ShapeDtypeStruct((1, n_tok), jnp.int32),
        jax.ShapeDtypeStruct((bsz, SUBLANES, D_LRU), F32),
        jax.ShapeDtypeStruct((bsz, SUBLANES, D_LRU), F32),
        jax.ShapeDtypeStruct((bsz, N_HEADS, DK, DV), F32),
        jax.ShapeDtypeStruct((SUBLANES, LANES), F32),
    ]
    return pl.pallas_call(
        _prompt_mixer_kernel,
        grid=(bsz, nt),
        in_specs=in_specs,
        out_specs=out_specs,
        out_shape=out_shape,
        scratch_shapes=[
            pltpu.VMEM((SUBLANES, D_LRU), F32),
            pltpu.VMEM((SUBLANES, D_LRU), F32),
            pltpu.VMEM((N_HEADS, DK, DV), F32),
            pltpu.VMEM((SUBLANES, LANES), F32),
            pltpu.VMEM((tt, D_IN_PROJ), F32),
        ],
        compiler_params=pltpu.CompilerParams(
            dimension_semantics=("arbitrary", "arbitrary"), vmem_limit_bytes=VMEM_LIMIT),
        name="prompt_mixer",
    )(x, mod3, cos2, sin2, *wts, *tables)


def _sample_mixer_kernel(x_ref, mod_ref, cos_ref, sin_ref, buf_ref, h0_ref, s0_ref, cnt_in_ref,
                         n_pre1_ref, n_post1_ref, n_pre2_ref,
                         w_in_ref, conv_w_ref, conv_b_ref, wg_ref, b_r_ref, b_i_ref, lam_ref,
                         gn_w_ref, w_out_ref, w_router_ref, b_router_ref,
                         smask_ref, wstate_ref, cross_ref, cdecay_ref, earlier_ref,
                         x1_ref, pk_ref, cls_ref, rank_ref, xr_out_ref, h_out_ref, ret_out_ref,
                         cnt_out_ref,
                         cnt_scr):
    bb, ts, _ = x_ref.shape
    m = bb * ts

    @pl.when(pl.program_id(0) == 0)
    def _():
        cnt_scr[...] = cnt_in_ref[...]

    x3 = x_ref[...]
    mod3 = mod_ref[...].reshape(bb, 1, 6 * D_MODEL)
    z = _in_proj(x3, mod3, n_pre1_ref, w_in_ref)
    xr, yg, q, k, v, g = (z[:, c * D_LRU:(c + 1) * D_LRU] for c in range(D_IN_PROJ // D_LRU))
    xr_out_ref[...] = xr

    xr3 = xr.reshape(bb, ts, D_LRU)
    buf3 = buf_ref[...]
    tpos = lax.broadcasted_iota(jnp.int32, (bb, ts, D_LRU), 1)
    xc3 = jnp.broadcast_to(conv_b_ref[...].reshape(1, 1, D_LRU), (bb, ts, D_LRU))
    for j in range(CONV_W):
        back = CONV_W - 1 - j
        w_j = conv_w_ref[j:j + 1, :].reshape(1, 1, D_LRU)
        if back == 0:
            term = xr3
        else:
            cur = pltpu.roll(xr3, back, axis=1)
            up = CONV_W - 1 - back
            old = buf3 if up == 0 else pltpu.roll(buf3, ts - up, axis=1)
            term = jnp.where(tpos >= back, cur, old)
        xc3 = xc3 + term * w_j
    xc = xc3.reshape(m, D_LRU)

    a, b = _lru_coeffs(xc, wg_ref, b_r_ref, b_i_ref, lam_ref)
    a3 = a.reshape(bb, ts, D_LRU)
    b3 = b.reshape(bb, ts, D_LRU) + a3 * h0_ref[...]
    _, h3 = _group_scan(a3, b3)
    hseq = h3.reshape(m, D_LRU)
    h_out_ref[...] = hseq
    out_a = hseq * jax.nn.gelu(yg, approximate=True)

    cos2 = cos_ref[...].reshape(1, ts, LANES)
    sin2 = sin_ref[...].reshape(1, ts, LANES)
    scale = DK ** -0.5
    o_heads = []
    for h in range(N_HEADS):
        hs = slice(h * DK, (h + 1) * DK)
        q3 = (_rope(q[:, hs].reshape(bb, ts, DK), cos2, sin2, 2) * scale).astype(BF16)
        k3 = _rope(k[:, hs].reshape(bb, ts, DK), cos2, sin2, 2)
        v3 = v[:, hs].reshape(bb, ts, DV).astype(BF16)
        q2 = q3.reshape(m, DK)
        k2 = k3.reshape(m, DK).astype(BF16)
        v2 = v3.reshape(m, DV)
        scores = lax.dot_general(q2, k2, (((1,), (1,)), ((), ())),
                                 preferred_element_type=F32) * smask_ref[h]
        inner = jnp.dot(scores.astype(BF16), v2, preferred_element_type=F32)
        s0h = s0_ref[:, h]
        cross = jnp.einsum('bid,bde->bie', q3, s0h.astype(BF16), preferred_element_type=F32)
        cross = cross * cross_ref[:, hs].reshape(1, ts, DV)
        kw3 = (k3 * wstate_ref[:, hs].reshape(1, ts, DK)).astype(BF16)
        kv = jnp.einsum('bjd,bje->bde', kw3, v3, preferred_element_type=F32)
        ret_out_ref[:, h] = cdecay_ref[:, hs].reshape(1, 1, DV) * s0h + kv
        o_heads.append(_group_norm(inner + cross.reshape(m, DV)))
    o = jnp.concatenate(o_heads, axis=1)
    out_b = o * gn_w_ref[...] * _silu(g)

    _post_mixer(x3, mod3, out_a, out_b, w_out_ref, n_post1_ref, n_pre2_ref, w_router_ref, b_router_ref,
                earlier_ref, x1_ref, pk_ref, cls_ref, rank_ref, cnt_scr)
    cnt_out_ref[...] = cnt_scr[...]


def _sample_mixer_call(x, mod3, cos2, sin2, buf8, h0p, s0, cnt_in, wts, tables):
    bsz, ts, _ = x.shape
    bb = SAMPLE_SEQS
    m = bb * ts
    n_tok = bsz * ts
    seq_spec = lambda w: pl.BlockSpec((bb, ts, w), lambda i: (i, 0, 0))
    tok_spec = lambda w: pl.BlockSpec((m, w), lambda i: (i, 0))
    in_specs = [
        seq_spec(D_MODEL),
        pl.BlockSpec((bb, 6 * D_MODEL), lambda i: (i, 0)),
        _const_spec(cos2.shape),
        _const_spec(sin2.shape),
        seq_spec(D_LRU),
        seq_spec(D_LRU),
        pl.BlockSpec((bb, N_HEADS, DK, DV), lambda i: (i, 0, 0, 0)),
        _const_spec(cnt_in.shape),
    ] + [_const_spec(w.shape) for w in wts] + [_const_spec(tb.shape) for tb in tables]
    out_specs = [
        tok_spec(D_MODEL),
        pl.BlockSpec((m * SUBLANES, LANES), lambda i: (i, 0)),
        pl.BlockSpec((1, m), lambda i: (0, i)),
        pl.BlockSpec((1, m), lambda i: (0, i)),
        tok_spec(D_LRU),
        tok_spec(D_LRU),
        pl.BlockSpec((bb, N_HEADS, DK, DV), lambda i: (i, 0, 0, 0)),
        _const_spec(cnt_in.shape),
    ]
    out_shape = [
        jax.ShapeDtypeStruct((n_tok, D_MODEL), F32),
        jax.ShapeDtypeStruct((n_tok * SUBLANES, LANES), F32),
        jax.ShapeDtypeStruct((1, n_tok), jnp.int32),
        jax.ShapeDtypeStruct((1, n_tok), jnp.int32),
        jax.ShapeDtypeStruct((n_tok, D_LRU), F32),
        jax.ShapeDtypeStruct((n_tok, D_LRU), F32),
        jax.ShapeDtypeStruct((bsz, N_HEADS, DK, DV), F32),
        jax.ShapeDtypeStruct(cnt_in.shape, F32),
    ]
    return pl.pallas_call(
        _sample_mixer_kernel,
        grid=(bsz // bb,),
        in_specs=in_specs,
        out_specs=out_specs,
        out_shape=out_shape,
        scratch_shapes=[pltpu.VMEM((SUBLANES, LANES), F32)],
        compiler_params=pltpu.CompilerParams(
            dimension_semantics=("arbitrary",), vmem_limit_bytes=VMEM_LIMIT),
        name="sample_mixer",
    )(x, mod3, cos2, sin2, buf8, h0p, s0, cnt_in, *wts, *tables)


def _slab(ref, r):
    return ref.at[pl.ds(pl.multiple_of(r * SUBLANES, SUBLANES), SUBLANES)]


def _dispatch_kernel(cls_ref, rank_ref, cnt_ref, padcnt_ref, start_ref, srcp_ref, srcs_ref, out_ref, ring, sem,
                     *, p_tiles):
    i = pl.program_id(0)
    n = pl.num_programs(0)
    tile_rows = ring.shape[1]
    td = tile_rows // SUBLANES
    moe_rows = MOE_TILE * SUBLANES

    def issue(slot):
        def body(g, carry):
            first = i * td + g * DMA_GROUP
            slots = [start_ref[cls_ref[first + j]] + rank_ref[first + j] for j in range(DMA_GROUP)]
            first = g * DMA_GROUP
            for j in range(DMA_GROUP):
                pltpu.make_async_copy(
                    _slab(ring.at[slot], first + j), _slab(out_ref, slots[j]), sem.at[slot]
                ).start(priority=j % 2)
            return carry
        lax.fori_loop(0, td // DMA_GROUP, body, 0)

    def wait_tile(slot):
        pltpu.make_async_copy(ring.at[slot], out_ref.at[pl.ds(0, tile_rows)], sem.at[slot]).wait()

    def zero_fill(slot):
        ring[slot] = jnp.zeros((tile_rows, LANES), F32)
        zero_src = ring.at[slot]

        def per_class(c, carry):
            lo = start_ref[c] + cnt_ref[c]
            hi = start_ref[c] + padcnt_ref[c]

            def fill(r, carry2):
                pltpu.make_async_copy(_slab(zero_src, 0), _slab(out_ref, r), sem.at[slot]).start()
                return carry2
            lax.fori_loop(lo, hi, fill, 0)

            def done(r, carry2):
                pltpu.make_async_copy(_slab(zero_src, 0), _slab(out_ref, 0), sem.at[slot]).wait()
                return carry2
            lax.fori_loop(lo, hi, done, 0)
            return carry
        lax.fori_loop(0, N_CLASSES, per_class, 0)

        used_tiles = (start_ref[N_CLASSES - 1] + padcnt_ref[N_CLASSES - 1]) // MOE_TILE
        all_tiles = out_ref.shape[0] // moe_rows

        def tile_copy(t):
            return pltpu.make_async_copy(
                zero_src.at[pl.ds(0, moe_rows)],
                out_ref.at[pl.ds(pl.multiple_of(t * moe_rows, moe_rows), moe_rows)], sem.at[slot])

        def fill_tile(t, carry):
            tile_copy(t).start()
            return carry
        lax.fori_loop(used_tiles, all_tiles, fill_tile, 0)

        def done_tile(t, carry):
            tile_copy(t).wait()
            return carry
        lax.fori_loop(used_tiles, all_tiles, done_tile, 0)

    for s in range(RING):
        @pl.when(i % RING == s)
        def _():
            @pl.when(i < p_tiles)
            def _():
                ring[s] = srcp_ref[...]

            @pl.when(i >= p_tiles)
            def _():
                ring[s] = srcs_ref[...]

            issue(s)

            @pl.when(i >= RING - 1)
            def _():
                wait_tile((s + 1) % RING)

            @pl.when(i == n - 1)
            def _():
                for back in range(RING - 2, -1, -1):
                    wait_tile((s - back) % RING)
                zero_fill(s)


def _dispatch_call(cls, rank, cnt, padcnt, start, pk_p, pk_s, n_rows):
    td = DISPATCH_TILE
    tile_rows = td * SUBLANES
    p_tiles = pk_p.shape[0] // tile_rows
    s_tiles = pk_s.shape[0] // tile_rows
    assert p_tiles + s_tiles >= RING and td >= MOE_TILE
    return pl.pallas_call(
        functools.partial(_dispatch_kernel, p_tiles=p_tiles),
        grid_spec=pltpu.PrefetchScalarGridSpec(
            num_scalar_prefetch=5,
            grid=(p_tiles + s_tiles,),
            in_specs=[
                pl.BlockSpec((tile_rows, LANES), lambda i, *_: (jnp.minimum(i, p_tiles - 1), 0)),
                pl.BlockSpec((tile_rows, LANES), lambda i, *_: (jnp.maximum(i - p_tiles, 0), 0)),
            ],
            out_specs=pl.BlockSpec(memory_space=pl.ANY),
            scratch_shapes=[pltpu.VMEM((RING, tile_rows, LANES), F32),
                            pltpu.SemaphoreType.DMA((RING,))],
        ),
        out_shape=jax.ShapeDtypeStruct((n_rows * SUBLANES, LANES), F32),
        compiler_params=pltpu.CompilerParams(
            dimension_semantics=("arbitrary",), has_side_effects=True, vmem_limit_bytes=VMEM_LIMIT),
        name="moe_dispatch",
    )(cls, rank, cnt, padcnt, start, pk_p, pk_s)


def _moe_kernel(ea_ref, eb_ref, valid_ref, xs_ref, w_router_ref, b_router_ref, *refs):
    f_ref = refs[-1]
    for s in range(MOE_TILES_PER_STEP):
        _moe_tile(pl.program_id(0) * MOE_TILES_PER_STEP + s, s * MOE_TILE * SUBLANES, ea_ref, eb_ref,
                  valid_ref, xs_ref, w_router_ref, b_router_ref, *refs[6 * s:6 * s + 6], f_ref)


def _moe_tile(t, row0, ea_ref, eb_ref, valid_ref, xs_ref, w_router_ref, b_router_ref,
              wga_ref, wua_ref, wda_ref, wgb_ref, wub_ref, wdb_ref, f_ref):
    tm = MOE_TILE

    @pl.when(valid_ref[t] == 1)
    def _():
        x = jnp.concatenate([xs_ref[pl.ds(row0 + j, tm, stride=SUBLANES), :] for j in range(ROW_GROUPS)],
                            axis=1).astype(BF16)
        e_a = ea_ref[t]
        e_b = eb_ref[t]
        e_lo = (e_a // PER_GROUP) * PER_GROUP
        logits = jnp.dot(x, w_router_ref[...], preferred_element_type=F32) + b_router_ref[...]
        lane = lax.broadcasted_iota(jnp.int32, (tm, LANES), 1)
        pick = lambda p, idx: jnp.sum(jnp.where(lane == idx, p, 0.0), axis=-1, keepdims=True)
        p_group = _masked_softmax(logits, (lane >= N_EXPERTS) & (lane < N_EXPERTS + N_GROUPS))
        p_g = pick(p_group, N_EXPERTS + e_a // PER_GROUP)
        p_e = _masked_softmax(logits, (lane >= e_lo) & (lane < e_lo + PER_GROUP))
        w_a = pick(p_e, e_a)
        w_b = pick(p_e, e_b)
        wsum = w_a + w_b

        def expert(wg_ref, wu_ref, gate):
            hg = jnp.dot(x, wg_ref[0].astype(BF16), preferred_element_type=F32)
            hu = jnp.dot(x, wu_ref[0].astype(BF16), preferred_element_type=F32)
            return (_silu(hg) * hu * gate).astype(BF16)

        ha = expert(wga_ref, wua_ref, p_g * (w_a / wsum))
        hb = expert(wgb_ref, wub_ref, p_g * (w_b / wsum))
        for c in range(D_MODEL // GATE_HALF):
            cols = slice(c * GATE_HALF, (c + 1) * GATE_HALF)
            f = (jnp.dot(ha, wda_ref[0, :, cols].astype(BF16), preferred_element_type=F32)
                 + jnp.dot(hb, wdb_ref[0, :, cols].astype(BF16), preferred_element_type=F32))
            for jj in range(GATE_HALF // LANES):
                j = c * (GATE_HALF // LANES) + jj
                f_ref[pl.ds(row0 + j, tm, stride=SUBLANES), :] = f[:, jj * LANES:(jj + 1) * LANES]

    @pl.when(valid_ref[t] == 0)
    def _():
        f_ref[pl.ds(row0, tm * SUBLANES), :] = jnp.zeros((tm * SUBLANES, LANES), F32)


def _moe_call(tile_ea, tile_eb, tile_valid, rows, w_router, b_router, wg, wu, wd):
    n_rows = rows.shape[0] // SUBLANES
    per_step = MOE_TILES_PER_STEP
    step_rows = MOE_TILE * per_step * SUBLANES
    assert n_rows % (MOE_TILE * per_step) == 0
    const = lambda a: pl.BlockSpec(a.shape, lambda t, ea, eb, v: (0,) * a.ndim)

    def expert_block(shape, table, s):
        return pl.BlockSpec(shape, lambda t, ea, eb, v: ((ea, eb)[table][t * per_step + s], 0, 0))

    weight_specs, weight_args = [], []
    for s in range(per_step):
        for table in range(2):
            weight_specs += [expert_block((1, D_MODEL, D_EXPERT), table, s),
                             expert_block((1, D_MODEL, D_EXPERT), table, s),
                             expert_block((1, D_EXPERT, D_MODEL), table, s)]
            weight_args += [wg, wu, wd]
    return pl.pallas_call(
        _moe_kernel,
        grid_spec=pltpu.PrefetchScalarGridSpec(
            num_scalar_prefetch=3,
            grid=(n_rows // (MOE_TILE * per_step),),
            in_specs=[
                pl.BlockSpec((step_rows, LANES), lambda t, ea, eb, v: (t, 0)),
                const(w_router), const(b_router),
            ] + weight_specs,
            out_specs=pl.BlockSpec((step_rows, LANES), lambda t, ea, eb, v: (t, 0)),
        ),
        out_shape=jax.ShapeDtypeStruct((n_rows * SUBLANES, LANES), F32),
        compiler_params=pltpu.CompilerParams(
            dimension_semantics=("arbitrary",), vmem_limit_bytes=VMEM_LIMIT),
        name="moe_experts",
    )(tile_ea, tile_eb, tile_valid, rows, w_router, b_router, *weight_args)


def _combine_kernel(cls_ref, rank_ref, start_ref, x1_ref, mod_ref, n_post2_ref, f_hbm, o_ref, fbuf, sem):
    i = pl.program_id(0)
    n = pl.num_programs(0)
    tf = x1_ref.shape[0]

    def issue(tile, slot):
        def body(g, carry):
            first = tile * tf + g * DMA_GROUP
            slots = [start_ref[cls_ref[first + j]] + rank_ref[first + j] for j in range(DMA_GROUP)]
            first = g * DMA_GROUP
            for j in range(DMA_GROUP):
                pltpu.make_async_copy(
                    _slab(f_hbm, slots[j]), _slab(fbuf.at[slot], first + j), sem.at[slot]
                ).start(priority=j % 2)
            return carry
        lax.fori_loop(0, tf // DMA_GROUP, body, 0)

    def finish(slot):
        pltpu.make_async_copy(f_hbm.at[pl.ds(0, tf * SUBLANES)], fbuf.at[slot], sem.at[slot]).wait()
        f = jnp.concatenate([fbuf[slot, pl.ds(j, tf, stride=SUBLANES), :] for j in range(D_MODEL // LANES)],
                            axis=1)
        bb = mod_ref.shape[0]
        mod3 = mod_ref[...].reshape(bb, 1, 6 * D_MODEL)
        f3 = f.reshape(bb, tf // bb, D_MODEL)
        g2 = mod3[:, :, 5 * D_MODEL:6 * D_MODEL]
        x13 = x1_ref[...].reshape(f3.shape)
        out = x13 + (f3 * _rms_scale(f3)) * (g2 * n_post2_ref[...].reshape(1, 1, D_MODEL))
        o_ref[...] = out.reshape(o_ref.shape)

    @pl.when(i == 0)
    def _():
        issue(0, 0)

    for parity in range(2):
        @pl.when(i % 2 == parity)
        def _():
            @pl.when(i + 1 < n)
            def _():
                issue(i + 1, 1 - parity)
            finish(parity)


def _combine_call(cls, rank, start, x1, mod3, n_post2, f_sorted, seqs_per_tile):
    n_tok = x1.shape[0]
    tf = FINAL_TILE
    tok_per_seq = n_tok // mod3.shape[0]
    if seqs_per_tile > 1:
        mod_spec = pl.BlockSpec((seqs_per_tile, 6 * D_MODEL), lambda i, *_: (i, 0))
    else:
        mod_spec = pl.BlockSpec((1, 1, 6 * D_MODEL), lambda i, *_: ((i * tf) // tok_per_seq, 0, 0))
    return pl.pallas_call(
        _combine_kernel,
        grid_spec=pltpu.PrefetchScalarGridSpec(
            num_scalar_prefetch=3,
            grid=(n_tok // tf,),
            in_specs=[
                pl.BlockSpec((tf, D_MODEL), lambda i, *_: (i, 0)),
                mod_spec,
                pl.BlockSpec((1, D_MODEL), lambda i, *_: (0, 0)),
                pl.BlockSpec(memory_space=pl.ANY),
            ],
            out_specs=pl.BlockSpec((tf, D_MODEL), lambda i, *_: (i, 0)),
            scratch_shapes=[pltpu.VMEM((2, tf * SUBLANES, LANES), F32), pltpu.SemaphoreType.DMA((2,))],
        ),
        out_shape=jax.ShapeDtypeStruct((n_tok, D_MODEL), F32),
        compiler_params=pltpu.CompilerParams(
            dimension_semantics=("arbitrary",), vmem_limit_bytes=VMEM_LIMIT),
        name="moe_combine",
    )(cls, rank, start, x1, mod3, n_post2, f_sorted)


def _block_diag_gate(w_r, w_i):
    per_half = GATE_HALF // LRU_BLOCK
    n_half = D_LRU // GATE_HALF
    eye = np.eye(per_half, dtype=np.float32)

    def block_diag(w):
        w4 = w.reshape(n_half, per_half, LRU_BLOCK, LRU_BLOCK)
        return jnp.einsum('hnij,nm->hnimj', w4, eye).reshape(n_half, GATE_HALF, GATE_HALF)

    return jnp.concatenate([block_diag(w_r), block_diag(w_i)], axis=-1).astype(BF16)


def _rope_tables(pos):
    half = DK // 2
    inv = np.float64(ROPE_BASE) ** (-np.arange(half, dtype=np.float64) / half)
    ang = np.asarray(pos, np.float64)[:, None] * inv[None, :]
    cos = np.cos(ang)
    sin = np.sin(ang)
    return (np.concatenate([cos, cos], axis=-1).astype(np.float32),
            np.concatenate([-sin, sin], axis=-1).astype(np.float32))


def _earlier_table(m):
    return jnp.asarray(np.tril(np.ones((m, m), np.float32), -1), BF16)


def _decay_tables(c):
    log_g = np.log1p(-np.exp2(-5.0 - np.arange(N_HEADS, dtype=np.float64)))
    idx = np.arange(c, dtype=np.float64)
    diff = idx[:, None] - idx[None, :]
    mask = np.where(diff[None] >= 0, np.exp(np.maximum(diff, 0.0)[None] * log_g[:, None, None]), 0.0)
    w_state = np.exp((c - 1.0 - idx)[None, :] * log_g[:, None])
    cross_decay = np.exp((idx + 1.0)[:, None] * log_g[None, :])
    chunk_decay = np.exp(c * log_g)
    wstate_full = np.repeat(w_state.T, DK, axis=1)
    cross_full = np.repeat(cross_decay, DV, axis=1)
    cdecay_full = np.repeat(chunk_decay, DV)[None, :]
    return tuple(t.astype(np.float32) for t in (mask, wstate_full, cross_full, cdecay_full))


def kernel(x_prompt, x_sample, state_conv, state_lru, state_ret, c_prompt, c_sample, w_mod, b_mod, norm_pre_mix, norm_post_mix, norm_pre_ffn, norm_post_ffn, w_in, conv_w, conv_b, w_rgate, b_rgate, w_igate, b_igate, lru_lambda, ret_gn_w, w_out, w_router_group, b_router_group, w_router_expert, b_router_expert, w_exp_gate, w_exp_up, w_exp_down):
    bp, tp, _ = x_prompt.shape
    bs, ts, _ = x_sample.shape
    l = 0

    mod = _mod_call(jnp.concatenate([c_sample, c_prompt], axis=0), w_mod[l], b_mod[l][None, :])
    mod_p = mod[bs:][:, None, :]
    mod_s = mod

    unused = LANES - N_EXPERTS - N_GROUPS
    w_router = jnp.concatenate(
        [w_router_expert[l], w_router_group[l], jnp.zeros((D_MODEL, unused), F32)], axis=1).astype(BF16)
    b_router = jnp.concatenate(
        [b_router_expert[l], b_router_group[l], jnp.zeros((unused,), F32)])[None, :]

    row = lambda vec: vec.reshape(1, -1)
    wts = (row(norm_pre_mix[l]), row(norm_post_mix[l]), row(norm_pre_ffn[l]),
           w_in[l].astype(BF16), conv_w[l], row(conv_b[l]),
           _block_diag_gate(w_rgate[l], w_igate[l]),
           row(b_rgate[l]), row(b_igate[l]), row(lru_lambda[l]), row(ret_gn_w[l]),
           w_out[l].astype(BF16), w_router, b_router)

    cos_p, sin_p = _rope_tables(np.arange(tp))
    x1_p, pk_p, cls_p, rank_p, conv_p8, lru_p8, ret_p, cnt_p = _prompt_mixer_call(
        x_prompt, mod_p, cos_p, sin_p, wts,
        _decay_tables(math.gcd(tp, RET_CHUNK)) + (_earlier_table(PROMPT_TILE),))

    cos_s, sin_s = _rope_tables(PAST_LEN + np.arange(ts))
    mask8, wstate_s, cross_s, cdecay_s = _decay_tables(math.gcd(ts, RET_CHUNK))
    eye = np.eye(SAMPLE_SEQS, dtype=np.float32)
    smask = np.stack([np.kron(eye, mask8[h]) for h in range(N_HEADS)])
    buf8 = jnp.pad(state_conv[l], ((0, 0), (0, ts - (CONV_W - 1)), (0, 0)))
    h0p = jnp.pad(state_lru[l][:, None, :], ((0, 0), (0, ts - 1), (0, 0)))
    x1_s, pk_s, cls_s, rank_s, xr_s, h_s, ret_s, cnt_all = _sample_mixer_call(
        x_sample, mod_s, cos_s, sin_s, buf8, h0p, state_ret[l], cnt_p, wts,
        (smask, wstate_s, cross_s, cdecay_s, _earlier_table(SAMPLE_SEQS * ts)))

    n_p = bp * tp
    n_tok = n_p + bs * ts
    tm = MOE_TILE
    max_tiles = n_tok // tm + N_CLASSES
    cls_p, rank_p, cls_s, rank_s = (a.reshape(-1) for a in (cls_p, rank_p, cls_s, rank_s))
    cls = jnp.concatenate([cls_p, cls_s])
    rank = jnp.concatenate([rank_p, rank_s])
    cnt = cnt_all[0, :N_CLASSES].astype(jnp.int32)
    ntile = (cnt + (tm - 1)) // tm
    padcnt = ntile * tm
    before = np.tril(np.ones((N_CLASSES, N_CLASSES), np.int32), -1)
    upto = np.tril(np.ones((N_CLASSES, N_CLASSES), np.int32))
    start = jnp.sum(before * padcnt[None, :], axis=1)
    tile_end = jnp.sum(upto * ntile[None, :], axis=1)
    n_used = jnp.sum(ntile)
    tile_ids = np.arange(max_tiles, dtype=np.int32)
    tile_valid = (tile_ids < n_used).astype(jnp.int32)
    last_used = jnp.minimum(tile_ids, n_used - 1)
    tile_cls = jnp.sum((last_used[:, None] >= tile_end[None, :]).astype(jnp.int32), axis=1)
    pair = tile_cls % N_PAIRS
    first_expert = jnp.where(pair == 5, 2, jnp.where((pair == 2) | (pair == 3), 1, 0))
    second_expert = jnp.where(pair == 0, 1, jnp.where(pair <= 2, 2, 3))
    tile_ea = (tile_cls // N_PAIRS) * PER_GROUP + first_expert
    tile_eb = (tile_cls // N_PAIRS) * PER_GROUP + second_expert

    n_post2 = row(norm_post_ffn[l])
    wg = w_exp_gate[l]
    wu = w_exp_up[l]
    wd = w_exp_down[l]
    rows = _dispatch_call(cls, rank, cnt, padcnt, start, pk_p, pk_s, max_tiles * tm)
    f_sorted = _moe_call(tile_ea, tile_eb, tile_valid, rows, w_router, b_router, wg, wu, wd)
    y_p = _combine_call(cls_p, rank_p, start, x1_p, mod_p, n_post2, f_sorted, 1)
    y_s = _combine_call(cls_s, rank_s, start, x1_s, mod_s, n_post2, f_sorted, FINAL_TILE // ts)

    conv_p = conv_p8[:, SUBLANES - (CONV_W - 1):, :]
    lru_p = lru_p8[:, SUBLANES - 1, :]
    xr_s3 = xr_s.reshape(bs, ts, D_LRU)
    conv_s = xr_s3[:, ts - (CONV_W - 1):, :]
    lru_s = h_s.reshape(bs, ts, D_LRU)[:, ts - 1, :]
    return (y_p.reshape(bp, tp, D_MODEL), y_s.reshape(bs, ts, D_MODEL),
            conv_p[None], lru_p[None], ret_p[None],
            conv_s[None], lru_s[None], ret_s[None])
```
